```python
import jax, jax.numpy as jnp
from jax import lax
import numpy as np

D_MODEL = 1024
BATCH = 4
SEQ = 8192
DEPTH = 2

CHUNK = 64
N_EVEN = (DEPTH + 1) // 2
N_ODD = DEPTH // 2
RMS_EPS = 1e-6

RWKV_WIDTH = D_MODEL // 2
RWKV_HEAD_DIM = 64
RWKV_HEADS = RWKV_WIDTH // RWKV_HEAD_DIM
RWKV_DECAY_RANK = 64
RWKV_A_RANK = 64
RWKV_GATE_RANK = 128
RWKV_COLS = 3 * RWKV_WIDTH + RWKV_DECAY_RANK + RWKV_A_RANK + RWKV_GATE_RANK
RWKV_GN_EPS = 64e-5

LRU_WIDTH = D_MODEL // 2
LRU_BLOCKS = 8
LRU_BLOCK_DIM = LRU_WIDTH // LRU_BLOCKS
CONV_WIDTH = 4
LRU_C = 8.0
LRU_COLS = 2 * LRU_WIDTH
AB_COLS = RWKV_COLS + LRU_COLS
AB_MIX_WIDTH = RWKV_WIDTH + LRU_WIDTH

HGRN_WIDTH = D_MODEL
HGRN_EXPAND = 128
HGRN_HEADS = HGRN_WIDTH // HGRN_EXPAND
HGRN_COLS = 4 * HGRN_WIDTH

N_GROUPS = 4
EXPERTS_PER_GROUP = 8
N_EXPERTS = N_GROUPS * EXPERTS_PER_GROUP
TOP_K = 2
EXPERT_FF = 512
ROUTE_BLOCK = 128

kernel_name = "hybrid_rwkv7_rglru_hgrn2_hmoe"


def rmsnorm(x, g):
    xf = x.astype(jnp.float32)
    y = xf * lax.rsqrt(jnp.mean(xf * xf, -1, keepdims=True) + RMS_EPS)
    return (y * g.astype(jnp.float32)).astype(x.dtype)


def token_shift(p, mu):
    prev = jnp.pad(p, ((0, 0), (1, 0), (0, 0)))[:, :-1]
    return p + (prev - p) * mu


def split_heads(t, heads):
    b, s, w = t.shape
    return t.reshape(b, s, heads, w // heads).astype(jnp.float32)


def rwkv7_recurrence(r, w, k, v, kk, a):
    b, _, h, n = r.shape

    def step(S, inp):
        r_t, w_t, k_t, v_t, kk_t, a_t = inp
        sa = jnp.einsum('bhvk,bhk->bhv', S, -kk_t)
        S = (S * w_t[:, :, None, :] + sa[..., None] * (kk_t * a_t)[:, :, None, :]
             + v_t[..., None] * k_t[:, :, None, :])
        return S, jnp.einsum('bhvk,bhk->bhv', S, r_t)

    xs = tuple(jnp.moveaxis(t, 1, 0) for t in (r, w, k, v, kk, a))
    _, y = lax.scan(step, jnp.zeros((b, h, n, n), jnp.float32), xs)
    return jnp.moveaxis(y, 0, 1)


def rwkv7_mixer(p, mu, w0, w2, a0, a2, g2, k_k, k_a, r_k, ln_w, ln_b):
    b, s, _ = p.shape
    p = token_shift(p, mu)
    idx = [RWKV_WIDTH, 2 * RWKV_WIDTH, 3 * RWKV_WIDTH,
           3 * RWKV_WIDTH + RWKV_DECAY_RANK, 3 * RWKV_WIDTH + RWKV_DECAY_RANK + RWKV_A_RANK]
    r, k, v, wl, al, gl = jnp.split(p, idx, axis=-1)
    w = -jax.nn.softplus(-(w0 + jnp.tanh(wl) @ w2)) - 0.5
    decay = jnp.exp(-jnp.exp(w.astype(jnp.float32)))
    a = jax.nn.sigmoid(a0 + al @ a2)
    g = jax.nn.sigmoid(gl) @ g2
    kk = split_heads(k * k_k, RWKV_HEADS)
    kk = kk / jnp.maximum(jnp.linalg.norm(kk, axis=-1, keepdims=True), 1e-12)
    k = k * (1.0 + (a - 1.0) * k_a)
    rh, kh, vh = (split_heads(t, RWKV_HEADS) for t in (r, k, v))
    y = rwkv7_recurrence(rh, split_heads(decay, RWKV_HEADS), kh, vh, kk,
                         split_heads(a, RWKV_HEADS))
    mean = jnp.mean(y, -1, keepdims=True)
    var = jnp.mean(jnp.square(y - mean), -1, keepdims=True)
    y = ((y - mean) * lax.rsqrt(var + RWKV_GN_EPS)).reshape(b, s, RWKV_WIDTH)
    y = y * ln_w + ln_b
    bonus = jnp.sum(rh * kh * r_k.astype(jnp.float32), -1, keepdims=True) * vh
    return ((y + bonus.reshape(b, s, RWKV_WIDTH)) * g).astype(p.dtype)


def rglru_mixer(p, conv_w, conv_b, w_a, b_a, w_x, b_x, lam):
    b, s, _ = p.shape
    gate, xb = jnp.split(p, [LRU_WIDTH], axis=-1)
    xc = lax.conv_general_dilated(xb, conv_w[:, None, :], window_strides=(1,),
                                  padding=[(CONV_WIDTH - 1, 0)],
                                  dimension_numbers=('NWC', 'WIO', 'NWC'),
                                  feature_group_count=LRU_WIDTH) + conv_b
    xg = xc.reshape(b, s, LRU_BLOCKS, LRU_BLOCK_DIM)
    r = jax.nn.sigmoid(jnp.einsum('btnd,nde->btne', xg, w_a).reshape(b, s, LRU_WIDTH) + b_a)
    i = jax.nn.sigmoid(jnp.einsum('btnd,nde->btne', xg, w_x).reshape(b, s, LRU_WIDTH) + b_x)
    log_a = -LRU_C * r.astype(jnp.float32) * jax.nn.softplus(-lam.astype(jnp.float32))
    a = jnp.exp(log_a)
    u = jnp.sqrt(-jnp.expm1(2.0 * log_a)) * i.astype(jnp.float32) * xc.astype(jnp.float32)

    def combine(c1, c2):
        return c1[0] * c2[0], c2[0] * c1[1] + c2[1]

    _, h = lax.associative_scan(combine, (a, u), axis=1)
    return (h * jax.nn.gelu(gate.astype(jnp.float32))).astype(p.dtype)


def gla_chunkwise(q, k, v, log_f):
    b, s, h, dk = q.shape
    dv = v.shape[-1]
    n_chunks = s // CHUNK

    def to_chunks(t):
        return t.reshape(b, n_chunks, CHUNK, h, t.shape[-1]).transpose(1, 0, 3, 2, 4)

    mask = jnp.tril(jnp.ones((CHUNK, CHUNK), dtype=bool))[:, :, None]

    def step(S, inp):
        q_c, k_c, v_c, lf_c = inp
        bc = jnp.cumsum(lf_c, axis=2)
        diff = bc[:, :, :, None, :] - bc[:, :, None, :, :]
        dec = jnp.exp(jnp.where(mask, diff, -jnp.inf))
        scores = jnp.einsum('bhtd,bhsd,bhtsd->bhts', q_c, k_c, dec)
        o = (jnp.einsum('bhts,bhsv->bhtv', scores, v_c)
             + jnp.einsum('bhtd,bhdv->bhtv', q_c * jnp.exp(bc), S))
        bl = bc[:, :, -1:, :]
        S = (jnp.exp(bl[:, :, 0, :])[..., None] * S
             + jnp.einsum('bhsd,bhsv->bhdv', k_c * jnp.exp(bl - bc), v_c))
        return S, o

    xs = tuple(to_chunks(t) for t in (q, k, v, log_f))
    _, o = lax.scan(step, jnp.zeros((b, h, dk, dv), jnp.float32), xs)
    return o.transpose(1, 0, 3, 2, 4).reshape(b, s, h, dv)


def hgrn2_mixer(hn, w_in, lower_bound, norm_w, w_out):
    b, s, _ = hn.shape
    q, f, i, g = jnp.split(hn @ w_in, 4, axis=-1)
    q = jax.nn.silu(q.astype(jnp.float32))
    fg = lower_bound + (1.0 - lower_bound) * jax.nn.sigmoid(f.astype(jnp.float32))
    k = 1.0 - fg
    log_f = jnp.log(fg)
    o = gla_chunkwise(split_heads(q, HGRN_HEADS), split_heads(k, HGRN_HEADS),
                      split_heads(i, HGRN_HEADS), split_heads(log_f, HGRN_HEADS))
    o = o * lax.rsqrt(jnp.mean(o * o, -1, keepdims=True) + RMS_EPS)
    o = o.reshape(b, s, HGRN_WIDTH) * norm_w * jax.nn.silu(g.astype(jnp.float32))
    return o.astype(hn.dtype) @ w_out


def hier_moe(hn, w_group, b_group, w_expert, b_expert, w_gate, w_up, w_down):
    b, s, d = hn.shape
    n_tok = b * s
    xf = hn.reshape(n_tok, d)
    g_logits = (xf @ w_group + b_group).astype(jnp.float32)
    g_sel = jnp.argmax(g_logits, -1)
    g_gate = jnp.take_along_axis(jax.nn.softmax(g_logits, -1), g_sel[:, None], 1)
    e_logits = (xf @ w_expert + b_expert).astype(jnp.float32).reshape(n_tok, N_GROUPS, EXPERTS_PER_GROUP)
    e_logits = jnp.take_along_axis(e_logits, g_sel[:, None, None], 1)[:, 0]
    top_v, top_i = lax.top_k(jax.nn.softmax(e_logits, -1), TOP_K)
    gates = (g_gate * top_v / jnp.sum(top_v, -1, keepdims=True)).reshape(-1)
    e_flat = (g_sel[:, None] * EXPERTS_PER_GROUP + top_i).reshape(-1).astype(jnp.int32)
    n_assign = n_tok * TOP_K
    tok = jnp.arange(n_assign, dtype=jnp.int32) // TOP_K
    order = jnp.argsort(e_flat)
    e_sorted = e_flat[order]
    counts = jnp.zeros((N_EXPERTS,), jnp.int32).at[e_flat].add(1)
    padded = (counts + ROUTE_BLOCK - 1) // ROUTE_BLOCK * ROUTE_BLOCK
    start = jnp.cumsum(counts) - counts
    pend = jnp.cumsum(padded)
    pstart = pend - padded
    dest = pstart[e_sorted] + jnp.arange(n_assign, dtype=jnp.int32) - start[e_sorted]
    n_blocks = (n_assign + ROUTE_BLOCK - 1) // ROUTE_BLOCK + N_EXPERTS
    n_rows = n_blocks * ROUTE_BLOCK
    row_tok = jnp.full((n_rows,), n_tok, jnp.int32).at[dest].set(tok[order])
    row_gate = jnp.zeros((n_rows,), hn.dtype).at[dest].set(gates[order].astype(hn.dtype))
    block_e = jnp.clip(jnp.searchsorted(pend, jnp.arange(n_blocks, dtype=jnp.int32) * ROUTE_BLOCK,
                                        side='right'), 0, N_EXPERTS - 1)
    x_pad = jnp.concatenate([xf, jnp.zeros((1, d), xf.dtype)], 0)
    xs = x_pad[row_tok].reshape(n_blocks, ROUTE_BLOCK, d)

    def expert_block(args):
        xb, e = args
        hid = jax.nn.silu(xb @ w_gate[e]) * (xb @ w_up[e])
        return hid @ w_down[e]

    ys = lax.map(expert_block, (xs, block_e)).reshape(n_rows, d)
    out = jax.ops.segment_sum(ys * row_gate[:, None], row_tok, num_segments=n_tok + 1)[:n_tok]
    return out.reshape(b, s, d)


def setup_inputs(seed: int = 0) -> dict:
    key = jax.random.key(seed)
    keys = iter(jax.random.split(key, 48))

    def nrm(shape, scale):
        return jax.random.normal(next(keys), shape, jnp.float32) * scale

    def gain(shape, base=1.0):
        return base + nrm(shape, 0.01)

    lam_u = jax.random.uniform(next(keys), (N_EVEN, LRU_WIDTH), jnp.float32, 0.9, 0.999)
    lam_a = lam_u ** (1.0 / LRU_C)
    return {
        "x": nrm((BATCH, SEQ, D_MODEL), 1.0),
        "norm_mix": gain((DEPTH, D_MODEL)),
        "norm_ffn": gain((DEPTH, D_MODEL)),
        "norm_final": gain((D_MODEL,)),
        "ab_w_in": nrm((N_EVEN, D_MODEL, AB_COLS), D_MODEL ** -0.5),
        "rw_mu": jax.random.uniform(next(keys), (N_EVEN, RWKV_COLS), jnp.float32),
        "rw_w0": jax.random.uniform(next(keys), (N_EVEN, RWKV_WIDTH), jnp.float32, -6.0, -0.5),
        "rw_w2": nrm((N_EVEN, RWKV_DECAY_RANK, RWKV_WIDTH), 0.5 * RWKV_DECAY_RANK ** -0.5),
        "rw_a0": nrm((N_EVEN, RWKV_WIDTH), 0.1),
        "rw_a2": nrm((N_EVEN, RWKV_A_RANK, RWKV_WIDTH), RWKV_A_RANK ** -0.5),
        "rw_g2": nrm((N_EVEN, RWKV_GATE_RANK, RWKV_WIDTH), RWKV_GATE_RANK ** -0.5),
        "rw_k_k": gain((N_EVEN, RWKV_WIDTH), 0.85),
        "rw_k_a": gain((N_EVEN, RWKV_WIDTH)),
        "rw_r_k": nrm((N_EVEN, RWKV_HEADS, RWKV_HEAD_DIM), 0.1),
        "rw_ln_w": gain((N_EVEN, RWKV_WIDTH)),
        "rw_ln_b": nrm((N_EVEN, RWKV_WIDTH), 0.01),
        "lru_conv_w": nrm((N_EVEN, CONV_WIDTH, LRU_WIDTH), CONV_WIDTH ** -0.5),
        "lru_conv_b": nrm((N_EVEN, LRU_WIDTH), 0.01),
        "lru_w_a": nrm((N_EVEN, LRU_BLOCKS, LRU_BLOCK_DIM, LRU_BLOCK_DIM), LRU_BLOCK_DIM ** -0.5),
        "lru_b_a": nrm((N_EVEN, LRU_WIDTH), 0.01),
        "lru_w_x": nrm((N_EVEN, LRU_BLOCKS, LRU_BLOCK_DIM, LRU_BLOCK_DIM), LRU_BLOCK_DIM ** -0.5),
        "lru_b_x": nrm((N_EVEN, LRU_WIDTH), 0.01),
        "lru_lambda": jnp.log(lam_a) - jnp.log1p(-lam_a),
        "ab_w_out": nrm((N_EVEN, AB_MIX_WIDTH, D_MODEL), AB_MIX_WIDTH ** -0.5),
        "c_w_in": nrm((N_ODD, D_MODEL, HGRN_COLS), D_MODEL ** -0.5),
        "c_lower_bound": nrm((DEPTH, HGRN_WIDTH), 0.1),
        "c_norm_w": gain((N_ODD, HGRN_WIDTH)),
        "c_w_out": nrm((N_ODD, HGRN_WIDTH, D_MODEL), HGRN_WIDTH ** -0.5),
        "moe_w_group": nrm((DEPTH, D_MODEL, N_GROUPS), D_MODEL ** -0.5),
        "moe_b_group": nrm((DEPTH, N_GROUPS), 0.01),
        "moe_w_expert": nrm((DEPTH, D_MODEL, N_EXPERTS), D_MODEL ** -0.5),
        "moe_b_expert": nrm((DEPTH, N_EXPERTS), 0.01),
        "moe_w_gate": nrm((DEPTH, N_EXPERTS, D_MODEL, EXPERT_FF), D_MODEL ** -0.5),
        "moe_w_up": nrm((DEPTH, N_EXPERTS, D_MODEL, EXPERT_FF), D_MODEL ** -0.5),
        "moe_w_down": nrm((DEPTH, N_EXPERTS, EXPERT_FF, D_MODEL), EXPERT_FF ** -0.5),
    }


def reference(x, norm_mix, norm_ffn, norm_final,
              ab_w_in, rw_mu, rw_w0, rw_w2, rw_a0, rw_a2, rw_g2, rw_k_k, rw_k_a, rw_r_k,
              rw_ln_w, rw_ln_b,
              lru_conv_w, lru_conv_b, lru_w_a, lru_b_a, lru_w_x, lru_b_x, lru_lambda, ab_w_out,
              c_w_in, c_lower_bound, c_norm_w, c_w_out,
              moe_w_group, moe_b_group, moe_w_expert, moe_b_expert, moe_w_gate, moe_w_up,
              moe_w_down):
    lbs = jnp.cumsum(jax.nn.softmax(c_lower_bound.astype(jnp.float32), axis=0), axis=0)
    lbs = lbs - lbs[0]
    for layer in range(DEPTH):
        j = layer // 2
        hn = rmsnorm(x, norm_mix[layer])
        if layer % 2 == 0:
            p = hn @ ab_w_in[j]
            ya = rwkv7_mixer(p[..., :RWKV_COLS], rw_mu[j], rw_w0[j], rw_w2[j], rw_a0[j],
                             rw_a2[j], rw_g2[j], rw_k_k[j], rw_k_a[j], rw_r_k[j],
                             rw_ln_w[j], rw_ln_b[j])
            yb = rglru_mixer(p[..., RWKV_COLS:], lru_conv_w[j], lru_conv_b[j], lru_w_a[j],
                             lru_b_a[j], lru_w_x[j], lru_b_x[j], lru_lambda[j])
            x = x + jnp.concatenate([ya, yb], axis=-1) @ ab_w_out[j]
        else:
            x = x + hgrn2_mixer(hn, c_w_in[j], lbs[layer], c_norm_w[j], c_w_out[j])
        hn = rmsnorm(x, norm_ffn[layer])
        x = x + hier_moe(hn, moe_w_group[layer], moe_b_group[layer], moe_w_expert[layer],
                         moe_b_expert[layer], moe_w_gate[layer], moe_w_up[layer],
                         moe_w_down[layer])
    return rmsnorm(x, norm_final)
```

```python
import functools

import jax
import jax.numpy as jnp
from jax import lax
from jax.experimental import pallas as pl
from jax.experimental.pallas import tpu as pltpu

F32 = jnp.float32
BF16 = jnp.bfloat16
I32 = jnp.int32

RMS_EPS = 1e-6
RWKV_GN_EPS = 64e-5
LRU_C = 8.0
CHUNK = 64
HEAD64 = 64
LANES = 128
N_GROUPS = 4
EXPERTS_PER_GROUP = 8
N_EXPERTS = N_GROUPS * EXPERTS_PER_GROUP
ROUTE_ROWS = 256
VMEM_LIMIT = 56 * 1024 * 1024
HIGHEST = lax.Precision.HIGHEST


def _cparams(sem):
    return pltpu.CompilerParams(dimension_semantics=sem, vmem_limit_bytes=VMEM_LIMIT)


def _sigmoid(x):
    return 1.0 / (1.0 + jnp.exp(-x))


def _softplus(x):
    return jnp.maximum(x, 0.0) + jnp.log(1.0 + jnp.exp(-jnp.abs(x)))


def _silu(x):
    return x * _sigmoid(x)


def _gelu_tanh(x):
    return 0.5 * x * (1.0 + jnp.tanh(0.7978845608028654 * (x + 0.044715 * x * x * x)))


def _dot(a, b):
    return jnp.dot(a, b, preferred_element_type=F32)


def _dot_nt(a, b):
    return lax.dot_general(a, b, (((1,), (1,)), ((), ())), preferred_element_type=F32)


def _dot_tn(a, b):
    return lax.dot_general(a, b, (((0,), (0,)), ((), ())), preferred_element_type=F32)


def _split3(x):
    h1 = x.astype(BF16)
    r1 = x - h1.astype(F32)
    h2 = r1.astype(BF16)
    r2 = r1 - h2.astype(F32)
    return h1, h2, r2.astype(BF16)


def _dot_exact_rhs(x, m_bf16):
    h1, h2, h3 = _split3(x)
    return _dot(h1, m_bf16) + _dot(h2, m_bf16) + _dot(h3, m_bf16)


def _dot_exact_lhs(m_bf16, x):
    h1, h2, h3 = _split3(x)
    return _dot(m_bf16, h1) + _dot(m_bf16, h2) + _dot(m_bf16, h3)


def _iota(shape, dim):
    return lax.broadcasted_iota(I32, shape, dim)


def _norm_matmul_kernel(x_ref, g_ref, w_ref, *o_refs, splits):
    x = x_ref[...]
    ms = jnp.mean(x * x, axis=-1, keepdims=True)
    y = (x * lax.rsqrt(ms + RMS_EPS) * g_ref[...]).astype(BF16)
    off = 0
    for o_ref, n in zip(o_refs, splits):
        o_ref[...] = _dot(y, w_ref[:, off:off + n])
        off += n


def _norm_matmul(x2d, g, w, splits, tm=256):
    n, d = x2d.shape
    ncols = w.shape[1]
    assert sum(splits) == ncols and n % tm == 0
    return pl.pallas_call(
        functools.partial(_norm_matmul_kernel, splits=splits),
        out_shape=[jax.ShapeDtypeStruct((n, s), F32) for s in splits],
        grid=(n // tm,),
        in_specs=[pl.BlockSpec((tm, d), lambda i: (i, 0)),
                  pl.BlockSpec((1, d), lambda i: (0, 0)),
                  pl.BlockSpec((d, ncols), lambda i: (0, 0))],
        out_specs=[pl.BlockSpec((tm, s), lambda i: (i, 0)) for s in splits],
        compiler_params=_cparams(("parallel",)),
        name="norm_matmul",
    )(x2d, g.reshape(1, d), w.astype(BF16))


def _proj_residual_kernel(*refs, n_in):
    x_ref = refs[0]
    y_refs = refs[1:1 + n_in]
    w_refs = refs[1 + n_in:1 + 2 * n_in]
    o_ref = refs[1 + 2 * n_in]
    acc = x_ref[...]
    for y_ref, w_ref in zip(y_refs, w_refs):
        acc = acc + _dot(y_ref[...].astype(BF16), w_ref[...])
    o_ref[...] = acc


def _proj_residual(x2d, ys, ws, tm=512):
    n, d = x2d.shape
    n_in = len(ys)
    in_specs = [pl.BlockSpec((tm, d), lambda i: (i, 0))]
    in_specs += [pl.BlockSpec((tm, y.shape[1]), lambda i: (i, 0)) for y in ys]
    in_specs += [pl.BlockSpec(w.shape, lambda i: (0, 0)) for w in ws]
    return pl.pallas_call(
        functools.partial(_proj_residual_kernel, n_in=n_in),
        out_shape=jax.ShapeDtypeStruct((n, d), F32),
        grid=(n // tm,),
        in_specs=in_specs,
        out_specs=pl.BlockSpec((tm, d), lambda i: (i, 0)),
        compiler_params=_cparams(("parallel",)),
        name="proj_residual",
    )(x2d, *ys, *[w.astype(BF16) for w in ws])


def _rwkv_prep_kernel(p_ref, mu_ref, w0_ref, a0_ref, kk_s_ref, ka_ref, rk_ref,
                      wcomb_ref, g2_ref,
                      r_o, lw_o, k_o, v_o, kk_o, b_o, g_o, bonus_o, prev_ref, *, width):
    tm = p_ref.shape[1]

    @pl.when(pl.program_id(1) == 0)
    def _():
        prev_ref[...] = jnp.zeros_like(prev_ref)

    p = p_ref[0]
    rolled = pltpu.roll(p, 1, axis=0)
    prev = jnp.where(_iota(p.shape, 0) == 0, prev_ref[...], rolled)
    prev_ref[...] = p[tm - 1:tm, :]
    ps = p + (prev - p) * mu_ref[...]

    r = ps[:, 0:width]
    k = ps[:, width:2 * width]
    v = ps[:, 2 * width:3 * width]
    lowrank = ps[:, 3 * width:3 * width + LANES]
    gl = ps[:, 3 * width + LANES:3 * width + 2 * LANES]

    lane = _iota(lowrank.shape, 1)
    lr_in = jnp.where(lane < HEAD64, jnp.tanh(lowrank), lowrank)
    t12 = jnp.dot(lr_in, wcomb_ref[...], precision=HIGHEST, preferred_element_type=F32)
    wlog = -_softplus(-(w0_ref[...] + t12[:, :width])) - 0.5
    lw = -jnp.exp(wlog)
    a = _sigmoid(a0_ref[...] + t12[:, width:])
    g = jnp.dot(_sigmoid(gl), g2_ref[...], precision=HIGHEST, preferred_element_type=F32)

    ri = _iota((width, width), 0) >> 6
    ci = _iota((width, width), 1) >> 6
    seg = jnp.where(ri == ci, 1.0, 0.0).astype(BF16)

    kk = k * kk_s_ref[...]
    nrm = jnp.sqrt(_dot_exact_rhs(kk * kk, seg))
    kk = kk / jnp.maximum(nrm, 1e-12)
    k2 = k * (1.0 + (a - 1.0) * ka_ref[...])
    bonus = _dot_exact_rhs(r * k2 * rk_ref[...], seg) * v

    r_o[0] = r
    lw_o[0] = lw
    k_o[0] = k2
    v_o[0] = v
    kk_o[0] = kk
    b_o[0] = kk * a
    g_o[0] = g
    bonus_o[0] = bonus


def _rwkv_prep(p_a, mu, w0, w2, a0, a2, g2, k_k, k_a, r_k, tm=256):
    b, t, cols = p_a.shape
    width = w0.shape[0]
    rank = w2.shape[0]
    assert rank == HEAD64 and a2.shape[0] == HEAD64 and g2.shape[0] == LANES
    zeros = jnp.zeros((rank, width), F32)
    wcomb = jnp.concatenate([jnp.concatenate([w2, zeros], 1),
                             jnp.concatenate([zeros, a2], 1)], 0)
    row = lambda a_: a_.reshape(1, -1)
    const = lambda shp: pl.BlockSpec(shp, lambda i, j: (0, 0))
    tok = pl.BlockSpec((1, tm, width), lambda i, j: (i, j, 0))
    return pl.pallas_call(
        functools.partial(_rwkv_prep_kernel, width=width),
        out_shape=[jax.ShapeDtypeStruct((b, t, width), F32)] * 8,
        grid=(b, t // tm),
        in_specs=[pl.BlockSpec((1, tm, cols), lambda i, j: (i, j, 0)),
                  const((1, cols)), const((1, width)), const((1, width)), const((1, width)),
                  const((1, width)), const((1, width)),
                  const((LANES, 2 * width)), const((LANES, width))],
        out_specs=[tok] * 8,
        scratch_shapes=[pltpu.VMEM((1, cols), F32)],
        compiler_params=_cparams(("parallel", "arbitrary")),
        name="rwkv_prep",
    )(p_a, row(mu), row(w0), row(a0), row(k_k), row(k_a), row(r_k), wcomb, g2)


def _bd(x, mask_bd):
    return jnp.concatenate([x, x], axis=0) * mask_bd


def _rwkv_scan_kernel(r_ref, lw_ref, k_ref, v_ref, kk_ref, b_ref, g_ref, bonus_ref,
                      lnw_ref, lnb_ref, o_ref, st_ref, *, n_pairs):
    c = CHUNK

    @pl.when(pl.program_id(1) == 0)
    def _():
        st_ref[...] = jnp.zeros_like(st_ref)

    rr = _iota((LANES, LANES), 0)
    cc = _iota((LANES, LANES), 1)
    mask_bd = jnp.where((rr >> 6) == (cc >> 6), 1.0, 0.0)
    mask_bd16 = mask_bd.astype(BF16)
    t_i = _iota((c, LANES), 0)
    s_i = _iota((c, LANES), 1) & (c - 1)
    strict = jnp.where(s_i < t_i, 1.0, 0.0)
    incl = jnp.where(s_i <= t_i, 1.0, 0.0)
    eye_ss = jnp.where(s_i == t_i, 1.0, 0.0)
    tril = jnp.where(_iota((c, c), 1) <= _iota((c, c), 0), 1.0, 0.0).astype(BF16)

    def bd16(x):
        return jnp.concatenate([x, x], axis=0).astype(BF16) * mask_bd16

    for p in range(n_pairs):
        sl = slice(p * LANES, (p + 1) * LANES)
        r = r_ref[0, :, sl]
        lw = lw_ref[0, :, sl]
        k = k_ref[0, :, sl]
        v = v_ref[0, :, sl]
        kk = kk_ref[0, :, sl]
        b = b_ref[0, :, sl]

        cum = _dot_exact_lhs(tril, lw)
        cum_c = cum[c - 1:c, :]
        e_neg = jnp.exp(-cum)
        e_end = jnp.exp(cum_c - cum)
        rt = r * jnp.exp(cum)
        kkt = kk * jnp.exp(cum - lw)
        kh = k * e_neg
        bh = b * e_neg
        kb = (k * e_end).astype(BF16)
        bb = (b * e_end).astype(BF16)

        lhs = jnp.concatenate([kkt, rt], axis=0).astype(BF16)
        g_k = _dot_nt(lhs, bd16(kh))
        g_b = _dot_nt(lhs, bd16(bh))
        l_kv = g_k[:c] * strict
        pm = g_k[c:] * incl
        l_kb = g_b[:c] * strict
        qm = g_b[c:] * incl

        xp = -l_kb
        tinv = eye_ss + xp
        n_sq = c.bit_length() - 2
        for _ in range(n_sq):
            xp = _dot(xp.astype(BF16), bd16(xp))
            tinv = tinv + _dot(tinv.astype(BF16), bd16(xp))
        tinv16 = tinv.astype(BF16)

        v_bd = bd16(v)
        w = _dot(l_kv.astype(BF16), v_bd)
        tk = _dot(tinv16, bd16(kkt))
        tw = _dot(tinv16, bd16(w))

        st = st_ref[p]
        st16 = st.astype(BF16)
        u = _dot_nt(tk.astype(BF16), st16) + tw
        u16 = u.astype(BF16)
        y = (_dot_nt(rt.astype(BF16), st16) + _dot(pm.astype(BF16), v_bd)
             - _dot(qm.astype(BF16), bd16(u)))
        upd = _dot_tn(v.astype(BF16), kb) - _dot_tn(u16, bb)
        st_ref[p] = st * jnp.exp(cum_c) + upd * mask_bd

        mean = _dot_exact_rhs(y, mask_bd16) * (1.0 / HEAD64)
        d = y - mean
        var = _dot_exact_rhs(d * d, mask_bd16) * (1.0 / HEAD64)
        yn = d * lax.rsqrt(var + RWKV_GN_EPS) * lnw_ref[:, sl] + lnb_ref[:, sl]
        o_ref[0, :, sl] = (yn + bonus_ref[0, :, sl]) * g_ref[0, :, sl]


def _rwkv_scan(r, lw, k, v, kk, b, g, bonus, ln_w, ln_b):
    bsz, t, width = r.shape
    n_pairs = width // LANES
    tok = pl.BlockSpec((1, CHUNK, width), lambda i, j: (i, j, 0))
    const = pl.BlockSpec((1, width), lambda i, j: (0, 0))
    return pl.pallas_call(
        functools.partial(_rwkv_scan_kernel, n_pairs=n_pairs),
        out_shape=jax.ShapeDtypeStruct((bsz, t, width), F32),
        grid=(bsz, t // CHUNK),
        in_specs=[tok] * 8 + [const, const],
        out_specs=tok,
        scratch_shapes=[pltpu.VMEM((n_pairs, LANES, LANES), F32)],
        compiler_params=_cparams(("parallel", "arbitrary")),
        name="rwkv_scan",
    )(r, lw, k, v, kk, b, g, bonus, ln_w.reshape(1, -1), ln_b.reshape(1, -1))


def _lru_kernel(p_ref, cw_ref, cb_ref, wg_ref, ba_ref, bx_ref, lam_ref, o_ref,
                xcarry_ref, hcarry_ref, a_s, u_s, *, width):
    tm = p_ref.shape[1]

    @pl.when(pl.program_id(1) == 0)
    def _():
        xcarry_ref[...] = jnp.zeros_like(xcarry_ref)
        hcarry_ref[...] = jnp.zeros_like(hcarry_ref)

    gate = p_ref[0, :, 0:width]
    xb = p_ref[0, :, width:2 * width]
    carry8 = xcarry_ref[...]
    row8 = _iota((8, width), 0)

    def shifted(s):
        rolled = pltpu.roll(xb, s, axis=0)
        first = jnp.where(row8 < s, pltpu.roll(carry8, s, axis=0), rolled[0:8])
        return jnp.concatenate([first, rolled[8:]], axis=0)

    xc = (cw_ref[0:1, :] * shifted(3) + cw_ref[1:2, :] * shifted(2)
          + cw_ref[2:3, :] * shifted(1) + cw_ref[3:4, :] * xb + cb_ref[...])
    xcarry_ref[...] = xb[tm - 8:tm, :]

    gates = _dot(xc.astype(BF16), wg_ref[...])
    rg = _sigmoid(gates[:, :width] + ba_ref[...])
    ig = _sigmoid(gates[:, width:] + bx_ref[...])
    log_a = -LRU_C * rg * _softplus(-lam_ref[...])
    a = jnp.exp(log_a)
    a_s[...] = a
    u_s[...] = jnp.sqrt(1.0 - a * a) * ig * xc

    m1 = row8 >= 1
    m2 = row8 >= 2
    m4 = row8 >= 4

    def body(i, h):
        off = pl.multiple_of(i * 8, 8)
        a8 = a_s[pl.ds(off, 8), :]
        u8 = u_s[pl.ds(off, 8), :]
        for s, m in ((1, m1), (2, m2), (4, m4)):
            u_sh = jnp.where(m, pltpu.roll(u8, s, axis=0), 0.0)
            a_sh = jnp.where(m, pltpu.roll(a8, s, axis=0), 1.0)
            u8 = u8 + a8 * u_sh
            a8 = a8 * a_sh
        h8 = u8 + a8 * h
        u_s[pl.ds(off, 8), :] = h8
        return jnp.broadcast_to(h8[7:8, :], (8, width))

    h_last = lax.fori_loop(0, tm // 8, body, hcarry_ref[...])
    hcarry_ref[...] = h_last
    o_ref[0] = u_s[...] * _gelu_tanh(gate)


def _block_diag(w):
    nb, di, do = w.shape
    eye = jnp.eye(nb, dtype=w.dtype)
    return (eye[:, None, :, None] * w[:, :, None, :]).reshape(nb * di, nb * do)


def _lru(p_b, conv_w, conv_b, w_a, b_a, w_x, b_x, lam, tm=512):
    b, t, cols = p_b.shape
    width = cols // 2
    wg = jnp.concatenate([_block_diag(w_a), _block_diag(w_x)], axis=1).astype(BF16)
    row = lambda a_: a_.reshape(1, -1)
    const = lambda shp: pl.BlockSpec(shp, lambda i, j: (0, 0))
    return pl.pallas_call(
        functools.partial(_lru_kernel, width=width),
        out_shape=jax.ShapeDtypeStruct((b, t, width), F32),
        grid=(b, t // tm),
        in_specs=[pl.BlockSpec((1, tm, cols), lambda i, j: (i, j, 0)),
                  const(conv_w.shape), const((1, width)), const((width, 2 * width)),
                  const((1, width)), const((1, width)), const((1, width))],
        out_specs=pl.BlockSpec((1, tm, width), lambda i, j: (i, j, 0)),
        scratch_shapes=[pltpu.VMEM((8, width), F32), pltpu.VMEM((8, width), F32),
                        pltpu.VMEM((tm, width), F32), pltpu.VMEM((tm, width), F32)],
        compiler_params=_cparams(("parallel", "arbitrary")),
        name="rglru",
    )(p_b, conv_w, row(conv_b), wg, row(b_a), row(b_x), row(lam))


def _gla_kernel(q_ref, f_ref, i_ref, g_ref, lb_ref, nw_ref, o_ref, st_ref, *, n_heads):
    c = CHUNK

    @pl.when(pl.program_id(1) == 0)
    def _():
        st_ref[...] = jnp.zeros_like(st_ref)

    trilf = jnp.where(_iota((c, c), 1) <= _iota((c, c), 0), 1.0, 0.0)
    tril16 = trilf.astype(BF16)
    mid = c // 2 - 1

    for h in range(n_heads):
        sl = slice(h * LANES, (h + 1) * LANES)
        lb = lb_ref[:, sl]
        q = _silu(q_ref[0, :, sl])
        fg = lb + (1.0 - lb) * _sigmoid(f_ref[0, :, sl])
        k = 1.0 - fg
        v = i_ref[0, :, sl]
        cum = _dot_exact_lhs(tril16, jnp.log(fg))
        cum_c = cum[c - 1:c, :]
        cum_m = cum[mid:mid + 1, :]
        qm = (q * jnp.exp(cum - cum_m)).astype(BF16)
        km = (k * jnp.exp(cum_m - cum)).astype(BF16)
        qg = (q * jnp.exp(cum)).astype(BF16)
        kb = (k * jnp.exp(cum_c - cum)).astype(BF16)
        v16 = v.astype(BF16)

        scores = _dot_nt(qm, km) * trilf
        st = st_ref[h]
        o = _dot(scores.astype(BF16), v16) + _dot_nt(qg, st.astype(BF16))
        st_ref[h] = st * jnp.exp(cum_c) + _dot_tn(v16, kb)

        o = o * lax.rsqrt(jnp.mean(o * o, axis=-1, keepdims=True) + RMS_EPS)
        o_ref[0, :, sl] = o * nw_ref[:, sl] * _silu(g_ref[0, :, sl])


def _gla(q, f, i, g, lower_bound, norm_w):
    bsz, t, width = q.shape
    n_heads = width // LANES
    tok = pl.BlockSpec((1, CHUNK, width), lambda b_, j: (b_, j, 0))
    const = pl.BlockSpec((1, width), lambda b_, j: (0, 0))
    return pl.pallas_call(
        functools.partial(_gla_kernel, n_heads=n_heads),
        out_shape=jax.ShapeDtypeStruct((bsz, t, width), F32),
        grid=(bsz, t // CHUNK),
        in_specs=[tok] * 4 + [const, const],
        out_specs=tok,
        scratch_shapes=[pltpu.VMEM((n_heads, LANES, LANES), F32)],
        compiler_params=_cparams(("parallel", "arbitrary")),
        name="hgrn2_gla",
    )(q, f, i, g, lower_bound.reshape(1, -1), norm_w.reshape(1, -1))


def _router_kernel(x_ref, g_ref, w_ref, b_ref, hn_o, e_o, gate_o):
    x = x_ref[...]
    ms = jnp.mean(x * x, axis=-1, keepdims=True)
    hn = x * lax.rsqrt(ms + RMS_EPS) * g_ref[...]
    hn_o[...] = hn
    lt = lax.dot_general(w_ref[...], hn, (((1,), (1,)), ((), ())),
                         precision=HIGHEST, preferred_element_type=F32) + b_ref[:, 0:1]
    tm = x.shape[0]
    gl = lt[0:8, :]
    row8 = _iota((8, tm), 0)
    gl = jnp.where(row8 < N_GROUPS, gl, -jnp.inf)
    gmax = jnp.max(gl, axis=0, keepdims=True)
    g_sel = jnp.min(jnp.where(gl == gmax, row8, 8), axis=0, keepdims=True)
    g_gate = 1.0 / jnp.sum(jnp.exp(gl - gmax), axis=0, keepdims=True)

    el = jnp.zeros((EXPERTS_PER_GROUP, tm), F32)
    for gi in range(N_GROUPS):
        lo = 8 + gi * EXPERTS_PER_GROUP
        el = jnp.where(g_sel == gi, lt[lo:lo + EXPERTS_PER_GROUP, :], el)
    m1 = jnp.max(el, axis=0, keepdims=True)
    i1 = jnp.min(jnp.where(el == m1, row8, 8), axis=0, keepdims=True)
    el2 = jnp.where(row8 == i1, -jnp.inf, el)
    m2 = jnp.max(el2, axis=0, keepdims=True)
    i2 = jnp.min(jnp.where(el2 == m2, row8, 8), axis=0, keepdims=True)
    e2 = jnp.exp(m2 - m1)
    inv = 1.0 / (1.0 + e2)
    e_o[...] = jnp.concatenate([g_sel * EXPERTS_PER_GROUP + i1,
                                g_sel * EXPERTS_PER_GROUP + i2], axis=0)
    gate_o[...] = jnp.concatenate([g_gate * inv, g_gate * e2 * inv], axis=0)


def _router(x2d, g, w_group, b_group, w_expert, b_expert, tm=512):
    n, d = x2d.shape
    wt = jnp.zeros((LANES, d), F32)
    wt = wt.at[0:N_GROUPS].set(w_group.T).at[8:8 + N_EXPERTS].set(w_expert.T)
    bt = jnp.zeros((LANES,), F32)
    bt = bt.at[0:N_GROUPS].set(b_group).at[8:8 + N_EXPERTS].set(b_expert)
    bt = jnp.broadcast_to(bt[:, None], (LANES, LANES))
    return pl.pallas_call(
        _router_kernel,
        out_shape=[jax.ShapeDtypeStruct((n, d), F32),
                   jax.ShapeDtypeStruct((2, n), I32),
                   jax.ShapeDtypeStruct((2, n), F32)],
        grid=(n // tm,),
        in_specs=[pl.BlockSpec((tm, d), lambda i: (i, 0)),
                  pl.BlockSpec((1, d), lambda i: (0, 0)),
                  pl.BlockSpec((LANES, d), lambda i: (0, 0)),
                  pl.BlockSpec((LANES, LANES), lambda i: (0, 0))],
        out_specs=[pl.BlockSpec((tm, d), lambda i: (i, 0)),
                   pl.BlockSpec((2, tm), lambda i: (0, i)),
                   pl.BlockSpec((2, tm), lambda i: (0, i))],
        compiler_params=_cparams(("parallel",)),
        name="moe_router",
    )(x2d, g.reshape(1, d), wt, bt)


def _row_gather_start(src_hbm, idx_ref, base, buf, sem, n_rows):
    def body(i, carry):
        tok = idx_ref[base + i]
        pltpu.make_async_copy(src_hbm.at[pl.ds(tok, 1), :], buf.at[pl.ds(i, 1), :], sem).start()
        return carry
    lax.fori_loop(0, n_rows, body, 0, unroll=8)


def _row_gather_wait(src_hbm, buf, sem, n_rows):
    pltpu.make_async_copy(src_hbm.at[pl.ds(0, n_rows), :], buf, sem).wait()


def _expert_kernel(be_ref, first_ref, nused_ref, tok_ref,
                   hn_hbm, gate_ref, wg_ref, wu_ref, wd_ref, o_ref,
                   xbuf, sems, wg16, wu16, wd16):
    j = pl.program_id(0)
    nb = pl.num_programs(0)
    rows = xbuf.shape[1]
    slot = lax.rem(j, 2)

    @pl.when(j == 0)
    def _():
        _row_gather_start(hn_hbm, tok_ref, 0, xbuf.at[0], sems.at[0], rows)

    @pl.when(j + 1 < nb)
    def _():
        nxt = 1 - slot
        _row_gather_start(hn_hbm, tok_ref, (j + 1) * rows, xbuf.at[nxt], sems.at[nxt], rows)

    @pl.when(first_ref[j] == 1)
    def _():
        wg16[...] = wg_ref[0].astype(BF16)
        wu16[...] = wu_ref[0].astype(BF16)
        wd16[...] = wd_ref[0].astype(BF16)

    _row_gather_wait(hn_hbm, xbuf.at[slot], sems.at[slot], rows)

    @pl.when(j < nused_ref[0])
    def _():
        x = xbuf[slot].astype(BF16)
        hg = _dot(x, wg16[...])
        hu = _dot(x, wu16[...])
        hid = (_silu(hg) * hu).astype(BF16)
        o_ref[...] = _dot(hid, wd16[...]) * gate_ref[...]

    @pl.when(j >= nused_ref[0])
    def _():
        o_ref[...] = jnp.zeros_like(o_ref)


def _expert_ffn(hn, row_tok, row_gate, block_e, first_flag, n_used, w_gate, w_up, w_down,
                rows=ROUTE_ROWS):
    n, d = hn.shape
    n_rows = row_tok.shape[0]
    n_blocks = n_rows // rows
    ff = w_gate.shape[-1]
    return pl.pallas_call(
        _expert_kernel,
        out_shape=jax.ShapeDtypeStruct((n_rows, d), F32),
        grid_spec=pltpu.PrefetchScalarGridSpec(
            num_scalar_prefetch=4,
            grid=(n_blocks,),
            in_specs=[pl.BlockSpec(memory_space=pl.ANY),
                      pl.BlockSpec((rows, 1), lambda j, be, fi, nu, tk: (j, 0)),
                      pl.BlockSpec((1, d, ff), lambda j, be, fi, nu, tk: (be[j], 0, 0)),
                      pl.BlockSpec((1, d, ff), lambda j, be, fi, nu, tk: (be[j], 0, 0)),
                      pl.BlockSpec((1, ff, d), lambda j, be, fi, nu, tk: (be[j], 0, 0))],
            out_specs=pl.BlockSpec((rows, d), lambda j, be, fi, nu, tk: (j, 0)),
            scratch_shapes=[pltpu.VMEM((2, rows, d), F32),
                            pltpu.SemaphoreType.DMA((2,)),
                            pltpu.VMEM((d, ff), BF16), pltpu.VMEM((d, ff), BF16),
                            pltpu.VMEM((ff, d), BF16)]),
        compiler_params=_cparams(("arbitrary",)),
        name="moe_expert_ffn",
    )(block_e, first_flag, n_used, row_tok, hn, row_gate.reshape(n_rows, 1),
      w_gate, w_up, w_down)


def _combine_kernel(pos_ref, x_ref, ys_hbm, gfin_ref, o_ref, buf, sems, *, final_norm):
    j = pl.program_id(0)
    nb = pl.num_programs(0)
    tm = x_ref.shape[0]
    n_tok = nb * tm
    slot = lax.rem(j, 2)

    def start(step, s):
        _row_gather_start(ys_hbm, pos_ref, step * tm, buf.at[s, 0], sems.at[s, 0], tm)
        _row_gather_start(ys_hbm, pos_ref, n_tok + step * tm, buf.at[s, 1], sems.at[s, 1], tm)

    @pl.when(j == 0)
    def _():
        start(0, 0)

    @pl.when(j + 1 < nb)
    def _():
        start(j + 1, 1 - slot)

    _row_gather_wait(ys_hbm, buf.at[slot, 0], sems.at[slot, 0], tm)
    _row_gather_wait(ys_hbm, buf.at[slot, 1], sems.at[slot, 1], tm)
    y = x_ref[...] + (buf[slot, 0] + buf[slot, 1])
    if final_norm:
        ms = jnp.mean(y * y, axis=-1, keepdims=True)
        y = y * lax.rsqrt(ms + RMS_EPS) * gfin_ref[...]
    o_ref[...] = y


def _combine(x2d, ys, pos, g_final, final_norm, tm=256):
    n, d = x2d.shape
    return pl.pallas_call(
        functools.partial(_combine_kernel, final_norm=final_norm),
        out_shape=jax.ShapeDtypeStruct((n, d), F32),
        grid_spec=pltpu.PrefetchScalarGridSpec(
            num_scalar_prefetch=1,
            grid=(n // tm,),
            in_specs=[pl.BlockSpec((tm, d), lambda j, ps: (j, 0)),
                      pl.BlockSpec(memory_space=pl.ANY),
                      pl.BlockSpec((1, d), lambda j, ps: (0, 0))],
            out_specs=pl.BlockSpec((tm, d), lambda j, ps: (j, 0)),
            scratch_shapes=[pltpu.VMEM((2, 2, tm, d), F32),
                            pltpu.SemaphoreType.DMA((2, 2))]),
        compiler_params=_cparams(("arbitrary",)),
        name="moe_combine",
    )(pos.reshape(-1), x2d, ys, g_final.reshape(1, d))


def _route_plan(e2, gates2, rows):
    n_tok = e2.shape[1]
    n_assign = 2 * n_tok
    e_flat = e2.T.reshape(-1)
    onehot = (e_flat[:, None] == jnp.arange(N_EXPERTS, dtype=I32)[None, :]).astype(I32)
    csum = jnp.cumsum(onehot, axis=0)
    rank = jnp.sum((csum - onehot) * onehot, axis=1)
    counts = csum[-1]
    padded = (counts + rows - 1) // rows * rows
    pend = jnp.cumsum(padded)
    pstart = pend - padded
    dest = pstart[e_flat] + rank
    n_blocks = n_assign // rows + N_EXPERTS
    n_rows = n_blocks * rows
    tok = jnp.arange(n_assign, dtype=I32) // 2
    row_tok = jnp.zeros((n_rows,), I32).at[dest].set(tok)
    row_gate = jnp.zeros((n_rows,), F32).at[dest].set(gates2.T.reshape(-1))
    starts = jnp.arange(n_blocks, dtype=I32) * rows
    block_e = jnp.clip(jnp.searchsorted(pend, starts, side='right'), 0, N_EXPERTS - 1).astype(I32)
    first = jnp.concatenate([jnp.ones((1,), I32),
                             (block_e[1:] != block_e[:-1]).astype(I32)])
    n_used = (pend[-1] // rows).astype(I32).reshape(1)
    pos = dest.reshape(n_tok, 2).T
    return row_tok, row_gate, block_e, first, n_used, pos


def _moe(x2d, g_ffn, w_group, b_group, w_expert, b_expert, w_gate, w_up, w_down,
         g_final, final_norm):
    hn, e2, gates2 = _router(x2d, g_ffn, w_group, b_group, w_expert, b_expert)
    row_tok, row_gate, block_e, first, n_used, pos = _route_plan(e2, gates2, ROUTE_ROWS)
    ys = _expert_ffn(hn, row_tok, row_gate, block_e, first, n_used, w_gate, w_up, w_down)
    return _combine(x2d, ys, pos, g_final, final_norm)


def kernel(x, norm_mix, norm_ffn, norm_final, ab_w_in, rw_mu, rw_w0, rw_w2, rw_a0, rw_a2, rw_g2, rw_k_k, rw_k_a, rw_r_k, rw_ln_w, rw_ln_b, lru_conv_w, lru_conv_b, lru_w_a, lru_b_a, lru_w_x, lru_b_x, lru_lambda, ab_w_out, c_w_in, c_lower_bound, c_norm_w, c_w_out, moe_w_group, moe_b_group, moe_w_expert, moe_b_expert, moe_w_gate, moe_w_up, moe_w_down):
    bsz, t, d = x.shape
    depth = norm_mix.shape[0]
    n = bsz * t
    lbs = jnp.cumsum(jax.nn.softmax(c_lower_bound.astype(F32), axis=0), axis=0)
    lbs = lbs - lbs[0]
    x2d = x.reshape(n, d)
    for layer in range(depth):
        j = layer // 2
        if layer % 2 == 0:
            rw_cols = rw_mu.shape[1]
            lru_cols = ab_w_in.shape[2] - rw_cols
            width = rw_w0.shape[1]
            p_a, p_b = _norm_matmul(x2d, norm_mix[layer], ab_w_in[j], (rw_cols, lru_cols))
            prep = _rwkv_prep(p_a.reshape(bsz, t, rw_cols), rw_mu[j], rw_w0[j], rw_w2[j],
                              rw_a0[j], rw_a2[j], rw_g2[j], rw_k_k[j], rw_k_a[j],
                              rw_r_k[j].reshape(-1))
            ya = _rwkv_scan(*prep, rw_ln_w[j], rw_ln_b[j])
            yb = _lru(p_b.reshape(bsz, t, lru_cols), lru_conv_w[j], lru_conv_b[j], lru_w_a[j],
                      lru_b_a[j], lru_w_x[j], lru_b_x[j], lru_lambda[j])
            x2d = _proj_residual(x2d, [ya.reshape(n, width), yb.reshape(n, -1)],
                                 [ab_w_out[j][:width], ab_w_out[j][width:]])
        else:
            hw = c_norm_w.shape[1]
            q, f, i_, g = _norm_matmul(x2d, norm_mix[layer], c_w_in[j], (hw,) * 4)
            shp = (bsz, t, hw)
            o = _gla(q.reshape(shp), f.reshape(shp), i_.reshape(shp), g.reshape(shp),
                     lbs[layer], c_norm_w[j])
            x2d = _proj_residual(x2d, [o.reshape(n, hw)], [c_w_out[j]])
        last = layer == depth - 1
        x2d = _moe(x2d, norm_ffn[layer], moe_w_group[layer], moe_b_group[layer],
                   moe_w_expert[layer], moe_b_expert[layer], moe_w_gate[layer],
                   moe_w_up[layer], moe_w_down[layer], norm_final, last)
    return x2d.reshape(bsz, t, d)
```

```python
import functools

import jax
import jax.numpy as jnp
from jax import lax
from jax.experimental import pallas as pl
from jax.experimental.pallas import tpu as pltpu

F32 = jnp.float32
BF16 = jnp.bfloat16
I32 = jnp.int32

RMS_EPS = 1e-6
RWKV_GN_EPS = 64e-5
LRU_C = 8.0
CHUNK = 64
CHUNK_SHIFT = CHUNK.bit_length() - 1
HEAD64 = 64
LANES = 128
N_GROUPS = 4
EXPERTS_PER_GROUP = 8
N_EXPERTS = N_GROUPS * EXPERTS_PER_GROUP
ROUTE_ROWS = 256
VMEM_LIMIT = 56 * 1024 * 1024
HIGHEST = lax.Precision.HIGHEST


def _cparams(sem):
    return pltpu.CompilerParams(dimension_semantics=sem, vmem_limit_bytes=VMEM_LIMIT)


def _sigmoid(x):
    return 1.0 / (1.0 + jnp.exp(-x))


def _softplus(x):
    return jnp.maximum(x, 0.0) + jnp.log(1.0 + jnp.exp(-jnp.abs(x)))


def _silu(x):
    return x * _sigmoid(x)


def _gelu_tanh(x):
    return 0.5 * x * (1.0 + jnp.tanh(0.7978845608028654 * (x + 0.044715 * x * x * x)))


def _dot(a, b):
    return jnp.dot(a, b, preferred_element_type=F32)


def _dot_nt(a, b):
    return lax.dot_general(a, b, (((1,), (1,)), ((), ())), preferred_element_type=F32)


def _dot_tn(a, b):
    return lax.dot_general(a, b, (((0,), (0,)), ((), ())), preferred_element_type=F32)


def _split3(x):
    h1 = x.astype(BF16)
    r1 = x - h1.astype(F32)
    h2 = r1.astype(BF16)
    r2 = r1 - h2.astype(F32)
    return h1, h2, r2.astype(BF16)


def _dot_exact_rhs(x, m_bf16):
    h1, h2, h3 = _split3(x)
    return _dot(h1, m_bf16) + _dot(h2, m_bf16) + _dot(h3, m_bf16)


def _dot_exact_lhs(m_bf16, x):
    h1, h2, h3 = _split3(x)
    return _dot(m_bf16, h1) + _dot(m_bf16, h2) + _dot(m_bf16, h3)


def _iota(shape, dim):
    return lax.broadcasted_iota(I32, shape, dim)


def _norm_matmul_kernel(x_ref, g_ref, w_ref, *o_refs, splits):
    x = x_ref[...]
    ms = jnp.mean(x * x, axis=-1, keepdims=True)
    y = (x * lax.rsqrt(ms + RMS_EPS) * g_ref[...]).astype(BF16)
    off = 0
    for o_ref, n in zip(o_refs, splits):
        o_ref[...] = _dot(y, w_ref[:, off:off + n])
        off += n


def _norm_matmul(x2d, g, w, splits, tm=256):
    n, d = x2d.shape
    ncols = w.shape[1]
    assert sum(splits) == ncols and n % tm == 0
    return pl.pallas_call(
        functools.partial(_norm_matmul_kernel, splits=splits),
        out_shape=[jax.ShapeDtypeStruct((n, s), F32) for s in splits],
        grid=(n // tm,),
        in_specs=[pl.BlockSpec((tm, d), lambda i: (i, 0)),
                  pl.BlockSpec((1, d), lambda i: (0, 0)),
                  pl.BlockSpec((d, ncols), lambda i: (0, 0))],
        out_specs=[pl.BlockSpec((tm, s), lambda i: (i, 0)) for s in splits],
        compiler_params=_cparams(("parallel",)),
        name="norm_matmul",
    )(x2d, g.reshape(1, d), w.astype(BF16))


def _proj_residual_kernel(*refs, n_in):
    x_ref = refs[0]
    y_refs = refs[1:1 + n_in]
    w_refs = refs[1 + n_in:1 + 2 * n_in]
    o_ref = refs[1 + 2 * n_in]
    acc = x_ref[...]
    for y_ref, w_ref in zip(y_refs, w_refs):
        acc = acc + _dot(y_ref[...].astype(BF16), w_ref[...])
    o_ref[...] = acc


def _proj_residual(x2d, ys, ws, tm=512):
    n, d = x2d.shape
    n_in = len(ys)
    in_specs = [pl.BlockSpec((tm, d), lambda i: (i, 0))]
    in_specs += [pl.BlockSpec((tm, y.shape[1]), lambda i: (i, 0)) for y in ys]
    in_specs += [pl.BlockSpec(w.shape, lambda i: (0, 0)) for w in ws]
    return pl.pallas_call(
        functools.partial(_proj_residual_kernel, n_in=n_in),
        out_shape=jax.ShapeDtypeStruct((n, d), F32),
        grid=(n // tm,),
        in_specs=in_specs,
        out_specs=pl.BlockSpec((tm, d), lambda i: (i, 0)),
        compiler_params=_cparams(("parallel",)),
        name="proj_residual",
    )(x2d, *ys, *[w.astype(BF16) for w in ws])


def _rwkv_prep_kernel(p_ref, mu_ref, w0_ref, a0_ref, kk_s_ref, ka_ref, rk_ref,
                      wcomb_ref, g2_ref,
                      rt_o, kkt_o, kh_o, bh_o, kb_o, bb_o, v_o, g_o, bonus_o, gc_o,
                      prev_ref, *, width):
    tm = p_ref.shape[1]

    @pl.when(pl.program_id(1) == 0)
    def _():
        prev_ref[...] = jnp.zeros_like(prev_ref)

    p = p_ref[0]
    rolled = pltpu.roll(p, 1, axis=0)
    prev = jnp.where(_iota(p.shape, 0) == 0, prev_ref[...], rolled)
    prev_ref[...] = p[tm - 1:tm, :]
    ps = p + (prev - p) * mu_ref[...]

    r = ps[:, 0:width]
    k = ps[:, width:2 * width]
    v = ps[:, 2 * width:3 * width]
    lowrank = ps[:, 3 * width:3 * width + LANES]
    gl = ps[:, 3 * width + LANES:3 * width + 2 * LANES]

    lane = _iota(lowrank.shape, 1)
    lr_in = jnp.where(lane < HEAD64, jnp.tanh(lowrank), lowrank)
    t12 = jnp.dot(lr_in, wcomb_ref[...], precision=HIGHEST, preferred_element_type=F32)
    wlog = -_softplus(-(w0_ref[...] + t12[:, :width])) - 0.5
    lw = -jnp.exp(wlog)
    a = _sigmoid(a0_ref[...] + t12[:, width:])
    g = jnp.dot(_sigmoid(gl), g2_ref[...], precision=HIGHEST, preferred_element_type=F32)

    ri = _iota((width, width), 0) >> 6
    ci = _iota((width, width), 1) >> 6
    seg = jnp.where(ri == ci, 1.0, 0.0).astype(BF16)

    kk = k * kk_s_ref[...]
    nrm = jnp.sqrt(_dot_exact_rhs(kk * kk, seg))
    kk = kk / jnp.maximum(nrm, 1e-12)
    k2 = k * (1.0 + (a - 1.0) * ka_ref[...])
    bonus = _dot_exact_rhs(r * k2 * rk_ref[...], seg) * v

    b = kk * a

    n_chunk = tm // CHUNK
    rr = _iota((tm, tm), 0)
    cc = _iota((tm, tm), 1)
    tril_bd = jnp.where(((rr >> CHUNK_SHIFT) == (cc >> CHUNK_SHIFT)) & (cc <= rr),
                        1.0, 0.0).astype(BF16)
    cum = _dot_exact_lhs(tril_bd, lw)
    cum3 = cum.reshape(n_chunk, CHUNK, width)
    cend = cum3[:, CHUNK - 1:CHUNK, :]
    e_neg = jnp.exp(-cum)
    e_end = jnp.exp(cend - cum3).reshape(tm, width)

    rt_o[0] = (r * jnp.exp(cum)).astype(BF16)
    kkt_o[0] = (kk * jnp.exp(cum - lw)).astype(BF16)
    kh_o[0] = (k2 * e_neg).astype(BF16)
    bh_o[0] = (b * e_neg).astype(BF16)
    kb_o[0] = (k2 * e_end).astype(BF16)
    bb_o[0] = (b * e_end).astype(BF16)
    v_o[0] = v.astype(BF16)
    g_o[0] = g
    bonus_o[0] = bonus
    gc_o[0] = jnp.exp(cend)


def _rwkv_prep(p_a, mu, w0, w2, a0, a2, g2, k_k, k_a, r_k, tm=512):
    b, t, cols = p_a.shape
    width = w0.shape[0]
    rank = w2.shape[0]
    assert rank == HEAD64 and a2.shape[0] == HEAD64 and g2.shape[0] == LANES
    zeros = jnp.zeros((rank, width), F32)
    wcomb = jnp.concatenate([jnp.concatenate([w2, zeros], 1),
                             jnp.concatenate([zeros, a2], 1)], 0)
    row = lambda a_: a_.reshape(1, -1)
    const = lambda shp: pl.BlockSpec(shp, lambda i, j: (0, 0))
    tok = pl.BlockSpec((1, tm, width), lambda i, j: (i, j, 0))
    n_chunk = tm // CHUNK
    return pl.pallas_call(
        functools.partial(_rwkv_prep_kernel, width=width),
        out_shape=([jax.ShapeDtypeStruct((b, t, width), BF16)] * 7
                   + [jax.ShapeDtypeStruct((b, t, width), F32)] * 2
                   + [jax.ShapeDtypeStruct((b, t // CHUNK, 1, width), F32)]),
        grid=(b, t // tm),
        in_specs=[pl.BlockSpec((1, tm, cols), lambda i, j: (i, j, 0)),
                  const((1, cols)), const((1, width)), const((1, width)), const((1, width)),
                  const((1, width)), const((1, width)),
                  const((LANES, 2 * width)), const((LANES, width))],
        out_specs=[tok] * 9 + [pl.BlockSpec((1, n_chunk, 1, width), lambda i, j: (i, j, 0, 0))],
        scratch_shapes=[pltpu.VMEM((1, cols), F32)],
        compiler_params=_cparams(("parallel", "arbitrary")),
        name="rwkv_prep",
    )(p_a, row(mu), row(w0), row(a0), row(k_k), row(k_a), row(r_k), wcomb, g2)


def _rwkv_scan_kernel(rt_ref, kkt_ref, kh_ref, bh_ref, kb_ref, bb_ref, v_ref, g_ref, bonus_ref,
                      gc_ref, lnw_ref, lnb_ref, o_ref, st_ref, *, n_batch, n_pairs):
    c = CHUNK

    @pl.when(pl.program_id(0) == 0)
    def _():
        st_ref[...] = jnp.zeros_like(st_ref)

    rr = _iota((LANES, LANES), 0)
    cc = _iota((LANES, LANES), 1)
    mask_bd = jnp.where((rr >> 6) == (cc >> 6), 1.0, 0.0)
    mask_bd16 = mask_bd.astype(BF16)
    t_i = _iota((c, LANES), 0)
    s_i = _iota((c, LANES), 1) & (c - 1)
    strict = jnp.where(s_i < t_i, 1.0, 0.0)
    incl = jnp.where(s_i <= t_i, 1.0, 0.0)
    eye_ss = jnp.where(s_i == t_i, 1.0, 0.0)

    def bd16(x):
        return jnp.concatenate([x, x], axis=0) * mask_bd16

    def b16(xs):
        return [x.astype(BF16) for x in xs]

    chains = [(bi, p) for bi in range(n_batch) for p in range(n_pairs)]
    n_ch = len(chains)
    idx = range(n_ch)

    def ld(ref):
        return [ref[bi, :, p * LANES:(p + 1) * LANES] for bi, p in chains]

    rt, kkt, kh, bh, kb, bb, v = (ld(ref) for ref in
                                  (rt_ref, kkt_ref, kh_ref, bh_ref, kb_ref, bb_ref, v_ref))

    lhs = [jnp.concatenate([kkt[i], rt[i]], axis=0) for i in idx]
    rhs = [jnp.concatenate([bd16(kh[i]), bd16(bh[i])], axis=0) for i in idx]
    gm = [_dot_nt(lhs[i], rhs[i]) for i in idx]
    l_kv = [gm[i][:c, :LANES] * strict for i in idx]
    xp = [-(gm[i][:c, LANES:] * strict) for i in idx]
    pm = [gm[i][c:, :LANES] * incl for i in idx]
    nqm = [-(gm[i][c:, LANES:] * incl) for i in idx]

    tinv = [eye_ss + xp[i] for i in idx]
    xp16 = b16(xp)
    xpb = [bd16(x) for x in xp16]
    for _ in range(c.bit_length() - 2):
        xp16 = b16([_dot(xp16[i], xpb[i]) for i in idx])
        xpb = [bd16(x) for x in xp16]
        t16 = b16(tinv)
        tinv = [tinv[i] + _dot(t16[i], xpb[i]) for i in idx]
    tinv16 = b16(tinv)

    v_bd = [bd16(x) for x in v]
    l_kv16 = b16(l_kv)
    w16 = b16([_dot(l_kv16[i], v_bd[i]) for i in idx])
    tkw = [_dot(tinv16[i], jnp.concatenate([bd16(kkt[i]), bd16(w16[i])], axis=1)) for i in idx]
    tk16 = b16([x[:, :LANES] for x in tkw])

    st = [st_ref[i] for i in idx]
    st16 = b16(st)
    u = [_dot_nt(tk16[i], st16[i]) + tkw[i][:, LANES:] for i in idx]
    u16 = b16(u)
    pq16 = [jnp.concatenate([pm[i], nqm[i]], axis=1).astype(BF16) for i in idx]
    vu = [jnp.concatenate([v_bd[i], bd16(u16[i])], axis=0) for i in idx]
    y = [_dot_nt(rt[i], st16[i]) + _dot(pq16[i], vu[i]) for i in idx]
    upd = [_dot_tn(jnp.concatenate([v[i], -u16[i]], axis=0),
                   jnp.concatenate([kb[i], bb[i]], axis=0)) for i in idx]
    for i, (bi, p) in enumerate(chains):
        st_ref[i] = st[i] * gc_ref[bi, 0, :, p * LANES:(p + 1) * LANES] + upd[i] * mask_bd

    ys = jnp.concatenate(y, axis=0)
    mean = _dot_exact_rhs(ys, mask_bd16) * (1.0 / HEAD64)
    d = ys - mean
    var = _dot_exact_rhs(d * d, mask_bd16) * (1.0 / HEAD64)
    dn = d * lax.rsqrt(var + RWKV_GN_EPS)
    for i, (bi, p) in enumerate(chains):
        sl = slice(p * LANES, (p + 1) * LANES)
        yn = dn[i * c:(i + 1) * c] * lnw_ref[:, sl] + lnb_ref[:, sl]
        o_ref[bi, :, sl] = (yn + bonus_ref[bi, :, sl]) * g_ref[bi, :, sl]


def _rwkv_scan(rt, kkt, kh, bh, kb, bb, v, g, bonus, gc, ln_w, ln_b):
    bsz, t, width = rt.shape
    n_pairs = width // LANES
    tok = pl.BlockSpec((bsz, CHUNK, width), lambda j: (0, j, 0))
    const = pl.BlockSpec((1, width), lambda j: (0, 0))
    return pl.pallas_call(
        functools.partial(_rwkv_scan_kernel, n_batch=bsz, n_pairs=n_pairs),
        out_shape=jax.ShapeDtypeStruct((bsz, t, width), F32),
        grid=(t // CHUNK,),
        in_specs=[tok] * 9 + [pl.BlockSpec((bsz, 1, 1, width), lambda j: (0, j, 0, 0)),
                              const, const],
        out_specs=tok,
        scratch_shapes=[pltpu.VMEM((bsz * n_pairs, LANES, LANES), F32)],
        compiler_params=_cparams(("arbitrary",)),
        name="rwkv_scan",
    )(rt, kkt, kh, bh, kb, bb, v, g, bonus, gc, ln_w.reshape(1, -1), ln_b.reshape(1, -1))


def _lru_kernel(p_ref, cw_ref, cb_ref, wg_ref, ba_ref, bx_ref, lam_ref, o_ref,
                xcarry_ref, hcarry_ref, a_s, u_s, *, width):
    tm = p_ref.shape[1]

    @pl.when(pl.program_id(1) == 0)
    def _():
        xcarry_ref[...] = jnp.zeros_like(xcarry_ref)
        hcarry_ref[...] = jnp.zeros_like(hcarry_ref)

    gate = p_ref[0, :, 0:width]
    xb = p_ref[0, :, width:2 * width]
    carry8 = xcarry_ref[...]
    row8 = _iota((8, width), 0)

    def shifted(s):
        rolled = pltpu.roll(xb, s, axis=0)
        first = jnp.where(row8 < s, pltpu.roll(carry8, s, axis=0), rolled[0:8])
        return jnp.concatenate([first, rolled[8:]], axis=0)

    xc = (cw_ref[0:1, :] * shifted(3) + cw_ref[1:2, :] * shifted(2)
          + cw_ref[2:3, :] * shifted(1) + cw_ref[3:4, :] * xb + cb_ref[...])
    xcarry_ref[...] = xb[tm - 8:tm, :]

    gates = _dot(xc.astype(BF16), wg_ref[...])
    rg = _sigmoid(gates[:, :width] + ba_ref[...])
    ig = _sigmoid(gates[:, width:] + bx_ref[...])
    log_a = -LRU_C * rg * _softplus(-lam_ref[...])
    a = jnp.exp(log_a)
    a_s[...] = a
    u_s[...] = jnp.sqrt(1.0 - a * a) * ig * xc

    m1 = row8 >= 1
    m2 = row8 >= 2
    m4 = row8 >= 4

    def body(i, h):
        off = pl.multiple_of(i * 8, 8)
        a8 = a_s[pl.ds(off, 8), :]
        u8 = u_s[pl.ds(off, 8), :]
        for s, m in ((1, m1), (2, m2), (4, m4)):
            u_sh = jnp.where(m, pltpu.roll(u8, s, axis=0), 0.0)
            a_sh = jnp.where(m, pltpu.roll(a8, s, axis=0), 1.0)
            u8 = u8 + a8 * u_sh
            a8 = a8 * a_sh
        h8 = u8 + a8 * h
        u_s[pl.ds(off, 8), :] = h8
        return jnp.broadcast_to(h8[7:8, :], (8, width))

    h_last = lax.fori_loop(0, tm // 8, body, hcarry_ref[...])
    hcarry_ref[...] = h_last
    o_ref[0] = u_s[...] * _gelu_tanh(gate)


def _block_diag(w):
    nb, di, do = w.shape
    eye = jnp.eye(nb, dtype=w.dtype)
    return (eye[:, None, :, None] * w[:, :, None, :]).reshape(nb * di, nb * do)


def _lru(p_b, conv_w, conv_b, w_a, b_a, w_x, b_x, lam, tm=512):
    b, t, cols = p_b.shape
    width = cols // 2
    wg = jnp.concatenate([_block_diag(w_a), _block_diag(w_x)], axis=1).astype(BF16)
    row = lambda a_: a_.reshape(1, -1)
    const = lambda shp: pl.BlockSpec(shp, lambda i, j: (0, 0))
    return pl.pallas_call(
        functools.partial(_lru_kernel, width=width),
        out_shape=jax.ShapeDtypeStruct((b, t, width), F32),
        grid=(b, t // tm),
        in_specs=[pl.BlockSpec((1, tm, cols), lambda i, j: (i, j, 0)),
                  const(conv_w.shape), const((1, width)), const((width, 2 * width)),
                  const((1, width)), const((1, width)), const((1, width))],
        out_specs=pl.BlockSpec((1, tm, width), lambda i, j: (i, j, 0)),
        scratch_shapes=[pltpu.VMEM((8, width), F32), pltpu.VMEM((8, width), F32),
                        pltpu.VMEM((tm, width), F32), pltpu.VMEM((tm, width), F32)],
        compiler_params=_cparams(("parallel", "arbitrary")),
        name="rglru",
    )(p_b, conv_w, row(conv_b), wg, row(b_a), row(b_x), row(lam))


def _gla_kernel(q_ref, f_ref, i_ref, g_ref, lb_ref, nw_ref, o_ref, st_ref, *, n_batch, n_heads):
    c = CHUNK

    @pl.when(pl.program_id(0) == 0)
    def _():
        st_ref[...] = jnp.zeros_like(st_ref)

    trilf = jnp.where(_iota((c, c), 1) <= _iota((c, c), 0), 1.0, 0.0)
    tril16 = trilf.astype(BF16)
    mid = c // 2 - 1
    lb = lb_ref[...]

    qm, km, qg, kb, v16, gc = [], [], [], [], [], []
    for bi in range(n_batch):
        q = _silu(q_ref[bi])
        fg = lb + (1.0 - lb) * _sigmoid(f_ref[bi])
        k = 1.0 - fg
        cum = _dot_exact_lhs(tril16, jnp.log(fg))
        cum_c = cum[c - 1:c, :]
        cum_m = cum[mid:mid + 1, :]
        qm.append((q * jnp.exp(cum - cum_m)).astype(BF16))
        km.append((k * jnp.exp(cum_m - cum)).astype(BF16))
        qg.append((q * jnp.exp(cum)).astype(BF16))
        kb.append((k * jnp.exp(cum_c - cum)).astype(BF16))
        v16.append(i_ref[bi].astype(BF16))
        gc.append(jnp.exp(cum_c))

    chains = [(bi, h) for bi in range(n_batch) for h in range(n_heads)]
    idx = range(len(chains))

    def hs(xs):
        return [xs[bi][:, h * LANES:(h + 1) * LANES] for bi, h in chains]

    qm_c, km_c, qg_c, kb_c, v_c, gc_c = hs(qm), hs(km), hs(qg), hs(kb), hs(v16), hs(gc)
    scores = [(_dot_nt(qm_c[i], km_c[i]) * trilf).astype(BF16) for i in idx]
    st = [st_ref[i] for i in idx]
    st16 = [s.astype(BF16) for s in st]
    o = [_dot(scores[i], v_c[i]) + _dot_nt(qg_c[i], st16[i]) for i in idx]
    for i in idx:
        st_ref[i] = st[i] * gc_c[i] + _dot_tn(v_c[i], kb_c[i])

    for i, (bi, h) in enumerate(chains):
        sl = slice(h * LANES, (h + 1) * LANES)
        on = o[i] * lax.rsqrt(jnp.mean(o[i] * o[i], axis=-1, keepdims=True) + RMS_EPS)
        o_ref[bi, :, sl] = on * nw_ref[:, sl] * _silu(g_ref[bi, :, sl])


def _gla(q, f, i, g, lower_bound, norm_w):
    bsz, t, width = q.shape
    n_heads = width // LANES
    tok = pl.BlockSpec((bsz, CHUNK, width), lambda j: (0, j, 0))
    const = pl.BlockSpec((1, width), lambda j: (0, 0))
    return pl.pallas_call(
        functools.partial(_gla_kernel, n_batch=bsz, n_heads=n_heads),
        out_shape=jax.ShapeDtypeStruct((bsz, t, width), F32),
        grid=(t // CHUNK,),
        in_specs=[tok] * 4 + [const, const],
        out_specs=tok,
        scratch_shapes=[pltpu.VMEM((bsz * n_heads, LANES, LANES), F32)],
        compiler_params=_cparams(("arbitrary",)),
        name="hgrn2_gla",
    )(q, f, i, g, lower_bound.reshape(1, -1), norm_w.reshape(1, -1))


def _router_kernel(x_ref, g_ref, w_ref, b_ref, hn_o, e_o, gate_o):
    x = x_ref[...]
    ms = jnp.mean(x * x, axis=-1, keepdims=True)
    hn = x * lax.rsqrt(ms + RMS_EPS) * g_ref[...]
    hn_o[...] = hn
    lt = lax.dot_general(w_ref[...], hn, (((1,), (1,)), ((), ())),
                         precision=HIGHEST, preferred_element_type=F32) + b_ref[:, 0:1]
    tm = x.shape[0]
    gl = lt[0:8, :]
    row8 = _iota((8, tm), 0)
    gl = jnp.where(row8 < N_GROUPS, gl, -jnp.inf)
    gmax = jnp.max(gl, axis=0, keepdims=True)
    g_sel = jnp.min(jnp.where(gl == gmax, row8, 8), axis=0, keepdims=True)
    g_gate = 1.0 / jnp.sum(jnp.exp(gl - gmax), axis=0, keepdims=True)

    el = jnp.zeros((EXPERTS_PER_GROUP, tm), F32)
    for gi in range(N_GROUPS):
        lo = 8 + gi * EXPERTS_PER_GROUP
        el = jnp.where(g_sel == gi, lt[lo:lo + EXPERTS_PER_GROUP, :], el)
    m1 = jnp.max(el, axis=0, keepdims=True)
    i1 = jnp.min(jnp.where(el == m1, row8, 8), axis=0, keepdims=True)
    el2 = jnp.where(row8 == i1, -jnp.inf, el)
    m2 = jnp.max(el2, axis=0, keepdims=True)
    i2 = jnp.min(jnp.where(el2 == m2, row8, 8), axis=0, keepdims=True)
    e2 = jnp.exp(m2 - m1)
    inv = 1.0 / (1.0 + e2)
    e_o[...] = jnp.concatenate([g_sel * EXPERTS_PER_GROUP + i1,
                                g_sel * EXPERTS_PER_GROUP + i2], axis=0)
    gate_o[...] = jnp.concatenate([g_gate * inv, g_gate * e2 * inv], axis=0)


def _router(x2d, g, w_group, b_group, w_expert, b_expert, tm=512):
    n, d = x2d.shape
    wt = jnp.zeros((LANES, d), F32)
    wt = wt.at[0:N_GROUPS].set(w_group.T).at[8:8 + N_EXPERTS].set(w_expert.T)
    bt = jnp.zeros((LANES,), F32)
    bt = bt.at[0:N_GROUPS].set(b_group).at[8:8 + N_EXPERTS].set(b_expert)
    bt = jnp.broadcast_to(bt[:, None], (LANES, LANES))
    return pl.pallas_call(
        _router_kernel,
        out_shape=[jax.ShapeDtypeStruct((n, d), F32),
                   jax.ShapeDtypeStruct((2, n), I32),
                   jax.ShapeDtypeStruct((2, n), F32)],
        grid=(n // tm,),
        in_specs=[pl.BlockSpec((tm, d), lambda i: (i, 0)),
                  pl.BlockSpec((1, d), lambda i: (0, 0)),
                  pl.BlockSpec((LANES, d), lambda i: (0, 0)),
                  pl.BlockSpec((LANES, LANES), lambda i: (0, 0))],
        out_specs=[pl.BlockSpec((tm, d), lambda i: (i, 0)),
                   pl.BlockSpec((2, tm), lambda i: (0, i)),
                   pl.BlockSpec((2, tm), lambda i: (0, i))],
        compiler_params=_cparams(("parallel",)),
        name="moe_router",
    )(x2d, g.reshape(1, d), wt, bt)


def _row_gather_start(src_hbm, idx_ref, base, buf, sem, n_rows):
    def body(i, carry):
        tok = idx_ref[base + i]
        pltpu.make_async_copy(src_hbm.at[pl.ds(tok, 1), :], buf.at[pl.ds(i, 1), :], sem).start()
        return carry
    lax.fori_loop(0, n_rows, body, 0, unroll=8)


def _row_gather_wait(src_hbm, buf, sem, n_rows):
    pltpu.make_async_copy(src_hbm.at[pl.ds(0, n_rows), :], buf, sem).wait()


def _expert_kernel(be_ref, first_ref, nused_ref, tok_ref,
                   hn_hbm, gate_ref, wg_ref, wu_ref, wd_ref, o_ref,
                   xbuf, sems, wg16, wu16, wd16):
    j = pl.program_id(0)
    nb = pl.num_programs(0)
    rows = xbuf.shape[1]
    slot = lax.rem(j, 2)

    @pl.when(j == 0)
    def _():
        _row_gather_start(hn_hbm, tok_ref, 0, xbuf.at[0], sems.at[0], rows)

    @pl.when(j + 1 < nb)
    def _():
        nxt = 1 - slot
        _row_gather_start(hn_hbm, tok_ref, (j + 1) * rows, xbuf.at[nxt], sems.at[nxt], rows)

    @pl.when(first_ref[j] == 1)
    def _():
        wg16[...] = wg_ref[0].astype(BF16)
        wu16[...] = wu_ref[0].astype(BF16)
        wd16[...] = wd_ref[0].astype(BF16)

    _row_gather_wait(hn_hbm, xbuf.at[slot], sems.at[slot], rows)

    @pl.when(j < nused_ref[0])
    def _():
        x = xbuf[slot].astype(BF16)
        hg = _dot(x, wg16[...])
        hu = _dot(x, wu16[...])
        hid = (_silu(hg) * hu).astype(BF16)
        o_ref[...] = _dot(hid, wd16[...]) * gate_ref[...]

    @pl.when(j >= nused_ref[0])
    def _():
        o_ref[...] = jnp.zeros_like(o_ref)


def _expert_ffn(hn, row_tok, row_gate, block_e, first_flag, n_used, w_gate, w_up, w_down,
                rows=ROUTE_ROWS):
    n, d = hn.shape
    n_rows = row_tok.shape[0]
    n_blocks = n_rows // rows
    ff = w_gate.shape[-1]
    return pl.pallas_call(
        _expert_kernel,
        out_shape=jax.ShapeDtypeStruct((n_rows, d), F32),
        grid_spec=pltpu.PrefetchScalarGridSpec(
            num_scalar_prefetch=4,
            grid=(n_blocks,),
            in_specs=[pl.BlockSpec(memory_space=pl.ANY),
                      pl.BlockSpec((rows, 1), lambda j, be, fi, nu, tk: (j, 0)),
                      pl.BlockSpec((1, d, ff), lambda j, be, fi, nu, tk: (be[j], 0, 0)),
                      pl.BlockSpec((1, d, ff), lambda j, be, fi, nu, tk: (be[j], 0, 0)),
                      pl.BlockSpec((1, ff, d), lambda j, be, fi, nu, tk: (be[j], 0, 0))],
            out_specs=pl.BlockSpec((rows, d), lambda j, be, fi, nu, tk: (j, 0)),
            scratch_shapes=[pltpu.VMEM((2, rows, d), F32),
                            pltpu.SemaphoreType.DMA((2,)),
                            pltpu.VMEM((d, ff), BF16), pltpu.VMEM((d, ff), BF16),
                            pltpu.VMEM((ff, d), BF16)]),
        compiler_params=_cparams(("arbitrary",)),
        name="moe_expert_ffn",
    )(block_e, first_flag, n_used, row_tok, hn, row_gate.reshape(n_rows, 1),
      w_gate, w_up, w_down)


def _combine_kernel(pos_ref, x_ref, ys_hbm, gfin_ref, o_ref, buf, sems, *, final_norm):
    j = pl.program_id(0)
    nb = pl.num_programs(0)
    tm = x_ref.shape[0]
    n_tok = nb * tm
    slot = lax.rem(j, 2)

    def start(step, s):
        _row_gather_start(ys_hbm, pos_ref, step * tm, buf.at[s, 0], sems.at[s, 0], tm)
        _row_gather_start(ys_hbm, pos_ref, n_tok + step * tm, buf.at[s, 1], sems.at[s, 1], tm)

    @pl.when(j == 0)
    def _():
        start(0, 0)

    @pl.when(j + 1 < nb)
    def _():
        start(j + 1, 1 - slot)

    _row_gather_wait(ys_hbm, buf.at[slot, 0], sems.at[slot, 0], tm)
    _row_gather_wait(ys_hbm, buf.at[slot, 1], sems.at[slot, 1], tm)
    y = x_ref[...] + (buf[slot, 0] + buf[slot, 1])
    if final_norm:
        ms = jnp.mean(y * y, axis=-1, keepdims=True)
        y = y * lax.rsqrt(ms + RMS_EPS) * gfin_ref[...]
    o_ref[...] = y


def _combine(x2d, ys, pos, g_final, final_norm, tm=256):
    n, d = x2d.shape
    return pl.pallas_call(
        functools.partial(_combine_kernel, final_norm=final_norm),
        out_shape=jax.ShapeDtypeStruct((n, d), F32),
        grid_spec=pltpu.PrefetchScalarGridSpec(
            num_scalar_prefetch=1,
            grid=(n // tm,),
            in_specs=[pl.BlockSpec((tm, d), lambda j, ps: (j, 0)),
                      pl.BlockSpec(memory_space=pl.ANY),
                      pl.BlockSpec((1, d), lambda j, ps: (0, 0))],
            out_specs=pl.BlockSpec((tm, d), lambda j, ps: (j, 0)),
            scratch_shapes=[pltpu.VMEM((2, 2, tm, d), F32),
                            pltpu.SemaphoreType.DMA((2, 2))]),
        compiler_params=_cparams(("arbitrary",)),
        name="moe_combine",
    )(pos.reshape(-1), x2d, ys, g_final.reshape(1, d))


def _route_plan(e2, gates2, rows):
    n_tok = e2.shape[1]
    n_assign = 2 * n_tok
    e_flat = e2.T.reshape(-1)
    onehot = (e_flat[:, None] == jnp.arange(N_EXPERTS, dtype=I32)[None, :]).astype(I32)
    csum = jnp.cumsum(onehot, axis=0)
    rank = jnp.sum((csum - onehot) * onehot, axis=1)
    counts = csum[-1]
    padded = (counts + rows - 1) // rows * rows
    pend = jnp.cumsum(padded)
    pstart = pend - padded
    dest = pstart[e_flat] + rank
    n_blocks = n_assign // rows + N_EXPERTS
    n_rows = n_blocks * rows
    tok = jnp.arange(n_assign, dtype=I32) // 2
    row_tok = jnp.zeros((n_rows,), I32).at[dest].set(tok)
    row_gate = jnp.zeros((n_rows,), F32).at[dest].set(gates2.T.reshape(-1))
    starts = jnp.arange(n_blocks, dtype=I32) * rows
    block_e = jnp.clip(jnp.searchsorted(pend, starts, side='right'), 0, N_EXPERTS - 1).astype(I32)
    first = jnp.concatenate([jnp.ones((1,), I32),
                             (block_e[1:] != block_e[:-1]).astype(I32)])
    n_used = (pend[-1] // rows).astype(I32).reshape(1)
    pos = dest.reshape(n_tok, 2).T
    return row_tok, row_gate, block_e, first, n_used, pos


def _moe(x2d, g_ffn, w_group, b_group, w_expert, b_expert, w_gate, w_up, w_down,
         g_final, final_norm):
    hn, e2, gates2 = _router(x2d, g_ffn, w_group, b_group, w_expert, b_expert)
    row_tok, row_gate, block_e, first, n_used, pos = _route_plan(e2, gates2, ROUTE_ROWS)
    ys = _expert_ffn(hn, row_tok, row_gate, block_e, first, n_used, w_gate, w_up, w_down)
    return _combine(x2d, ys, pos, g_final, final_norm)


def kernel(x, norm_mix, norm_ffn, norm_final, ab_w_in, rw_mu, rw_w0, rw_w2, rw_a0, rw_a2, rw_g2, rw_k_k, rw_k_a, rw_r_k, rw_ln_w, rw_ln_b, lru_conv_w, lru_conv_b, lru_w_a, lru_b_a, lru_w_x, lru_b_x, lru_lambda, ab_w_out, c_w_in, c_lower_bound, c_norm_w, c_w_out, moe_w_group, moe_b_group, moe_w_expert, moe_b_expert, moe_w_gate, moe_w_up, moe_w_down):
    bsz, t, d = x.shape
    depth = norm_mix.shape[0]
    n = bsz * t
    lbs = jnp.cumsum(jax.nn.softmax(c_lower_bound.astype(F32), axis=0), axis=0)
    lbs = lbs - lbs[0]
    x2d = x.reshape(n, d)
    for layer in range(depth):
        j = layer // 2
        if layer % 2 == 0:
            rw_cols = rw_mu.shape[1]
            lru_cols = ab_w_in.shape[2] - rw_cols
            width = rw_w0.shape[1]
            p_a, p_b = _norm_matmul(x2d, norm_mix[layer], ab_w_in[j], (rw_cols, lru_cols))
            prep = _rwkv_prep(p_a.reshape(bsz, t, rw_cols), rw_mu[j], rw_w0[j], rw_w2[j],
                              rw_a0[j], rw_a2[j], rw_g2[j], rw_k_k[j], rw_k_a[j],
                              rw_r_k[j].reshape(-1))
            ya = _rwkv_scan(*prep, rw_ln_w[j], rw_ln_b[j])
            yb = _lru(p_b.reshape(bsz, t, lru_cols), lru_conv_w[j], lru_conv_b[j], lru_w_a[j],
                      lru_b_a[j], lru_w_x[j], lru_b_x[j], lru_lambda[j])
            x2d = _proj_residual(x2d, [ya.reshape(n, width), yb.reshape(n, -1)],
                                 [ab_w_out[j][:width], ab_w_out[j][width:]])
        else:
            hw = c_norm_w.shape[1]
            q, f, i_, g = _norm_matmul(x2d, norm_mix[layer], c_w_in[j], (hw,) * 4)
            shp = (bsz, t, hw)
            o = _gla(q.reshape(shp), f.reshape(shp), i_.reshape(shp), g.reshape(shp),
                     lbs[layer], c_norm_w[j])
            x2d = _proj_residual(x2d, [o.reshape(n, hw)], [c_w_out[j]])
        last = layer == depth - 1
        x2d = _moe(x2d, norm_ffn[layer], moe_w_group[layer], moe_b_group[layer],
                   moe_w_expert[layer], moe_b_expert[layer], moe_w_gate[layer],
                   moe_w_up[layer], moe_w_down[layer], norm_final, last)
    return x2d.reshape(bsz, t, d)
```

```python
import functools

import jax
import jax.numpy as jnp
from jax import lax
from jax.experimental import pallas as pl
from jax.experimental.pallas import tpu as pltpu

F32 = jnp.float32
BF16 = jnp.bfloat16
I32 = jnp.int32

RMS_EPS = 1e-6
RWKV_GN_EPS = 64e-5
LRU_C = 8.0
CHUNK = 64
CHUNK_SHIFT = CHUNK.bit_length() - 1
HEAD64 = 64
LANES = 128
N_GROUPS = 4
EXPERTS_PER_GROUP = 8
N_EXPERTS = N_GROUPS * EXPERTS_PER_GROUP
ROUTE_ROWS = 256
VMEM_LIMIT = 56 * 1024 * 1024
HIGHEST = lax.Precision.HIGHEST


def _cparams(sem):
    return pltpu.CompilerParams(dimension_semantics=sem, vmem_limit_bytes=VMEM_LIMIT)


def _sigmoid(x):
    return 1.0 / (1.0 + jnp.exp(-x))


def _softplus(x):
    return jnp.maximum(x, 0.0) + jnp.log(1.0 + jnp.exp(-jnp.abs(x)))


def _silu(x):
    return x * _sigmoid(x)


def _gelu_tanh(x):
    return 0.5 * x * (1.0 + jnp.tanh(0.7978845608028654 * (x + 0.044715 * x * x * x)))


def _dot(a, b):
    return jnp.dot(a, b, preferred_element_type=F32)


def _dot_nt(a, b):
    return lax.dot_general(a, b, (((1,), (1,)), ((), ())), preferred_element_type=F32)


def _dot_tn(a, b):
    return lax.dot_general(a, b, (((0,), (0,)), ((), ())), preferred_element_type=F32)


def _split3(x):
    h1 = x.astype(BF16)
    r1 = x - h1.astype(F32)
    h2 = r1.astype(BF16)
    r2 = r1 - h2.astype(F32)
    return h1, h2, r2.astype(BF16)


def _dot_exact_rhs(x, m_bf16):
    h1, h2, h3 = _split3(x)
    return _dot(h1, m_bf16) + _dot(h2, m_bf16) + _dot(h3, m_bf16)


def _dot_exact_lhs(m_bf16, x):
    h1, h2, h3 = _split3(x)
    return _dot(m_bf16, h1) + _dot(m_bf16, h2) + _dot(m_bf16, h3)


def _iota(shape, dim):
    return lax.broadcasted_iota(I32, shape, dim)


def _norm_matmul_kernel(x_ref, g_ref, w_ref, *o_refs, splits):
    x = x_ref[...]
    ms = jnp.mean(x * x, axis=-1, keepdims=True)
    y = (x * lax.rsqrt(ms + RMS_EPS) * g_ref[...]).astype(BF16)
    off = 0
    for o_ref, n in zip(o_refs, splits):
        o_ref[...] = _dot(y, w_ref[:, off:off + n])
        off += n


def _norm_matmul(x2d, g, w, splits, tm=256):
    n, d = x2d.shape
    ncols = w.shape[1]
    assert sum(splits) == ncols and n % tm == 0
    return pl.pallas_call(
        functools.partial(_norm_matmul_kernel, splits=splits),
        out_shape=[jax.ShapeDtypeStruct((n, s), F32) for s in splits],
        grid=(n // tm,),
        in_specs=[pl.BlockSpec((tm, d), lambda i: (i, 0)),
                  pl.BlockSpec((1, d), lambda i: (0, 0)),
                  pl.BlockSpec((d, ncols), lambda i: (0, 0))],
        out_specs=[pl.BlockSpec((tm, s), lambda i: (i, 0)) for s in splits],
        compiler_params=_cparams(("parallel",)),
        name="norm_matmul",
    )(x2d, g.reshape(1, d), w.astype(BF16))


def _proj_residual_kernel(*refs, n_in):
    x_ref = refs[0]
    y_refs = refs[1:1 + n_in]
    w_refs = refs[1 + n_in:1 + 2 * n_in]
    o_ref = refs[1 + 2 * n_in]
    acc = x_ref[...]
    for y_ref, w_ref in zip(y_refs, w_refs):
        acc = acc + _dot(y_ref[...].astype(BF16), w_ref[...])
    o_ref[...] = acc


def _proj_residual(x2d, ys, ws, tm=512):
    n, d = x2d.shape
    n_in = len(ys)
    in_specs = [pl.BlockSpec((tm, d), lambda i: (i, 0))]
    in_specs += [pl.BlockSpec((tm, y.shape[1]), lambda i: (i, 0)) for y in ys]
    in_specs += [pl.BlockSpec(w.shape, lambda i: (0, 0)) for w in ws]
    return pl.pallas_call(
        functools.partial(_proj_residual_kernel, n_in=n_in),
        out_shape=jax.ShapeDtypeStruct((n, d), F32),
        grid=(n // tm,),
        in_specs=in_specs,
        out_specs=pl.BlockSpec((tm, d), lambda i: (i, 0)),
        compiler_params=_cparams(("parallel",)),
        name="proj_residual",
    )(x2d, *ys, *[w.astype(BF16) for w in ws])


def _rwkv_prep_kernel(p_ref, mu_ref, w0_ref, a0_ref, kk_s_ref, ka_ref, rk_ref,
                      wcomb_ref, g2_ref,
                      rt_o, kkt_o, kh_o, bh_o, kb_o, bb_o, v_o, g_o, bonus_o, gc_o,
                      prev_ref, *, width):
    tm = p_ref.shape[1]

    @pl.when(pl.program_id(1) == 0)
    def _():
        prev_ref[...] = jnp.zeros_like(prev_ref)

    p = p_ref[0]
    rolled = pltpu.roll(p, 1, axis=0)
    prev = jnp.where(_iota(p.shape, 0) == 0, prev_ref[...], rolled)
    prev_ref[...] = p[tm - 1:tm, :]
    ps = p + (prev - p) * mu_ref[...]

    r = ps[:, 0:width]
    k = ps[:, width:2 * width]
    v = ps[:, 2 * width:3 * width]
    lowrank = ps[:, 3 * width:3 * width + LANES]
    gl = ps[:, 3 * width + LANES:3 * width + 2 * LANES]

    lane = _iota(lowrank.shape, 1)
    lr_in = jnp.where(lane < HEAD64, jnp.tanh(lowrank), lowrank)
    t12 = jnp.dot(lr_in, wcomb_ref[...], precision=HIGHEST, preferred_element_type=F32)
    wlog = -_softplus(-(w0_ref[...] + t12[:, :width])) - 0.5
    lw = -jnp.exp(wlog)
    a = _sigmoid(a0_ref[...] + t12[:, width:])
    g = jnp.dot(_sigmoid(gl), g2_ref[...], precision=HIGHEST, preferred_element_type=F32)

    ri = _iota((width, width), 0) >> 6
    ci = _iota((width, width), 1) >> 6
    seg = jnp.where(ri == ci, 1.0, 0.0).astype(BF16)

    kk = k * kk_s_ref[...]
    nrm = jnp.sqrt(_dot_exact_rhs(kk * kk, seg))
    kk = kk / jnp.maximum(nrm, 1e-12)
    k2 = k * (1.0 + (a - 1.0) * ka_ref[...])
    bonus = _dot_exact_rhs(r * k2 * rk_ref[...], seg) * v

    b = kk * a

    n_chunk = tm // CHUNK
    rr = _iota((tm, tm), 0)
    cc = _iota((tm, tm), 1)
    tril_bd = jnp.where(((rr >> CHUNK_SHIFT) == (cc >> CHUNK_SHIFT)) & (cc <= rr),
                        1.0, 0.0).astype(BF16)
    cum = _dot_exact_lhs(tril_bd, lw)
    cum3 = cum.reshape(n_chunk, CHUNK, width)
    cend = cum3[:, CHUNK - 1:CHUNK, :]
    e_neg = jnp.exp(-cum)
    e_end = jnp.exp(cend - cum3).reshape(tm, width)

    rt_o[0] = (r * jnp.exp(cum)).astype(BF16)
    kkt_o[0] = (kk * jnp.exp(cum - lw)).astype(BF16)
    kh_o[0] = (k2 * e_neg).astype(BF16)
    bh_o[0] = (b * e_neg).astype(BF16)
    kb_o[0] = (k2 * e_end).astype(BF16)
    bb_o[0] = (b * e_end).astype(BF16)
    v_o[0] = v.astype(BF16)
    g_o[0] = g
    bonus_o[0] = bonus
    gc_o[0] = jnp.exp(cend)


def _rwkv_prep(p_a, mu, w0, w2, a0, a2, g2, k_k, k_a, r_k, tm=512):
    b, t, cols = p_a.shape
    width = w0.shape[0]
    rank = w2.shape[0]
    assert rank == HEAD64 and a2.shape[0] == HEAD64 and g2.shape[0] == LANES
    zeros = jnp.zeros((rank, width), F32)
    wcomb = jnp.concatenate([jnp.concatenate([w2, zeros], 1),
                             jnp.concatenate([zeros, a2], 1)], 0)
    row = lambda a_: a_.reshape(1, -1)
    const = lambda shp: pl.BlockSpec(shp, lambda i, j: (0, 0))
    tok = pl.BlockSpec((1, tm, width), lambda i, j: (i, j, 0))
    n_chunk = tm // CHUNK
    return pl.pallas_call(
        functools.partial(_rwkv_prep_kernel, width=width),
        out_shape=([jax.ShapeDtypeStruct((b, t, width), BF16)] * 7
                   + [jax.ShapeDtypeStruct((b, t, width), F32)] * 2
                   + [jax.ShapeDtypeStruct((b, t // CHUNK, 1, width), F32)]),
        grid=(b, t // tm),
        in_specs=[pl.BlockSpec((1, tm, cols), lambda i, j: (i, j, 0)),
                  const((1, cols)), const((1, width)), const((1, width)), const((1, width)),
                  const((1, width)), const((1, width)),
                  const((LANES, 2 * width)), const((LANES, width))],
        out_specs=[tok] * 9 + [pl.BlockSpec((1, n_chunk, 1, width), lambda i, j: (i, j, 0, 0))],
        scratch_shapes=[pltpu.VMEM((1, cols), F32)],
        compiler_params=_cparams(("parallel", "arbitrary")),
        name="rwkv_prep",
    )(p_a, row(mu), row(w0), row(a0), row(k_k), row(k_a), row(r_k), wcomb, g2)


def _rwkv_scan_kernel(rt_ref, kkt_ref, kh_ref, bh_ref, kb_ref, bb_ref, v_ref, g_ref, bonus_ref,
                      gc_ref, lnw_ref, lnb_ref, o_ref, st_ref, *, n_batch, n_pairs):
    c = CHUNK

    @pl.when(pl.program_id(0) == 0)
    def _():
        st_ref[...] = jnp.zeros_like(st_ref)

    rr = _iota((LANES, LANES), 0)
    cc = _iota((LANES, LANES), 1)
    mask_bd = jnp.where((rr >> 6) == (cc >> 6), 1.0, 0.0)
    mask_bd16 = mask_bd.astype(BF16)
    t_i = _iota((c, LANES), 0)
    s_i = _iota((c, LANES), 1) & (c - 1)
    strict = jnp.where(s_i < t_i, 1.0, 0.0)
    incl = jnp.where(s_i <= t_i, 1.0, 0.0)
    eye_ss = jnp.where(s_i == t_i, 1.0, 0.0)

    def bd16(x):
        return jnp.concatenate([x, x], axis=0) * mask_bd16

    def b16(xs):
        return [x.astype(BF16) for x in xs]

    chains = [(bi, p) for bi in range(n_batch) for p in range(n_pairs)]
    n_ch = len(chains)
    idx = range(n_ch)

    def ld(ref):
        return [ref[bi, :, p * LANES:(p + 1) * LANES] for bi, p in chains]

    rt, kkt, kh, bh, kb, bb, v = (ld(ref) for ref in
                                  (rt_ref, kkt_ref, kh_ref, bh_ref, kb_ref, bb_ref, v_ref))

    lhs = [jnp.concatenate([kkt[i], rt[i]], axis=0) for i in idx]
    rhs = [jnp.concatenate([bd16(kh[i]), bd16(bh[i])], axis=0) for i in idx]
    gm = [_dot_nt(lhs[i], rhs[i]) for i in idx]
    l_kv = [gm[i][:c, :LANES] * strict for i in idx]
    xp = [-(gm[i][:c, LANES:] * strict) for i in idx]
    pm = [gm[i][c:, :LANES] * incl for i in idx]
    nqm = [-(gm[i][c:, LANES:] * incl) for i in idx]

    tinv = [eye_ss + xp[i] for i in idx]
    xp16 = b16(xp)
    xpb = [bd16(x) for x in xp16]
    for _ in range(c.bit_length() - 2):
        xp16 = b16([_dot(xp16[i], xpb[i]) for i in idx])
        xpb = [bd16(x) for x in xp16]
        t16 = b16(tinv)
        tinv = [tinv[i] + _dot(t16[i], xpb[i]) for i in idx]
    tinv16 = b16(tinv)

    v_bd = [bd16(x) for x in v]
    l_kv16 = b16(l_kv)
    w16 = b16([_dot(l_kv16[i], v_bd[i]) for i in idx])
    tkw = [_dot(tinv16[i], jnp.concatenate([bd16(kkt[i]), bd16(w16[i])], axis=1)) for i in idx]
    tk16 = b16([x[:, :LANES] for x in tkw])

    st = [st_ref[i] for i in idx]
    st16 = b16(st)
    u = [_dot_nt(tk16[i], st16[i]) + tkw[i][:, LANES:] for i in idx]
    u16 = b16(u)
    pq16 = [jnp.concatenate([pm[i], nqm[i]], axis=1).astype(BF16) for i in idx]
    vu = [jnp.concatenate([v_bd[i], bd16(u16[i])], axis=0) for i in idx]
    y = [_dot_nt(rt[i], st16[i]) + _dot(pq16[i], vu[i]) for i in idx]
    upd = [_dot_tn(jnp.concatenate([v[i], -u16[i]], axis=0),
                   jnp.concatenate([kb[i], bb[i]], axis=0)) for i in idx]
    for i, (bi, p) in enumerate(chains):
        st_ref[i] = st[i] * gc_ref[bi, 0, :, p * LANES:(p + 1) * LANES] + upd[i] * mask_bd

    ys = jnp.concatenate(y, axis=0)
    mean = _dot_exact_rhs(ys, mask_bd16) * (1.0 / HEAD64)
    d = ys - mean
    var = _dot_exact_rhs(d * d, mask_bd16) * (1.0 / HEAD64)
    dn = d * lax.rsqrt(var + RWKV_GN_EPS)
    for i, (bi, p) in enumerate(chains):
        sl = slice(p * LANES, (p + 1) * LANES)
        yn = dn[i * c:(i + 1) * c] * lnw_ref[:, sl] + lnb_ref[:, sl]
        o_ref[bi, :, sl] = (yn + bonus_ref[bi, :, sl]) * g_ref[bi, :, sl]


def _rwkv_scan(rt, kkt, kh, bh, kb, bb, v, g, bonus, gc, ln_w, ln_b):
    bsz, t, width = rt.shape
    n_pairs = width // LANES
    tok = pl.BlockSpec((bsz, CHUNK, width), lambda j: (0, j, 0))
    const = pl.BlockSpec((1, width), lambda j: (0, 0))
    return pl.pallas_call(
        functools.partial(_rwkv_scan_kernel, n_batch=bsz, n_pairs=n_pairs),
        out_shape=jax.ShapeDtypeStruct((bsz, t, width), F32),
        grid=(t // CHUNK,),
        in_specs=[tok] * 9 + [pl.BlockSpec((bsz, 1, 1, width), lambda j: (0, j, 0, 0)),
                              const, const],
        out_specs=tok,
        scratch_shapes=[pltpu.VMEM((bsz * n_pairs, LANES, LANES), F32)],
        compiler_params=_cparams(("arbitrary",)),
        name="rwkv_scan",
    )(rt, kkt, kh, bh, kb, bb, v, g, bonus, gc, ln_w.reshape(1, -1), ln_b.reshape(1, -1))


def _lru_kernel(p_ref, cw_ref, cb_ref, wg_ref, ba_ref, bx_ref, lam_ref, o_ref,
                xcarry_ref, hcarry_ref, a_s, u_s, *, width):
    tm = p_ref.shape[1]

    @pl.when(pl.program_id(1) == 0)
    def _():
        xcarry_ref[...] = jnp.zeros_like(xcarry_ref)
        hcarry_ref[...] = jnp.zeros_like(hcarry_ref)

    gate = p_ref[0, :, 0:width]
    xb = p_ref[0, :, width:2 * width]
    carry8 = xcarry_ref[...]
    row8 = _iota((8, width), 0)

    def shifted(s):
        rolled = pltpu.roll(xb, s, axis=0)
        first = jnp.where(row8 < s, pltpu.roll(carry8, s, axis=0), rolled[0:8])
        return jnp.concatenate([first, rolled[8:]], axis=0)

    xc = (cw_ref[0:1, :] * shifted(3) + cw_ref[1:2, :] * shifted(2)
          + cw_ref[2:3, :] * shifted(1) + cw_ref[3:4, :] * xb + cb_ref[...])
    xcarry_ref[...] = xb[tm - 8:tm, :]

    gates = _dot(xc.astype(BF16), wg_ref[...])
    rg = _sigmoid(gates[:, :width] + ba_ref[...])
    ig = _sigmoid(gates[:, width:] + bx_ref[...])
    log_a = -LRU_C * rg * _softplus(-lam_ref[...])
    a = jnp.exp(log_a)
    a_s[...] = a
    u_s[...] = jnp.sqrt(1.0 - a * a) * ig * xc

    m1 = row8 >= 1
    m2 = row8 >= 2
    m4 = row8 >= 4

    def body(i, h):
        off = pl.multiple_of(i * 8, 8)
        a8 = a_s[pl.ds(off, 8), :]
        u8 = u_s[pl.ds(off, 8), :]
        for s, m in ((1, m1), (2, m2), (4, m4)):
            u_sh = jnp.where(m, pltpu.roll(u8, s, axis=0), 0.0)
            a_sh = jnp.where(m, pltpu.roll(a8, s, axis=0), 1.0)
            u8 = u8 + a8 * u_sh
            a8 = a8 * a_sh
        h8 = u8 + a8 * h
        u_s[pl.ds(off, 8), :] = h8
        return jnp.broadcast_to(h8[7:8, :], (8, width))

    h_last = lax.fori_loop(0, tm // 8, body, hcarry_ref[...])
    hcarry_ref[...] = h_last
    o_ref[0] = u_s[...] * _gelu_tanh(gate)


def _block_diag(w):
    nb, di, do = w.shape
    eye = jnp.eye(nb, dtype=w.dtype)
    return (eye[:, None, :, None] * w[:, :, None, :]).reshape(nb * di, nb * do)


def _lru(p_b, conv_w, conv_b, w_a, b_a, w_x, b_x, lam, tm=512):
    b, t, cols = p_b.shape
    width = cols // 2
    wg = jnp.concatenate([_block_diag(w_a), _block_diag(w_x)], axis=1).astype(BF16)
    row = lambda a_: a_.reshape(1, -1)
    const = lambda shp: pl.BlockSpec(shp, lambda i, j: (0, 0))
    return pl.pallas_call(
        functools.partial(_lru_kernel, width=width),
        out_shape=jax.ShapeDtypeStruct((b, t, width), F32),
        grid=(b, t // tm),
        in_specs=[pl.BlockSpec((1, tm, cols), lambda i, j: (i, j, 0)),
                  const(conv_w.shape), const((1, width)), const((width, 2 * width)),
                  const((1, width)), const((1, width)), const((1, width))],
        out_specs=pl.BlockSpec((1, tm, width), lambda i, j: (i, j, 0)),
        scratch_shapes=[pltpu.VMEM((8, width), F32), pltpu.VMEM((8, width), F32),
                        pltpu.VMEM((tm, width), F32), pltpu.VMEM((tm, width), F32)],
        compiler_params=_cparams(("parallel", "arbitrary")),
        name="rglru",
    )(p_b, conv_w, row(conv_b), wg, row(b_a), row(b_x), row(lam))


def _gla_kernel(q_ref, f_ref, i_ref, g_ref, lb_ref, nw_ref, o_ref, st_ref, *, n_batch, n_heads):
    c = CHUNK

    @pl.when(pl.program_id(0) == 0)
    def _():
        st_ref[...] = jnp.zeros_like(st_ref)

    trilf = jnp.where(_iota((c, c), 1) <= _iota((c, c), 0), 1.0, 0.0)
    tril16 = trilf.astype(BF16)
    mid = c // 2 - 1
    lb = lb_ref[...]

    qm, km, qg, kb, v16, gc = [], [], [], [], [], []
    for bi in range(n_batch):
        q = _silu(q_ref[bi])
        fg = lb + (1.0 - lb) * _sigmoid(f_ref[bi])
        k = 1.0 - fg
        cum = _dot_exact_lhs(tril16, jnp.log(fg))
        cum_c = cum[c - 1:c, :]
        cum_m = cum[mid:mid + 1, :]
        qm.append((q * jnp.exp(cum - cum_m)).astype(BF16))
        km.append((k * jnp.exp(cum_m - cum)).astype(BF16))
        qg.append((q * jnp.exp(cum)).astype(BF16))
        kb.append((k * jnp.exp(cum_c - cum)).astype(BF16))
        v16.append(i_ref[bi].astype(BF16))
        gc.append(jnp.exp(cum_c))

    chains = [(bi, h) for bi in range(n_batch) for h in range(n_heads)]
    idx = range(len(chains))

    def hs(xs):
        return [xs[bi][:, h * LANES:(h + 1) * LANES] for bi, h in chains]

    qm_c, km_c, qg_c, kb_c, v_c, gc_c = hs(qm), hs(km), hs(qg), hs(kb), hs(v16), hs(gc)
    scores = [(_dot_nt(qm_c[i], km_c[i]) * trilf).astype(BF16) for i in idx]
    st = [st_ref[i] for i in idx]
    st16 = [s.astype(BF16) for s in st]
    o = [_dot(scores[i], v_c[i]) + _dot_nt(qg_c[i], st16[i]) for i in idx]
    for i in idx:
        st_ref[i] = st[i] * gc_c[i] + _dot_tn(v_c[i], kb_c[i])

    for i, (bi, h) in enumerate(chains):
        sl = slice(h * LANES, (h + 1) * LANES)
        on = o[i] * lax.rsqrt(jnp.mean(o[i] * o[i], axis=-1, keepdims=True) + RMS_EPS)
        o_ref[bi, :, sl] = on * nw_ref[:, sl] * _silu(g_ref[bi, :, sl])


def _gla(q, f, i, g, lower_bound, norm_w):
    bsz, t, width = q.shape
    n_heads = width // LANES
    tok = pl.BlockSpec((bsz, CHUNK, width), lambda j: (0, j, 0))
    const = pl.BlockSpec((1, width), lambda j: (0, 0))
    return pl.pallas_call(
        functools.partial(_gla_kernel, n_batch=bsz, n_heads=n_heads),
        out_shape=jax.ShapeDtypeStruct((bsz, t, width), F32),
        grid=(t // CHUNK,),
        in_specs=[tok] * 4 + [const, const],
        out_specs=tok,
        scratch_shapes=[pltpu.VMEM((bsz * n_heads, LANES, LANES), F32)],
        compiler_params=_cparams(("arbitrary",)),
        name="hgrn2_gla",
    )(q, f, i, g, lower_bound.reshape(1, -1), norm_w.reshape(1, -1))


def _router_kernel(x_ref, g_ref, w_ref, b_ref, e_o, gate_o, rank_o, cnt_o, run_ref):
    @pl.when(pl.program_id(0) == 0)
    def _():
        run_ref[...] = jnp.zeros_like(run_ref)

    x = x_ref[...]
    ms = jnp.mean(x * x, axis=-1, keepdims=True)
    hn = x * lax.rsqrt(ms + RMS_EPS) * g_ref[...]
    lt = lax.dot_general(w_ref[...], hn, (((1,), (1,)), ((), ())),
                         precision=HIGHEST, preferred_element_type=F32) + b_ref[:, 0:1]
    tm = x.shape[0]
    gl = lt[0:8, :]
    row8 = _iota((8, tm), 0)
    gl = jnp.where(row8 < N_GROUPS, gl, -jnp.inf)
    gmax = jnp.max(gl, axis=0, keepdims=True)
    g_sel = jnp.min(jnp.where(gl == gmax, row8, 8), axis=0, keepdims=True)
    g_gate = 1.0 / jnp.sum(jnp.exp(gl - gmax), axis=0, keepdims=True)

    el = jnp.zeros((EXPERTS_PER_GROUP, tm), F32)
    for gi in range(N_GROUPS):
        lo = 8 + gi * EXPERTS_PER_GROUP
        el = jnp.where(g_sel == gi, lt[lo:lo + EXPERTS_PER_GROUP, :], el)
    m1 = jnp.max(el, axis=0, keepdims=True)
    i1 = jnp.min(jnp.where(el == m1, row8, 8), axis=0, keepdims=True)
    el2 = jnp.where(row8 == i1, -jnp.inf, el)
    m2 = jnp.max(el2, axis=0, keepdims=True)
    i2 = jnp.min(jnp.where(el2 == m2, row8, 8), axis=0, keepdims=True)
    e2 = jnp.exp(m2 - m1)
    inv = 1.0 / (1.0 + e2)
    ea = g_sel * EXPERTS_PER_GROUP + i1
    eb = g_sel * EXPERTS_PER_GROUP + i2
    e_o[...] = jnp.concatenate([ea, eb], axis=0)
    gate_o[...] = jnp.concatenate([g_gate * inv, g_gate * e2 * inv], axis=0)

    erow = _iota((N_EXPERTS, tm), 0)
    oh_a = jnp.where(erow == ea, 1.0, 0.0)
    oh_b = jnp.where(erow == eb, 1.0, 0.0)
    upper = jnp.where(_iota((tm, tm), 0) < _iota((tm, tm), 1), 1.0, 0.0).astype(BF16)
    pre_a = _dot(oh_a.astype(BF16), upper)
    pre_b = _dot(oh_b.astype(BF16), upper)
    cnt_a = jnp.sum(oh_a, axis=1, keepdims=True)
    cnt_b = jnp.sum(oh_b, axis=1, keepdims=True)
    run = run_ref[:, 0:1]
    rank_a = jnp.sum(oh_a * (pre_a + run), axis=0, keepdims=True)
    rank_b = jnp.sum(oh_b * (pre_b + (run + cnt_a)), axis=0, keepdims=True)
    rank_o[...] = jnp.concatenate([rank_a, rank_b], axis=0).astype(I32)
    run_ref[...] = jnp.broadcast_to(run + cnt_a + cnt_b, run_ref.shape)
    cnt_o[...] = run_ref[...]


def _router(x2d, g, w_group, b_group, w_expert, b_expert, tm=512):
    n, d = x2d.shape
    tm = min(tm, n)
    wt = jnp.zeros((LANES, d), F32)
    wt = wt.at[0:N_GROUPS].set(w_group.T).at[8:8 + N_EXPERTS].set(w_expert.T)
    bt = jnp.zeros((LANES,), F32)
    bt = bt.at[0:N_GROUPS].set(b_group).at[8:8 + N_EXPERTS].set(b_expert)
    bt = jnp.broadcast_to(bt[:, None], (LANES, LANES))
    return pl.pallas_call(
        _router_kernel,
        out_shape=[jax.ShapeDtypeStruct((2, n), I32),
                   jax.ShapeDtypeStruct((2, n), F32),
                   jax.ShapeDtypeStruct((2, n), I32),
                   jax.ShapeDtypeStruct((N_EXPERTS, LANES), F32)],
        grid=(n // tm,),
        in_specs=[pl.BlockSpec((tm, d), lambda i: (i, 0)),
                  pl.BlockSpec((1, d), lambda i: (0, 0)),
                  pl.BlockSpec((LANES, d), lambda i: (0, 0)),
                  pl.BlockSpec((LANES, LANES), lambda i: (0, 0))],
        out_specs=[pl.BlockSpec((2, tm), lambda i: (0, i)),
                   pl.BlockSpec((2, tm), lambda i: (0, i)),
                   pl.BlockSpec((2, tm), lambda i: (0, i)),
                   pl.BlockSpec((N_EXPERTS, LANES), lambda i: (0, 0))],
        scratch_shapes=[pltpu.VMEM((N_EXPERTS, LANES), F32)],
        compiler_params=_cparams(("arbitrary",)),
        name="moe_router",
    )(x2d, g.reshape(1, d), wt, bt)


def _row_gather_start(src_hbm, idx_ref, base, buf, sem, n_rows):
    def body(i, carry):
        tok = idx_ref[base + i]
        pltpu.make_async_copy(src_hbm.at[pl.ds(tok, 1), :], buf.at[pl.ds(i, 1), :], sem).start()
        return carry
    lax.fori_loop(0, n_rows, body, 0, unroll=8)


def _row_gather_wait(src_hbm, buf, sem, n_rows):
    pltpu.make_async_copy(src_hbm.at[pl.ds(0, n_rows), :], buf, sem).wait()


def _dispatch_kernel(pos_ref, x_ref, g_ref, xs_in, xs_out, hbuf, sems):
    del xs_in
    j = pl.program_id(0)
    nb = pl.num_programs(0)
    tm = x_ref.shape[0]
    n_tok = nb * tm
    slot = lax.rem(j, 2)

    def wait(s):
        for _ in range(2):
            pltpu.make_async_copy(hbuf.at[s], xs_out.at[pl.ds(0, tm), :], sems.at[s]).wait()

    @pl.when(j >= 2)
    def _():
        wait(slot)

    x = x_ref[...]
    ms = jnp.mean(x * x, axis=-1, keepdims=True)
    hbuf[slot] = x * lax.rsqrt(ms + RMS_EPS) * g_ref[...]

    def body(i, carry):
        src = hbuf.at[slot, pl.ds(i, 1), :]
        pa = pos_ref[j * tm + i]
        pb = pos_ref[n_tok + j * tm + i]
        pltpu.make_async_copy(src, xs_out.at[pl.ds(pa, 1), :], sems.at[slot]).start()
        pltpu.make_async_copy(src, xs_out.at[pl.ds(pb, 1), :], sems.at[slot]).start()
        return carry
    lax.fori_loop(0, tm, body, 0, unroll=8)

    @pl.when(j == nb - 1)
    def _():
        wait(slot)

    @pl.when((j == nb - 1) & (nb >= 2))
    def _():
        wait(1 - slot)


def _dispatch(x2d, g, pos, n_rows, tm=256):
    n, d = x2d.shape
    tm = min(tm, n)
    return pl.pallas_call(
        _dispatch_kernel,
        out_shape=jax.ShapeDtypeStruct((n_rows, d), F32),
        grid_spec=pltpu.PrefetchScalarGridSpec(
            num_scalar_prefetch=1,
            grid=(n // tm,),
            in_specs=[pl.BlockSpec((tm, d), lambda j, ps: (j, 0)),
                      pl.BlockSpec((1, d), lambda j, ps: (0, 0)),
                      pl.BlockSpec(memory_space=pl.ANY)],
            out_specs=pl.BlockSpec(memory_space=pl.ANY),
            scratch_shapes=[pltpu.VMEM((2, tm, d), F32), pltpu.SemaphoreType.DMA((2,))]),
        input_output_aliases={3: 0},
        compiler_params=_cparams(("arbitrary",)),
        name="moe_dispatch",
    )(pos.reshape(-1), x2d, g.reshape(1, d), jnp.zeros((n_rows, d), F32))


def _expert_kernel(be_ref, first_ref, nused_ref, x_ref, wg_ref, wu_ref, wd_ref, o_ref,
                   wg16, wu16, wd16):
    j = pl.program_id(0)

    @pl.when(first_ref[j] == 1)
    def _():
        wg16[...] = wg_ref[0, 0].astype(BF16)
        wu16[...] = wu_ref[0, 0].astype(BF16)
        wd16[...] = wd_ref[0, 0].astype(BF16)

    @pl.when(j < nused_ref[0])
    def _():
        x = x_ref[...].astype(BF16)
        hg = _dot(x, wg16[...])
        hu = _dot(x, wu16[...])
        hid = (_silu(hg) * hu).astype(BF16)
        o_ref[...] = _dot(hid, wd16[...])

    @pl.when(j >= nused_ref[0])
    def _():
        o_ref[...] = jnp.zeros_like(o_ref)


def _expert_ffn(xs, block_e, first_flag, n_used, w_gate, w_up, w_down, layer, rows):
    n_rows, d = xs.shape
    n_blocks = n_rows // rows
    ff = w_gate.shape[-1]
    wspec = lambda shp: pl.BlockSpec((1, 1) + shp, lambda j, be, fi, nu: (layer, be[j], 0, 0))
    return pl.pallas_call(
        _expert_kernel,
        out_shape=jax.ShapeDtypeStruct((n_rows, d), F32),
        grid_spec=pltpu.PrefetchScalarGridSpec(
            num_scalar_prefetch=3,
            grid=(n_blocks,),
            in_specs=[pl.BlockSpec((rows, d), lambda j, be, fi, nu: (j, 0)),
                      wspec((d, ff)), wspec((d, ff)), wspec((ff, d))],
            out_specs=pl.BlockSpec((rows, d), lambda j, be, fi, nu: (j, 0)),
            scratch_shapes=[pltpu.VMEM((d, ff), BF16), pltpu.VMEM((d, ff), BF16),
                            pltpu.VMEM((ff, d), BF16)]),
        compiler_params=_cparams(("arbitrary",)),
        name="moe_expert_ffn",
    )(block_e, first_flag, n_used, xs, w_gate, w_up, w_down)


def _combine_kernel(pos_ref, x_ref, gt_ref, ys_hbm, gfin_ref, o_ref, buf, sems, *, final_norm):
    j = pl.program_id(0)
    nb = pl.num_programs(0)
    tm = x_ref.shape[0]
    n_tok = nb * tm
    slot = lax.rem(j, 2)

    def start(step, s):
        _row_gather_start(ys_hbm, pos_ref, step * tm, buf.at[s, 0], sems.at[s, 0], tm)
        _row_gather_start(ys_hbm, pos_ref, n_tok + step * tm, buf.at[s, 1], sems.at[s, 1], tm)

    @pl.when(j == 0)
    def _():
        start(0, 0)

    @pl.when(j + 1 < nb)
    def _():
        start(j + 1, 1 - slot)

    _row_gather_wait(ys_hbm, buf.at[slot, 0], sems.at[slot, 0], tm)
    _row_gather_wait(ys_hbm, buf.at[slot, 1], sems.at[slot, 1], tm)
    gt = gt_ref[...]
    y = x_ref[...] + (gt[:, 0:1] * buf[slot, 0] + gt[:, 1:2] * buf[slot, 1])
    if final_norm:
        ms = jnp.mean(y * y, axis=-1, keepdims=True)
        y = y * lax.rsqrt(ms + RMS_EPS) * gfin_ref[...]
    o_ref[...] = y


def _combine(x2d, ys, pos, gates_t, g_final, final_norm, tm=256):
    n, d = x2d.shape
    tm = min(tm, n)
    return pl.pallas_call(
        functools.partial(_combine_kernel, final_norm=final_norm),
        out_shape=jax.ShapeDtypeStruct((n, d), F32),
        grid_spec=pltpu.PrefetchScalarGridSpec(
            num_scalar_prefetch=1,
            grid=(n // tm,),
            in_specs=[pl.BlockSpec((tm, d), lambda j, ps: (j, 0)),
                      pl.BlockSpec((tm, 2), lambda j, ps: (j, 0)),
                      pl.BlockSpec(memory_space=pl.ANY),
                      pl.BlockSpec((1, d), lambda j, ps: (0, 0))],
            out_specs=pl.BlockSpec((tm, d), lambda j, ps: (j, 0)),
            scratch_shapes=[pltpu.VMEM((2, 2, tm, d), F32),
                            pltpu.SemaphoreType.DMA((2, 2))]),
        compiler_params=_cparams(("arbitrary",)),
        name="moe_combine",
    )(pos.reshape(-1), x2d, gates_t, ys, g_final.reshape(1, d))


def _route_plan(e2, rank2, counts, rows):
    n_tok = e2.shape[1]
    padded = (counts + rows - 1) // rows * rows
    pend = jnp.cumsum(padded)
    pstart = pend - padded
    pos = pstart[e2] + rank2
    n_blocks = 2 * n_tok // rows + N_EXPERTS
    starts = jnp.arange(n_blocks, dtype=I32) * rows
    block_e = jnp.clip(jnp.searchsorted(pend, starts, side='right'), 0, N_EXPERTS - 1).astype(I32)
    first = jnp.concatenate([jnp.ones((1,), I32),
                             (block_e[1:] != block_e[:-1]).astype(I32)])
    n_used = (pend[-1] // rows).astype(I32).reshape(1)
    return pos.astype(I32), block_e, first, n_used, n_blocks * rows


def _moe(x2d, g_ffn, w_group, b_group, w_expert, b_expert, w_gate, w_up, w_down, layer,
         g_final, final_norm, rows=ROUTE_ROWS):
    e2, gates2, rank2, cnt = _router(x2d, g_ffn, w_group, b_group, w_expert, b_expert)
    pos, block_e, first, n_used, n_rows = _route_plan(e2, rank2, cnt[:, 0].astype(I32), rows)
    xs = _dispatch(x2d, g_ffn, pos, n_rows)
    ys = _expert_ffn(xs, block_e, first, n_used, w_gate, w_up, w_down, layer, rows)
    return _combine(x2d, ys, pos, gates2.T, g_final, final_norm)


def kernel(x, norm_mix, norm_ffn, norm_final, ab_w_in, rw_mu, rw_w0, rw_w2, rw_a0, rw_a2, rw_g2, rw_k_k, rw_k_a, rw_r_k, rw_ln_w, rw_ln_b, lru_conv_w, lru_conv_b, lru_w_a, lru_b_a, lru_w_x, lru_b_x, lru_lambda, ab_w_out, c_w_in, c_lower_bound, c_norm_w, c_w_out, moe_w_group, moe_b_group, moe_w_expert, moe_b_expert, moe_w_gate, moe_w_up, moe_w_down):
    bsz, t, d = x.shape
    depth = norm_mix.shape[0]
    n = bsz * t
    lbs = jnp.cumsum(jax.nn.softmax(c_lower_bound.astype(F32), axis=0), axis=0)
    lbs = lbs - lbs[0]
    x2d = x.reshape(n, d)
    for layer in range(depth):
        j = layer // 2
        if layer % 2 == 0:
            rw_cols = rw_mu.shape[1]
            lru_cols = ab_w_in.shape[2] - rw_cols
            width = rw_w0.shape[1]
            p_a, p_b = _norm_matmul(x2d, norm_mix[layer], ab_w_in[j], (rw_cols, lru_cols))
            prep = _rwkv_prep(p_a.reshape(bsz, t, rw_cols), rw_mu[j], rw_w0[j], rw_w2[j],
                              rw_a0[j], rw_a2[j], rw_g2[j], rw_k_k[j], rw_k_a[j],
                              rw_r_k[j].reshape(-1))
            ya = _rwkv_scan(*prep, rw_ln_w[j], rw_ln_b[j])
            yb = _lru(p_b.reshape(bsz, t, lru_cols), lru_conv_w[j], lru_conv_b[j], lru_w_a[j],
                      lru_b_a[j], lru_w_x[j], lru_b_x[j], lru_lambda[j])
            x2d = _proj_residual(x2d, [ya.reshape(n, width), yb.reshape(n, -1)],
                                 [ab_w_out[j][:width], ab_w_out[j][width:]])
        else:
            hw = c_norm_w.shape[1]
            q, f, i_, g = _norm_matmul(x2d, norm_mix[layer], c_w_in[j], (hw,) * 4)
            shp = (bsz, t, hw)
            o = _gla(q.reshape(shp), f.reshape(shp), i_.reshape(shp), g.reshape(shp),
                     lbs[layer], c_norm_w[j])
            x2d = _proj_residual(x2d, [o.reshape(n, hw)], [c_w_out[j]])
        last = layer == depth - 1
        x2d = _moe(x2d, norm_ffn[layer], moe_w_group[layer], moe_b_group[layer],
                   moe_w_expert[layer], moe_b_expert[layer], moe_w_gate, moe_w_up,
                   moe_w_down, layer, norm_final, last)
    return x2d.reshape(bsz, t, d)
```

```python
import functools

import jax
import jax.numpy as jnp
from jax import lax
from jax.experimental import pallas as pl
from jax.experimental.pallas import tpu as pltpu

F32 = jnp.float32
BF16 = jnp.bfloat16
I32 = jnp.int32

RMS_EPS = 1e-6
RWKV_GN_EPS = 64e-5
LRU_C = 8.0
CHUNK = 64
CHUNK_SHIFT = CHUNK.bit_length() - 1
HEAD64 = 64
LANES = 128
SUBLANES = 8
N_GROUPS = 4
EXPERTS_PER_GROUP = 8
N_EXPERTS = N_GROUPS * EXPERTS_PER_GROUP
ROUTE_ROWS = 256
ROUTE_TILE = 512
VMEM_LIMIT = 56 * 1024 * 1024
HIGHEST = lax.Precision.HIGHEST


def _cparams(sem):
    return pltpu.CompilerParams(dimension_semantics=sem, vmem_limit_bytes=VMEM_LIMIT)


def _sigmoid(x):
    return 1.0 / (1.0 + jnp.exp(-x))


def _softplus(x):
    return jnp.maximum(x, 0.0) + jnp.log(1.0 + jnp.exp(-jnp.abs(x)))


def _silu(x):
    return x * _sigmoid(x)


def _gelu_tanh(x):
    return 0.5 * x * (1.0 + jnp.tanh(0.7978845608028654 * (x + 0.044715 * x * x * x)))


def _dot(a, b):
    return jnp.dot(a, b, preferred_element_type=F32)


def _dot_nt(a, b):
    return lax.dot_general(a, b, (((1,), (1,)), ((), ())), preferred_element_type=F32)


def _dot_tn(a, b):
    return lax.dot_general(a, b, (((0,), (0,)), ((), ())), preferred_element_type=F32)


def _split3(x):
    h1 = x.astype(BF16)
    r1 = x - h1.astype(F32)
    h2 = r1.astype(BF16)
    r2 = r1 - h2.astype(F32)
    return h1, h2, r2.astype(BF16)


def _dot_exact_rhs(x, m_bf16):
    h1, h2, h3 = _split3(x)
    return _dot(h1, m_bf16) + _dot(h2, m_bf16) + _dot(h3, m_bf16)


def _dot_exact_lhs(m_bf16, x):
    h1, h2, h3 = _split3(x)
    return _dot(m_bf16, h1) + _dot(m_bf16, h2) + _dot(m_bf16, h3)


def _iota(shape, dim):
    return lax.broadcasted_iota(I32, shape, dim)


def _norm_matmul_kernel(x_ref, g_ref, w_ref, *o_refs, splits):
    x = x_ref[...]
    ms = jnp.mean(x * x, axis=-1, keepdims=True)
    y = (x * lax.rsqrt(ms + RMS_EPS) * g_ref[...]).astype(BF16)
    off = 0
    for o_ref, n in zip(o_refs, splits):
        o_ref[...] = _dot(y, w_ref[:, off:off + n])
        off += n


def _norm_matmul(x2d, g, w, splits, tm=256):
    n, d = x2d.shape
    ncols = w.shape[1]
    assert sum(splits) == ncols and n % tm == 0
    return pl.pallas_call(
        functools.partial(_norm_matmul_kernel, splits=splits),
        out_shape=[jax.ShapeDtypeStruct((n, s), F32) for s in splits],
        grid=(n // tm,),
        in_specs=[pl.BlockSpec((tm, d), lambda i: (i, 0)),
                  pl.BlockSpec((1, d), lambda i: (0, 0)),
                  pl.BlockSpec((d, ncols), lambda i: (0, 0))],
        out_specs=[pl.BlockSpec((tm, s), lambda i: (i, 0)) for s in splits],
        compiler_params=_cparams(("parallel",)),
        name="norm_matmul",
    )(x2d, g.reshape(1, d), w.astype(BF16))


def _proj_residual_kernel(*refs, n_in):
    x_ref = refs[0]
    y_refs = refs[1:1 + n_in]
    w_refs = refs[1 + n_in:1 + 2 * n_in]
    o_ref = refs[1 + 2 * n_in]
    acc = x_ref[...]
    for y_ref, w_ref in zip(y_refs, w_refs):
        acc = acc + _dot(y_ref[...].astype(BF16), w_ref[...])
    o_ref[...] = acc


def _proj_residual(x2d, ys, ws, tm=512):
    n, d = x2d.shape
    n_in = len(ys)
    in_specs = [pl.BlockSpec((tm, d), lambda i: (i, 0))]
    in_specs += [pl.BlockSpec((tm, y.shape[1]), lambda i: (i, 0)) for y in ys]
    in_specs += [pl.BlockSpec(w.shape, lambda i: (0, 0)) for w in ws]
    return pl.pallas_call(
        functools.partial(_proj_residual_kernel, n_in=n_in),
        out_shape=jax.ShapeDtypeStruct((n, d), F32),
        grid=(n // tm,),
        in_specs=in_specs,
        out_specs=pl.BlockSpec((tm, d), lambda i: (i, 0)),
        compiler_params=_cparams(("parallel",)),
        name="proj_residual",
    )(x2d, *ys, *[w.astype(BF16) for w in ws])


def _rwkv_prep_kernel(p_ref, mu_ref, w0_ref, a0_ref, kk_s_ref, ka_ref, rk_ref,
                      wcomb_ref, g2_ref,
                      rt_o, kkt_o, kh_o, bh_o, kb_o, bb_o, v_o, g_o, bonus_o, gc_o,
                      prev_ref, *, width):
    tm = p_ref.shape[1]

    @pl.when(pl.program_id(1) == 0)
    def _():
        prev_ref[...] = jnp.zeros_like(prev_ref)

    p = p_ref[0]
    rolled = pltpu.roll(p, 1, axis=0)
    prev = jnp.where(_iota(p.shape, 0) == 0, prev_ref[...], rolled)
    prev_ref[...] = p[tm - 1:tm, :]
    ps = p + (prev - p) * mu_ref[...]

    r = ps[:, 0:width]
    k = ps[:, width:2 * width]
    v = ps[:, 2 * width:3 * width]
    lowrank = ps[:, 3 * width:3 * width + LANES]
    gl = ps[:, 3 * width + LANES:3 * width + 2 * LANES]

    lane = _iota(lowrank.shape, 1)
    lr_in = jnp.where(lane < HEAD64, jnp.tanh(lowrank), lowrank)
    t12 = jnp.dot(lr_in, wcomb_ref[...], precision=HIGHEST, preferred_element_type=F32)
    wlog = -_softplus(-(w0_ref[...] + t12[:, :width])) - 0.5
    lw = -jnp.exp(wlog)
    a = _sigmoid(a0_ref[...] + t12[:, width:])
    g = jnp.dot(_sigmoid(gl), g2_ref[...], precision=HIGHEST, preferred_element_type=F32)

    ri = _iota((width, width), 0) >> 6
    ci = _iota((width, width), 1) >> 6
    seg = jnp.where(ri == ci, 1.0, 0.0).astype(BF16)

    kk = k * kk_s_ref[...]
    nrm = jnp.sqrt(_dot_exact_rhs(kk * kk, seg))
    kk = kk / jnp.maximum(nrm, 1e-12)
    k2 = k * (1.0 + (a - 1.0) * ka_ref[...])
    bonus = _dot_exact_rhs(r * k2 * rk_ref[...], seg) * v

    b = kk * a

    n_chunk = tm // CHUNK
    rr = _iota((tm, tm), 0)
    cc = _iota((tm, tm), 1)
    tril_bd = jnp.where(((rr >> CHUNK_SHIFT) == (cc >> CHUNK_SHIFT)) & (cc <= rr),
                        1.0, 0.0).astype(BF16)
    cum = _dot_exact_lhs(tril_bd, lw)
    cum3 = cum.reshape(n_chunk, CHUNK, width)
    cend = cum3[:, CHUNK - 1:CHUNK, :]
    e_neg = jnp.exp(-cum)
    e_end = jnp.exp(cend - cum3).reshape(tm, width)

    rt_o[0] = (r * jnp.exp(cum)).astype(BF16)
    kkt_o[0] = (kk * jnp.exp(cum - lw)).astype(BF16)
    kh_o[0] = (k2 * e_neg).astype(BF16)
    bh_o[0] = (b * e_neg).astype(BF16)
    kb_o[0] = (k2 * e_end).astype(BF16)
    bb_o[0] = (b * e_end).astype(BF16)
    v_o[0] = v.astype(BF16)
    g_o[0] = g
    bonus_o[0] = bonus
    gc_o[0] = jnp.exp(cend)


def _rwkv_prep(p_a, mu, w0, w2, a0, a2, g2, k_k, k_a, r_k, tm=512):
    b, t, cols = p_a.shape
    width = w0.shape[0]
    rank = w2.shape[0]
    assert rank == HEAD64 and a2.shape[0] == HEAD64 and g2.shape[0] == LANES
    zeros = jnp.zeros((rank, width), F32)
    wcomb = jnp.concatenate([jnp.concatenate([w2, zeros], 1),
                             jnp.concatenate([zeros, a2], 1)], 0)
    row = lambda a_: a_.reshape(1, -1)
    const = lambda shp: pl.BlockSpec(shp, lambda i, j: (0, 0))
    tok = pl.BlockSpec((1, tm, width), lambda i, j: (i, j, 0))
    n_chunk = tm // CHUNK
    return pl.pallas_call(
        functools.partial(_rwkv_prep_kernel, width=width),
        out_shape=([jax.ShapeDtypeStruct((b, t, width), BF16)] * 7
                   + [jax.ShapeDtypeStruct((b, t, width), F32)] * 2
                   + [jax.ShapeDtypeStruct((b, t // CHUNK, 1, width), F32)]),
        grid=(b, t // tm),
        in_specs=[pl.BlockSpec((1, tm, cols), lambda i, j: (i, j, 0)),
                  const((1, cols)), const((1, width)), const((1, width)), const((1, width)),
                  const((1, width)), const((1, width)),
                  const((LANES, 2 * width)), const((LANES, width))],
        out_specs=[tok] * 9 + [pl.BlockSpec((1, n_chunk, 1, width), lambda i, j: (i, j, 0, 0))],
        scratch_shapes=[pltpu.VMEM((1, cols), F32)],
        compiler_params=_cparams(("parallel", "arbitrary")),
        name="rwkv_prep",
    )(p_a, row(mu), row(w0), row(a0), row(k_k), row(k_a), row(r_k), wcomb, g2)


def _rwkv_scan_kernel(rt_ref, kkt_ref, kh_ref, bh_ref, kb_ref, bb_ref, v_ref, g_ref, bonus_ref,
                      gc_ref, lnw_ref, lnb_ref, o_ref, st_ref, *, n_batch, n_pairs):
    c = CHUNK

    @pl.when(pl.program_id(0) == 0)
    def _():
        st_ref[...] = jnp.zeros_like(st_ref)

    rr = _iota((LANES, LANES), 0)
    cc = _iota((LANES, LANES), 1)
    mask_bd = jnp.where((rr >> 6) == (cc >> 6), 1.0, 0.0)
    mask_bd16 = mask_bd.astype(BF16)
    t_i = _iota((c, LANES), 0)
    s_i = _iota((c, LANES), 1) & (c - 1)
    strict = jnp.where(s_i < t_i, 1.0, 0.0)
    incl = jnp.where(s_i <= t_i, 1.0, 0.0)
    eye_ss = jnp.where(s_i == t_i, 1.0, 0.0)

    def bd16(x):
        return jnp.concatenate([x, x], axis=0) * mask_bd16

    def b16(xs):
        return [x.astype(BF16) for x in xs]

    chains = [(bi, p) for bi in range(n_batch) for p in range(n_pairs)]
    n_ch = len(chains)
    idx = range(n_ch)

    def ld(ref):
        return [ref[bi, :, p * LANES:(p + 1) * LANES] for bi, p in chains]

    rt, kkt, kh, bh, kb, bb, v = (ld(ref) for ref in
                                  (rt_ref, kkt_ref, kh_ref, bh_ref, kb_ref, bb_ref, v_ref))

    lhs = [jnp.concatenate([kkt[i], rt[i]], axis=0) for i in idx]
    rhs = [jnp.concatenate([bd16(kh[i]), bd16(bh[i])], axis=0) for i in idx]
    gm = [_dot_nt(lhs[i], rhs[i]) for i in idx]
    l_kv = [gm[i][:c, :LANES] * strict for i in idx]
    xp = [-(gm[i][:c, LANES:] * strict) for i in idx]
    pm = [gm[i][c:, :LANES] * incl for i in idx]
    nqm = [-(gm[i][c:, LANES:] * incl) for i in idx]

    tinv = [eye_ss + xp[i] for i in idx]
    xp16 = b16(xp)
    xpb = [bd16(x) for x in xp16]
    for _ in range(c.bit_length() - 2):
        xp16 = b16([_dot(xp16[i], xpb[i]) for i in idx])
        xpb = [bd16(x) for x in xp16]
        t16 = b16(tinv)
        tinv = [tinv[i] + _dot(t16[i], xpb[i]) for i in idx]
    tinv16 = b16(tinv)

    v_bd = [bd16(x) for x in v]
    l_kv16 = b16(l_kv)
    w16 = b16([_dot(l_kv16[i], v_bd[i]) for i in idx])
    tkw = [_dot(tinv16[i], jnp.concatenate([bd16(kkt[i]), bd16(w16[i])], axis=1)) for i in idx]
    tk16 = b16([x[:, :LANES] for x in tkw])

    st = [st_ref[i] for i in idx]
    st16 = b16(st)
    u = [_dot_nt(tk16[i], st16[i]) + tkw[i][:, LANES:] for i in idx]
    u16 = b16(u)
    pq16 = [jnp.concatenate([pm[i], nqm[i]], axis=1).astype(BF16) for i in idx]
    vu = [jnp.concatenate([v_bd[i], bd16(u16[i])], axis=0) for i in idx]
    y = [_dot_nt(rt[i], st16[i]) + _dot(pq16[i], vu[i]) for i in idx]
    upd = [_dot_tn(jnp.concatenate([v[i], -u16[i]], axis=0),
                   jnp.concatenate([kb[i], bb[i]], axis=0)) for i in idx]
    for i, (bi, p) in enumerate(chains):
        st_ref[i] = st[i] * gc_ref[bi, 0, :, p * LANES:(p + 1) * LANES] + upd[i] * mask_bd

    ys = jnp.concatenate(y, axis=0)
    mean = _dot_exact_rhs(ys, mask_bd16) * (1.0 / HEAD64)
    d = ys - mean
    var = _dot_exact_rhs(d * d, mask_bd16) * (1.0 / HEAD64)
    dn = d * lax.rsqrt(var + RWKV_GN_EPS)
    for i, (bi, p) in enumerate(chains):
        sl = slice(p * LANES, (p + 1) * LANES)
        yn = dn[i * c:(i + 1) * c] * lnw_ref[:, sl] + lnb_ref[:, sl]
        o_ref[bi, :, sl] = (yn + bonus_ref[bi, :, sl]) * g_ref[bi, :, sl]


def _rwkv_scan(rt, kkt, kh, bh, kb, bb, v, g, bonus, gc, ln_w, ln_b):
    bsz, t, width = rt.shape
    n_pairs = width // LANES
    tok = pl.BlockSpec((bsz, CHUNK, width), lambda j: (0, j, 0))
    const = pl.BlockSpec((1, width), lambda j: (0, 0))
    return pl.pallas_call(
        functools.partial(_rwkv_scan_kernel, n_batch=bsz, n_pairs=n_pairs),
        out_shape=jax.ShapeDtypeStruct((bsz, t, width), F32),
        grid=(t // CHUNK,),
        in_specs=[tok] * 9 + [pl.BlockSpec((bsz, 1, 1, width), lambda j: (0, j, 0, 0)),
                              const, const],
        out_specs=tok,
        scratch_shapes=[pltpu.VMEM((bsz * n_pairs, LANES, LANES), F32)],
        compiler_params=_cparams(("arbitrary",)),
        name="rwkv_scan",
    )(rt, kkt, kh, bh, kb, bb, v, g, bonus, gc, ln_w.reshape(1, -1), ln_b.reshape(1, -1))


def _lru_kernel(p_ref, cw_ref, cb_ref, wg_ref, ba_ref, bx_ref, lam_ref, o_ref,
                xcarry_ref, hcarry_ref, a_s, u_s, *, width):
    tm = p_ref.shape[1]

    @pl.when(pl.program_id(1) == 0)
    def _():
        xcarry_ref[...] = jnp.zeros_like(xcarry_ref)
        hcarry_ref[...] = jnp.zeros_like(hcarry_ref)

    gate = p_ref[0, :, 0:width]
    xb = p_ref[0, :, width:2 * width]
    carry8 = xcarry_ref[...]
    row8 = _iota((8, width), 0)

    def shifted(s):
        rolled = pltpu.roll(xb, s, axis=0)
        first = jnp.where(row8 < s, pltpu.roll(carry8, s, axis=0), rolled[0:8])
        return jnp.concatenate([first, rolled[8:]], axis=0)

    xc = (cw_ref[0:1, :] * shifted(3) + cw_ref[1:2, :] * shifted(2)
          + cw_ref[2:3, :] * shifted(1) + cw_ref[3:4, :] * xb + cb_ref[...])
    xcarry_ref[...] = xb[tm - 8:tm, :]

    gates = _dot(xc.astype(BF16), wg_ref[...])
    rg = _sigmoid(gates[:, :width] + ba_ref[...])
    ig = _sigmoid(gates[:, width:] + bx_ref[...])
    log_a = -LRU_C * rg * _softplus(-lam_ref[...])
    a = jnp.exp(log_a)
    a_s[...] = a
    u_s[...] = jnp.sqrt(1.0 - a * a) * ig * xc

    m1 = row8 >= 1
    m2 = row8 >= 2
    m4 = row8 >= 4

    def body(i, h):
        off = pl.multiple_of(i * 8, 8)
        a8 = a_s[pl.ds(off, 8), :]
        u8 = u_s[pl.ds(off, 8), :]
        for s, m in ((1, m1), (2, m2), (4, m4)):
            u_sh = jnp.where(m, pltpu.roll(u8, s, axis=0), 0.0)
            a_sh = jnp.where(m, pltpu.roll(a8, s, axis=0), 1.0)
            u8 = u8 + a8 * u_sh
            a8 = a8 * a_sh
        h8 = u8 + a8 * h
        u_s[pl.ds(off, 8), :] = h8
        return jnp.broadcast_to(h8[7:8, :], (8, width))

    h_last = lax.fori_loop(0, tm // 8, body, hcarry_ref[...])
    hcarry_ref[...] = h_last
    o_ref[0] = u_s[...] * _gelu_tanh(gate)


def _block_diag(w):
    nb, di, do = w.shape
    eye = jnp.eye(nb, dtype=w.dtype)
    return (eye[:, None, :, None] * w[:, :, None, :]).reshape(nb * di, nb * do)


def _lru(p_b, conv_w, conv_b, w_a, b_a, w_x, b_x, lam, tm=512):
    b, t, cols = p_b.shape
    width = cols // 2
    wg = jnp.concatenate([_block_diag(w_a), _block_diag(w_x)], axis=1).astype(BF16)
    row = lambda a_: a_.reshape(1, -1)
    const = lambda shp: pl.BlockSpec(shp, lambda i, j: (0, 0))
    return pl.pallas_call(
        functools.partial(_lru_kernel, width=width),
        out_shape=jax.ShapeDtypeStruct((b, t, width), F32),
        grid=(b, t // tm),
        in_specs=[pl.BlockSpec((1, tm, cols), lambda i, j: (i, j, 0)),
                  const(conv_w.shape), const((1, width)), const((width, 2 * width)),
                  const((1, width)), const((1, width)), const((1, width))],
        out_specs=pl.BlockSpec((1, tm, width), lambda i, j: (i, j, 0)),
        scratch_shapes=[pltpu.VMEM((8, width), F32), pltpu.VMEM((8, width), F32),
                        pltpu.VMEM((tm, width), F32), pltpu.VMEM((tm, width), F32)],
        compiler_params=_cparams(("parallel", "arbitrary")),
        name="rglru",
    )(p_b, conv_w, row(conv_b), wg, row(b_a), row(b_x), row(lam))


def _gla_kernel(q_ref, f_ref, i_ref, g_ref, lb_ref, nw_ref, o_ref, st_ref, *, n_batch, n_heads):
    c = CHUNK

    @pl.when(pl.program_id(0) == 0)
    def _():
        st_ref[...] = jnp.zeros_like(st_ref)

    trilf = jnp.where(_iota((c, c), 1) <= _iota((c, c), 0), 1.0, 0.0)
    tril16 = trilf.astype(BF16)
    mid = c // 2 - 1
    lb = lb_ref[...]

    qm, km, qg, kb, v16, gc = [], [], [], [], [], []
    for bi in range(n_batch):
        q = _silu(q_ref[bi])
        fg = lb + (1.0 - lb) * _sigmoid(f_ref[bi])
        k = 1.0 - fg
        cum = _dot_exact_lhs(tril16, jnp.log(fg))
        cum_c = cum[c - 1:c, :]
        cum_m = cum[mid:mid + 1, :]
        qm.append((q * jnp.exp(cum - cum_m)).astype(BF16))
        km.append((k * jnp.exp(cum_m - cum)).astype(BF16))
        qg.append((q * jnp.exp(cum)).astype(BF16))
        kb.append((k * jnp.exp(cum_c - cum)).astype(BF16))
        v16.append(i_ref[bi].astype(BF16))
        gc.append(jnp.exp(cum_c))

    chains = [(bi, h) for bi in range(n_batch) for h in range(n_heads)]
    idx = range(len(chains))

    def hs(xs):
        return [xs[bi][:, h * LANES:(h + 1) * LANES] for bi, h in chains]

    qm_c, km_c, qg_c, kb_c, v_c, gc_c = hs(qm), hs(km), hs(qg), hs(kb), hs(v16), hs(gc)
    scores = [(_dot_nt(qm_c[i], km_c[i]) * trilf).astype(BF16) for i in idx]
    st = [st_ref[i] for i in idx]
    st16 = [s.astype(BF16) for s in st]
    o = [_dot(scores[i], v_c[i]) + _dot_nt(qg_c[i], st16[i]) for i in idx]
    for i in idx:
        st_ref[i] = st[i] * gc_c[i] + _dot_tn(v_c[i], kb_c[i])

    for i, (bi, h) in enumerate(chains):
        sl = slice(h * LANES, (h + 1) * LANES)
        on = o[i] * lax.rsqrt(jnp.mean(o[i] * o[i], axis=-1, keepdims=True) + RMS_EPS)
        o_ref[bi, :, sl] = on * nw_ref[:, sl] * _silu(g_ref[bi, :, sl])


def _gla(q, f, i, g, lower_bound, norm_w):
    bsz, t, width = q.shape
    n_heads = width // LANES
    tok = pl.BlockSpec((bsz, CHUNK, width), lambda j: (0, j, 0))
    const = pl.BlockSpec((1, width), lambda j: (0, 0))
    return pl.pallas_call(
        functools.partial(_gla_kernel, n_batch=bsz, n_heads=n_heads),
        out_shape=jax.ShapeDtypeStruct((bsz, t, width), F32),
        grid=(t // CHUNK,),
        in_specs=[tok] * 4 + [const, const],
        out_specs=tok,
        scratch_shapes=[pltpu.VMEM((bsz * n_heads, LANES, LANES), F32)],
        compiler_params=_cparams(("arbitrary",)),
        name="hgrn2_gla",
    )(q, f, i, g, lower_bound.reshape(1, -1), norm_w.reshape(1, -1))


def _router_kernel(x_ref, g_ref, w_ref, b_ref, e_o, gate_o, rank_o, cnt_o):
    x = x_ref[...]
    ms = jnp.mean(x * x, axis=-1, keepdims=True)
    hn = x * lax.rsqrt(ms + RMS_EPS) * g_ref[...]
    lt = lax.dot_general(w_ref[...], hn, (((1,), (1,)), ((), ())),
                         precision=HIGHEST, preferred_element_type=F32) + b_ref[:, 0:1]
    tm = x.shape[0]
    gl = lt[0:8, :]
    row8 = _iota((8, tm), 0)
    gl = jnp.where(row8 < N_GROUPS, gl, -jnp.inf)
    gmax = jnp.max(gl, axis=0, keepdims=True)
    g_sel = jnp.min(jnp.where(gl == gmax, row8, 8), axis=0, keepdims=True)
    g_gate = 1.0 / jnp.sum(jnp.exp(gl - gmax), axis=0, keepdims=True)

    el = jnp.zeros((EXPERTS_PER_GROUP, tm), F32)
    for gi in range(N_GROUPS):
        lo = 8 + gi * EXPERTS_PER_GROUP
        el = jnp.where(g_sel == gi, lt[lo:lo + EXPERTS_PER_GROUP, :], el)
    m1 = jnp.max(el, axis=0, keepdims=True)
    i1 = jnp.min(jnp.where(el == m1, row8, 8), axis=0, keepdims=True)
    el2 = jnp.where(row8 == i1, -jnp.inf, el)
    m2 = jnp.max(el2, axis=0, keepdims=True)
    i2 = jnp.min(jnp.where(el2 == m2, row8, 8), axis=0, keepdims=True)
    e2 = jnp.exp(m2 - m1)
    inv = 1.0 / (1.0 + e2)
    ea = g_sel * EXPERTS_PER_GROUP + i1
    eb = g_sel * EXPERTS_PER_GROUP + i2
    e_o[...] = jnp.concatenate([ea, eb], axis=0)
    gate_o[...] = jnp.concatenate([g_gate * inv, g_gate * e2 * inv], axis=0)

    erow = _iota((N_EXPERTS, tm), 0)
    oh_a = jnp.where(erow == ea, 1.0, 0.0)
    oh_b = jnp.where(erow == eb, 1.0, 0.0)
    upper = jnp.where(_iota((tm, tm), 0) < _iota((tm, tm), 1), 1.0, 0.0).astype(BF16)
    pre_a = _dot(oh_a.astype(BF16), upper)
    pre_b = _dot(oh_b.astype(BF16), upper)
    cnt_a = jnp.sum(oh_a, axis=1, keepdims=True)
    cnt_b = jnp.sum(oh_b, axis=1, keepdims=True)
    rank_a = jnp.sum(oh_a * pre_a, axis=0, keepdims=True)
    rank_b = jnp.sum(oh_b * (pre_b + cnt_a), axis=0, keepdims=True)
    rank_o[...] = jnp.concatenate([rank_a, rank_b], axis=0).astype(I32)
    cnt_o[0] = jnp.broadcast_to(cnt_a + cnt_b, (N_EXPERTS, LANES))


def _router(x2d, g, w_group, b_group, w_expert, b_expert, tm):
    n, d = x2d.shape
    wt = jnp.zeros((LANES, d), F32)
    wt = wt.at[0:N_GROUPS].set(w_group.T).at[8:8 + N_EXPERTS].set(w_expert.T)
    bt = jnp.zeros((LANES,), F32)
    bt = bt.at[0:N_GROUPS].set(b_group).at[8:8 + N_EXPERTS].set(b_expert)
    bt = jnp.broadcast_to(bt[:, None], (LANES, LANES))
    return pl.pallas_call(
        _router_kernel,
        out_shape=[jax.ShapeDtypeStruct((2, n), I32),
                   jax.ShapeDtypeStruct((2, n), F32),
                   jax.ShapeDtypeStruct((2, n), I32),
                   jax.ShapeDtypeStruct((n // tm, N_EXPERTS, LANES), F32)],
        grid=(n // tm,),
        in_specs=[pl.BlockSpec((tm, d), lambda i: (i, 0)),
                  pl.BlockSpec((1, d), lambda i: (0, 0)),
                  pl.BlockSpec((LANES, d), lambda i: (0, 0)),
                  pl.BlockSpec((LANES, LANES), lambda i: (0, 0))],
        out_specs=[pl.BlockSpec((2, tm), lambda i: (0, i)),
                   pl.BlockSpec((2, tm), lambda i: (0, i)),
                   pl.BlockSpec((2, tm), lambda i: (0, i)),
                   pl.BlockSpec((1, N_EXPERTS, LANES), lambda i: (i, 0, 0))],
        compiler_params=_cparams(("parallel",)),
        name="moe_router",
    )(x2d, g.reshape(1, d), wt, bt)


def _seg_local_rows(tm):
    return 2 * tm + N_EXPERTS * SUBLANES


def _segment_copies(cnt_ref, off_ref, dst_ref, base, local, remote, sem, to_remote, wait):
    def body(e, carry):
        c = pl.multiple_of(cnt_ref[base + e], SUBLANES)

        @pl.when(c > 0)
        def _():
            off = 0 if off_ref is None else pl.multiple_of(off_ref[base + e], SUBLANES)
            loc = local.at[pl.ds(off, c), :]
            rem = remote.at[pl.ds(pl.multiple_of(dst_ref[base + e], SUBLANES), c), :]
            cp = (pltpu.make_async_copy(loc, rem, sem) if to_remote
                  else pltpu.make_async_copy(rem, loc, sem))
            if wait:
                cp.wait()
            else:
                cp.start()
        return carry
    lax.fori_loop(0, N_EXPERTS, body, 0)


def _dispatch_kernel(cnt_ref, off_ref, dst_ref, tcnt_ref, tdst_ref, nused_ref,
                     x_ref, g_ref, lidx_ref, xs_out, sbuf, zbuf, sems, zsem):
    j = pl.program_id(0)
    nb = pl.num_programs(0)
    tm = x_ref.shape[0]
    slot = lax.rem(j, 2)

    def copies(tile, s, wait):
        _segment_copies(cnt_ref, off_ref, dst_ref, tile * N_EXPERTS, sbuf.at[s], xs_out,
                        sems.at[s], True, wait)

    @pl.when(j == 0)
    def _():
        zbuf[...] = jnp.zeros_like(zbuf)
        rows = zbuf.shape[0]
        n_blocks = xs_out.shape[0] // rows
        for wait in (False, True):
            _segment_copies(tcnt_ref, None, tdst_ref, 0, zbuf, xs_out, zsem, True, wait)

            def body(b, carry):
                start = pl.multiple_of(b * rows, SUBLANES)
                cp = pltpu.make_async_copy(zbuf, xs_out.at[pl.ds(start, rows), :], zsem)
                if wait:
                    cp.wait()
                else:
                    cp.start()
                return carry
            lax.fori_loop(nused_ref[0], n_blocks, body, 0)

    @pl.when(j >= 2)
    def _():
        copies(j - 2, slot, True)

    x = x_ref[...]
    ms = jnp.mean(x * x, axis=-1, keepdims=True)
    hn = (x * lax.rsqrt(ms + RMS_EPS) * g_ref[...]).astype(BF16)
    r = _iota((_seg_local_rows(tm), tm), 0)
    perm = jnp.where((r == lidx_ref[0:1, :]) | (r == lidx_ref[1:2, :]), 1.0, 0.0).astype(BF16)
    sbuf[slot] = _dot(perm, hn)
    copies(j, slot, False)

    @pl.when(j == nb - 1)
    def _():
        copies(j, slot, True)

    @pl.when((j == nb - 1) & (nb >= 2))
    def _():
        copies(j - 1, 1 - slot, True)


def _dispatch(x2d, g, lidx, seg, tails, n_used, n_rows, rows, tm):
    n, d = x2d.shape
    return pl.pallas_call(
        _dispatch_kernel,
        out_shape=jax.ShapeDtypeStruct((n_rows, d), F32),
        grid_spec=pltpu.PrefetchScalarGridSpec(
            num_scalar_prefetch=6,
            grid=(n // tm,),
            in_specs=[pl.BlockSpec((tm, d), lambda j, *_: (j, 0)),
                      pl.BlockSpec((1, d), lambda j, *_: (0, 0)),
                      pl.BlockSpec((2, tm), lambda j, *_: (0, j))],
            out_specs=pl.BlockSpec(memory_space=pl.ANY),
            scratch_shapes=[pltpu.VMEM((2, _seg_local_rows(tm), d), F32),
                            pltpu.VMEM((rows, d), F32),
                            pltpu.SemaphoreType.DMA((2,)), pltpu.SemaphoreType.DMA(())]),
        compiler_params=_cparams(("arbitrary",)),
        name="moe_dispatch",
    )(*seg, *tails, n_used, x2d, g.reshape(1, d), lidx)


def _expert_kernel(be_ref, first_ref, nused_ref, x_ref, wg_ref, wu_ref, wd_ref, o_ref,
                   wg16, wu16, wd16):
    j = pl.program_id(0)

    @pl.when(first_ref[j] == 1)
    def _():
        wg16[...] = wg_ref[0, 0].astype(BF16)
        wu16[...] = wu_ref[0, 0].astype(BF16)
        wd16[...] = wd_ref[0, 0].astype(BF16)

    @pl.when(j < nused_ref[0])
    def _():
        x = x_ref[...].astype(BF16)
        hg = _dot(x, wg16[...])
        hu = _dot(x, wu16[...])
        hid = (_silu(hg) * hu).astype(BF16)
        o_ref[...] = _dot(hid, wd16[...])

    @pl.when(j >= nused_ref[0])
    def _():
        o_ref[...] = jnp.zeros_like(o_ref)


def _expert_ffn(xs, block_e, first_flag, n_used, w_gate, w_up, w_down, layer, rows):
    n_rows, d = xs.shape
    n_blocks = n_rows // rows
    ff = w_gate.shape[-1]
    wspec = lambda shp: pl.BlockSpec((1, 1) + shp, lambda j, be, fi, nu: (layer, be[j], 0, 0))
    return pl.pallas_call(
        _expert_kernel,
        out_shape=jax.ShapeDtypeStruct((n_rows, d), F32),
        grid_spec=pltpu.PrefetchScalarGridSpec(
            num_scalar_prefetch=3,
            grid=(n_blocks,),
            in_specs=[pl.BlockSpec((rows, d),
                                   lambda j, be, fi, nu: (jnp.minimum(j, nu[0] - 1), 0)),
                      wspec((d, ff)), wspec((d, ff)), wspec((ff, d))],
            out_specs=pl.BlockSpec((rows, d), lambda j, be, fi, nu: (j, 0)),
            scratch_shapes=[pltpu.VMEM((d, ff), BF16), pltpu.VMEM((d, ff), BF16),
                            pltpu.VMEM((ff, d), BF16)]),
        compiler_params=_cparams(("arbitrary",)),
        name="moe_expert_ffn",
    )(block_e, first_flag, n_used, xs, w_gate, w_up, w_down)


def _combine_kernel(cnt_ref, off_ref, dst_ref, x_ref, lidx_ref, gt_ref, ys_hbm, gfin_ref, o_ref,
                    ybuf, sems, *, final_norm):
    j = pl.program_id(0)
    nb = pl.num_programs(0)
    tm = x_ref.shape[0]
    slot = lax.rem(j, 2)

    def copies(tile, s, wait):
        _segment_copies(cnt_ref, off_ref, dst_ref, tile * N_EXPERTS, ybuf.at[s], ys_hbm,
                        sems.at[s], False, wait)

    @pl.when(j == 0)
    def _():
        ybuf[...] = jnp.zeros_like(ybuf)
        copies(0, 0, False)

    @pl.when(j + 1 < nb)
    def _():
        copies(j + 1, 1 - slot, False)

    copies(j, slot, True)
    c = _iota((tm, _seg_local_rows(tm)), 1)
    li = lidx_ref[...]
    gt = gt_ref[...]
    mix = (jnp.where(c == li[:, 0:1], gt[:, 0:1], 0.0)
           + jnp.where(c == li[:, 1:2], gt[:, 1:2], 0.0)).astype(BF16)
    y = x_ref[...] + _dot(mix, ybuf[slot].astype(BF16))
    if final_norm:
        ms = jnp.mean(y * y, axis=-1, keepdims=True)
        y = y * lax.rsqrt(ms + RMS_EPS) * gfin_ref[...]
    o_ref[...] = y


def _combine(x2d, ys, lidx_t, gates_t, seg, g_final, final_norm, tm):
    n, d = x2d.shape
    return pl.pallas_call(
        functools.partial(_combine_kernel, final_norm=final_norm),
        out_shape=jax.ShapeDtypeStruct((n, d), F32),
        grid_spec=pltpu.PrefetchScalarGridSpec(
            num_scalar_prefetch=3,
            grid=(n // tm,),
            in_specs=[pl.BlockSpec((tm, d), lambda j, *_: (j, 0)),
                      pl.BlockSpec((tm, 2), lambda j, *_: (j, 0)),
                      pl.BlockSpec((tm, 2), lambda j, *_: (j, 0)),
                      pl.BlockSpec(memory_space=pl.ANY),
                      pl.BlockSpec((1, d), lambda j, *_: (0, 0))],
            out_specs=pl.BlockSpec((tm, d), lambda j, *_: (j, 0)),
            scratch_shapes=[pltpu.VMEM((2, _seg_local_rows(tm), d), F32),
                            pltpu.SemaphoreType.DMA((2,))]),
        compiler_params=_cparams(("arbitrary",)),
        name="moe_combine",
    )(*seg, x2d, lidx_t, gates_t, ys, g_final.reshape(1, d))


def _route_plan(e2, rank2, tile_cnt, rows, tm):
    n_tok = e2.shape[1]
    n_tiles = n_tok // tm
    seg_cnt = (tile_cnt + SUBLANES - 1) // SUBLANES * SUBLANES
    counts = jnp.sum(seg_cnt, axis=0)
    padded = (counts + rows - 1) // rows * rows
    pend = jnp.cumsum(padded)
    pstart = pend - padded
    seg_dst = pstart[None, :] + jnp.cumsum(seg_cnt, axis=0) - seg_cnt
    seg_off = jnp.cumsum(seg_cnt, axis=1) - seg_cnt
    eid = jnp.arange(N_EXPERTS, dtype=I32)
    e3 = e2.reshape(2, n_tiles, tm)
    lidx = rank2.reshape(2, n_tiles, tm) + jnp.sum(
        jnp.where(e3[None] == eid[:, None, None, None], seg_off.T[:, None, :, None], 0), axis=0)
    lidx = lidx.reshape(2, n_tok).astype(I32)
    n_blocks = -(-(2 * n_tok + n_tiles * N_EXPERTS * SUBLANES) // rows) + N_EXPERTS
    starts = jnp.arange(n_blocks, dtype=I32) * rows
    block_e = jnp.minimum(jnp.sum((starts[:, None] >= pend[None, :]).astype(I32), axis=1),
                          N_EXPERTS - 1)
    first = jnp.concatenate([jnp.ones((1,), I32),
                             (block_e[1:] != block_e[:-1]).astype(I32)])
    n_used = (pend[-1] // rows).astype(I32).reshape(1)
    seg = tuple(a.reshape(-1).astype(I32) for a in (seg_cnt, seg_off, seg_dst))
    tails = ((padded - counts).astype(I32), (pstart + counts).astype(I32))
    return lidx, seg, tails, block_e, first, n_used, n_blocks * rows


def _moe(x2d, g_ffn, w_group, b_group, w_expert, b_expert, w_gate, w_up, w_down, layer,
         g_final, final_norm, rows=ROUTE_ROWS, tm=ROUTE_TILE):
    tm = min(tm, x2d.shape[0])
    e2, gates2, rank2, cnt = _router(x2d, g_ffn, w_group, b_group, w_expert, b_expert, tm)
    lidx, seg, tails, block_e, first, n_used, n_rows = _route_plan(
        e2, rank2, cnt[:, :, 0].astype(I32), rows, tm)
    xs = _dispatch(x2d, g_ffn, lidx, seg, tails, n_used, n_rows, rows, tm)
    ys = _expert_ffn(xs, block_e, first, n_used, w_gate, w_up, w_down, layer, rows)
    return _combine(x2d, ys, lidx.T, gates2.T, seg, g_final, final_norm, tm)


def kernel(x, norm_mix, norm_ffn, norm_final, ab_w_in, rw_mu, rw_w0, rw_w2, rw_a0, rw_a2, rw_g2, rw_k_k, rw_k_a, rw_r_k, rw_ln_w, rw_ln_b, lru_conv_w, lru_conv_b, lru_w_a, lru_b_a, lru_w_x, lru_b_x, lru_lambda, ab_w_out, c_w_in, c_lower_bound, c_norm_w, c_w_out, moe_w_group, moe_b_group, moe_w_expert, moe_b_expert, moe_w_gate, moe_w_up, moe_w_down):
    bsz, t, d = x.shape
    depth = norm_mix.shape[0]
    n = bsz * t
    lbs = jnp.cumsum(jax.nn.softmax(c_lower_bound.astype(F32), axis=0), axis=0)
    lbs = lbs - lbs[0]
    x2d = x.reshape(n, d)
    for layer in range(depth):
        j = layer // 2
        if layer % 2 == 0:
            rw_cols = rw_mu.shape[1]
            lru_cols = ab_w_in.shape[2] - rw_cols
            width = rw_w0.shape[1]
            p_a, p_b = _norm_matmul(x2d, norm_mix[layer], ab_w_in[j], (rw_cols, lru_cols))
            prep = _rwkv_prep(p_a.reshape(bsz, t, rw_cols), rw_mu[j], rw_w0[j], rw_w2[j],
                              rw_a0[j], rw_a2[j], rw_g2[j], rw_k_k[j], rw_k_a[j],
                              rw_r_k[j].reshape(-1))
            ya = _rwkv_scan(*prep, rw_ln_w[j], rw_ln_b[j])
            yb = _lru(p_b.reshape(bsz, t, lru_cols), lru_conv_w[j], lru_conv_b[j], lru_w_a[j],
                      lru_b_a[j], lru_w_x[j], lru_b_x[j], lru_lambda[j])
            x2d = _proj_residual(x2d, [ya.reshape(n, width), yb.reshape(n, -1)],
                                 [ab_w_out[j][:width], ab_w_out[j][width:]])
        else:
            hw = c_norm_w.shape[1]
            q, f, i_, g = _norm_matmul(x2d, norm_mix[layer], c_w_in[j], (hw,) * 4)
            shp = (bsz, t, hw)
            o = _gla(q.reshape(shp), f.reshape(shp), i_.reshape(shp), g.reshape(shp),
                     lbs[layer], c_norm_w[j])
            x2d = _proj_residual(x2d, [o.reshape(n, hw)], [c_w_out[j]])
        last = layer == depth - 1
        x2d = _moe(x2d, norm_ffn[layer], moe_w_group[layer], moe_b_group[layer],
                   moe_w_expert[layer], moe_b_expert[layer], moe_w_gate, moe_w_up,
                   moe_w_down, layer, norm_final, last)
    return x2d.reshape(bsz, t, d)
```

```python
import functools

import jax
import jax.numpy as jnp
from jax import lax
from jax.experimental import pallas as pl
from jax.experimental.pallas import tpu as pltpu

F32 = jnp.float32
BF16 = jnp.bfloat16
I32 = jnp.int32
U32 = jnp.uint32

RMS_EPS = 1e-6
RWKV_GN_EPS = 64e-5
LRU_C = 8.0
CHUNK = 64
CHUNK_SHIFT = CHUNK.bit_length() - 1
HEAD64 = 64
LANES = 128
SUBLANES = 8
N_GROUPS = 4
EXPERTS_PER_GROUP = 8
N_EXPERTS = N_GROUPS * EXPERTS_PER_GROUP
ROUTE_ROWS = 512
ROUTE_TILE = 512
VMEM_LIMIT = 56 * 1024 * 1024
HIGHEST = lax.Precision.HIGHEST


def _cparams(sem):
    return pltpu.CompilerParams(dimension_semantics=sem, vmem_limit_bytes=VMEM_LIMIT)


def _sigmoid(x):
    return 0.5 * jnp.tanh(0.5 * x) + 0.5


def _pack_pairs(x):
    w = x.shape[1] // 2
    lo = lax.bitcast_convert_type(x[:, :w].astype(BF16).astype(F32), U32)
    hi = lax.bitcast_convert_type(x[:, w:].astype(BF16).astype(F32), U32)
    return (lo >> 16) | hi


def _unpack_pairs(p):
    lo = lax.bitcast_convert_type(p << 16, F32).astype(BF16)
    hi = lax.bitcast_convert_type(p & jnp.uint32(0xFFFF0000), F32).astype(BF16)
    return lo, hi


def _softplus(x):
    return jnp.maximum(x, 0.0) + jnp.log(1.0 + jnp.exp(-jnp.abs(x)))


def _silu(x):
    return x * _sigmoid(x)


def _gelu_tanh(x):
    return 0.5 * x * (1.0 + jnp.tanh(0.7978845608028654 * (x + 0.044715 * x * x * x)))


def _dot(a, b):
    return jnp.dot(a, b, preferred_element_type=F32)


def _dot_nt(a, b):
    return lax.dot_general(a, b, (((1,), (1,)), ((), ())), preferred_element_type=F32)


def _dot_tn(a, b):
    return lax.dot_general(a, b, (((0,), (0,)), ((), ())), preferred_element_type=F32)


def _split3(x):
    h1 = x.astype(BF16)
    r1 = x - h1.astype(F32)
    h2 = r1.astype(BF16)
    r2 = r1 - h2.astype(F32)
    return h1, h2, r2.astype(BF16)


def _dot_exact_rhs(x, m_bf16):
    h1, h2, h3 = _split3(x)
    return _dot(h1, m_bf16) + _dot(h2, m_bf16) + _dot(h3, m_bf16)


def _dot_exact_lhs(m_bf16, x):
    h1, h2, h3 = _split3(x)
    return _dot(m_bf16, h1) + _dot(m_bf16, h2) + _dot(m_bf16, h3)


def _iota(shape, dim):
    return lax.broadcasted_iota(I32, shape, dim)


def _norm_matmul_kernel(x_ref, g_ref, w_ref, *o_refs, splits):
    x = x_ref[...]
    ms = jnp.mean(x * x, axis=-1, keepdims=True)
    y = (x * lax.rsqrt(ms + RMS_EPS) * g_ref[...]).astype(BF16)
    off = 0
    for o_ref, n in zip(o_refs, splits):
        o_ref[...] = _dot(y, w_ref[:, off:off + n]).astype(o_ref.dtype)
        off += n


def _norm_matmul(x2d, g, w, splits, out_dtypes, tm=256):
    n, d = x2d.shape
    ncols = w.shape[1]
    assert sum(splits) == ncols and n % tm == 0 and len(out_dtypes) == len(splits)
    return pl.pallas_call(
        functools.partial(_norm_matmul_kernel, splits=splits),
        out_shape=[jax.ShapeDtypeStruct((n, s), dt) for s, dt in zip(splits, out_dtypes)],
        grid=(n // tm,),
        in_specs=[pl.BlockSpec((tm, d), lambda i: (i, 0)),
                  pl.BlockSpec((1, d), lambda i: (0, 0)),
                  pl.BlockSpec((d, ncols), lambda i: (0, 0))],
        out_specs=[pl.BlockSpec((tm, s), lambda i: (i, 0)) for s in splits],
        compiler_params=_cparams(("parallel",)),
        name="norm_matmul",
    )(x2d, g.reshape(1, d), w.astype(BF16))


def _proj_residual_kernel(*refs, n_in):
    x_ref = refs[0]
    y_refs = refs[1:1 + n_in]
    w_refs = refs[1 + n_in:1 + 2 * n_in]
    o_ref = refs[1 + 2 * n_in]
    acc = x_ref[...]
    for y_ref, w_ref in zip(y_refs, w_refs):
        acc = acc + _dot(y_ref[...].astype(BF16), w_ref[...])
    o_ref[...] = acc


def _proj_residual(x2d, ys, ws, tm=512):
    n, d = x2d.shape
    n_in = len(ys)
    in_specs = [pl.BlockSpec((tm, d), lambda i: (i, 0))]
    in_specs += [pl.BlockSpec((tm, y.shape[1]), lambda i: (i, 0)) for y in ys]
    in_specs += [pl.BlockSpec(w.shape, lambda i: (0, 0)) for w in ws]
    return pl.pallas_call(
        functools.partial(_proj_residual_kernel, n_in=n_in),
        out_shape=jax.ShapeDtypeStruct((n, d), F32),
        grid=(n // tm,),
        in_specs=in_specs,
        out_specs=pl.BlockSpec((tm, d), lambda i: (i, 0)),
        compiler_params=_cparams(("parallel",)),
        name="proj_residual",
    )(x2d, *ys, *[w.astype(BF16) for w in ws])


def _rwkv_prep_kernel(p_ref, mu_ref, w0_ref, a0_ref, kk_s_ref, ka_ref, rk_ref,
                      wcomb_ref, g2_ref,
                      rt_o, kkt_o, kh_o, bh_o, kb_o, bb_o, v_o, g_o, bonus_o, gc_o,
                      prev_ref, *, width):
    tm = p_ref.shape[1]

    @pl.when(pl.program_id(1) == 0)
    def _():
        prev_ref[...] = jnp.zeros_like(prev_ref)

    p = p_ref[0]
    rolled = pltpu.roll(p, 1, axis=0)
    prev = jnp.where(_iota(p.shape, 0) == 0, prev_ref[...], rolled)
    prev_ref[...] = p[tm - 1:tm, :]
    ps = p + (prev - p) * mu_ref[...]

    r = ps[:, 0:width]
    k = ps[:, width:2 * width]
    v = ps[:, 2 * width:3 * width]
    lowrank = ps[:, 3 * width:3 * width + LANES]
    gl = ps[:, 3 * width + LANES:3 * width + 2 * LANES]

    lane = _iota(lowrank.shape, 1)
    lr_in = jnp.where(lane < HEAD64, jnp.tanh(lowrank), lowrank)
    t12 = jnp.dot(lr_in, wcomb_ref[...], precision=HIGHEST, preferred_element_type=F32)
    wlog = -_softplus(-(w0_ref[...] + t12[:, :width])) - 0.5
    lw = -jnp.exp(wlog)
    a = _sigmoid(a0_ref[...] + t12[:, width:])
    g = jnp.dot(_sigmoid(gl), g2_ref[...], precision=HIGHEST, preferred_element_type=F32)

    ri = _iota((width, width), 0) >> 6
    ci = _iota((width, width), 1) >> 6
    seg = jnp.where(ri == ci, 1.0, 0.0).astype(BF16)

    kk = k * kk_s_ref[...]
    nrm = jnp.sqrt(_dot_exact_rhs(kk * kk, seg))
    kk = kk / jnp.maximum(nrm, 1e-12)
    k2 = k * (1.0 + (a - 1.0) * ka_ref[...])
    bonus = _dot_exact_rhs(r * k2 * rk_ref[...], seg) * v

    b = kk * a

    n_chunk = tm // CHUNK
    rr = _iota((tm, tm), 0)
    cc = _iota((tm, tm), 1)
    tril_bd = jnp.where(((rr >> CHUNK_SHIFT) == (cc >> CHUNK_SHIFT)) & (cc <= rr),
                        1.0, 0.0).astype(BF16)
    cum = _dot_exact_lhs(tril_bd, lw)
    cum3 = cum.reshape(n_chunk, CHUNK, width)
    cend = cum3[:, CHUNK - 1:CHUNK, :]
    e_neg = jnp.exp(-cum)
    e_end = jnp.exp(cend - cum3).reshape(tm, width)

    rt_o[0] = (r * jnp.exp(cum)).astype(BF16)
    kkt_o[0] = (kk * jnp.exp(cum - lw)).astype(BF16)
    kh_o[0] = (k2 * e_neg).astype(BF16)
    bh_o[0] = (b * e_neg).astype(BF16)
    kb_o[0] = (k2 * e_end).astype(BF16)
    bb_o[0] = (b * e_end).astype(BF16)
    v_o[0] = v.astype(BF16)
    g_o[0] = g
    bonus_o[0] = bonus
    gc_o[0] = jnp.exp(cend)


def _rwkv_prep(p_a, mu, w0, w2, a0, a2, g2, k_k, k_a, r_k, tm=512):
    b, t, cols = p_a.shape
    width = w0.shape[0]
    rank = w2.shape[0]
    assert rank == HEAD64 and a2.shape[0] == HEAD64 and g2.shape[0] == LANES
    zeros = jnp.zeros((rank, width), F32)
    wcomb = jnp.concatenate([jnp.concatenate([w2, zeros], 1),
                             jnp.concatenate([zeros, a2], 1)], 0)
    row = lambda a_: a_.reshape(1, -1)
    const = lambda shp: pl.BlockSpec(shp, lambda i, j: (0, 0))
    tok = pl.BlockSpec((1, tm, width), lambda i, j: (i, j, 0))
    n_chunk = tm // CHUNK
    return pl.pallas_call(
        functools.partial(_rwkv_prep_kernel, width=width),
        out_shape=([jax.ShapeDtypeStruct((b, t, width), BF16)] * 7
                   + [jax.ShapeDtypeStruct((b, t, width), F32)] * 2
                   + [jax.ShapeDtypeStruct((b, t // CHUNK, 1, width), F32)]),
        grid=(b, t // tm),
        in_specs=[pl.BlockSpec((1, tm, cols), lambda i, j: (i, j, 0)),
                  const((1, cols)), const((1, width)), const((1, width)), const((1, width)),
                  const((1, width)), const((1, width)),
                  const((LANES, 2 * width)), const((LANES, width))],
        out_specs=[tok] * 9 + [pl.BlockSpec((1, n_chunk, 1, width), lambda i, j: (i, j, 0, 0))],
        scratch_shapes=[pltpu.VMEM((1, cols), F32)],
        compiler_params=_cparams(("parallel", "arbitrary")),
        name="rwkv_prep",
    )(p_a, row(mu), row(w0), row(a0), row(k_k), row(k_a), row(r_k), wcomb, g2)


def _rwkv_scan_kernel(rt_ref, kkt_ref, kh_ref, bh_ref, kb_ref, bb_ref, v_ref, g_ref, bonus_ref,
                      gc_ref, lnw_ref, lnb_ref, o_ref, st_ref, *, n_batch, n_pairs):
    c = CHUNK

    @pl.when(pl.program_id(0) == 0)
    def _():
        st_ref[...] = jnp.zeros_like(st_ref)

    rr = _iota((LANES, LANES), 0)
    cc = _iota((LANES, LANES), 1)
    mask_bd = jnp.where((rr >> 6) == (cc >> 6), 1.0, 0.0)
    mask_bd16 = mask_bd.astype(BF16)
    t_i = _iota((c, LANES), 0)
    s_i = _iota((c, LANES), 1) & (c - 1)
    strict = jnp.where(s_i < t_i, 1.0, 0.0)
    incl = jnp.where(s_i <= t_i, 1.0, 0.0)
    eye_ss = jnp.where(s_i == t_i, 1.0, 0.0)

    def bd16(x):
        return jnp.concatenate([x, x], axis=0) * mask_bd16

    def b16(xs):
        return [x.astype(BF16) for x in xs]

    chains = [(bi, p) for bi in range(n_batch) for p in range(n_pairs)]
    n_ch = len(chains)
    idx = range(n_ch)

    def ld(ref):
        return [ref[bi, :, p * LANES:(p + 1) * LANES] for bi, p in chains]

    rt, kkt, kh, bh, kb, bb, v = (ld(ref) for ref in
                                  (rt_ref, kkt_ref, kh_ref, bh_ref, kb_ref, bb_ref, v_ref))

    lhs = [jnp.concatenate([kkt[i], rt[i]], axis=0) for i in idx]
    rhs = [jnp.concatenate([bd16(kh[i]), bd16(bh[i])], axis=0) for i in idx]
    gm = [_dot_nt(lhs[i], rhs[i]) for i in idx]
    l_kv = [gm[i][:c, :LANES] * strict for i in idx]
    xp = [-(gm[i][:c, LANES:] * strict) for i in idx]
    pm = [gm[i][c:, :LANES] * incl for i in idx]
    nqm = [-(gm[i][c:, LANES:] * incl) for i in idx]

    tinv = [eye_ss + xp[i] for i in idx]
    xp16 = b16(xp)
    xpb = [bd16(x) for x in xp16]
    for _ in range(c.bit_length() - 2):
        xp16 = b16([_dot(xp16[i], xpb[i]) for i in idx])
        xpb = [bd16(x) for x in xp16]
        t16 = b16(tinv)
        tinv = [tinv[i] + _dot(t16[i], xpb[i]) for i in idx]
    tinv16 = b16(tinv)

    v_bd = [bd16(x) for x in v]
    l_kv16 = b16(l_kv)
    w16 = b16([_dot(l_kv16[i], v_bd[i]) for i in idx])
    tkw = [_dot(tinv16[i], jnp.concatenate([bd16(kkt[i]), bd16(w16[i])], axis=1)) for i in idx]
    tk16 = b16([x[:, :LANES] for x in tkw])

    st = [st_ref[i] for i in idx]
    st16 = b16(st)
    u = [_dot_nt(tk16[i], st16[i]) + tkw[i][:, LANES:] for i in idx]
    u16 = b16(u)
    pq16 = [jnp.concatenate([pm[i], nqm[i]], axis=1).astype(BF16) for i in idx]
    vu = [jnp.concatenate([v_bd[i], bd16(u16[i])], axis=0) for i in idx]
    y = [_dot_nt(rt[i], st16[i]) + _dot(pq16[i], vu[i]) for i in idx]
    upd = [_dot_tn(jnp.concatenate([v[i], -u16[i]], axis=0),
                   jnp.concatenate([kb[i], bb[i]], axis=0)) for i in idx]
    for i, (bi, p) in enumerate(chains):
        st_ref[i] = st[i] * gc_ref[bi, 0, :, p * LANES:(p + 1) * LANES] + upd[i] * mask_bd

    ys = jnp.concatenate(y, axis=0)
    mean = _dot_exact_rhs(ys, mask_bd16) * (1.0 / HEAD64)
    d = ys - mean
    var = _dot_exact_rhs(d * d, mask_bd16) * (1.0 / HEAD64)
    dn = d * lax.rsqrt(var + RWKV_GN_EPS)
    for i, (bi, p) in enumerate(chains):
        sl = slice(p * LANES, (p + 1) * LANES)
        yn = dn[i * c:(i + 1) * c] * lnw_ref[:, sl] + lnb_ref[:, sl]
        o_ref[bi, :, sl] = ((yn + bonus_ref[bi, :, sl]) * g_ref[bi, :, sl]).astype(o_ref.dtype)


def _rwkv_scan(rt, kkt, kh, bh, kb, bb, v, g, bonus, gc, ln_w, ln_b):
    bsz, t, width = rt.shape
    n_pairs = width // LANES
    tok = pl.BlockSpec((bsz, CHUNK, width), lambda j: (0, j, 0))
    const = pl.BlockSpec((1, width), lambda j: (0, 0))
    return pl.pallas_call(
        functools.partial(_rwkv_scan_kernel, n_batch=bsz, n_pairs=n_pairs),
        out_shape=jax.ShapeDtypeStruct((bsz, t, width), BF16),
        grid=(t // CHUNK,),
        in_specs=[tok] * 9 + [pl.BlockSpec((bsz, 1, 1, width), lambda j: (0, j, 0, 0)),
                              const, const],
        out_specs=tok,
        scratch_shapes=[pltpu.VMEM((bsz * n_pairs, LANES, LANES), F32)],
        compiler_params=_cparams(("arbitrary",)),
        name="rwkv_scan",
    )(rt, kkt, kh, bh, kb, bb, v, g, bonus, gc, ln_w.reshape(1, -1), ln_b.reshape(1, -1))


def _lru_kernel(p_ref, cw_ref, cb_ref, wg_ref, ba_ref, bx_ref, lam_ref, o_ref,
                xcarry_ref, hcarry_ref, a_s, u_s, *, width):
    tm = p_ref.shape[1]

    @pl.when(pl.program_id(1) == 0)
    def _():
        xcarry_ref[...] = jnp.zeros_like(xcarry_ref)
        hcarry_ref[...] = jnp.zeros_like(hcarry_ref)

    gate = p_ref[0, :, 0:width]
    xb = p_ref[0, :, width:2 * width]
    carry8 = xcarry_ref[...]
    row8 = _iota((8, width), 0)

    def shifted(s):
        rolled = pltpu.roll(xb, s, axis=0)
        first = jnp.where(row8 < s, pltpu.roll(carry8, s, axis=0), rolled[0:8])
        return jnp.concatenate([first, rolled[8:]], axis=0)

    xc = (cw_ref[0:1, :] * shifted(3) + cw_ref[1:2, :] * shifted(2)
          + cw_ref[2:3, :] * shifted(1) + cw_ref[3:4, :] * xb + cb_ref[...])
    xcarry_ref[...] = xb[tm - 8:tm, :]

    gates = _dot(xc.astype(BF16), wg_ref[...])
    rg = _sigmoid(gates[:, :width] + ba_ref[...])
    ig = _sigmoid(gates[:, width:] + bx_ref[...])
    log_a = -LRU_C * rg * _softplus(-lam_ref[...])
    a = jnp.exp(log_a)
    a_s[...] = a
    u_s[...] = jnp.sqrt(1.0 - a * a) * ig * xc

    m1 = row8 >= 1
    m2 = row8 >= 2
    m4 = row8 >= 4

    def body(i, h):
        off = pl.multiple_of(i * 8, 8)
        a8 = a_s[pl.ds(off, 8), :]
        u8 = u_s[pl.ds(off, 8), :]
        for s, m in ((1, m1), (2, m2), (4, m4)):
            u_sh = jnp.where(m, pltpu.roll(u8, s, axis=0), 0.0)
            a_sh = jnp.where(m, pltpu.roll(a8, s, axis=0), 1.0)
            u8 = u8 + a8 * u_sh
            a8 = a8 * a_sh
        h8 = u8 + a8 * h
        u_s[pl.ds(off, 8), :] = h8
        return jnp.broadcast_to(h8[7:8, :], (8, width))

    h_last = lax.fori_loop(0, tm // 8, body, hcarry_ref[...])
    hcarry_ref[...] = h_last
    o_ref[0] = (u_s[...] * _gelu_tanh(gate)).astype(o_ref.dtype)


def _block_diag(w):
    nb, di, do = w.shape
    eye = jnp.eye(nb, dtype=w.dtype)
    return (eye[:, None, :, None] * w[:, :, None, :]).reshape(nb * di, nb * do)


def _lru(p_b, conv_w, conv_b, w_a, b_a, w_x, b_x, lam, tm=512):
    b, t, cols = p_b.shape
    width = cols // 2
    wg = jnp.concatenate([_block_diag(w_a), _block_diag(w_x)], axis=1).astype(BF16)
    row = lambda a_: a_.reshape(1, -1)
    const = lambda shp: pl.BlockSpec(shp, lambda i, j: (0, 0))
    return pl.pallas_call(
        functools.partial(_lru_kernel, width=width),
        out_shape=jax.ShapeDtypeStruct((b, t, width), BF16),
        grid=(b, t // tm),
        in_specs=[pl.BlockSpec((1, tm, cols), lambda i, j: (i, j, 0)),
                  const(conv_w.shape), const((1, width)), const((width, 2 * width)),
                  const((1, width)), const((1, width)), const((1, width))],
        out_specs=pl.BlockSpec((1, tm, width), lambda i, j: (i, j, 0)),
        scratch_shapes=[pltpu.VMEM((8, width), F32), pltpu.VMEM((8, width), F32),
                        pltpu.VMEM((tm, width), F32), pltpu.VMEM((tm, width), F32)],
        compiler_params=_cparams(("parallel", "arbitrary")),
        name="rglru",
    )(p_b, conv_w, row(conv_b), wg, row(b_a), row(b_x), row(lam))


def _gla_kernel(q_ref, f_ref, i_ref, g_ref, lb_ref, nw_ref, o_ref, st_ref, *, n_batch, n_heads):
    c = CHUNK

    @pl.when(pl.program_id(0) == 0)
    def _():
        st_ref[...] = jnp.zeros_like(st_ref)

    trilf = jnp.where(_iota((c, c), 1) <= _iota((c, c), 0), 1.0, 0.0)
    tril16 = trilf.astype(BF16)
    mid = c // 2 - 1
    lb = lb_ref[...]

    qm, km, qg, kb, v16, gc = [], [], [], [], [], []
    for bi in range(n_batch):
        q = _silu(q_ref[bi].astype(F32))
        fg = lb + (1.0 - lb) * _sigmoid(f_ref[bi])
        k = 1.0 - fg
        cum = _dot_exact_lhs(tril16, jnp.log(fg))
        cum_c = cum[c - 1:c, :]
        cum_m = cum[mid:mid + 1, :]
        qm.append((q * jnp.exp(cum - cum_m)).astype(BF16))
        km.append((k * jnp.exp(cum_m - cum)).astype(BF16))
        qg.append((q * jnp.exp(cum)).astype(BF16))
        kb.append((k * jnp.exp(cum_c - cum)).astype(BF16))
        v16.append(i_ref[bi].astype(BF16))
        gc.append(jnp.exp(cum_c))

    chains = [(bi, h) for bi in range(n_batch) for h in range(n_heads)]
    idx = range(len(chains))

    def hs(xs):
        return [xs[bi][:, h * LANES:(h + 1) * LANES] for bi, h in chains]

    qm_c, km_c, qg_c, kb_c, v_c, gc_c = hs(qm), hs(km), hs(qg), hs(kb), hs(v16), hs(gc)
    scores = [(_dot_nt(qm_c[i], km_c[i]) * trilf).astype(BF16) for i in idx]
    st = [st_ref[i] for i in idx]
    st16 = [s.astype(BF16) for s in st]
    o = [_dot(scores[i], v_c[i]) + _dot_nt(qg_c[i], st16[i]) for i in idx]
    for i in idx:
        st_ref[i] = st[i] * gc_c[i] + _dot_tn(v_c[i], kb_c[i])

    for i, (bi, h) in enumerate(chains):
        sl = slice(h * LANES, (h + 1) * LANES)
        on = o[i] * lax.rsqrt(jnp.mean(o[i] * o[i], axis=-1, keepdims=True) + RMS_EPS)
        gate = _silu(g_ref[bi, :, sl].astype(F32))
        o_ref[bi, :, sl] = (on * nw_ref[:, sl] * gate).astype(o_ref.dtype)


def _gla(q, f, i, g, lower_bound, norm_w):
    bsz, t, width = q.shape
    n_heads = width // LANES
    tok = pl.BlockSpec((bsz, CHUNK, width), lambda j: (0, j, 0))
    const = pl.BlockSpec((1, width), lambda j: (0, 0))
    return pl.pallas_call(
        functools.partial(_gla_kernel, n_batch=bsz, n_heads=n_heads),
        out_shape=jax.ShapeDtypeStruct((bsz, t, width), BF16),
        grid=(t // CHUNK,),
        in_specs=[tok] * 4 + [const, const],
        out_specs=tok,
        scratch_shapes=[pltpu.VMEM((bsz * n_heads, LANES, LANES), F32)],
        compiler_params=_cparams(("arbitrary",)),
        name="hgrn2_gla",
    )(q, f, i, g, lower_bound.reshape(1, -1), norm_w.reshape(1, -1))


def _router_kernel(x_ref, g_ref, w_ref, b_ref, e_o, gate_o, rank_o, cnt_o):
    x = x_ref[...]
    ms = jnp.mean(x * x, axis=-1, keepdims=True)
    hn = x * lax.rsqrt(ms + RMS_EPS) * g_ref[...]
    lt = lax.dot_general(w_ref[...], hn, (((1,), (1,)), ((), ())),
                         precision=HIGHEST, preferred_element_type=F32) + b_ref[:, 0:1]
    tm = x.shape[0]
    gl = lt[0:8, :]
    row8 = _iota((8, tm), 0)
    gl = jnp.where(row8 < N_GROUPS, gl, -jnp.inf)
    gmax = jnp.max(gl, axis=0, keepdims=True)
    g_sel = jnp.min(jnp.where(gl == gmax, row8, 8), axis=0, keepdims=True)
    g_gate = 1.0 / jnp.sum(jnp.exp(gl - gmax), axis=0, keepdims=True)

    el = jnp.zeros((EXPERTS_PER_GROUP, tm), F32)
    for gi in range(N_GROUPS):
        lo = 8 + gi * EXPERTS_PER_GROUP
        el = jnp.where(g_sel == gi, lt[lo:lo + EXPERTS_PER_GROUP, :], el)
    m1 = jnp.max(el, axis=0, keepdims=True)
    i1 = jnp.min(jnp.where(el == m1, row8, 8), axis=0, keepdims=True)
    el2 = jnp.where(row8 == i1, -jnp.inf, el)
    m2 = jnp.max(el2, axis=0, keepdims=True)
    i2 = jnp.min(jnp.where(el2 == m2, row8, 8), axis=0, keepdims=True)
    e2 = jnp.exp(m2 - m1)
    inv = 1.0 / (1.0 + e2)
    ea = g_sel * EXPERTS_PER_GROUP + i1
    eb = g_sel * EXPERTS_PER_GROUP + i2
    e_o[...] = jnp.concatenate([ea, eb], axis=0)
    gate_o[...] = jnp.concatenate([g_gate * inv, g_gate * e2 * inv], axis=0)

    erow = _iota((N_EXPERTS, tm), 0)
    oh_a = jnp.where(erow == ea, 1.0, 0.0)
    oh_b = jnp.where(erow == eb, 1.0, 0.0)
    upper = jnp.where(_iota((tm, tm), 0) < _iota((tm, tm), 1), 1.0, 0.0).astype(BF16)
    pre_a = _dot(oh_a.astype(BF16), upper)
    pre_b = _dot(oh_b.astype(BF16), upper)
    cnt_a = jnp.sum(oh_a, axis=1, keepdims=True)
    cnt_b = jnp.sum(oh_b, axis=1, keepdims=True)
    rank_a = jnp.sum(oh_a * pre_a, axis=0, keepdims=True)
    rank_b = jnp.sum(oh_b * (pre_b + cnt_a), axis=0, keepdims=True)
    rank_o[...] = jnp.concatenate([rank_a, rank_b], axis=0).astype(I32)
    cnt_o[0] = jnp.broadcast_to(cnt_a + cnt_b, (N_EXPERTS, LANES))


def _router(x2d, g, w_group, b_group, w_expert, b_expert, tm):
    n, d = x2d.shape
    wt = jnp.zeros((LANES, d), F32)
    wt = wt.at[0:N_GROUPS].set(w_group.T).at[8:8 + N_EXPERTS].set(w_expert.T)
    bt = jnp.zeros((LANES,), F32)
    bt = bt.at[0:N_GROUPS].set(b_group).at[8:8 + N_EXPERTS].set(b_expert)
    bt = jnp.broadcast_to(bt[:, None], (LANES, LANES))
    return pl.pallas_call(
        _router_kernel,
        out_shape=[jax.ShapeDtypeStruct((2, n), I32),
                   jax.ShapeDtypeStruct((2, n), F32),
                   jax.ShapeDtypeStruct((2, n), I32),
                   jax.ShapeDtypeStruct((n // tm, N_EXPERTS, LANES), F32)],
        grid=(n // tm,),
        in_specs=[pl.BlockSpec((tm, d), lambda i: (i, 0)),
                  pl.BlockSpec((1, d), lambda i: (0, 0)),
                  pl.BlockSpec((LANES, d), lambda i: (0, 0)),
                  pl.BlockSpec((LANES, LANES), lambda i: (0, 0))],
        out_specs=[pl.BlockSpec((2, tm), lambda i: (0, i)),
                   pl.BlockSpec((2, tm), lambda i: (0, i)),
                   pl.BlockSpec((2, tm), lambda i: (0, i)),
                   pl.BlockSpec((1, N_EXPERTS, LANES), lambda i: (i, 0, 0))],
        compiler_params=_cparams(("parallel",)),
        name="moe_router",
    )(x2d, g.reshape(1, d), wt, bt)


def _seg_local_rows(tm):
    return 2 * tm + N_EXPERTS * SUBLANES


def _segment_copies(cnt_ref, off_ref, dst_ref, base, local, remote, sem, to_remote, wait):
    def body(e, carry):
        c = pl.multiple_of(cnt_ref[base + e], SUBLANES)

        @pl.when(c > 0)
        def _():
            off = 0 if off_ref is None else pl.multiple_of(off_ref[base + e], SUBLANES)
            loc = local.at[pl.ds(off, c), :]
            rem = remote.at[pl.ds(pl.multiple_of(dst_ref[base + e], SUBLANES), c), :]
            cp = (pltpu.make_async_copy(loc, rem, sem) if to_remote
                  else pltpu.make_async_copy(rem, loc, sem))
            if wait:
                cp.wait()
            else:
                cp.start()
        return carry
    lax.fori_loop(0, N_EXPERTS, body, 0)


def _dispatch_kernel(cnt_ref, off_ref, dst_ref, tcnt_ref, tdst_ref, nused_ref,
                     x_ref, g_ref, lidx_ref, xs_out, sbuf, zbuf, sems, zsem):
    j = pl.program_id(0)
    nb = pl.num_programs(0)
    tm = x_ref.shape[0]
    slot = lax.rem(j, 2)

    def copies(tile, s, wait):
        _segment_copies(cnt_ref, off_ref, dst_ref, tile * N_EXPERTS, sbuf.at[s], xs_out,
                        sems.at[s], True, wait)

    @pl.when(j == 0)
    def _():
        zbuf[...] = jnp.zeros_like(zbuf)
        rows = zbuf.shape[0]
        n_blocks = xs_out.shape[0] // rows
        for wait in (False, True):
            _segment_copies(tcnt_ref, None, tdst_ref, 0, zbuf, xs_out, zsem, True, wait)

            def body(b, carry):
                start = pl.multiple_of(b * rows, SUBLANES)
                cp = pltpu.make_async_copy(zbuf, xs_out.at[pl.ds(start, rows), :], zsem)
                if wait:
                    cp.wait()
                else:
                    cp.start()
                return carry
            lax.fori_loop(nused_ref[0], n_blocks, body, 0)

    @pl.when(j >= 2)
    def _():
        copies(j - 2, slot, True)

    x = x_ref[...]
    ms = jnp.mean(x * x, axis=-1, keepdims=True)
    hn = (x * lax.rsqrt(ms + RMS_EPS) * g_ref[...]).astype(BF16)
    r = _iota((_seg_local_rows(tm), tm), 0)
    perm = jnp.where((r == lidx_ref[0:1, :]) | (r == lidx_ref[1:2, :]), 1.0, 0.0).astype(BF16)
    sbuf[slot] = _pack_pairs(_dot(perm, hn))
    copies(j, slot, False)

    @pl.when(j == nb - 1)
    def _():
        copies(j, slot, True)

    @pl.when((j == nb - 1) & (nb >= 2))
    def _():
        copies(j - 1, 1 - slot, True)


def _dispatch(x2d, g, lidx, seg, tails, n_used, n_rows, rows, tm):
    n, d = x2d.shape
    return pl.pallas_call(
        _dispatch_kernel,
        out_shape=jax.ShapeDtypeStruct((n_rows, d // 2), U32),
        grid_spec=pltpu.PrefetchScalarGridSpec(
            num_scalar_prefetch=6,
            grid=(n // tm,),
            in_specs=[pl.BlockSpec((tm, d), lambda j, *_: (j, 0)),
                      pl.BlockSpec((1, d), lambda j, *_: (0, 0)),
                      pl.BlockSpec((2, tm), lambda j, *_: (0, j))],
            out_specs=pl.BlockSpec(memory_space=pl.ANY),
            scratch_shapes=[pltpu.VMEM((2, _seg_local_rows(tm), d // 2), U32),
                            pltpu.VMEM((rows, d // 2), U32),
                            pltpu.SemaphoreType.DMA((2,)), pltpu.SemaphoreType.DMA(())]),
        compiler_params=_cparams(("arbitrary",)),
        name="moe_dispatch",
    )(*seg, *tails, n_used, x2d, g.reshape(1, d), lidx)


def _expert_kernel(be_ref, first_ref, nused_ref, x_ref, wg_ref, wu_ref, wd_ref, o_ref,
                   wg16, wu16, wd16):
    j = pl.program_id(0)

    @pl.when(first_ref[j] == 1)
    def _():
        wg16[...] = wg_ref[0, 0].astype(BF16)
        wu16[...] = wu_ref[0, 0].astype(BF16)
        wd16[...] = wd_ref[0, 0].astype(BF16)

    @pl.when(j < nused_ref[0])
    def _():
        x_lo, x_hi = _unpack_pairs(x_ref[...])
        half = x_lo.shape[1]
        hg = _dot(x_lo, wg16[:half, :]) + _dot(x_hi, wg16[half:, :])
        hu = _dot(x_lo, wu16[:half, :]) + _dot(x_hi, wu16[half:, :])
        hid = (_silu(hg) * hu).astype(BF16)
        o_ref[...] = _pack_pairs(_dot(hid, wd16[...]))

    @pl.when(j >= nused_ref[0])
    def _():
        o_ref[...] = jnp.zeros_like(o_ref)


def _expert_ffn(xs, block_e, first_flag, n_used, w_gate, w_up, w_down, layer, rows):
    n_rows, dh = xs.shape
    n_blocks = n_rows // rows
    d, ff = w_gate.shape[-2:]
    assert d == 2 * dh
    wspec = lambda shp: pl.BlockSpec((1, 1) + shp, lambda j, be, fi, nu: (layer, be[j], 0, 0))
    return pl.pallas_call(
        _expert_kernel,
        out_shape=jax.ShapeDtypeStruct((n_rows, dh), U32),
        grid_spec=pltpu.PrefetchScalarGridSpec(
            num_scalar_prefetch=3,
            grid=(n_blocks,),
            in_specs=[pl.BlockSpec((rows, dh),
                                   lambda j, be, fi, nu: (jnp.minimum(j, nu[0] - 1), 0)),
                      wspec((d, ff)), wspec((d, ff)), wspec((ff, d))],
            out_specs=pl.BlockSpec((rows, dh), lambda j, be, fi, nu: (j, 0)),
            scratch_shapes=[pltpu.VMEM((d, ff), BF16), pltpu.VMEM((d, ff), BF16),
                            pltpu.VMEM((ff, d), BF16)]),
        compiler_params=_cparams(("arbitrary",)),
        name="moe_expert_ffn",
    )(block_e, first_flag, n_used, xs, w_gate, w_up, w_down)


def _combine_kernel(cnt_ref, off_ref, dst_ref, x_ref, lidx_ref, gt_ref, ys_hbm, gfin_ref, o_ref,
                    ybuf, sems, *, final_norm):
    j = pl.program_id(0)
    nb = pl.num_programs(0)
    tm = x_ref.shape[0]
    slot = lax.rem(j, 2)

    def copies(tile, s, wait):
        _segment_copies(cnt_ref, off_ref, dst_ref, tile * N_EXPERTS, ybuf.at[s], ys_hbm,
                        sems.at[s], False, wait)

    @pl.when(j == 0)
    def _():
        ybuf[...] = jnp.zeros_like(ybuf)
        copies(0, 0, False)

    @pl.when(j + 1 < nb)
    def _():
        copies(j + 1, 1 - slot, False)

    copies(j, slot, True)
    c = _iota((tm, _seg_local_rows(tm)), 1)
    li = lidx_ref[...]
    gt = gt_ref[...]
    mix = (jnp.where(c == li[:, 0:1], gt[:, 0:1], 0.0)
           + jnp.where(c == li[:, 1:2], gt[:, 1:2], 0.0)).astype(BF16)
    y_lo, y_hi = _unpack_pairs(ybuf[slot])
    y = x_ref[...] + jnp.concatenate([_dot(mix, y_lo), _dot(mix, y_hi)], axis=1)
    if final_norm:
        ms = jnp.mean(y * y, axis=-1, keepdims=True)
        y = y * lax.rsqrt(ms + RMS_EPS) * gfin_ref[...]
    o_ref[...] = y


def _combine(x2d, ys, lidx_t, gates_t, seg, g_final, final_norm, tm):
    n, d = x2d.shape
    return pl.pallas_call(
        functools.partial(_combine_kernel, final_norm=final_norm),
        out_shape=jax.ShapeDtypeStruct((n, d), F32),
        grid_spec=pltpu.PrefetchScalarGridSpec(
            num_scalar_prefetch=3,
            grid=(n // tm,),
            in_specs=[pl.BlockSpec((tm, d), lambda j, *_: (j, 0)),
                      pl.BlockSpec((tm, 2), lambda j, *_: (j, 0)),
                      pl.BlockSpec((tm, 2), lambda j, *_: (j, 0)),
                      pl.BlockSpec(memory_space=pl.ANY),
                      pl.BlockSpec((1, d), lambda j, *_: (0, 0))],
            out_specs=pl.BlockSpec((tm, d), lambda j, *_: (j, 0)),
            scratch_shapes=[pltpu.VMEM((2, _seg_local_rows(tm), d // 2), U32),
                            pltpu.SemaphoreType.DMA((2,))]),
        compiler_params=_cparams(("arbitrary",)),
        name="moe_combine",
    )(*seg, x2d, lidx_t, gates_t, ys, g_final.reshape(1, d))


def _route_plan(e2, rank2, tile_cnt, rows, tm):
    n_tok = e2.shape[1]
    n_tiles = n_tok // tm
    seg_cnt = (tile_cnt + SUBLANES - 1) // SUBLANES * SUBLANES
    counts = jnp.sum(seg_cnt, axis=0)
    padded = (counts + rows - 1) // rows * rows
    pend = jnp.cumsum(padded)
    pstart = pend - padded
    seg_dst = pstart[None, :] + jnp.cumsum(seg_cnt, axis=0) - seg_cnt
    seg_off = jnp.cumsum(seg_cnt, axis=1) - seg_cnt
    eid = jnp.arange(N_EXPERTS, dtype=I32)
    e3 = e2.reshape(2, n_tiles, tm)
    lidx = rank2.reshape(2, n_tiles, tm) + jnp.sum(
        jnp.where(e3[None] == eid[:, None, None, None], seg_off.T[:, None, :, None], 0), axis=0)
    lidx = lidx.reshape(2, n_tok).astype(I32)
    n_blocks = -(-(2 * n_tok + n_tiles * N_EXPERTS * SUBLANES) // rows) + N_EXPERTS
    starts = jnp.arange(n_blocks, dtype=I32) * rows
    block_e = jnp.minimum(jnp.sum((starts[:, None] >= pend[None, :]).astype(I32), axis=1),
                          N_EXPERTS - 1)
    first = jnp.concatenate([jnp.ones((1,), I32),
                             (block_e[1:] != block_e[:-1]).astype(I32)])
    n_used = (pend[-1] // rows).astype(I32).reshape(1)
    seg = tuple(a.reshape(-1).astype(I32) for a in (seg_cnt, seg_off, seg_dst))
    tails = ((padded - counts).astype(I32), (pstart + counts).astype(I32))
    return lidx, seg, tails, block_e, first, n_used, n_blocks * rows


def _moe(x2d, g_ffn, w_group, b_group, w_expert, b_expert, w_gate, w_up, w_down, layer,
         g_final, final_norm, rows=ROUTE_ROWS, tm=ROUTE_TILE):
    tm = min(tm, x2d.shape[0])
    e2, gates2, rank2, cnt = _router(x2d, g_ffn, w_group, b_group, w_expert, b_expert, tm)
    lidx, seg, tails, block_e, first, n_used, n_rows = _route_plan(
        e2, rank2, cnt[:, :, 0].astype(I32), rows, tm)
    xs = _dispatch(x2d, g_ffn, lidx, seg, tails, n_used, n_rows, rows, tm)
    ys = _expert_ffn(xs, block_e, first, n_used, w_gate, w_up, w_down, layer, rows)
    return _combine(x2d, ys, lidx.T, gates2.T, seg, g_final, final_norm, tm)


def kernel(x, norm_mix, norm_ffn, norm_final, ab_w_in, rw_mu, rw_w0, rw_w2, rw_a0, rw_a2, rw_g2, rw_k_k, rw_k_a, rw_r_k, rw_ln_w, rw_ln_b, lru_conv_w, lru_conv_b, lru_w_a, lru_b_a, lru_w_x, lru_b_x, lru_lambda, ab_w_out, c_w_in, c_lower_bound, c_norm_w, c_w_out, moe_w_group, moe_b_group, moe_w_expert, moe_b_expert, moe_w_gate, moe_w_up, moe_w_down):
    bsz, t, d = x.shape
    depth = norm_mix.shape[0]
    n = bsz * t
    lbs = jnp.cumsum(jax.nn.softmax(c_lower_bound.astype(F32), axis=0), axis=0)
    lbs = lbs - lbs[0]
    x2d = x.reshape(n, d)
    for layer in range(depth):
        j = layer // 2
        if layer % 2 == 0:
            rw_cols = rw_mu.shape[1]
            lru_cols = ab_w_in.shape[2] - rw_cols
            width = rw_w0.shape[1]
            p_a, p_b = _norm_matmul(x2d, norm_mix[layer], ab_w_in[j], (rw_cols, lru_cols),
                                    (F32, F32))
            prep = _rwkv_prep(p_a.reshape(bsz, t, rw_cols), rw_mu[j], rw_w0[j], rw_w2[j],
                              rw_a0[j], rw_a2[j], rw_g2[j], rw_k_k[j], rw_k_a[j],
                              rw_r_k[j].reshape(-1))
            ya = _rwkv_scan(*prep, rw_ln_w[j], rw_ln_b[j])
            yb = _lru(p_b.reshape(bsz, t, lru_cols), lru_conv_w[j], lru_conv_b[j], lru_w_a[j],
                      lru_b_a[j], lru_w_x[j], lru_b_x[j], lru_lambda[j])
            x2d = _proj_residual(x2d, [ya.reshape(n, width), yb.reshape(n, -1)],
                                 [ab_w_out[j][:width], ab_w_out[j][width:]])
        else:
            hw = c_norm_w.shape[1]
            q, f, i_, g = _norm_matmul(x2d, norm_mix[layer], c_w_in[j], (hw,) * 4,
                                       (BF16, F32, BF16, BF16))
            shp = (bsz, t, hw)
            o = _gla(q.reshape(shp), f.reshape(shp), i_.reshape(shp), g.reshape(shp),
                     lbs[layer], c_norm_w[j])
            x2d = _proj_residual(x2d, [o.reshape(n, hw)], [c_w_out[j]])
        last = layer == depth - 1
        x2d = _moe(x2d, norm_ffn[layer], moe_w_group[layer], moe_b_group[layer],
                   moe_w_expert[layer], moe_b_expert[layer], moe_w_gate, moe_w_up,
                   moe_w_down, layer, norm_final, last)
    return x2d.reshape(bsz, t, d)
```

```python
import functools

import jax
import jax.numpy as jnp
from jax import lax
from jax.experimental import pallas as pl
from jax.experimental.pallas import tpu as pltpu

F32 = jnp.float32
BF16 = jnp.bfloat16
I32 = jnp.int32
U32 = jnp.uint32

RMS_EPS = 1e-6
RWKV_GN_EPS = 64e-5
LRU_C = 8.0
CHUNK = 64
CHUNK_SHIFT = CHUNK.bit_length() - 1
HEAD64 = 64
LANES = 128
SUBLANES = 8
N_GROUPS = 4
EXPERTS_PER_GROUP = 8
N_EXPERTS = N_GROUPS * EXPERTS_PER_GROUP
ROUTE_ROWS = 512
ROUTE_TILE = 512
VMEM_LIMIT = 56 * 1024 * 1024


def _cparams(sem):
    return pltpu.CompilerParams(dimension_semantics=sem, vmem_limit_bytes=VMEM_LIMIT)


def _sigmoid(x):
    return 0.5 * jnp.tanh(0.5 * x) + 0.5


def _pack_pairs(x):
    w = x.shape[1] // 2
    lo = lax.bitcast_convert_type(x[:, :w].astype(BF16).astype(F32), U32)
    hi = lax.bitcast_convert_type(x[:, w:].astype(BF16).astype(F32), U32)
    return (lo >> 16) | hi


def _unpack_pairs(p):
    lo = lax.bitcast_convert_type(p << 16, F32).astype(BF16)
    hi = lax.bitcast_convert_type(p & jnp.uint32(0xFFFF0000), F32).astype(BF16)
    return lo, hi


def _softplus(x):
    return jnp.maximum(x, 0.0) + jnp.log(1.0 + jnp.exp(-jnp.abs(x)))


def _silu(x):
    return x * _sigmoid(x)


def _gelu_tanh(x):
    return 0.5 * x * (1.0 + jnp.tanh(0.7978845608028654 * (x + 0.044715 * x * x * x)))


def _dot(a, b):
    return jnp.dot(a, b, preferred_element_type=F32)


def _dot_nt(a, b):
    return lax.dot_general(a, b, (((1,), (1,)), ((), ())), preferred_element_type=F32)


def _dot_tn(a, b):
    return lax.dot_general(a, b, (((0,), (0,)), ((), ())), preferred_element_type=F32)


def _split2(x):
    hi = x.astype(BF16)
    lo = (x - hi.astype(F32)).astype(BF16)
    return hi, lo


def _dot_exact_rhs(x, m_bf16):
    hi, lo = _split2(x)
    return _dot(hi, m_bf16) + _dot(lo, m_bf16)


def _dot_exact_lhs(m_bf16, x):
    hi, lo = _split2(x)
    return _dot(m_bf16, hi) + _dot(m_bf16, lo)


def _dot_split(a, w_hi, w_lo):
    a_hi, a_lo = _split2(a)
    return _dot(a_hi, w_hi) + (_dot(a_lo, w_hi) + _dot(a_hi, w_lo))


def _hi_lo(w):
    hi = w.astype(BF16)
    return hi, (w - hi.astype(F32)).astype(BF16)


def _iota(shape, dim):
    return lax.broadcasted_iota(I32, shape, dim)


def _norm_matmul_kernel(x_ref, g_ref, w_ref, *o_refs, splits):
    x = x_ref[...]
    ms = jnp.mean(x * x, axis=-1, keepdims=True)
    y = (x * lax.rsqrt(ms + RMS_EPS) * g_ref[...]).astype(BF16)
    off = 0
    for o_ref, n in zip(o_refs, splits):
        o_ref[...] = _dot(y, w_ref[:, off:off + n]).astype(o_ref.dtype)
        off += n


def _norm_matmul(x2d, g, w, splits, out_dtypes, tm=512):
    n, d = x2d.shape
    ncols = w.shape[1]
    assert sum(splits) == ncols and n % tm == 0 and len(out_dtypes) == len(splits)
    return pl.pallas_call(
        functools.partial(_norm_matmul_kernel, splits=splits),
        out_shape=[jax.ShapeDtypeStruct((n, s), dt) for s, dt in zip(splits, out_dtypes)],
        grid=(n // tm,),
        in_specs=[pl.BlockSpec((tm, d), lambda i: (i, 0)),
                  pl.BlockSpec((1, d), lambda i: (0, 0)),
                  pl.BlockSpec((d, ncols), lambda i: (0, 0))],
        out_specs=[pl.BlockSpec((tm, s), lambda i: (i, 0)) for s in splits],
        compiler_params=_cparams(("parallel",)),
        name="norm_matmul",
    )(x2d, g.reshape(1, d), w.astype(BF16))


def _proj_residual_kernel(*refs, n_in):
    x_ref = refs[0]
    y_refs = refs[1:1 + n_in]
    w_refs = refs[1 + n_in:1 + 2 * n_in]
    o_ref = refs[1 + 2 * n_in]
    acc = x_ref[...]
    for y_ref, w_ref in zip(y_refs, w_refs):
        acc = acc + _dot(y_ref[...].astype(BF16), w_ref[...])
    o_ref[...] = acc


def _proj_residual(x2d, ys, ws, tm=512):
    n, d = x2d.shape
    n_in = len(ys)
    in_specs = [pl.BlockSpec((tm, d), lambda i: (i, 0))]
    in_specs += [pl.BlockSpec((tm, y.shape[1]), lambda i: (i, 0)) for y in ys]
    in_specs += [pl.BlockSpec(w.shape, lambda i: (0, 0)) for w in ws]
    return pl.pallas_call(
        functools.partial(_proj_residual_kernel, n_in=n_in),
        out_shape=jax.ShapeDtypeStruct((n, d), F32),
        grid=(n // tm,),
        in_specs=in_specs,
        out_specs=pl.BlockSpec((tm, d), lambda i: (i, 0)),
        compiler_params=_cparams(("parallel",)),
        name="proj_residual",
    )(x2d, *ys, *[w.astype(BF16) for w in ws])


def _rwkv_prep_kernel(p_ref, mu_ref, w0_ref, a0_ref, kk_s_ref, ka_ref, rk_ref,
                      wcomb_ref, g2_ref, tril_ref,
                      rt_o, kkt_o, kh_o, bh_o, kb_o, bb_o, v_o, g_o, bonus_o, gc_o,
                      prev_ref, *, width):
    tm = p_ref.shape[1]

    @pl.when(pl.program_id(1) == 0)
    def _():
        prev_ref[...] = jnp.zeros_like(prev_ref)

    p = p_ref[0]
    rolled = pltpu.roll(p, 1, axis=0)
    prev = jnp.where(_iota(p.shape, 0) == 0, prev_ref[...], rolled)
    prev_ref[...] = p[tm - 1:tm, :]
    ps = p + (prev - p) * mu_ref[...]

    r = ps[:, 0:width]
    k = ps[:, width:2 * width]
    v = ps[:, 2 * width:3 * width]
    lowrank = ps[:, 3 * width:3 * width + LANES]
    gl = ps[:, 3 * width + LANES:3 * width + 2 * LANES]

    lane = _iota(lowrank.shape, 1)
    lr_in = jnp.where(lane < HEAD64, jnp.tanh(lowrank), lowrank)
    t12 = _dot_split(lr_in, wcomb_ref[0], wcomb_ref[1])
    wlog = -_softplus(-(w0_ref[...] + t12[:, :width])) - 0.5
    lw = -jnp.exp(wlog)
    a = _sigmoid(a0_ref[...] + t12[:, width:])
    g = _dot_split(_sigmoid(gl), g2_ref[0], g2_ref[1])

    seg = jnp.where((_iota((LANES, LANES), 0) >> 6) == (_iota((LANES, LANES), 1) >> 6),
                    1.0, 0.0).astype(BF16)

    def head_sums(x):
        return jnp.concatenate([_dot_exact_rhs(x[:, q * LANES:(q + 1) * LANES], seg)
                                for q in range(width // LANES)], axis=1)

    kk = k * kk_s_ref[...]
    nrm = jnp.sqrt(head_sums(kk * kk))
    kk = kk / jnp.maximum(nrm, 1e-12)
    k2 = k * (1.0 + (a - 1.0) * ka_ref[...])
    bonus = head_sums(r * k2 * rk_ref[...]) * v

    b = kk * a

    n_chunk = tm // CHUNK
    cum = _dot_exact_lhs(tril_ref[...], lw)
    cum3 = cum.reshape(n_chunk, CHUNK, width)
    cend = cum3[:, CHUNK - 1:CHUNK, :]
    e_neg = jnp.exp(-cum)
    e_end = jnp.exp(cend - cum3).reshape(tm, width)

    rt_o[0] = (r * jnp.exp(cum)).astype(BF16)
    kkt_o[0] = (kk * jnp.exp(cum - lw)).astype(BF16)
    kh_o[0] = (k2 * e_neg).astype(BF16)
    bh_o[0] = (b * e_neg).astype(BF16)
    kb_o[0] = (k2 * e_end).astype(BF16)
    bb_o[0] = (b * e_end).astype(BF16)
    v_o[0] = v.astype(BF16)
    g_o[0] = g
    bonus_o[0] = bonus
    gc_o[0] = jnp.exp(cend)


def _rwkv_prep(p_a, mu, w0, w2, a0, a2, g2, k_k, k_a, r_k, tm=512):
    b, t, cols = p_a.shape
    width = w0.shape[0]
    rank = w2.shape[0]
    assert rank == HEAD64 and a2.shape[0] == HEAD64 and g2.shape[0] == LANES
    zeros = jnp.zeros((rank, width), F32)
    wcomb = jnp.concatenate([jnp.concatenate([w2, zeros], 1),
                             jnp.concatenate([zeros, a2], 1)], 0)
    row = lambda a_: a_.reshape(1, -1)
    const = lambda shp: pl.BlockSpec(shp, lambda i, j: (0,) * len(shp))
    tok = pl.BlockSpec((1, tm, width), lambda i, j: (i, j, 0))
    n_chunk = tm // CHUNK
    ti = jnp.arange(tm, dtype=I32)
    tril_bd = (((ti[:, None] >> CHUNK_SHIFT) == (ti[None, :] >> CHUNK_SHIFT))
               & (ti[None, :] <= ti[:, None])).astype(BF16)
    return pl.pallas_call(
        functools.partial(_rwkv_prep_kernel, width=width),
        out_shape=([jax.ShapeDtypeStruct((b, t, width), BF16)] * 7
                   + [jax.ShapeDtypeStruct((b, t, width), F32)] * 2
                   + [jax.ShapeDtypeStruct((b, t // CHUNK, 1, width), F32)]),
        grid=(b, t // tm),
        in_specs=[pl.BlockSpec((1, tm, cols), lambda i, j: (i, j, 0)),
                  const((1, cols)), const((1, width)), const((1, width)), const((1, width)),
                  const((1, width)), const((1, width)),
                  const((2, LANES, 2 * width)), const((2, LANES, width)), const((tm, tm))],
        out_specs=[tok] * 9 + [pl.BlockSpec((1, n_chunk, 1, width), lambda i, j: (i, j, 0, 0))],
        scratch_shapes=[pltpu.VMEM((1, cols), F32)],
        compiler_params=_cparams(("parallel", "arbitrary")),
        name="rwkv_prep",
    )(p_a, row(mu), row(w0), row(a0), row(k_k), row(k_a), row(r_k),
      jnp.stack(_hi_lo(wcomb)), jnp.stack(_hi_lo(g2)), tril_bd)


def _rwkv_scan_kernel(rt_ref, kkt_ref, kh_ref, bh_ref, kb_ref, bb_ref, v_ref, g_ref, bonus_ref,
                      gc_ref, lnw_ref, lnb_ref, o_ref, st_ref, *, n_batch, n_pairs):
    c = CHUNK

    @pl.when(pl.program_id(0) == 0)
    def _():
        st_ref[...] = jnp.zeros_like(st_ref)

    rr = _iota((LANES, LANES), 0)
    cc = _iota((LANES, LANES), 1)
    mask_bd = jnp.where((rr >> 6) == (cc >> 6), 1.0, 0.0)
    mask_bd16 = mask_bd.astype(BF16)
    t_i = _iota((c, LANES), 0)
    s_i = _iota((c, LANES), 1) & (c - 1)
    strict = jnp.where(s_i < t_i, 1.0, 0.0)
    incl = jnp.where(s_i <= t_i, 1.0, 0.0)
    eye_ss = jnp.where(s_i == t_i, 1.0, 0.0)

    def bd16(x):
        return jnp.concatenate([x, x], axis=0) * mask_bd16

    def b16(xs):
        return [x.astype(BF16) for x in xs]

    chains = [(bi, p) for bi in range(n_batch) for p in range(n_pairs)]
    n_ch = len(chains)
    idx = range(n_ch)

    def ld(ref):
        return [ref[bi, :, p * LANES:(p + 1) * LANES] for bi, p in chains]

    rt, kkt, kh, bh, kb, bb, v = (ld(ref) for ref in
                                  (rt_ref, kkt_ref, kh_ref, bh_ref, kb_ref, bb_ref, v_ref))

    lhs = [jnp.concatenate([kkt[i], rt[i]], axis=0) for i in idx]
    rhs = [jnp.concatenate([bd16(kh[i]), bd16(bh[i])], axis=0) for i in idx]
    gm = [_dot_nt(lhs[i], rhs[i]) for i in idx]
    l_kv = [gm[i][:c, :LANES] * strict for i in idx]
    xp = [-(gm[i][:c, LANES:] * strict) for i in idx]
    pm = [gm[i][c:, :LANES] * incl for i in idx]
    nqm = [-(gm[i][c:, LANES:] * incl) for i in idx]

    tinv = [eye_ss + xp[i] for i in idx]
    xp16 = b16(xp)
    xpb = [bd16(x) for x in xp16]
    for _ in range(c.bit_length() - 2):
        xp16 = b16([_dot(xp16[i], xpb[i]) for i in idx])
        xpb = [bd16(x) for x in xp16]
        t16 = b16(tinv)
        tinv = [tinv[i] + _dot(t16[i], xpb[i]) for i in idx]
    tinv16 = b16(tinv)

    v_bd = [bd16(x) for x in v]
    l_kv16 = b16(l_kv)
    w16 = b16([_dot(l_kv16[i], v_bd[i]) for i in idx])
    tkw = [_dot(tinv16[i], jnp.concatenate([bd16(kkt[i]), bd16(w16[i])], axis=1)) for i in idx]
    tk16 = b16([x[:, :LANES] for x in tkw])

    st = [st_ref[i] for i in idx]
    st16 = b16(st)
    u = [_dot_nt(tk16[i], st16[i]) + tkw[i][:, LANES:] for i in idx]
    u16 = b16(u)
    pq16 = [jnp.concatenate([pm[i], nqm[i]], axis=1).astype(BF16) for i in idx]
    vu = [jnp.concatenate([v_bd[i], bd16(u16[i])], axis=0) for i in idx]
    y = [_dot_nt(rt[i], st16[i]) + _dot(pq16[i], vu[i]) for i in idx]
    upd = [_dot_tn(jnp.concatenate([v[i], -u16[i]], axis=0),
                   jnp.concatenate([kb[i], bb[i]], axis=0)) for i in idx]
    for i, (bi, p) in enumerate(chains):
        st_ref[i] = st[i] * gc_ref[bi, 0, :, p * LANES:(p + 1) * LANES] + upd[i] * mask_bd

    ys = jnp.concatenate(y, axis=0)
    mean = _dot_exact_rhs(ys, mask_bd16) * (1.0 / HEAD64)
    d = ys - mean
    var = _dot_exact_rhs(d * d, mask_bd16) * (1.0 / HEAD64)
    dn = d * lax.rsqrt(var + RWKV_GN_EPS)
    for i, (bi, p) in enumerate(chains):
        sl = slice(p * LANES, (p + 1) * LANES)
        yn = dn[i * c:(i + 1) * c] * lnw_ref[:, sl] + lnb_ref[:, sl]
        o_ref[bi, :, sl] = ((yn + bonus_ref[bi, :, sl]) * g_ref[bi, :, sl]).astype(o_ref.dtype)


def _rwkv_scan(rt, kkt, kh, bh, kb, bb, v, g, bonus, gc, ln_w, ln_b):
    bsz, t, width = rt.shape
    n_pairs = width // LANES
    tok = pl.BlockSpec((bsz, CHUNK, width), lambda j: (0, j, 0))
    const = pl.BlockSpec((1, width), lambda j: (0, 0))
    return pl.pallas_call(
        functools.partial(_rwkv_scan_kernel, n_batch=bsz, n_pairs=n_pairs),
        out_shape=jax.ShapeDtypeStruct((bsz, t, width), BF16),
        grid=(t // CHUNK,),
        in_specs=[tok] * 9 + [pl.BlockSpec((bsz, 1, 1, width), lambda j: (0, j, 0, 0)),
                              const, const],
        out_specs=tok,
        scratch_shapes=[pltpu.VMEM((bsz * n_pairs, LANES, LANES), F32)],
        compiler_params=_cparams(("arbitrary",)),
        name="rwkv_scan",
    )(rt, kkt, kh, bh, kb, bb, v, g, bonus, gc, ln_w.reshape(1, -1), ln_b.reshape(1, -1))


def _lru_kernel(p_ref, cw_ref, cb_ref, wg_ref, ba_ref, bx_ref, lam_ref, o_ref,
                xcarry_ref, hcarry_ref, a_s, u_s, *, width):
    tm = p_ref.shape[1]

    @pl.when(pl.program_id(1) == 0)
    def _():
        xcarry_ref[...] = jnp.zeros_like(xcarry_ref)
        hcarry_ref[...] = jnp.zeros_like(hcarry_ref)

    gate = p_ref[0, :, 0:width]
    xb = p_ref[0, :, width:2 * width]
    carry8 = xcarry_ref[...]
    row8 = _iota((8, width), 0)

    def shifted(s):
        rolled = pltpu.roll(xb, s, axis=0)
        first = jnp.where(row8 < s, pltpu.roll(carry8, s, axis=0), rolled[0:8])
        return jnp.concatenate([first, rolled[8:]], axis=0)

    xc = (cw_ref[0:1, :] * shifted(3) + cw_ref[1:2, :] * shifted(2)
          + cw_ref[2:3, :] * shifted(1) + cw_ref[3:4, :] * xb + cb_ref[...])
    xcarry_ref[...] = xb[tm - 8:tm, :]

    gates = _dot(xc.astype(BF16), wg_ref[...])
    rg = _sigmoid(gates[:, :width] + ba_ref[...])
    ig = _sigmoid(gates[:, width:] + bx_ref[...])
    log_a = -LRU_C * rg * _softplus(-lam_ref[...])
    a = jnp.exp(log_a)
    a_s[...] = a
    u_s[...] = jnp.sqrt(1.0 - a * a) * ig * xc

    m1 = row8 >= 1
    m2 = row8 >= 2
    m4 = row8 >= 4

    def body(i, h):
        off = pl.multiple_of(i * 8, 8)
        a8 = a_s[pl.ds(off, 8), :]
        u8 = u_s[pl.ds(off, 8), :]
        for s, m in ((1, m1), (2, m2), (4, m4)):
            u_sh = jnp.where(m, pltpu.roll(u8, s, axis=0), 0.0)
            a_sh = jnp.where(m, pltpu.roll(a8, s, axis=0), 1.0)
            u8 = u8 + a8 * u_sh
            a8 = a8 * a_sh
        h8 = u8 + a8 * h
        u_s[pl.ds(off, 8), :] = h8
        return jnp.broadcast_to(h8[7:8, :], (8, width))

    h_last = lax.fori_loop(0, tm // 8, body, hcarry_ref[...])
    hcarry_ref[...] = h_last
    o_ref[0] = (u_s[...] * _gelu_tanh(gate)).astype(o_ref.dtype)


def _block_diag(w):
    nb, di, do = w.shape
    eye = jnp.eye(nb, dtype=w.dtype)
    return (eye[:, None, :, None] * w[:, :, None, :]).reshape(nb * di, nb * do)


def _lru(p_b, conv_w, conv_b, w_a, b_a, w_x, b_x, lam, tm=512):
    b, t, cols = p_b.shape
    width = cols // 2
    wg = jnp.concatenate([_block_diag(w_a), _block_diag(w_x)], axis=1).astype(BF16)
    row = lambda a_: a_.reshape(1, -1)
    const = lambda shp: pl.BlockSpec(shp, lambda i, j: (0, 0))
    return pl.pallas_call(
        functools.partial(_lru_kernel, width=width),
        out_shape=jax.ShapeDtypeStruct((b, t, width), BF16),
        grid=(b, t // tm),
        in_specs=[pl.BlockSpec((1, tm, cols), lambda i, j: (i, j, 0)),
                  const(conv_w.shape), const((1, width)), const((width, 2 * width)),
                  const((1, width)), const((1, width)), const((1, width))],
        out_specs=pl.BlockSpec((1, tm, width), lambda i, j: (i, j, 0)),
        scratch_shapes=[pltpu.VMEM((8, width), F32), pltpu.VMEM((8, width), F32),
                        pltpu.VMEM((tm, width), F32), pltpu.VMEM((tm, width), F32)],
        compiler_params=_cparams(("parallel", "arbitrary")),
        name="rglru",
    )(p_b, conv_w, row(conv_b), wg, row(b_a), row(b_x), row(lam))


def _gla_kernel(q_ref, f_ref, i_ref, g_ref, lb_ref, nw_ref, o_ref, st_ref, *, n_batch, n_heads):
    c = CHUNK

    @pl.when(pl.program_id(0) == 0)
    def _():
        st_ref[...] = jnp.zeros_like(st_ref)

    trilf = jnp.where(_iota((c, c), 1) <= _iota((c, c), 0), 1.0, 0.0)
    tril16 = trilf.astype(BF16)
    mid = c // 2 - 1
    lb = lb_ref[...]

    qm, km, qg, kb, v16, gc = [], [], [], [], [], []
    for bi in range(n_batch):
        q = _silu(q_ref[bi].astype(F32))
        fg = lb + (1.0 - lb) * _sigmoid(f_ref[bi])
        k = 1.0 - fg
        cum = _dot_exact_lhs(tril16, jnp.log(fg))
        cum_c = cum[c - 1:c, :]
        cum_m = cum[mid:mid + 1, :]
        qm.append((q * jnp.exp(cum - cum_m)).astype(BF16))
        km.append((k * jnp.exp(cum_m - cum)).astype(BF16))
        qg.append((q * jnp.exp(cum)).astype(BF16))
        kb.append((k * jnp.exp(cum_c - cum)).astype(BF16))
        v16.append(i_ref[bi].astype(BF16))
        gc.append(jnp.exp(cum_c))

    chains = [(bi, h) for bi in range(n_batch) for h in range(n_heads)]
    idx = range(len(chains))

    def hs(xs):
        return [xs[bi][:, h * LANES:(h + 1) * LANES] for bi, h in chains]

    qm_c, km_c, qg_c, kb_c, v_c, gc_c = hs(qm), hs(km), hs(qg), hs(kb), hs(v16), hs(gc)
    scores = [(_dot_nt(qm_c[i], km_c[i]) * trilf).astype(BF16) for i in idx]
    st = [st_ref[i] for i in idx]
    st16 = [s.astype(BF16) for s in st]
    o = [_dot(scores[i], v_c[i]) + _dot_nt(qg_c[i], st16[i]) for i in idx]
    for i in idx:
        st_ref[i] = st[i] * gc_c[i] + _dot_tn(v_c[i], kb_c[i])

    for i, (bi, h) in enumerate(chains):
        sl = slice(h * LANES, (h + 1) * LANES)
        on = o[i] * lax.rsqrt(jnp.mean(o[i] * o[i], axis=-1, keepdims=True) + RMS_EPS)
        gate = _silu(g_ref[bi, :, sl].astype(F32))
        o_ref[bi, :, sl] = (on * nw_ref[:, sl] * gate).astype(o_ref.dtype)


def _gla(q, f, i, g, lower_bound, norm_w):
    bsz, t, width = q.shape
    n_heads = width // LANES
    tok = pl.BlockSpec((bsz, CHUNK, width), lambda j: (0, j, 0))
    const = pl.BlockSpec((1, width), lambda j: (0, 0))
    return pl.pallas_call(
        functools.partial(_gla_kernel, n_batch=bsz, n_heads=n_heads),
        out_shape=jax.ShapeDtypeStruct((bsz, t, width), BF16),
        grid=(t // CHUNK,),
        in_specs=[tok] * 4 + [const, const],
        out_specs=tok,
        scratch_shapes=[pltpu.VMEM((bsz * n_heads, LANES, LANES), F32)],
        compiler_params=_cparams(("arbitrary",)),
        name="hgrn2_gla",
    )(q, f, i, g, lower_bound.reshape(1, -1), norm_w.reshape(1, -1))


def _router_kernel(x_ref, g_ref, w_ref, b_ref, upper_ref, e_o, gate_o, rank_o, cnt_o):
    x = x_ref[...]
    ms = jnp.mean(x * x, axis=-1, keepdims=True)
    hn = x * lax.rsqrt(ms + RMS_EPS) * g_ref[...]
    lt = _dot_split(hn, w_ref[0], w_ref[1]).T + b_ref[:, 0:1]
    tm = x.shape[0]
    gl = lt[0:8, :]
    row8 = _iota((8, tm), 0)
    gl = jnp.where(row8 < N_GROUPS, gl, -jnp.inf)
    gmax = jnp.max(gl, axis=0, keepdims=True)
    g_sel = jnp.min(jnp.where(gl == gmax, row8, 8), axis=0, keepdims=True)
    g_gate = 1.0 / jnp.sum(jnp.exp(gl - gmax), axis=0, keepdims=True)

    el = jnp.zeros((EXPERTS_PER_GROUP, tm), F32)
    for gi in range(N_GROUPS):
        lo = 8 + gi * EXPERTS_PER_GROUP
        el = jnp.where(g_sel == gi, lt[lo:lo + EXPERTS_PER_GROUP, :], el)
    m1 = jnp.max(el, axis=0, keepdims=True)
    i1 = jnp.min(jnp.where(el == m1, row8, 8), axis=0, keepdims=True)
    el2 = jnp.where(row8 == i1, -jnp.inf, el)
    m2 = jnp.max(el2, axis=0, keepdims=True)
    i2 = jnp.min(jnp.where(el2 == m2, row8, 8), axis=0, keepdims=True)
    e2 = jnp.exp(m2 - m1)
    inv = 1.0 / (1.0 + e2)
    ea = g_sel * EXPERTS_PER_GROUP + i1
    eb = g_sel * EXPERTS_PER_GROUP + i2
    e_o[...] = jnp.concatenate([ea, eb], axis=0)
    gate_o[...] = jnp.concatenate([g_gate * inv, g_gate * e2 * inv], axis=0)

    erow = _iota((N_EXPERTS, tm), 0)
    oh_a = jnp.where(erow == ea, 1.0, 0.0)
    oh_b = jnp.where(erow == eb, 1.0, 0.0)
    pre_a = _dot(oh_a.astype(BF16), upper_ref[...])
    pre_b = _dot(oh_b.astype(BF16), upper_ref[...])
    cnt_a = jnp.sum(oh_a, axis=1, keepdims=True)
    cnt_b = jnp.sum(oh_b, axis=1, keepdims=True)
    rank_a = jnp.sum(oh_a * pre_a, axis=0, keepdims=True)
    rank_b = jnp.sum(oh_b * (pre_b + cnt_a), axis=0, keepdims=True)
    rank_o[...] = jnp.concatenate([rank_a, rank_b], axis=0).astype(I32)
    cnt_o[0] = jnp.broadcast_to(cnt_a + cnt_b, (N_EXPERTS, LANES))


def _router(x2d, g, w_group, b_group, w_expert, b_expert, tm):
    n, d = x2d.shape
    wt = jnp.zeros((d, LANES), F32)
    wt = wt.at[:, 0:N_GROUPS].set(w_group).at[:, 8:8 + N_EXPERTS].set(w_expert)
    wt = jnp.stack(_hi_lo(wt))
    bt = jnp.zeros((LANES,), F32)
    bt = bt.at[0:N_GROUPS].set(b_group).at[8:8 + N_EXPERTS].set(b_expert)
    bt = jnp.broadcast_to(bt[:, None], (LANES, LANES))
    ti = jnp.arange(tm, dtype=I32)
    upper = (ti[:, None] < ti[None, :]).astype(BF16)
    return pl.pallas_call(
        _router_kernel,
        out_shape=[jax.ShapeDtypeStruct((2, n), I32),
                   jax.ShapeDtypeStruct((2, n), F32),
                   jax.ShapeDtypeStruct((2, n), I32),
                   jax.ShapeDtypeStruct((n // tm, N_EXPERTS, LANES), F32)],
        grid=(n // tm,),
        in_specs=[pl.BlockSpec((tm, d), lambda i: (i, 0)),
                  pl.BlockSpec((1, d), lambda i: (0, 0)),
                  pl.BlockSpec((2, d, LANES), lambda i: (0, 0, 0)),
                  pl.BlockSpec((LANES, LANES), lambda i: (0, 0)),
                  pl.BlockSpec((tm, tm), lambda i: (0, 0))],
        out_specs=[pl.BlockSpec((2, tm), lambda i: (0, i)),
                   pl.BlockSpec((2, tm), lambda i: (0, i)),
                   pl.BlockSpec((2, tm), lambda i: (0, i)),
                   pl.BlockSpec((1, N_EXPERTS, LANES), lambda i: (i, 0, 0))],
        compiler_params=_cparams(("parallel",)),
        name="moe_router",
    )(x2d, g.reshape(1, d), wt, bt, upper)


def _seg_local_rows(tm):
    return 2 * tm + N_EXPERTS * SUBLANES


def _segment_copies(cnt_ref, off_ref, dst_ref, base, local, remote, sem, to_remote, wait):
    def body(e, carry):
        c = pl.multiple_of(cnt_ref[base + e], SUBLANES)

        @pl.when(c > 0)
        def _():
            off = 0 if off_ref is None else pl.multiple_of(off_ref[base + e], SUBLANES)
            loc = local.at[pl.ds(off, c), :]
            rem = remote.at[pl.ds(pl.multiple_of(dst_ref[base + e], SUBLANES), c), :]
            cp = (pltpu.make_async_copy(loc, rem, sem) if to_remote
                  else pltpu.make_async_copy(rem, loc, sem))
            if wait:
                cp.wait()
            else:
                cp.start()
        return carry
    lax.fori_loop(0, N_EXPERTS, body, 0)


def _dispatch_kernel(cnt_ref, off_ref, dst_ref, tcnt_ref, tdst_ref, nused_ref,
                     x_ref, g_ref, lidx_ref, xs_out, sbuf, zbuf, sems, zsem):
    j = pl.program_id(0)
    nb = pl.num_programs(0)
    tm = x_ref.shape[0]
    slot = lax.rem(j, 2)

    def copies(tile, s, wait):
        _segment_copies(cnt_ref, off_ref, dst_ref, tile * N_EXPERTS, sbuf.at[s], xs_out,
                        sems.at[s], True, wait)

    @pl.when(j == 0)
    def _():
        zbuf[...] = jnp.zeros_like(zbuf)
        rows = zbuf.shape[0]
        n_blocks = xs_out.shape[0] // rows
        for wait in (False, True):
            _segment_copies(tcnt_ref, None, tdst_ref, 0, zbuf, xs_out, zsem, True, wait)

            def body(b, carry):
                start = pl.multiple_of(b * rows, SUBLANES)
                cp = pltpu.make_async_copy(zbuf, xs_out.at[pl.ds(start, rows), :], zsem)
                if wait:
                    cp.wait()
                else:
                    cp.start()
                return carry
            lax.fori_loop(nused_ref[0], n_blocks, body, 0)

    @pl.when(j >= 2)
    def _():
        copies(j - 2, slot, True)

    x = x_ref[...]
    ms = jnp.mean(x * x, axis=-1, keepdims=True)
    hn = (x * lax.rsqrt(ms + RMS_EPS) * g_ref[...]).astype(BF16)
    r = _iota((_seg_local_rows(tm), tm), 0)
    perm = jnp.where((r == lidx_ref[0:1, :]) | (r == lidx_ref[1:2, :]), 1.0, 0.0).astype(BF16)
    sbuf[slot] = _pack_pairs(_dot(perm, hn))
    copies(j, slot, False)

    @pl.when(j == nb - 1)
    def _():
        copies(j, slot, True)

    @pl.when((j == nb - 1) & (nb >= 2))
    def _():
        copies(j - 1, 1 - slot, True)


def _dispatch(x2d, g, lidx, seg, tails, n_used, n_rows, rows, tm):
    n, d = x2d.shape
    return pl.pallas_call(
        _dispatch_kernel,
        out_shape=jax.ShapeDtypeStruct((n_rows, d // 2), U32),
        grid_spec=pltpu.PrefetchScalarGridSpec(
            num_scalar_prefetch=6,
            grid=(n // tm,),
            in_specs=[pl.BlockSpec((tm, d), lambda j, *_: (j, 0)),
                      pl.BlockSpec((1, d), lambda j, *_: (0, 0)),
                      pl.BlockSpec((2, tm), lambda j, *_: (0, j))],
            out_specs=pl.BlockSpec(memory_space=pl.ANY),
            scratch_shapes=[pltpu.VMEM((2, _seg_local_rows(tm), d // 2), U32),
                            pltpu.VMEM((rows, d // 2), U32),
                            pltpu.SemaphoreType.DMA((2,)), pltpu.SemaphoreType.DMA(())]),
        compiler_params=_cparams(("arbitrary",)),
        name="moe_dispatch",
    )(*seg, *tails, n_used, x2d, g.reshape(1, d), lidx)


def _expert_kernel(be_ref, first_ref, nused_ref, x_ref, wg_ref, wu_ref, wd_ref, o_ref,
                   wg16, wu16, wd16):
    j = pl.program_id(0)

    @pl.when(first_ref[j] == 1)
    def _():
        wg16[...] = wg_ref[0, 0].astype(BF16)
        wu16[...] = wu_ref[0, 0].astype(BF16)
        wd16[...] = wd_ref[0, 0].astype(BF16)

    @pl.when(j < nused_ref[0])
    def _():
        x_lo, x_hi = _unpack_pairs(x_ref[...])
        half = x_lo.shape[1]
        hg = _dot(x_lo, wg16[:half, :]) + _dot(x_hi, wg16[half:, :])
        hu = _dot(x_lo, wu16[:half, :]) + _dot(x_hi, wu16[half:, :])
        hid = (_silu(hg) * hu).astype(BF16)
        o_ref[...] = _pack_pairs(_dot(hid, wd16[...]))

    @pl.when(j >= nused_ref[0])
    def _():
        o_ref[...] = jnp.zeros_like(o_ref)


def _expert_ffn(xs, block_e, first_flag, n_used, w_gate, w_up, w_down, layer, rows):
    n_rows, dh = xs.shape
    n_blocks = n_rows // rows
    d, ff = w_gate.shape[-2:]
    assert d == 2 * dh
    wspec = lambda shp: pl.BlockSpec((1, 1) + shp, lambda j, be, fi, nu: (layer, be[j], 0, 0))
    return pl.pallas_call(
        _expert_kernel,
        out_shape=jax.ShapeDtypeStruct((n_rows, dh), U32),
        grid_spec=pltpu.PrefetchScalarGridSpec(
            num_scalar_prefetch=3,
            grid=(n_blocks,),
            in_specs=[pl.BlockSpec((rows, dh),
                                   lambda j, be, fi, nu: (jnp.minimum(j, nu[0] - 1), 0)),
                      wspec((d, ff)), wspec((d, ff)), wspec((ff, d))],
            out_specs=pl.BlockSpec((rows, dh), lambda j, be, fi, nu: (j, 0)),
            scratch_shapes=[pltpu.VMEM((d, ff), BF16), pltpu.VMEM((d, ff), BF16),
                            pltpu.VMEM((ff, d), BF16)]),
        compiler_params=_cparams(("arbitrary",)),
        name="moe_expert_ffn",
    )(block_e, first_flag, n_used, xs, w_gate, w_up, w_down)


def _combine_kernel(cnt_ref, off_ref, dst_ref, x_ref, lidx_ref, gt_ref, ys_hbm, gfin_ref, o_ref,
                    ybuf, sems, *, final_norm):
    j = pl.program_id(0)
    nb = pl.num_programs(0)
    tm = x_ref.shape[0]
    slot = lax.rem(j, 2)

    def copies(tile, s, wait):
        _segment_copies(cnt_ref, off_ref, dst_ref, tile * N_EXPERTS, ybuf.at[s], ys_hbm,
                        sems.at[s], False, wait)

    @pl.when(j == 0)
    def _():
        ybuf[...] = jnp.zeros_like(ybuf)
        copies(0, 0, False)

    @pl.when(j + 1 < nb)
    def _():
        copies(j + 1, 1 - slot, False)

    copies(j, slot, True)
    c = _iota((tm, _seg_local_rows(tm)), 1)
    li = lidx_ref[...]
    gt = gt_ref[...]
    mix = (jnp.where(c == li[:, 0:1], gt[:, 0:1], 0.0)
           + jnp.where(c == li[:, 1:2], gt[:, 1:2], 0.0)).astype(BF16)
    y_lo, y_hi = _unpack_pairs(ybuf[slot])
    y = x_ref[...] + jnp.concatenate([_dot(mix, y_lo), _dot(mix, y_hi)], axis=1)
    if final_norm:
        ms = jnp.mean(y * y, axis=-1, keepdims=True)
        y = y * lax.rsqrt(ms + RMS_EPS) * gfin_ref[...]
    o_ref[...] = y


def _combine(x2d, ys, lidx_t, gates_t, seg, g_final, final_norm, tm):
    n, d = x2d.shape
    return pl.pallas_call(
        functools.partial(_combine_kernel, final_norm=final_norm),
        out_shape=jax.ShapeDtypeStruct((n, d), F32),
        grid_spec=pltpu.PrefetchScalarGridSpec(
            num_scalar_prefetch=3,
            grid=(n // tm,),
            in_specs=[pl.BlockSpec((tm, d), lambda j, *_: (j, 0)),
                      pl.BlockSpec((tm, 2), lambda j, *_: (j, 0)),
                      pl.BlockSpec((tm, 2), lambda j, *_: (j, 0)),
                      pl.BlockSpec(memory_space=pl.ANY),
                      pl.BlockSpec((1, d), lambda j, *_: (0, 0))],
            out_specs=pl.BlockSpec((tm, d), lambda j, *_: (j, 0)),
            scratch_shapes=[pltpu.VMEM((2, _seg_local_rows(tm), d // 2), U32),
                            pltpu.SemaphoreType.DMA((2,))]),
        compiler_params=_cparams(("arbitrary",)),
        name="moe_combine",
    )(*seg, x2d, lidx_t, gates_t, ys, g_final.reshape(1, d))


def _route_plan(e2, rank2, tile_cnt, rows, tm):
    n_tok = e2.shape[1]
    n_tiles = n_tok // tm
    seg_cnt = (tile_cnt + SUBLANES - 1) // SUBLANES * SUBLANES
    counts = jnp.sum(seg_cnt, axis=0)
    padded = (counts + rows - 1) // rows * rows
    pend = jnp.cumsum(padded)
    pstart = pend - padded
    seg_dst = pstart[None, :] + jnp.cumsum(seg_cnt, axis=0) - seg_cnt
    seg_off = jnp.cumsum(seg_cnt, axis=1) - seg_cnt
    eid = jnp.arange(N_EXPERTS, dtype=I32)
    e3 = e2.reshape(2, n_tiles, tm)
    lidx = rank2.reshape(2, n_tiles, tm) + jnp.sum(
        jnp.where(e3[None] == eid[:, None, None, None], seg_off.T[:, None, :, None], 0), axis=0)
    lidx = lidx.reshape(2, n_tok).astype(I32)
    n_blocks = -(-(2 * n_tok + n_tiles * N_EXPERTS * SUBLANES) // rows) + N_EXPERTS
    starts = jnp.arange(n_blocks, dtype=I32) * rows
    block_e = jnp.minimum(jnp.sum((starts[:, None] >= pend[None, :]).astype(I32), axis=1),
                          N_EXPERTS - 1)
    first = jnp.concatenate([jnp.ones((1,), I32),
                             (block_e[1:] != block_e[:-1]).astype(I32)])
    n_used = (pend[-1] // rows).astype(I32).reshape(1)
    seg = tuple(a.reshape(-1).astype(I32) for a in (seg_cnt, seg_off, seg_dst))
    tails = ((padded - counts).astype(I32), (pstart + counts).astype(I32))
    return lidx, seg, tails, block_e, first, n_used, n_blocks * rows


def _moe(x2d, g_ffn, w_group, b_group, w_expert, b_expert, w_gate, w_up, w_down, layer,
         g_final, final_norm, rows=ROUTE_ROWS, tm=ROUTE_TILE):
    tm = min(tm, x2d.shape[0])
    e2, gates2, rank2, cnt = _router(x2d, g_ffn, w_group, b_group, w_expert, b_expert, tm)
    lidx, seg, tails, block_e, first, n_used, n_rows = _route_plan(
        e2, rank2, cnt[:, :, 0].astype(I32), rows, tm)
    xs = _dispatch(x2d, g_ffn, lidx, seg, tails, n_used, n_rows, rows, tm)
    ys = _expert_ffn(xs, block_e, first, n_used, w_gate, w_up, w_down, layer, rows)
    return _combine(x2d, ys, lidx.T, gates2.T, seg, g_final, final_norm, tm)


def kernel(x, norm_mix, norm_ffn, norm_final, ab_w_in, rw_mu, rw_w0, rw_w2, rw_a0, rw_a2, rw_g2, rw_k_k, rw_k_a, rw_r_k, rw_ln_w, rw_ln_b, lru_conv_w, lru_conv_b, lru_w_a, lru_b_a, lru_w_x, lru_b_x, lru_lambda, ab_w_out, c_w_in, c_lower_bound, c_norm_w, c_w_out, moe_w_group, moe_b_group, moe_w_expert, moe_b_expert, moe_w_gate, moe_w_up, moe_w_down):
    bsz, t, d = x.shape
    depth = norm_mix.shape[0]
    n = bsz * t
    lbs = jnp.cumsum(jax.nn.softmax(c_lower_bound.astype(F32), axis=0), axis=0)
    lbs = lbs - lbs[0]
    x2d = x.reshape(n, d)
    for layer in range(depth):
        j = layer // 2
        if layer % 2 == 0:
            rw_cols = rw_mu.shape[1]
            lru_cols = ab_w_in.shape[2] - rw_cols
            width = rw_w0.shape[1]
            p_a, p_b = _norm_matmul(x2d, norm_mix[layer], ab_w_in[j], (rw_cols, lru_cols),
                                    (F32, F32))
            prep = _rwkv_prep(p_a.reshape(bsz, t, rw_cols), rw_mu[j], rw_w0[j], rw_w2[j],
                              rw_a0[j], rw_a2[j], rw_g2[j], rw_k_k[j], rw_k_a[j],
                              rw_r_k[j].reshape(-1))
            ya = _rwkv_scan(*prep, rw_ln_w[j], rw_ln_b[j])
            yb = _lru(p_b.reshape(bsz, t, lru_cols), lru_conv_w[j], lru_conv_b[j], lru_w_a[j],
                      lru_b_a[j], lru_w_x[j], lru_b_x[j], lru_lambda[j])
            x2d = _proj_residual(x2d, [ya.reshape(n, width), yb.reshape(n, -1)],
                                 [ab_w_out[j][:width], ab_w_out[j][width:]])
        else:
            hw = c_norm_w.shape[1]
            q, f, i_, g = _norm_matmul(x2d, norm_mix[layer], c_w_in[j], (hw,) * 4,
                                       (BF16, F32, BF16, BF16))
            shp = (bsz, t, hw)
            o = _gla(q.reshape(shp), f.reshape(shp), i_.reshape(shp), g.reshape(shp),
                     lbs[layer], c_norm_w[j])
            x2d = _proj_residual(x2d, [o.reshape(n, hw)], [c_w_out[j]])
        last = layer == depth - 1
        x2d = _moe(x2d, norm_ffn[layer], moe_w_group[layer], moe_b_group[layer],
                   moe_w_expert[layer], moe_b_expert[layer], moe_w_gate, moe_w_up,
                   moe_w_down, layer, norm_final, last)
    return x2d.reshape(bsz, t, d)
```

```python
import functools

import jax
import jax.numpy as jnp
from jax import lax
from jax.experimental import pallas as pl
from jax.experimental.pallas import tpu as pltpu

F32 = jnp.float32
BF16 = jnp.bfloat16
I32 = jnp.int32
U32 = jnp.uint32

RMS_EPS = 1e-6
RWKV_GN_EPS = 64e-5
LRU_C = 8.0
CHUNK = 64
CHUNK_SHIFT = CHUNK.bit_length() - 1
HEAD64 = 64
LANES = 128
SUBLANES = 8
N_GROUPS = 4
EXPERTS_PER_GROUP = 8
N_EXPERTS = N_GROUPS * EXPERTS_PER_GROUP
ROUTE_ROWS = 512
ROUTE_TILE = 512
VMEM_LIMIT = 56 * 1024 * 1024


def _cparams(sem):
    return pltpu.CompilerParams(dimension_semantics=sem, vmem_limit_bytes=VMEM_LIMIT)


def _sigmoid(x):
    return 0.5 * jnp.tanh(0.5 * x) + 0.5


def _pack_pairs(x):
    w = x.shape[1] // 2
    lo = lax.bitcast_convert_type(x[:, :w].astype(BF16).astype(F32), U32)
    hi = lax.bitcast_convert_type(x[:, w:].astype(BF16).astype(F32), U32)
    return (lo >> 16) | hi


def _unpack_pairs(p):
    lo = lax.bitcast_convert_type(p << 16, F32).astype(BF16)
    hi = lax.bitcast_convert_type(p & jnp.uint32(0xFFFF0000), F32).astype(BF16)
    return lo, hi


def _softplus(x):
    return jnp.maximum(x, 0.0) + jnp.log(1.0 + jnp.exp(-jnp.abs(x)))


def _silu(x):
    return x * _sigmoid(x)


def _gelu_tanh(x):
    return 0.5 * x * (1.0 + jnp.tanh(0.7978845608028654 * (x + 0.044715 * x * x * x)))


def _dot(a, b):
    return jnp.dot(a, b, preferred_element_type=F32)


def _dot_nt(a, b):
    return lax.dot_general(a, b, (((1,), (1,)), ((), ())), preferred_element_type=F32)


def _dot_tn(a, b):
    return lax.dot_general(a, b, (((0,), (0,)), ((), ())), preferred_element_type=F32)


def _split2(x):
    hi = x.astype(BF16)
    lo = (x - hi.astype(F32)).astype(BF16)
    return hi, lo


def _dot_exact_rhs(x, m_bf16):
    hi, lo = _split2(x)
    return _dot(hi, m_bf16) + _dot(lo, m_bf16)


def _dot_exact_lhs(m_bf16, x):
    hi, lo = _split2(x)
    return _dot(m_bf16, hi) + _dot(m_bf16, lo)


def _dot_split(a, w_hi, w_lo):
    a_hi, a_lo = _split2(a)
    return _dot(a_hi, w_hi) + (_dot(a_lo, w_hi) + _dot(a_hi, w_lo))


def _hi_lo(w):
    hi = w.astype(BF16)
    return hi, (w - hi.astype(F32)).astype(BF16)


def _iota(shape, dim):
    return lax.broadcasted_iota(I32, shape, dim)


def _norm_matmul_kernel(x_ref, g_ref, w_ref, *o_refs, splits):
    x = x_ref[...]
    ms = jnp.mean(x * x, axis=-1, keepdims=True)
    y = (x * lax.rsqrt(ms + RMS_EPS) * g_ref[...]).astype(BF16)
    off = 0
    for o_ref, n in zip(o_refs, splits):
        o_ref[...] = _dot(y, w_ref[:, off:off + n]).astype(o_ref.dtype)
        off += n


def _norm_matmul(x2d, g, w, splits, out_dtypes, tm=512):
    n, d = x2d.shape
    ncols = w.shape[1]
    assert sum(splits) == ncols and n % tm == 0 and len(out_dtypes) == len(splits)
    return pl.pallas_call(
        functools.partial(_norm_matmul_kernel, splits=splits),
        out_shape=[jax.ShapeDtypeStruct((n, s), dt) for s, dt in zip(splits, out_dtypes)],
        grid=(n // tm,),
        in_specs=[pl.BlockSpec((tm, d), lambda i: (i, 0)),
                  pl.BlockSpec((1, d), lambda i: (0, 0)),
                  pl.BlockSpec((d, ncols), lambda i: (0, 0))],
        out_specs=[pl.BlockSpec((tm, s), lambda i: (i, 0)) for s in splits],
        compiler_params=_cparams(("parallel",)),
        name="norm_matmul",
    )(x2d, g.reshape(1, d), w.astype(BF16))


def _proj_route_kernel(*refs, n_in):
    x_ref = refs[0]
    y_refs = refs[1:1 + n_in]
    w_refs = refs[1 + n_in:1 + 2 * n_in]
    route_in = refs[1 + 2 * n_in:5 + 2 * n_in]
    o_ref = refs[5 + 2 * n_in]
    route_out = refs[6 + 2 * n_in:]
    acc = x_ref[...]
    for y_ref, w_ref in zip(y_refs, w_refs):
        acc = acc + _dot(y_ref[...].astype(BF16), w_ref[...])
    o_ref[...] = acc
    _route_tile(acc, *route_in, *route_out)


def _proj_route(x2d, ys, ws, g_ffn, w_group, b_group, w_expert, b_expert, tm):
    n, d = x2d.shape
    n_in = len(ys)
    wt = jnp.zeros((d, LANES), F32)
    wt = wt.at[:, 0:N_GROUPS].set(w_group).at[:, 8:8 + N_EXPERTS].set(w_expert)
    wt = jnp.stack(_hi_lo(wt))
    bt = jnp.zeros((LANES,), F32)
    bt = bt.at[0:N_GROUPS].set(b_group).at[8:8 + N_EXPERTS].set(b_expert)
    bt = jnp.broadcast_to(bt[:, None], (LANES, LANES))
    ti = jnp.arange(tm, dtype=I32)
    upper = (ti[:, None] < ti[None, :]).astype(BF16)
    in_specs = [pl.BlockSpec((tm, d), lambda i: (i, 0))]
    in_specs += [pl.BlockSpec((tm, y.shape[1]), lambda i: (i, 0)) for y in ys]
    in_specs += [pl.BlockSpec(w.shape, lambda i: (0, 0)) for w in ws]
    in_specs += [pl.BlockSpec((1, d), lambda i: (0, 0)),
                 pl.BlockSpec((2, d, LANES), lambda i: (0, 0, 0)),
                 pl.BlockSpec((LANES, LANES), lambda i: (0, 0)),
                 pl.BlockSpec((tm, tm), lambda i: (0, 0))]
    tok2 = pl.BlockSpec((2, tm), lambda i: (0, i))
    return pl.pallas_call(
        functools.partial(_proj_route_kernel, n_in=n_in),
        out_shape=[jax.ShapeDtypeStruct((n, d), F32),
                   jax.ShapeDtypeStruct((2, n), I32),
                   jax.ShapeDtypeStruct((2, n), F32),
                   jax.ShapeDtypeStruct((2, n), I32),
                   jax.ShapeDtypeStruct((n // tm, N_EXPERTS, LANES), F32)],
        grid=(n // tm,),
        in_specs=in_specs,
        out_specs=[pl.BlockSpec((tm, d), lambda i: (i, 0)), tok2, tok2, tok2,
                   pl.BlockSpec((1, N_EXPERTS, LANES), lambda i: (i, 0, 0))],
        compiler_params=_cparams(("parallel",)),
        name="proj_route",
    )(x2d, *ys, *[w.astype(BF16) for w in ws], g_ffn.reshape(1, d), wt, bt, upper)


def _rwkv_prep_kernel(p_ref, mu_ref, w0_ref, a0_ref, kk_s_ref, ka_ref, rk_ref,
                      wcomb_ref, g2_ref, tril_ref,
                      rt_o, kkt_o, kh_o, bh_o, kb_o, bb_o, v_o, g_o, bonus_o, gc_o,
                      prev_ref, *, width):
    tm = p_ref.shape[1]

    @pl.when(pl.program_id(1) == 0)
    def _():
        prev_ref[...] = jnp.zeros_like(prev_ref)

    p = p_ref[0]
    rolled = pltpu.roll(p, 1, axis=0)
    prev = jnp.where(_iota(p.shape, 0) == 0, prev_ref[...], rolled)
    prev_ref[...] = p[tm - 1:tm, :]
    ps = p + (prev - p) * mu_ref[...]

    r = ps[:, 0:width]
    k = ps[:, width:2 * width]
    v = ps[:, 2 * width:3 * width]
    lowrank = ps[:, 3 * width:3 * width + LANES]
    gl = ps[:, 3 * width + LANES:3 * width + 2 * LANES]

    lane = _iota(lowrank.shape, 1)
    lr_in = jnp.where(lane < HEAD64, jnp.tanh(lowrank), lowrank)
    t12 = _dot_split(lr_in, wcomb_ref[0], wcomb_ref[1])
    wlog = -_softplus(-(w0_ref[...] + t12[:, :width])) - 0.5
    lw = -jnp.exp(wlog)
    a = _sigmoid(a0_ref[...] + t12[:, width:])
    g = _dot_split(_sigmoid(gl), g2_ref[0], g2_ref[1])

    seg = jnp.where((_iota((LANES, LANES), 0) >> 6) == (_iota((LANES, LANES), 1) >> 6),
                    1.0, 0.0).astype(BF16)

    def head_sums(x):
        return jnp.concatenate([_dot_exact_rhs(x[:, q * LANES:(q + 1) * LANES], seg)
                                for q in range(width // LANES)], axis=1)

    kk = k * kk_s_ref[...]
    nrm = jnp.sqrt(head_sums(kk * kk))
    kk = kk / jnp.maximum(nrm, 1e-12)
    k2 = k * (1.0 + (a - 1.0) * ka_ref[...])
    bonus = head_sums(r * k2 * rk_ref[...]) * v

    b = kk * a

    n_chunk = tm // CHUNK
    cum = _dot_exact_lhs(tril_ref[...], lw)
    cum3 = cum.reshape(n_chunk, CHUNK, width)
    cend = cum3[:, CHUNK - 1:CHUNK, :]
    e_neg = jnp.exp(-cum)
    e_end = jnp.exp(cend - cum3).reshape(tm, width)

    rt_o[0] = (r * jnp.exp(cum)).astype(BF16)
    kkt_o[0] = (kk * jnp.exp(cum - lw)).astype(BF16)
    kh_o[0] = (k2 * e_neg).astype(BF16)
    bh_o[0] = (b * e_neg).astype(BF16)
    kb_o[0] = (k2 * e_end).astype(BF16)
    bb_o[0] = (b * e_end).astype(BF16)
    v_o[0] = v.astype(BF16)
    g_o[0] = g.astype(BF16)
    bonus_o[0] = bonus.astype(BF16)
    gc_o[0] = jnp.exp(cend)


def _rwkv_prep(p_a, mu, w0, w2, a0, a2, g2, k_k, k_a, r_k, tm=512):
    b, t, cols = p_a.shape
    width = w0.shape[0]
    rank = w2.shape[0]
    assert rank == HEAD64 and a2.shape[0] == HEAD64 and g2.shape[0] == LANES
    zeros = jnp.zeros((rank, width), F32)
    wcomb = jnp.concatenate([jnp.concatenate([w2, zeros], 1),
                             jnp.concatenate([zeros, a2], 1)], 0)
    row = lambda a_: a_.reshape(1, -1)
    const = lambda shp: pl.BlockSpec(shp, lambda i, j: (0,) * len(shp))
    tok = pl.BlockSpec((1, tm, width), lambda i, j: (i, j, 0))
    n_chunk = tm // CHUNK
    ti = jnp.arange(tm, dtype=I32)
    tril_bd = (((ti[:, None] >> CHUNK_SHIFT) == (ti[None, :] >> CHUNK_SHIFT))
               & (ti[None, :] <= ti[:, None])).astype(BF16)
    return pl.pallas_call(
        functools.partial(_rwkv_prep_kernel, width=width),
        out_shape=([jax.ShapeDtypeStruct((b, t, width), BF16)] * 9
                   + [jax.ShapeDtypeStruct((b, t // CHUNK, 1, width), F32)]),
        grid=(b, t // tm),
        in_specs=[pl.BlockSpec((1, tm, cols), lambda i, j: (i, j, 0)),
                  const((1, cols)), const((1, width)), const((1, width)), const((1, width)),
                  const((1, width)), const((1, width)),
                  const((2, LANES, 2 * width)), const((2, LANES, width)), const((tm, tm))],
        out_specs=[tok] * 9 + [pl.BlockSpec((1, n_chunk, 1, width), lambda i, j: (i, j, 0, 0))],
        scratch_shapes=[pltpu.VMEM((1, cols), F32)],
        compiler_params=_cparams(("parallel", "arbitrary")),
        name="rwkv_prep",
    )(p_a, row(mu), row(w0), row(a0), row(k_k), row(k_a), row(r_k),
      jnp.stack(_hi_lo(wcomb)), jnp.stack(_hi_lo(g2)), tril_bd)


def _rwkv_scan_kernel(rt_ref, kkt_ref, kh_ref, bh_ref, kb_ref, bb_ref, v_ref, g_ref, bonus_ref,
                      gc_ref, lnw_ref, lnb_ref, o_ref, st_ref, *, n_batch, n_pairs):
    c = CHUNK

    @pl.when(pl.program_id(0) == 0)
    def _():
        st_ref[...] = jnp.zeros_like(st_ref)

    rr = _iota((LANES, LANES), 0)
    cc = _iota((LANES, LANES), 1)
    mask_bd = jnp.where((rr >> 6) == (cc >> 6), 1.0, 0.0)
    mask_bd16 = mask_bd.astype(BF16)
    t_i = _iota((c, LANES), 0)
    s_i = _iota((c, LANES), 1) & (c - 1)
    strict = jnp.where(s_i < t_i, 1.0, 0.0)
    incl = jnp.where(s_i <= t_i, 1.0, 0.0)
    eye_ss = jnp.where(s_i == t_i, 1.0, 0.0)

    def bd16(x):
        return jnp.concatenate([x, x], axis=0) * mask_bd16

    def b16(xs):
        return [x.astype(BF16) for x in xs]

    chains = [(bi, p) for bi in range(n_batch) for p in range(n_pairs)]
    n_ch = len(chains)
    idx = range(n_ch)

    def ld(ref):
        return [ref[bi, :, p * LANES:(p + 1) * LANES] for bi, p in chains]

    rt, kkt, kh, bh, kb, bb, v = (ld(ref) for ref in
                                  (rt_ref, kkt_ref, kh_ref, bh_ref, kb_ref, bb_ref, v_ref))

    lhs = [jnp.concatenate([kkt[i], rt[i]], axis=0) for i in idx]
    rhs = [jnp.concatenate([bd16(kh[i]), bd16(bh[i])], axis=0) for i in idx]
    gm = [_dot_nt(lhs[i], rhs[i]) for i in idx]
    l_kv = [gm[i][:c, :LANES] * strict for i in idx]
    xp = [-(gm[i][:c, LANES:] * strict) for i in idx]
    pm = [gm[i][c:, :LANES] * incl for i in idx]
    nqm = [-(gm[i][c:, LANES:] * incl) for i in idx]

    tinv = [eye_ss + xp[i] for i in idx]
    xp16 = b16(xp)
    xpb = [bd16(x) for x in xp16]
    for _ in range(c.bit_length() - 2):
        xp16 = b16([_dot(xp16[i], xpb[i]) for i in idx])
        xpb = [bd16(x) for x in xp16]
        t16 = b16(tinv)
        tinv = [tinv[i] + _dot(t16[i], xpb[i]) for i in idx]
    tinv16 = b16(tinv)

    v_bd = [bd16(x) for x in v]
    l_kv16 = b16(l_kv)
    w16 = b16([_dot(l_kv16[i], v_bd[i]) for i in idx])
    tkw = [_dot(tinv16[i], jnp.concatenate([bd16(kkt[i]), bd16(w16[i])], axis=1)) for i in idx]
    tk16 = b16([x[:, :LANES] for x in tkw])

    st = [st_ref[i] for i in idx]
    st16 = b16(st)
    u = [_dot_nt(tk16[i], st16[i]) + tkw[i][:, LANES:] for i in idx]
    u16 = b16(u)
    pq16 = [jnp.concatenate([pm[i], nqm[i]], axis=1).astype(BF16) for i in idx]
    vu = [jnp.concatenate([v_bd[i], bd16(u16[i])], axis=0) for i in idx]
    y = [_dot_nt(rt[i], st16[i]) + _dot(pq16[i], vu[i]) for i in idx]
    upd = [_dot_tn(jnp.concatenate([v[i], -u16[i]], axis=0),
                   jnp.concatenate([kb[i], bb[i]], axis=0)) for i in idx]
    for i, (bi, p) in enumerate(chains):
        st_ref[i] = st[i] * gc_ref[bi, 0, :, p * LANES:(p + 1) * LANES] + upd[i] * mask_bd

    ys = jnp.concatenate(y, axis=0)
    mean = _dot_exact_rhs(ys, mask_bd16) * (1.0 / HEAD64)
    d = ys - mean
    var = _dot_exact_rhs(d * d, mask_bd16) * (1.0 / HEAD64)
    dn = d * lax.rsqrt(var + RWKV_GN_EPS)
    for i, (bi, p) in enumerate(chains):
        sl = slice(p * LANES, (p + 1) * LANES)
        yn = dn[i * c:(i + 1) * c] * lnw_ref[:, sl] + lnb_ref[:, sl]
        o_ref[bi, :, sl] = ((yn + bonus_ref[bi, :, sl]) * g_ref[bi, :, sl]).astype(o_ref.dtype)


def _rwkv_scan(rt, kkt, kh, bh, kb, bb, v, g, bonus, gc, ln_w, ln_b):
    bsz, t, width = rt.shape
    n_pairs = width // LANES
    tok = pl.BlockSpec((bsz, CHUNK, width), lambda j: (0, j, 0))
    const = pl.BlockSpec((1, width), lambda j: (0, 0))
    return pl.pallas_call(
        functools.partial(_rwkv_scan_kernel, n_batch=bsz, n_pairs=n_pairs),
        out_shape=jax.ShapeDtypeStruct((bsz, t, width), BF16),
        grid=(t // CHUNK,),
        in_specs=[tok] * 9 + [pl.BlockSpec((bsz, 1, 1, width), lambda j: (0, j, 0, 0)),
                              const, const],
        out_specs=tok,
        scratch_shapes=[pltpu.VMEM((bsz * n_pairs, LANES, LANES), F32)],
        compiler_params=_cparams(("arbitrary",)),
        name="rwkv_scan",
    )(rt, kkt, kh, bh, kb, bb, v, g, bonus, gc, ln_w.reshape(1, -1), ln_b.reshape(1, -1))


def _lru_kernel(p_ref, cw_ref, cb_ref, wg_ref, ba_ref, bx_ref, lam_ref, o_ref,
                xcarry_ref, hcarry_ref, a_s, u_s, *, width):
    tm = p_ref.shape[1]

    @pl.when(pl.program_id(1) == 0)
    def _():
        xcarry_ref[...] = jnp.zeros_like(xcarry_ref)
        hcarry_ref[...] = jnp.zeros_like(hcarry_ref)

    gate = p_ref[0, :, 0:width]
    xb = p_ref[0, :, width:2 * width]
    carry8 = xcarry_ref[...]
    row8 = _iota((8, width), 0)

    def shifted(s):
        rolled = pltpu.roll(xb, s, axis=0)
        first = jnp.where(row8 < s, pltpu.roll(carry8, s, axis=0), rolled[0:8])
        return jnp.concatenate([first, rolled[8:]], axis=0)

    xc = (cw_ref[0:1, :] * shifted(3) + cw_ref[1:2, :] * shifted(2)
          + cw_ref[2:3, :] * shifted(1) + cw_ref[3:4, :] * xb + cb_ref[...])
    xcarry_ref[...] = xb[tm - 8:tm, :]

    gates = _dot(xc.astype(BF16), wg_ref[...])
    rg = _sigmoid(gates[:, :width] + ba_ref[...])
    ig = _sigmoid(gates[:, width:] + bx_ref[...])
    log_a = -LRU_C * rg * _softplus(-lam_ref[...])
    a = jnp.exp(log_a)
    a_s[...] = a
    u_s[...] = jnp.sqrt(1.0 - a * a) * ig * xc

    m1 = row8 >= 1
    m2 = row8 >= 2
    m4 = row8 >= 4

    def body(i, h):
        off = pl.multiple_of(i * 8, 8)
        a8 = a_s[pl.ds(off, 8), :]
        u8 = u_s[pl.ds(off, 8), :]
        for s, m in ((1, m1), (2, m2), (4, m4)):
            u_sh = jnp.where(m, pltpu.roll(u8, s, axis=0), 0.0)
            a_sh = jnp.where(m, pltpu.roll(a8, s, axis=0), 1.0)
            u8 = u8 + a8 * u_sh
            a8 = a8 * a_sh
        h8 = u8 + a8 * h
        u_s[pl.ds(off, 8), :] = h8
        return jnp.broadcast_to(h8[7:8, :], (8, width))

    h_last = lax.fori_loop(0, tm // 8, body, hcarry_ref[...])
    hcarry_ref[...] = h_last
    o_ref[0] = (u_s[...] * _gelu_tanh(gate)).astype(o_ref.dtype)


def _block_diag(w):
    nb, di, do = w.shape
    eye = jnp.eye(nb, dtype=w.dtype)
    return (eye[:, None, :, None] * w[:, :, None, :]).reshape(nb * di, nb * do)


def _lru(p_b, conv_w, conv_b, w_a, b_a, w_x, b_x, lam, tm=512):
    b, t, cols = p_b.shape
    width = cols // 2
    wg = jnp.concatenate([_block_diag(w_a), _block_diag(w_x)], axis=1).astype(BF16)
    row = lambda a_: a_.reshape(1, -1)
    const = lambda shp: pl.BlockSpec(shp, lambda i, j: (0, 0))
    return pl.pallas_call(
        functools.partial(_lru_kernel, width=width),
        out_shape=jax.ShapeDtypeStruct((b, t, width), BF16),
        grid=(b, t // tm),
        in_specs=[pl.BlockSpec((1, tm, cols), lambda i, j: (i, j, 0)),
                  const(conv_w.shape), const((1, width)), const((width, 2 * width)),
                  const((1, width)), const((1, width)), const((1, width))],
        out_specs=pl.BlockSpec((1, tm, width), lambda i, j: (i, j, 0)),
        scratch_shapes=[pltpu.VMEM((8, width), F32), pltpu.VMEM((8, width), F32),
                        pltpu.VMEM((tm, width), F32), pltpu.VMEM((tm, width), F32)],
        compiler_params=_cparams(("parallel", "arbitrary")),
        name="rglru",
    )(p_b, conv_w, row(conv_b), wg, row(b_a), row(b_x), row(lam))


def _gla_kernel(q_ref, f_ref, i_ref, g_ref, lb_ref, nw_ref, o_ref, st_ref, *, n_batch, n_heads):
    c = CHUNK

    @pl.when(pl.program_id(0) == 0)
    def _():
        st_ref[...] = jnp.zeros_like(st_ref)

    trilf = jnp.where(_iota((c, c), 1) <= _iota((c, c), 0), 1.0, 0.0)
    tril16 = trilf.astype(BF16)
    mid = c // 2 - 1
    lb = lb_ref[...]

    qm, km, qg, kb, v16, gc = [], [], [], [], [], []
    for bi in range(n_batch):
        q = _silu(q_ref[bi].astype(F32))
        fg = lb + (1.0 - lb) * _sigmoid(f_ref[bi])
        k = 1.0 - fg
        cum = _dot_exact_lhs(tril16, jnp.log(fg))
        cum_c = cum[c - 1:c, :]
        cum_m = cum[mid:mid + 1, :]
        qm.append((q * jnp.exp(cum - cum_m)).astype(BF16))
        km.append((k * jnp.exp(cum_m - cum)).astype(BF16))
        qg.append((q * jnp.exp(cum)).astype(BF16))
        kb.append((k * jnp.exp(cum_c - cum)).astype(BF16))
        v16.append(i_ref[bi].astype(BF16))
        gc.append(jnp.exp(cum_c))

    chains = [(bi, h) for bi in range(n_batch) for h in range(n_heads)]
    idx = range(len(chains))

    def hs(xs):
        return [xs[bi][:, h * LANES:(h + 1) * LANES] for bi, h in chains]

    qm_c, km_c, qg_c, kb_c, v_c, gc_c = hs(qm), hs(km), hs(qg), hs(kb), hs(v16), hs(gc)
    scores = [(_dot_nt(qm_c[i], km_c[i]) * trilf).astype(BF16) for i in idx]
    st = [st_ref[i] for i in idx]
    st16 = [s.astype(BF16) for s in st]
    o = [_dot(scores[i], v_c[i]) + _dot_nt(qg_c[i], st16[i]) for i in idx]
    for i in idx:
        st_ref[i] = st[i] * gc_c[i] + _dot_tn(v_c[i], kb_c[i])

    for i, (bi, h) in enumerate(chains):
        sl = slice(h * LANES, (h + 1) * LANES)
        on = o[i] * lax.rsqrt(jnp.mean(o[i] * o[i], axis=-1, keepdims=True) + RMS_EPS)
        gate = _silu(g_ref[bi, :, sl].astype(F32))
        o_ref[bi, :, sl] = (on * nw_ref[:, sl] * gate).astype(o_ref.dtype)


def _gla(q, f, i, g, lower_bound, norm_w):
    bsz, t, width = q.shape
    n_heads = width // LANES
    tok = pl.BlockSpec((bsz, CHUNK, width), lambda j: (0, j, 0))
    const = pl.BlockSpec((1, width), lambda j: (0, 0))
    return pl.pallas_call(
        functools.partial(_gla_kernel, n_batch=bsz, n_heads=n_heads),
        out_shape=jax.ShapeDtypeStruct((bsz, t, width), BF16),
        grid=(t // CHUNK,),
        in_specs=[tok] * 4 + [const, const],
        out_specs=tok,
        scratch_shapes=[pltpu.VMEM((bsz * n_heads, LANES, LANES), F32)],
        compiler_params=_cparams(("arbitrary",)),
        name="hgrn2_gla",
    )(q, f, i, g, lower_bound.reshape(1, -1), norm_w.reshape(1, -1))


def _route_tile(x, g_ref, w_ref, b_ref, upper_ref, e_o, gate_o, rank_o, cnt_o):
    ms = jnp.mean(x * x, axis=-1, keepdims=True)
    hn = x * lax.rsqrt(ms + RMS_EPS) * g_ref[...]
    lt = _dot_split(hn, w_ref[0], w_ref[1]).T + b_ref[:, 0:1]
    tm = x.shape[0]
    gl = lt[0:8, :]
    row8 = _iota((8, tm), 0)
    gl = jnp.where(row8 < N_GROUPS, gl, -jnp.inf)
    gmax = jnp.max(gl, axis=0, keepdims=True)
    g_sel = jnp.min(jnp.where(gl == gmax, row8, 8), axis=0, keepdims=True)
    g_gate = 1.0 / jnp.sum(jnp.exp(gl - gmax), axis=0, keepdims=True)

    el = jnp.zeros((EXPERTS_PER_GROUP, tm), F32)
    for gi in range(N_GROUPS):
        lo = 8 + gi * EXPERTS_PER_GROUP
        el = jnp.where(g_sel == gi, lt[lo:lo + EXPERTS_PER_GROUP, :], el)
    m1 = jnp.max(el, axis=0, keepdims=True)
    i1 = jnp.min(jnp.where(el == m1, row8, 8), axis=0, keepdims=True)
    el2 = jnp.where(row8 == i1, -jnp.inf, el)
    m2 = jnp.max(el2, axis=0, keepdims=True)
    i2 = jnp.min(jnp.where(el2 == m2, row8, 8), axis=0, keepdims=True)
    e2 = jnp.exp(m2 - m1)
    inv = 1.0 / (1.0 + e2)
    ea = g_sel * EXPERTS_PER_GROUP + i1
    eb = g_sel * EXPERTS_PER_GROUP + i2
    e_o[...] = jnp.concatenate([ea, eb], axis=0)
    gate_o[...] = jnp.concatenate([g_gate * inv, g_gate * e2 * inv], axis=0)

    erow = _iota((N_EXPERTS, tm), 0)
    oh_a = jnp.where(erow == ea, 1.0, 0.0)
    oh_b = jnp.where(erow == eb, 1.0, 0.0)
    pre_a = _dot(oh_a.astype(BF16), upper_ref[...])
    pre_b = _dot(oh_b.astype(BF16), upper_ref[...])
    cnt_a = jnp.sum(oh_a, axis=1, keepdims=True)
    cnt_b = jnp.sum(oh_b, axis=1, keepdims=True)
    rank_a = jnp.sum(oh_a * pre_a, axis=0, keepdims=True)
    rank_b = jnp.sum(oh_b * (pre_b + cnt_a), axis=0, keepdims=True)
    rank_o[...] = jnp.concatenate([rank_a, rank_b], axis=0).astype(I32)
    cnt_o[0] = jnp.broadcast_to(cnt_a + cnt_b, (N_EXPERTS, LANES))


def _seg_local_rows(tm):
    return 2 * tm + N_EXPERTS * SUBLANES


def _segment_copies(cnt_ref, off_ref, dst_ref, base, local, remote, sem, to_remote, wait):
    def body(e, carry):
        c = pl.multiple_of(cnt_ref[base + e], SUBLANES)

        @pl.when(c > 0)
        def _():
            off = 0 if off_ref is None else pl.multiple_of(off_ref[base + e], SUBLANES)
            loc = local.at[pl.ds(off, c), :]
            rem = remote.at[pl.ds(pl.multiple_of(dst_ref[base + e], SUBLANES), c), :]
            cp = (pltpu.make_async_copy(loc, rem, sem) if to_remote
                  else pltpu.make_async_copy(rem, loc, sem))
            if wait:
                cp.wait()
            else:
                cp.start()
        return carry
    lax.fori_loop(0, N_EXPERTS, body, 0)


def _dispatch_kernel(cnt_ref, off_ref, dst_ref, tcnt_ref, tdst_ref, nused_ref,
                     x_ref, g_ref, lidx_ref, xs_out, sbuf, zbuf, sems, zsem):
    j = pl.program_id(0)
    nb = pl.num_programs(0)
    tm = x_ref.shape[0]
    slot = lax.rem(j, 2)

    def copies(tile, s, wait):
        _segment_copies(cnt_ref, off_ref, dst_ref, tile * N_EXPERTS, sbuf.at[s], xs_out,
                        sems.at[s], True, wait)

    @pl.when(j == 0)
    def _():
        zbuf[...] = jnp.zeros_like(zbuf)
        rows = zbuf.shape[0]
        n_blocks = xs_out.shape[0] // rows
        for wait in (False, True):
            _segment_copies(tcnt_ref, None, tdst_ref, 0, zbuf, xs_out, zsem, True, wait)

            def body(b, carry):
                start = pl.multiple_of(b * rows, SUBLANES)
                cp = pltpu.make_async_copy(zbuf, xs_out.at[pl.ds(start, rows), :], zsem)
                if wait:
                    cp.wait()
                else:
                    cp.start()
                return carry
            lax.fori_loop(nused_ref[0], n_blocks, body, 0)

    @pl.when(j >= 2)
    def _():
        copies(j - 2, slot, True)

    x = x_ref[...]
    ms = jnp.mean(x * x, axis=-1, keepdims=True)
    hn = (x * lax.rsqrt(ms + RMS_EPS) * g_ref[...]).astype(BF16)
    r = _iota((_seg_local_rows(tm), tm), 0)
    perm = jnp.where((r == lidx_ref[0:1, :]) | (r == lidx_ref[1:2, :]), 1.0, 0.0).astype(BF16)
    sbuf[slot] = _pack_pairs(_dot(perm, hn))
    copies(j, slot, False)

    @pl.when(j == nb - 1)
    def _():
        copies(j, slot, True)

    @pl.when((j == nb - 1) & (nb >= 2))
    def _():
        copies(j - 1, 1 - slot, True)


def _dispatch(x2d, g, lidx, seg, tails, n_used, n_rows, rows, tm):
    n, d = x2d.shape
    return pl.pallas_call(
        _dispatch_kernel,
        out_shape=jax.ShapeDtypeStruct((n_rows, d // 2), U32),
        grid_spec=pltpu.PrefetchScalarGridSpec(
            num_scalar_prefetch=6,
            grid=(n // tm,),
            in_specs=[pl.BlockSpec((tm, d), lambda j, *_: (j, 0)),
                      pl.BlockSpec((1, d), lambda j, *_: (0, 0)),
                      pl.BlockSpec((2, tm), lambda j, *_: (0, j))],
            out_specs=pl.BlockSpec(memory_space=pl.ANY),
            scratch_shapes=[pltpu.VMEM((2, _seg_local_rows(tm), d // 2), U32),
                            pltpu.VMEM((rows, d // 2), U32),
                            pltpu.SemaphoreType.DMA((2,)), pltpu.SemaphoreType.DMA(())]),
        compiler_params=_cparams(("arbitrary",)),
        name="moe_dispatch",
    )(*seg, *tails, n_used, x2d, g.reshape(1, d), lidx)


def _expert_kernel(be_ref, first_ref, nused_ref, x_ref, wg_ref, wu_ref, wd_ref, o_ref,
                   wg16, wu16, wd16):
    j = pl.program_id(0)

    @pl.when(first_ref[j] == 1)
    def _():
        wg16[...] = wg_ref[0, 0].astype(BF16)
        wu16[...] = wu_ref[0, 0].astype(BF16)
        wd16[...] = wd_ref[0, 0].astype(BF16)

    @pl.when(j < nused_ref[0])
    def _():
        x_lo, x_hi = _unpack_pairs(x_ref[...])
        half = x_lo.shape[1]
        hg = _dot(x_lo, wg16[:half, :]) + _dot(x_hi, wg16[half:, :])
        hu = _dot(x_lo, wu16[:half, :]) + _dot(x_hi, wu16[half:, :])
        hid = (_silu(hg) * hu).astype(BF16)
        o_ref[...] = _pack_pairs(_dot(hid, wd16[...]))

    @pl.when(j >= nused_ref[0])
    def _():
        o_ref[...] = jnp.zeros_like(o_ref)


def _expert_ffn(xs, block_e, first_flag, n_used, w_gate, w_up, w_down, layer, rows):
    n_rows, dh = xs.shape
    n_blocks = n_rows // rows
    d, ff = w_gate.shape[-2:]
    assert d == 2 * dh
    wspec = lambda shp: pl.BlockSpec((1, 1) + shp, lambda j, be, fi, nu: (layer, be[j], 0, 0))
    return pl.pallas_call(
        _expert_kernel,
        out_shape=jax.ShapeDtypeStruct((n_rows, dh), U32),
        grid_spec=pltpu.PrefetchScalarGridSpec(
            num_scalar_prefetch=3,
            grid=(n_blocks,),
            in_specs=[pl.BlockSpec((rows, dh),
                                   lambda j, be, fi, nu: (jnp.minimum(j, nu[0] - 1), 0)),
                      wspec((d, ff)), wspec((d, ff)), wspec((ff, d))],
            out_specs=pl.BlockSpec((rows, dh), lambda j, be, fi, nu: (j, 0)),
            scratch_shapes=[pltpu.VMEM((d, ff), BF16), pltpu.VMEM((d, ff), BF16),
                            pltpu.VMEM((ff, d), BF16)]),
        compiler_params=_cparams(("arbitrary",)),
        name="moe_expert_ffn",
    )(block_e, first_flag, n_used, xs, w_gate, w_up, w_down)


def _combine_kernel(cnt_ref, off_ref, dst_ref, x_ref, lidx_ref, gt_ref, ys_hbm, gfin_ref, o_ref,
                    ybuf, sems, *, final_norm):
    j = pl.program_id(0)
    nb = pl.num_programs(0)
    tm = x_ref.shape[0]
    slot = lax.rem(j, 2)

    def copies(tile, s, wait):
        _segment_copies(cnt_ref, off_ref, dst_ref, tile * N_EXPERTS, ybuf.at[s], ys_hbm,
                        sems.at[s], False, wait)

    @pl.when(j == 0)
    def _():
        ybuf[...] = jnp.zeros_like(ybuf)
        copies(0, 0, False)

    @pl.when(j + 1 < nb)
    def _():
        copies(j + 1, 1 - slot, False)

    copies(j, slot, True)
    c = _iota((tm, _seg_local_rows(tm)), 1)
    li = lidx_ref[...]
    gt = gt_ref[...]
    mix = (jnp.where(c == li[:, 0:1], gt[:, 0:1], 0.0)
           + jnp.where(c == li[:, 1:2], gt[:, 1:2], 0.0)).astype(BF16)
    y_lo, y_hi = _unpack_pairs(ybuf[slot])
    y = x_ref[...] + jnp.concatenate([_dot(mix, y_lo), _dot(mix, y_hi)], axis=1)
    if final_norm:
        ms = jnp.mean(y * y, axis=-1, keepdims=True)
        y = y * lax.rsqrt(ms + RMS_EPS) * gfin_ref[...]
    o_ref[...] = y


def _combine(x2d, ys, lidx_t, gates_t, seg, g_final, final_norm, tm):
    n, d = x2d.shape
    return pl.pallas_call(
        functools.partial(_combine_kernel, final_norm=final_norm),
        out_shape=jax.ShapeDtypeStruct((n, d), F32),
        grid_spec=pltpu.PrefetchScalarGridSpec(
            num_scalar_prefetch=3,
            grid=(n // tm,),
            in_specs=[pl.BlockSpec((tm, d), lambda j, *_: (j, 0)),
                      pl.BlockSpec((tm, 2), lambda j, *_: (j, 0)),
                      pl.BlockSpec((tm, 2), lambda j, *_: (j, 0)),
                      pl.BlockSpec(memory_space=pl.ANY),
                      pl.BlockSpec((1, d), lambda j, *_: (0, 0))],
            out_specs=pl.BlockSpec((tm, d), lambda j, *_: (j, 0)),
            scratch_shapes=[pltpu.VMEM((2, _seg_local_rows(tm), d // 2), U32),
                            pltpu.SemaphoreType.DMA((2,))]),
        compiler_params=_cparams(("arbitrary",)),
        name="moe_combine",
    )(*seg, x2d, lidx_t, gates_t, ys, g_final.reshape(1, d))


def _route_plan(e2, rank2, tile_cnt, rows, tm):
    n_tok = e2.shape[1]
    n_tiles = n_tok // tm
    seg_cnt = (tile_cnt + SUBLANES - 1) // SUBLANES * SUBLANES
    counts = jnp.sum(seg_cnt, axis=0)
    padded = (counts + rows - 1) // rows * rows
    pend = jnp.cumsum(padded)
    pstart = pend - padded
    seg_dst = pstart[None, :] + jnp.cumsum(seg_cnt, axis=0) - seg_cnt
    seg_off = jnp.cumsum(seg_cnt, axis=1) - seg_cnt
    eid = jnp.arange(N_EXPERTS, dtype=I32)
    e3 = e2.reshape(2, n_tiles, tm)
    lidx = rank2.reshape(2, n_tiles, tm) + jnp.sum(
        jnp.where(e3[None] == eid[:, None, None, None], seg_off.T[:, None, :, None], 0), axis=0)
    lidx = lidx.reshape(2, n_tok).astype(I32)
    n_blocks = -(-(2 * n_tok + n_tiles * N_EXPERTS * SUBLANES) // rows) + N_EXPERTS
    starts = jnp.arange(n_blocks, dtype=I32) * rows
    block_e = jnp.minimum(jnp.sum((starts[:, None] >= pend[None, :]).astype(I32), axis=1),
                          N_EXPERTS - 1)
    first = jnp.concatenate([jnp.ones((1,), I32),
                             (block_e[1:] != block_e[:-1]).astype(I32)])
    n_used = (pend[-1] // rows).astype(I32).reshape(1)
    seg = tuple(a.reshape(-1).astype(I32) for a in (seg_cnt, seg_off, seg_dst))
    tails = ((padded - counts).astype(I32), (pstart + counts).astype(I32))
    return lidx, seg, tails, block_e, first, n_used, n_blocks * rows


def _moe(x2d, route, g_ffn, w_gate, w_up, w_down, layer, g_final, final_norm, tm,
         rows=ROUTE_ROWS):
    e2, gates2, rank2, cnt = route
    lidx, seg, tails, block_e, first, n_used, n_rows = _route_plan(
        e2, rank2, cnt[:, :, 0].astype(I32), rows, tm)
    xs = _dispatch(x2d, g_ffn, lidx, seg, tails, n_used, n_rows, rows, tm)
    ys = _expert_ffn(xs, block_e, first, n_used, w_gate, w_up, w_down, layer, rows)
    return _combine(x2d, ys, lidx.T, gates2.T, seg, g_final, final_norm, tm)


def kernel(x, norm_mix, norm_ffn, norm_final, ab_w_in, rw_mu, rw_w0, rw_w2, rw_a0, rw_a2, rw_g2, rw_k_k, rw_k_a, rw_r_k, rw_ln_w, rw_ln_b, lru_conv_w, lru_conv_b, lru_w_a, lru_b_a, lru_w_x, lru_b_x, lru_lambda, ab_w_out, c_w_in, c_lower_bound, c_norm_w, c_w_out, moe_w_group, moe_b_group, moe_w_expert, moe_b_expert, moe_w_gate, moe_w_up, moe_w_down):
    bsz, t, d = x.shape
    depth = norm_mix.shape[0]
    n = bsz * t
    lbs = jnp.cumsum(jax.nn.softmax(c_lower_bound.astype(F32), axis=0), axis=0)
    lbs = lbs - lbs[0]
    x2d = x.reshape(n, d)
    for layer in range(depth):
        j = layer // 2
        if layer % 2 == 0:
            rw_cols = rw_mu.shape[1]
            lru_cols = ab_w_in.shape[2] - rw_cols
            width = rw_w0.shape[1]
            p_a, p_b = _norm_matmul(x2d, norm_mix[layer], ab_w_in[j], (rw_cols, lru_cols),
                                    (F32, F32))
            prep = _rwkv_prep(p_a.reshape(bsz, t, rw_cols), rw_mu[j], rw_w0[j], rw_w2[j],
                              rw_a0[j], rw_a2[j], rw_g2[j], rw_k_k[j], rw_k_a[j],
                              rw_r_k[j].reshape(-1))
            ya = _rwkv_scan(*prep, rw_ln_w[j], rw_ln_b[j])
            yb = _lru(p_b.reshape(bsz, t, lru_cols), lru_conv_w[j], lru_conv_b[j], lru_w_a[j],
                      lru_b_a[j], lru_w_x[j], lru_b_x[j], lru_lambda[j])
            ys = [ya.reshape(n, width), yb.reshape(n, -1)]
            ws = [ab_w_out[j][:width], ab_w_out[j][width:]]
        else:
            hw = c_norm_w.shape[1]
            q, f, i_, g = _norm_matmul(x2d, norm_mix[layer], c_w_in[j], (hw,) * 4,
                                       (BF16, F32, BF16, BF16))
            shp = (bsz, t, hw)
            o = _gla(q.reshape(shp), f.reshape(shp), i_.reshape(shp), g.reshape(shp),
                     lbs[layer], c_norm_w[j])
            ys = [o.reshape(n, hw)]
            ws = [c_w_out[j]]
        tm = min(ROUTE_TILE, n)
        x2d, *route = _proj_route(x2d, ys, ws, norm_ffn[layer], moe_w_group[layer],
                                  moe_b_group[layer], moe_w_expert[layer], moe_b_expert[layer],
                                  tm)
        x2d = _moe(x2d, route, norm_ffn[layer], moe_w_gate, moe_w_up, moe_w_down, layer,
                   norm_final, layer == depth - 1, tm)
    return x2d.reshape(bsz, t, d)
```

```python
import functools

import jax
import jax.numpy as jnp
from jax import lax
from jax.experimental import pallas as pl
from jax.experimental.pallas import tpu as pltpu

F32 = jnp.float32
BF16 = jnp.bfloat16
I32 = jnp.int32
U32 = jnp.uint32

RMS_EPS = 1e-6
RWKV_GN_EPS = 64e-5
LRU_C = 8.0
CHUNK = 64
CHUNK_SHIFT = CHUNK.bit_length() - 1
HEAD64 = 64
LANES = 128
SUBLANES = 8
N_GROUPS = 4
EXPERTS_PER_GROUP = 8
N_EXPERTS = N_GROUPS * EXPERTS_PER_GROUP
ROUTE_ROWS = 512
ROUTE_TILE = 512
VMEM_LIMIT = 56 * 1024 * 1024


def _cparams(sem):
    return pltpu.CompilerParams(dimension_semantics=sem, vmem_limit_bytes=VMEM_LIMIT)


def _sigmoid(x):
    return 0.5 * jnp.tanh(0.5 * x) + 0.5


def _pack_pairs(x):
    w = x.shape[1] // 2
    lo = lax.bitcast_convert_type(x[:, :w].astype(BF16).astype(F32), U32)
    hi = lax.bitcast_convert_type(x[:, w:].astype(BF16).astype(F32), U32)
    return (lo >> 16) | hi


def _unpack_pairs(p):
    lo = lax.bitcast_convert_type(p << 16, F32).astype(BF16)
    hi = lax.bitcast_convert_type(p & jnp.uint32(0xFFFF0000), F32).astype(BF16)
    return lo, hi


def _softplus(x):
    return jnp.maximum(x, 0.0) + jnp.log(1.0 + jnp.exp(-jnp.abs(x)))


def _silu(x):
    return x * _sigmoid(x)


def _gelu_tanh(x):
    return 0.5 * x * (1.0 + jnp.tanh(0.7978845608028654 * (x + 0.044715 * x * x * x)))


def _dot(a, b):
    return jnp.dot(a, b, preferred_element_type=F32)


def _dot_nt(a, b):
    return lax.dot_general(a, b, (((1,), (1,)), ((), ())), preferred_element_type=F32)


def _dot_tn(a, b):
    return lax.dot_general(a, b, (((0,), (0,)), ((), ())), preferred_element_type=F32)


def _split2(x):
    hi = x.astype(BF16)
    lo = (x - hi.astype(F32)).astype(BF16)
    return hi, lo


def _dot_exact_rhs(x, m_bf16):
    hi, lo = _split2(x)
    return _dot(hi, m_bf16) + _dot(lo, m_bf16)


def _dot_exact_lhs(m_bf16, x):
    hi, lo = _split2(x)
    return _dot(m_bf16, hi) + _dot(m_bf16, lo)


def _dot_split(a, w_hi, w_lo):
    a_hi, a_lo = _split2(a)
    return _dot(a_hi, w_hi) + (_dot(a_lo, w_hi) + _dot(a_hi, w_lo))


def _hi_lo(w):
    hi = w.astype(BF16)
    return hi, (w - hi.astype(F32)).astype(BF16)


def _iota(shape, dim):
    return lax.broadcasted_iota(I32, shape, dim)


def _norm_matmul_kernel(x_ref, g_ref, w_ref, *o_refs, splits):
    x = x_ref[...]
    ms = jnp.mean(x * x, axis=-1, keepdims=True)
    y = (x * lax.rsqrt(ms + RMS_EPS) * g_ref[...]).astype(BF16)
    off = 0
    for o_ref, n in zip(o_refs, splits):
        o_ref[...] = _dot(y, w_ref[:, off:off + n]).astype(o_ref.dtype)
        off += n


def _norm_matmul(x2d, g, w, splits, out_dtypes, tm=1024):
    n, d = x2d.shape
    tm = min(tm, n)
    ncols = w.shape[1]
    assert sum(splits) == ncols and n % tm == 0 and len(out_dtypes) == len(splits)
    return pl.pallas_call(
        functools.partial(_norm_matmul_kernel, splits=splits),
        out_shape=[jax.ShapeDtypeStruct((n, s), dt) for s, dt in zip(splits, out_dtypes)],
        grid=(n // tm,),
        in_specs=[pl.BlockSpec((tm, d), lambda i: (i, 0)),
                  pl.BlockSpec((1, d), lambda i: (0, 0)),
                  pl.BlockSpec((d, ncols), lambda i: (0, 0))],
        out_specs=[pl.BlockSpec((tm, s), lambda i: (i, 0)) for s in splits],
        compiler_params=_cparams(("parallel",)),
        name="norm_matmul",
    )(x2d, g.reshape(1, d), w.astype(BF16))


def _proj_residual_kernel(*refs, n_in):
    x_ref = refs[0]
    y_refs = refs[1:1 + n_in]
    w_refs = refs[1 + n_in:1 + 2 * n_in]
    o_ref = refs[1 + 2 * n_in]
    acc = x_ref[...]
    for y_ref, w_ref in zip(y_refs, w_refs):
        acc = acc + _dot(y_ref[...].astype(BF16), w_ref[...])
    o_ref[...] = acc


def _proj_residual(x2d, ys, ws, tm=1024):
    n, d = x2d.shape
    tm = min(tm, n)
    n_in = len(ys)
    in_specs = [pl.BlockSpec((tm, d), lambda i: (i, 0))]
    in_specs += [pl.BlockSpec((tm, y.shape[1]), lambda i: (i, 0)) for y in ys]
    in_specs += [pl.BlockSpec(w.shape, lambda i: (0, 0)) for w in ws]
    return pl.pallas_call(
        functools.partial(_proj_residual_kernel, n_in=n_in),
        out_shape=jax.ShapeDtypeStruct((n, d), F32),
        grid=(n // tm,),
        in_specs=in_specs,
        out_specs=pl.BlockSpec((tm, d), lambda i: (i, 0)),
        compiler_params=_cparams(("parallel",)),
        name="proj_residual",
    )(x2d, *ys, *[w.astype(BF16) for w in ws])


def _router_kernel(x_ref, g_ref, w_ref, b_ref, upper_ref, e_o, gate_o, rank_o, cnt_o):
    _route_tile(x_ref[...], g_ref, w_ref, b_ref, upper_ref, e_o, gate_o, rank_o, cnt_o)


def _router(x2d, g_ffn, w_group, b_group, w_expert, b_expert, tm):
    n, d = x2d.shape
    wt = jnp.zeros((d, LANES), F32)
    wt = wt.at[:, 0:N_GROUPS].set(w_group).at[:, 8:8 + N_EXPERTS].set(w_expert)
    wt = jnp.stack(_hi_lo(wt))
    bt = jnp.zeros((LANES,), F32)
    bt = bt.at[0:N_GROUPS].set(b_group).at[8:8 + N_EXPERTS].set(b_expert)
    bt = jnp.broadcast_to(bt[:, None], (LANES, LANES))
    ti = jnp.arange(tm, dtype=I32)
    upper = (ti[:, None] < ti[None, :]).astype(BF16)
    tok2 = pl.BlockSpec((2, tm), lambda i: (0, i))
    return pl.pallas_call(
        _router_kernel,
        out_shape=[jax.ShapeDtypeStruct((2, n), I32),
                   jax.ShapeDtypeStruct((2, n), F32),
                   jax.ShapeDtypeStruct((2, n), I32),
                   jax.ShapeDtypeStruct((n // tm, N_EXPERTS, LANES), F32)],
        grid=(n // tm,),
        in_specs=[pl.BlockSpec((tm, d), lambda i: (i, 0)),
                  pl.BlockSpec((1, d), lambda i: (0, 0)),
                  pl.BlockSpec((2, d, LANES), lambda i: (0, 0, 0)),
                  pl.BlockSpec((LANES, LANES), lambda i: (0, 0)),
                  pl.BlockSpec((tm, tm), lambda i: (0, 0))],
        out_specs=[tok2, tok2, tok2,
                   pl.BlockSpec((1, N_EXPERTS, LANES), lambda i: (i, 0, 0))],
        compiler_params=_cparams(("parallel",)),
        name="moe_router",
    )(x2d, g_ffn.reshape(1, d), wt, bt, upper)


def _rwkv_prep_kernel(p_ref, mu_ref, w0_ref, a0_ref, kk_s_ref, ka_ref, rk_ref,
                      wcomb_ref, g2_ref, tril_ref,
                      rt_o, kkt_o, kh_o, bh_o, kb_o, bb_o, v_o, g_o, bonus_o, gc_o,
                      prev_ref, *, width):
    tm = p_ref.shape[1]

    @pl.when(pl.program_id(1) == 0)
    def _():
        prev_ref[...] = jnp.zeros_like(prev_ref)

    p = p_ref[0]
    rolled = pltpu.roll(p, 1, axis=0)
    prev = jnp.where(_iota(p.shape, 0) == 0, prev_ref[...], rolled)
    prev_ref[...] = p[tm - 1:tm, :]
    ps = p + (prev - p) * mu_ref[...]

    r = ps[:, 0:width]
    k = ps[:, width:2 * width]
    v = ps[:, 2 * width:3 * width]
    lowrank = ps[:, 3 * width:3 * width + LANES]
    gl = ps[:, 3 * width + LANES:3 * width + 2 * LANES]

    lane = _iota(lowrank.shape, 1)
    lr_in = jnp.where(lane < HEAD64, jnp.tanh(lowrank), lowrank)
    t12 = _dot_split(lr_in, wcomb_ref[0], wcomb_ref[1])
    wlog = -_softplus(-(w0_ref[...] + t12[:, :width])) - 0.5
    lw = -jnp.exp(wlog)
    a = _sigmoid(a0_ref[...] + t12[:, width:])
    g = _dot_split(_sigmoid(gl), g2_ref[0], g2_ref[1])

    seg = jnp.where((_iota((LANES, LANES), 0) >> 6) == (_iota((LANES, LANES), 1) >> 6),
                    1.0, 0.0).astype(BF16)

    def head_sums(x):
        return jnp.concatenate([_dot_exact_rhs(x[:, q * LANES:(q + 1) * LANES], seg)
                                for q in range(width // LANES)], axis=1)

    kk = k * kk_s_ref[...]
    nrm = jnp.sqrt(head_sums(kk * kk))
    kk = kk / jnp.maximum(nrm, 1e-12)
    k2 = k * (1.0 + (a - 1.0) * ka_ref[...])
    bonus = head_sums(r * k2 * rk_ref[...]) * v

    b = kk * a

    n_chunk = tm // CHUNK
    cum = _dot_exact_lhs(tril_ref[...], lw)
    cum3 = cum.reshape(n_chunk, CHUNK, width)
    cend = cum3[:, CHUNK - 1:CHUNK, :]
    e_neg = jnp.exp(-cum)
    e_end = jnp.exp(cend - cum3).reshape(tm, width)

    rt_o[0] = (r * jnp.exp(cum)).astype(BF16)
    kkt_o[0] = (kk * jnp.exp(cum - lw)).astype(BF16)
    kh_o[0] = (k2 * e_neg).astype(BF16)
    bh_o[0] = (b * e_neg).astype(BF16)
    kb_o[0] = (k2 * e_end).astype(BF16)
    bb_o[0] = (b * e_end).astype(BF16)
    v_o[0] = v.astype(BF16)
    g_o[0] = g.astype(BF16)
    bonus_o[0] = bonus.astype(BF16)
    gc_o[0] = jnp.exp(cend)


def _rwkv_prep(p_a, mu, w0, w2, a0, a2, g2, k_k, k_a, r_k, tm=512):
    b, t, cols = p_a.shape
    width = w0.shape[0]
    rank = w2.shape[0]
    assert rank == HEAD64 and a2.shape[0] == HEAD64 and g2.shape[0] == LANES
    zeros = jnp.zeros((rank, width), F32)
    wcomb = jnp.concatenate([jnp.concatenate([w2, zeros], 1),
                             jnp.concatenate([zeros, a2], 1)], 0)
    row = lambda a_: a_.reshape(1, -1)
    const = lambda shp: pl.BlockSpec(shp, lambda i, j: (0,) * len(shp))
    tok = pl.BlockSpec((1, tm, width), lambda i, j: (i, j, 0))
    n_chunk = tm // CHUNK
    ti = jnp.arange(tm, dtype=I32)
    tril_bd = (((ti[:, None] >> CHUNK_SHIFT) == (ti[None, :] >> CHUNK_SHIFT))
               & (ti[None, :] <= ti[:, None])).astype(BF16)
    return pl.pallas_call(
        functools.partial(_rwkv_prep_kernel, width=width),
        out_shape=([jax.ShapeDtypeStruct((b, t, width), BF16)] * 9
                   + [jax.ShapeDtypeStruct((b, t // CHUNK, 1, width), F32)]),
        grid=(b, t // tm),
        in_specs=[pl.BlockSpec((1, tm, cols), lambda i, j: (i, j, 0)),
                  const((1, cols)), const((1, width)), const((1, width)), const((1, width)),
                  const((1, width)), const((1, width)),
                  const((2, LANES, 2 * width)), const((2, LANES, width)), const((tm, tm))],
        out_specs=[tok] * 9 + [pl.BlockSpec((1, n_chunk, 1, width), lambda i, j: (i, j, 0, 0))],
        scratch_shapes=[pltpu.VMEM((1, cols), F32)],
        compiler_params=_cparams(("parallel", "arbitrary")),
        name="rwkv_prep",
    )(p_a, row(mu), row(w0), row(a0), row(k_k), row(k_a), row(r_k),
      jnp.stack(_hi_lo(wcomb)), jnp.stack(_hi_lo(g2)), tril_bd)


def _rwkv_scan_kernel(rt_ref, kkt_ref, kh_ref, bh_ref, kb_ref, bb_ref, v_ref, g_ref, bonus_ref,
                      gc_ref, lnw_ref, lnb_ref, o_ref, st_ref, *, n_batch, n_pairs):
    c = CHUNK

    @pl.when(pl.program_id(0) == 0)
    def _():
        st_ref[...] = jnp.zeros_like(st_ref)

    rr = _iota((LANES, LANES), 0)
    cc = _iota((LANES, LANES), 1)
    mask_bd = jnp.where((rr >> 6) == (cc >> 6), 1.0, 0.0)
    mask_bd16 = mask_bd.astype(BF16)
    t_i = _iota((c, LANES), 0)
    s_i = _iota((c, LANES), 1) & (c - 1)
    strict = jnp.where(s_i < t_i, 1.0, 0.0)
    incl = jnp.where(s_i <= t_i, 1.0, 0.0)
    eye_ss = jnp.where(s_i == t_i, 1.0, 0.0)

    def bd16(x):
        return jnp.concatenate([x, x], axis=0) * mask_bd16

    def b16(xs):
        return [x.astype(BF16) for x in xs]

    chains = [(bi, p) for bi in range(n_batch) for p in range(n_pairs)]
    n_ch = len(chains)
    idx = range(n_ch)

    def ld(ref):
        return [ref[bi, :, p * LANES:(p + 1) * LANES] for bi, p in chains]

    rt, kkt, kh, bh, kb, bb, v = (ld(ref) for ref in
                                  (rt_ref, kkt_ref, kh_ref, bh_ref, kb_ref, bb_ref, v_ref))

    lhs = [jnp.concatenate([kkt[i], rt[i]], axis=0) for i in idx]
    rhs = [jnp.concatenate([bd16(kh[i]), bd16(bh[i])], axis=0) for i in idx]
    gm = [_dot_nt(lhs[i], rhs[i]) for i in idx]
    l_kv = [gm[i][:c, :LANES] * strict for i in idx]
    xp = [-(gm[i][:c, LANES:] * strict) for i in idx]
    pm = [gm[i][c:, :LANES] * incl for i in idx]
    nqm = [-(gm[i][c:, LANES:] * incl) for i in idx]

    tinv = [eye_ss + xp[i] for i in idx]
    xp16 = b16(xp)
    xpb = [bd16(x) for x in xp16]
    for _ in range(c.bit_length() - 2):
        xp16 = b16([_dot(xp16[i], xpb[i]) for i in idx])
        xpb = [bd16(x) for x in xp16]
        t16 = b16(tinv)
        tinv = [tinv[i] + _dot(t16[i], xpb[i]) for i in idx]
    tinv16 = b16(tinv)

    v_bd = [bd16(x) for x in v]
    l_kv16 = b16(l_kv)
    w16 = b16([_dot(l_kv16[i], v_bd[i]) for i in idx])
    tkw = [_dot(tinv16[i], jnp.concatenate([bd16(kkt[i]), bd16(w16[i])], axis=1)) for i in idx]
    tk16 = b16([x[:, :LANES] for x in tkw])

    st = [st_ref[i] for i in idx]
    st16 = b16(st)
    u = [_dot_nt(tk16[i], st16[i]) + tkw[i][:, LANES:] for i in idx]
    u16 = b16(u)
    pq16 = [jnp.concatenate([pm[i], nqm[i]], axis=1).astype(BF16) for i in idx]
    vu = [jnp.concatenate([v_bd[i], bd16(u16[i])], axis=0) for i in idx]
    y = [_dot_nt(rt[i], st16[i]) + _dot(pq16[i], vu[i]) for i in idx]
    upd = [_dot_tn(jnp.concatenate([v[i], -u16[i]], axis=0),
                   jnp.concatenate([kb[i], bb[i]], axis=0)) for i in idx]
    for i, (bi, p) in enumerate(chains):
        st_ref[i] = st[i] * gc_ref[bi, 0, :, p * LANES:(p + 1) * LANES] + upd[i] * mask_bd

    ys = jnp.concatenate(y, axis=0)
    mean = _dot_exact_rhs(ys, mask_bd16) * (1.0 / HEAD64)
    d = ys - mean
    var = _dot_exact_rhs(d * d, mask_bd16) * (1.0 / HEAD64)
    dn = d * lax.rsqrt(var + RWKV_GN_EPS)
    for i, (bi, p) in enumerate(chains):
        sl = slice(p * LANES, (p + 1) * LANES)
        yn = dn[i * c:(i + 1) * c] * lnw_ref[:, sl] + lnb_ref[:, sl]
        o_ref[bi, :, sl] = ((yn + bonus_ref[bi, :, sl]) * g_ref[bi, :, sl]).astype(o_ref.dtype)


def _rwkv_scan(rt, kkt, kh, bh, kb, bb, v, g, bonus, gc, ln_w, ln_b):
    bsz, t, width = rt.shape
    n_pairs = width // LANES
    tok = pl.BlockSpec((bsz, CHUNK, width), lambda j: (0, j, 0))
    const = pl.BlockSpec((1, width), lambda j: (0, 0))
    return pl.pallas_call(
        functools.partial(_rwkv_scan_kernel, n_batch=bsz, n_pairs=n_pairs),
        out_shape=jax.ShapeDtypeStruct((bsz, t, width), BF16),
        grid=(t // CHUNK,),
        in_specs=[tok] * 9 + [pl.BlockSpec((bsz, 1, 1, width), lambda j: (0, j, 0, 0)),
                              const, const],
        out_specs=tok,
        scratch_shapes=[pltpu.VMEM((bsz * n_pairs, LANES, LANES), F32)],
        compiler_params=_cparams(("arbitrary",)),
        name="rwkv_scan",
    )(rt, kkt, kh, bh, kb, bb, v, g, bonus, gc, ln_w.reshape(1, -1), ln_b.reshape(1, -1))


def _lru_kernel(p_ref, cw_ref, cb_ref, wg_ref, ba_ref, bx_ref, lam_ref, o_ref,
                xcarry_ref, hcarry_ref, a_s, u_s, *, width):
    tm = p_ref.shape[1]

    @pl.when(pl.program_id(1) == 0)
    def _():
        xcarry_ref[...] = jnp.zeros_like(xcarry_ref)
        hcarry_ref[...] = jnp.zeros_like(hcarry_ref)

    gate = p_ref[0, :, 0:width]
    xb = p_ref[0, :, width:2 * width]
    carry8 = xcarry_ref[...]
    row8 = _iota((8, width), 0)

    def shifted(s):
        rolled = pltpu.roll(xb, s, axis=0)
        first = jnp.where(row8 < s, pltpu.roll(carry8, s, axis=0), rolled[0:8])
        return jnp.concatenate([first, rolled[8:]], axis=0)

    xc = (cw_ref[0:1, :] * shifted(3) + cw_ref[1:2, :] * shifted(2)
          + cw_ref[2:3, :] * shifted(1) + cw_ref[3:4, :] * xb + cb_ref[...])
    xcarry_ref[...] = xb[tm - 8:tm, :]

    gates = _dot(xc.astype(BF16), wg_ref[...])
    rg = _sigmoid(gates[:, :width] + ba_ref[...])
    ig = _sigmoid(gates[:, width:] + bx_ref[...])
    log_a = -LRU_C * rg * _softplus(-lam_ref[...])
    a = jnp.exp(log_a)
    a_s[...] = a
    u_s[...] = jnp.sqrt(1.0 - a * a) * ig * xc

    m1 = row8 >= 1
    m2 = row8 >= 2
    m4 = row8 >= 4

    def body(i, h):
        off = pl.multiple_of(i * 8, 8)
        a8 = a_s[pl.ds(off, 8), :]
        u8 = u_s[pl.ds(off, 8), :]
        for s, m in ((1, m1), (2, m2), (4, m4)):
            u_sh = jnp.where(m, pltpu.roll(u8, s, axis=0), 0.0)
            a_sh = jnp.where(m, pltpu.roll(a8, s, axis=0), 1.0)
            u8 = u8 + a8 * u_sh
            a8 = a8 * a_sh
        h8 = u8 + a8 * h
        u_s[pl.ds(off, 8), :] = h8
        return jnp.broadcast_to(h8[7:8, :], (8, width))

    h_last = lax.fori_loop(0, tm // 8, body, hcarry_ref[...])
    hcarry_ref[...] = h_last
    o_ref[0] = (u_s[...] * _gelu_tanh(gate)).astype(o_ref.dtype)


def _block_diag(w):
    nb, di, do = w.shape
    eye = jnp.eye(nb, dtype=w.dtype)
    return (eye[:, None, :, None] * w[:, :, None, :]).reshape(nb * di, nb * do)


def _lru(p_b, conv_w, conv_b, w_a, b_a, w_x, b_x, lam, tm=1024):
    b, t, cols = p_b.shape
    width = cols // 2
    wg = jnp.concatenate([_block_diag(w_a), _block_diag(w_x)], axis=1).astype(BF16)
    row = lambda a_: a_.reshape(1, -1)
    const = lambda shp: pl.BlockSpec(shp, lambda i, j: (0, 0))
    return pl.pallas_call(
        functools.partial(_lru_kernel, width=width),
        out_shape=jax.ShapeDtypeStruct((b, t, width), BF16),
        grid=(b, t // tm),
        in_specs=[pl.BlockSpec((1, tm, cols), lambda i, j: (i, j, 0)),
                  const(conv_w.shape), const((1, width)), const((width, 2 * width)),
                  const((1, width)), const((1, width)), const((1, width))],
        out_specs=pl.BlockSpec((1, tm, width), lambda i, j: (i, j, 0)),
        scratch_shapes=[pltpu.VMEM((8, width), F32), pltpu.VMEM((8, width), F32),
                        pltpu.VMEM((tm, width), F32), pltpu.VMEM((tm, width), F32)],
        compiler_params=_cparams(("parallel", "arbitrary")),
        name="rglru",
    )(p_b, conv_w, row(conv_b), wg, row(b_a), row(b_x), row(lam))


def _gla_kernel(q_ref, f_ref, i_ref, g_ref, lb_ref, nw_ref, o_ref, st_ref, *, n_batch, n_heads):
    c = CHUNK

    @pl.when(pl.program_id(0) == 0)
    def _():
        st_ref[...] = jnp.zeros_like(st_ref)

    trilf = jnp.where(_iota((c, c), 1) <= _iota((c, c), 0), 1.0, 0.0)
    tril16 = trilf.astype(BF16)
    mid = c // 2 - 1
    lb = lb_ref[...]

    qm, km, qg, kb, v16, gc = [], [], [], [], [], []
    for bi in range(n_batch):
        q = _silu(q_ref[bi].astype(F32))
        fg = lb + (1.0 - lb) * _sigmoid(f_ref[bi])
        k = 1.0 - fg
        cum = _dot_exact_lhs(tril16, jnp.log(fg))
        cum_c = cum[c - 1:c, :]
        cum_m = cum[mid:mid + 1, :]
        qm.append((q * jnp.exp(cum - cum_m)).astype(BF16))
        km.append((k * jnp.exp(cum_m - cum)).astype(BF16))
        qg.append((q * jnp.exp(cum)).astype(BF16))
        kb.append((k * jnp.exp(cum_c - cum)).astype(BF16))
        v16.append(i_ref[bi].astype(BF16))
        gc.append(jnp.exp(cum_c))

    chains = [(bi, h) for bi in range(n_batch) for h in range(n_heads)]
    idx = range(len(chains))

    def hs(xs):
        return [xs[bi][:, h * LANES:(h + 1) * LANES] for bi, h in chains]

    qm_c, km_c, qg_c, kb_c, v_c, gc_c = hs(qm), hs(km), hs(qg), hs(kb), hs(v16), hs(gc)
    scores = [(_dot_nt(qm_c[i], km_c[i]) * trilf).astype(BF16) for i in idx]
    st = [st_ref[i] for i in idx]
    st16 = [s.astype(BF16) for s in st]
    o = [_dot(scores[i], v_c[i]) + _dot_nt(qg_c[i], st16[i]) for i in idx]
    for i in idx:
        st_ref[i] = st[i] * gc_c[i] + _dot_tn(v_c[i], kb_c[i])

    for i, (bi, h) in enumerate(chains):
        sl = slice(h * LANES, (h + 1) * LANES)
        on = o[i] * lax.rsqrt(jnp.mean(o[i] * o[i], axis=-1, keepdims=True) + RMS_EPS)
        gate = _silu(g_ref[bi, :, sl].astype(F32))
        o_ref[bi, :, sl] = (on * nw_ref[:, sl] * gate).astype(o_ref.dtype)


def _gla(q, f, i, g, lower_bound, norm_w):
    bsz, t, width = q.shape
    n_heads = width // LANES
    tok = pl.BlockSpec((bsz, CHUNK, width), lambda j: (0, j, 0))
    const = pl.BlockSpec((1, width), lambda j: (0, 0))
    return pl.pallas_call(
        functools.partial(_gla_kernel, n_batch=bsz, n_heads=n_heads),
        out_shape=jax.ShapeDtypeStruct((bsz, t, width), BF16),
        grid=(t // CHUNK,),
        in_specs=[tok] * 4 + [const, const],
        out_specs=tok,
        scratch_shapes=[pltpu.VMEM((bsz * n_heads, LANES, LANES), F32)],
        compiler_params=_cparams(("arbitrary",)),
        name="hgrn2_gla",
    )(q, f, i, g, lower_bound.reshape(1, -1), norm_w.reshape(1, -1))


def _route_tile(x, g_ref, w_ref, b_ref, upper_ref, e_o, gate_o, rank_o, cnt_o):
    ms = jnp.mean(x * x, axis=-1, keepdims=True)
    hn = x * lax.rsqrt(ms + RMS_EPS) * g_ref[...]
    lt = _dot_split(hn, w_ref[0], w_ref[1]).T + b_ref[:, 0:1]
    tm = x.shape[0]
    gl = lt[0:8, :]
    row8 = _iota((8, tm), 0)
    gl = jnp.where(row8 < N_GROUPS, gl, -jnp.inf)
    gmax = jnp.max(gl, axis=0, keepdims=True)
    g_sel = jnp.min(jnp.where(gl == gmax, row8, 8), axis=0, keepdims=True)
    g_gate = 1.0 / jnp.sum(jnp.exp(gl - gmax), axis=0, keepdims=True)

    el = jnp.zeros((EXPERTS_PER_GROUP, tm), F32)
    for gi in range(N_GROUPS):
        lo = 8 + gi * EXPERTS_PER_GROUP
        el = jnp.where(g_sel == gi, lt[lo:lo + EXPERTS_PER_GROUP, :], el)
    m1 = jnp.max(el, axis=0, keepdims=True)
    i1 = jnp.min(jnp.where(el == m1, row8, 8), axis=0, keepdims=True)
    el2 = jnp.where(row8 == i1, -jnp.inf, el)
    m2 = jnp.max(el2, axis=0, keepdims=True)
    i2 = jnp.min(jnp.where(el2 == m2, row8, 8), axis=0, keepdims=True)
    e2 = jnp.exp(m2 - m1)
    inv = 1.0 / (1.0 + e2)
    ea = g_sel * EXPERTS_PER_GROUP + i1
    eb = g_sel * EXPERTS_PER_GROUP + i2
    e_o[...] = jnp.concatenate([ea, eb], axis=0)
    gate_o[...] = jnp.concatenate([g_gate * inv, g_gate * e2 * inv], axis=0)

    erow = _iota((N_EXPERTS, tm), 0)
    oh_a = jnp.where(erow == ea, 1.0, 0.0)
    oh_b = jnp.where(erow == eb, 1.0, 0.0)
    pre_a = _dot(oh_a.astype(BF16), upper_ref[...])
    pre_b = _dot(oh_b.astype(BF16), upper_ref[...])
    cnt_a = jnp.sum(oh_a, axis=1, keepdims=True)
    cnt_b = jnp.sum(oh_b, axis=1, keepdims=True)
    rank_a = jnp.sum(oh_a * pre_a, axis=0, keepdims=True)
    rank_b = jnp.sum(oh_b * (pre_b + cnt_a), axis=0, keepdims=True)
    rank_o[...] = jnp.concatenate([rank_a, rank_b], axis=0).astype(I32)
    cnt_o[0] = jnp.broadcast_to(cnt_a + cnt_b, (N_EXPERTS, LANES))


def _seg_local_rows(tm):
    return 2 * tm + N_EXPERTS * SUBLANES


def _segment_copies(cnt_ref, off_ref, dst_ref, base, local, remote, sem, to_remote, wait):
    def body(e, carry):
        c = pl.multiple_of(cnt_ref[base + e], SUBLANES)

        @pl.when(c > 0)
        def _():
            off = 0 if off_ref is None else pl.multiple_of(off_ref[base + e], SUBLANES)
            loc = local.at[pl.ds(off, c), :]
            rem = remote.at[pl.ds(pl.multiple_of(dst_ref[base + e], SUBLANES), c), :]
            cp = (pltpu.make_async_copy(loc, rem, sem) if to_remote
                  else pltpu.make_async_copy(rem, loc, sem))
            if wait:
                cp.wait()
            else:
                cp.start()
        return carry
    lax.fori_loop(0, N_EXPERTS, body, 0)


def _dispatch_kernel(cnt_ref, off_ref, dst_ref, tcnt_ref, tdst_ref, nused_ref,
                     x_ref, g_ref, lidx_ref, xs_out, sbuf, zbuf, sems, zsem):
    j = pl.program_id(0)
    nb = pl.num_programs(0)
    tm = x_ref.shape[0]
    slot = lax.rem(j, 2)

    def copies(tile, s, wait):
        _segment_copies(cnt_ref, off_ref, dst_ref, tile * N_EXPERTS, sbuf.at[s], xs_out,
                        sems.at[s], True, wait)

    @pl.when(j == 0)
    def _():
        zbuf[...] = jnp.zeros_like(zbuf)
        rows = zbuf.shape[0]
        n_blocks = xs_out.shape[0] // rows
        for wait in (False, True):
            _segment_copies(tcnt_ref, None, tdst_ref, 0, zbuf, xs_out, zsem, True, wait)

            def body(b, carry):
                start = pl.multiple_of(b * rows, SUBLANES)
                cp = pltpu.make_async_copy(zbuf, xs_out.at[pl.ds(start, rows), :], zsem)
                if wait:
                    cp.wait()
                else:
                    cp.start()
                return carry
            lax.fori_loop(nused_ref[0], n_blocks, body, 0)

    @pl.when(j >= 2)
    def _():
        copies(j - 2, slot, True)

    x = x_ref[...]
    ms = jnp.mean(x * x, axis=-1, keepdims=True)
    hn = (x * lax.rsqrt(ms + RMS_EPS) * g_ref[...]).astype(BF16)
    r = _iota((_seg_local_rows(tm), tm), 0)
    perm = jnp.where((r == lidx_ref[0:1, :]) | (r == lidx_ref[1:2, :]), 1.0, 0.0).astype(BF16)
    sbuf[slot] = _pack_pairs(_dot(perm, hn))
    copies(j, slot, False)

    @pl.when(j == nb - 1)
    def _():
        copies(j, slot, True)

    @pl.when((j == nb - 1) & (nb >= 2))
    def _():
        copies(j - 1, 1 - slot, True)


def _dispatch(x2d, g, lidx, seg, tails, n_used, n_rows, rows, tm):
    n, d = x2d.shape
    return pl.pallas_call(
        _dispatch_kernel,
        out_shape=jax.ShapeDtypeStruct((n_rows, d // 2), U32),
        grid_spec=pltpu.PrefetchScalarGridSpec(
            num_scalar_prefetch=6,
            grid=(n // tm,),
            in_specs=[pl.BlockSpec((tm, d), lambda j, *_: (j, 0)),
                      pl.BlockSpec((1, d), lambda j, *_: (0, 0)),
                      pl.BlockSpec((2, tm), lambda j, *_: (0, j))],
            out_specs=pl.BlockSpec(memory_space=pl.ANY),
            scratch_shapes=[pltpu.VMEM((2, _seg_local_rows(tm), d // 2), U32),
                            pltpu.VMEM((rows, d // 2), U32),
                            pltpu.SemaphoreType.DMA((2,)), pltpu.SemaphoreType.DMA(())]),
        compiler_params=_cparams(("arbitrary",)),
        name="moe_dispatch",
    )(*seg, *tails, n_used, x2d, g.reshape(1, d), lidx)


def _expert_kernel(be_ref, first_ref, nused_ref, x_ref, wg_ref, wu_ref, wd_ref, o_ref,
                   wg16, wu16, wd16):
    j = pl.program_id(0)

    @pl.when(first_ref[j] == 1)
    def _():
        wg16[...] = wg_ref[0, 0].astype(BF16)
        wu16[...] = wu_ref[0, 0].astype(BF16)
        wd16[...] = wd_ref[0, 0].astype(BF16)

    @pl.when(j < nused_ref[0])
    def _():
        x_lo, x_hi = _unpack_pairs(x_ref[...])
        half = x_lo.shape[1]
        hg = _dot(x_lo, wg16[:half, :]) + _dot(x_hi, wg16[half:, :])
        hu = _dot(x_lo, wu16[:half, :]) + _dot(x_hi, wu16[half:, :])
        hid = (_silu(hg) * hu).astype(BF16)
        o_ref[...] = _pack_pairs(_dot(hid, wd16[...]))

    @pl.when(j >= nused_ref[0])
    def _():
        o_ref[...] = jnp.zeros_like(o_ref)


def _expert_ffn(xs, block_e, first_flag, n_used, w_gate, w_up, w_down, layer, rows):
    n_rows, dh = xs.shape
    n_blocks = n_rows // rows
    d, ff = w_gate.shape[-2:]
    assert d == 2 * dh
    wspec = lambda shp: pl.BlockSpec((1, 1) + shp, lambda j, be, fi, nu: (layer, be[j], 0, 0))
    return pl.pallas_call(
        _expert_kernel,
        out_shape=jax.ShapeDtypeStruct((n_rows, dh), U32),
        grid_spec=pltpu.PrefetchScalarGridSpec(
            num_scalar_prefetch=3,
            grid=(n_blocks,),
            in_specs=[pl.BlockSpec((rows, dh),
                                   lambda j, be, fi, nu: (jnp.minimum(j, nu[0] - 1), 0)),
                      wspec((d, ff)), wspec((d, ff)), wspec((ff, d))],
            out_specs=pl.BlockSpec((rows, dh), lambda j, be, fi, nu: (j, 0)),
            scratch_shapes=[pltpu.VMEM((d, ff), BF16), pltpu.VMEM((d, ff), BF16),
                            pltpu.VMEM((ff, d), BF16)]),
        compiler_params=_cparams(("arbitrary",)),
        name="moe_expert_ffn",
    )(block_e, first_flag, n_used, xs, w_gate, w_up, w_down)


def _combine_kernel(cnt_ref, off_ref, dst_ref, x_ref, lidx_ref, gt_ref, ys_hbm, gfin_ref, o_ref,
                    ybuf, sems, *, final_norm):
    j = pl.program_id(0)
    nb = pl.num_programs(0)
    tm = x_ref.shape[0]
    slot = lax.rem(j, 2)

    def copies(tile, s, wait):
        _segment_copies(cnt_ref, off_ref, dst_ref, tile * N_EXPERTS, ybuf.at[s], ys_hbm,
                        sems.at[s], False, wait)

    @pl.when(j == 0)
    def _():
        ybuf[...] = jnp.zeros_like(ybuf)
        copies(0, 0, False)

    @pl.when(j + 1 < nb)
    def _():
        copies(j + 1, 1 - slot, False)

    copies(j, slot, True)
    c = _iota((tm, _seg_local_rows(tm)), 1)
    li = lidx_ref[...]
    gt = gt_ref[...]
    mix = (jnp.where(c == li[:, 0:1], gt[:, 0:1], 0.0)
           + jnp.where(c == li[:, 1:2], gt[:, 1:2], 0.0)).astype(BF16)
    y_lo, y_hi = _unpack_pairs(ybuf[slot])
    y = x_ref[...] + jnp.concatenate([_dot(mix, y_lo), _dot(mix, y_hi)], axis=1)
    if final_norm:
        ms = jnp.mean(y * y, axis=-1, keepdims=True)
        y = y * lax.rsqrt(ms + RMS_EPS) * gfin_ref[...]
    o_ref[...] = y


def _combine(x2d, ys, lidx_t, gates_t, seg, g_final, final_norm, tm):
    n, d = x2d.shape
    return pl.pallas_call(
        functools.partial(_combine_kernel, final_norm=final_norm),
        out_shape=jax.ShapeDtypeStruct((n, d), F32),
        grid_spec=pltpu.PrefetchScalarGridSpec(
            num_scalar_prefetch=3,
            grid=(n // tm,),
            in_specs=[pl.BlockSpec((tm, d), lambda j, *_: (j, 0)),
                      pl.BlockSpec((tm, 2), lambda j, *_: (j, 0)),
                      pl.BlockSpec((tm, 2), lambda j, *_: (j, 0)),
                      pl.BlockSpec(memory_space=pl.ANY),
                      pl.BlockSpec((1, d), lambda j, *_: (0, 0))],
            out_specs=pl.BlockSpec((tm, d), lambda j, *_: (j, 0)),
            scratch_shapes=[pltpu.VMEM((2, _seg_local_rows(tm), d // 2), U32),
                            pltpu.SemaphoreType.DMA((2,))]),
        compiler_params=_cparams(("arbitrary",)),
        name="moe_combine",
    )(*seg, x2d, lidx_t, gates_t, ys, g_final.reshape(1, d))


def _route_plan(e2, rank2, tile_cnt, rows, tm):
    n_tok = e2.shape[1]
    n_tiles = n_tok // tm
    seg_cnt = (tile_cnt + SUBLANES - 1) // SUBLANES * SUBLANES
    counts = jnp.sum(seg_cnt, axis=0)
    padded = (counts + rows - 1) // rows * rows
    pend = jnp.cumsum(padded)
    pstart = pend - padded
    seg_dst = pstart[None, :] + jnp.cumsum(seg_cnt, axis=0) - seg_cnt
    seg_off = jnp.cumsum(seg_cnt, axis=1) - seg_cnt
    eid = jnp.arange(N_EXPERTS, dtype=I32)
    e3 = e2.reshape(2, n_tiles, tm)
    lidx = rank2.reshape(2, n_tiles, tm) + jnp.sum(
        jnp.where(e3[None] == eid[:, None, None, None], seg_off.T[:, None, :, None], 0), axis=0)
    lidx = lidx.reshape(2, n_tok).astype(I32)
    n_blocks = -(-(2 * n_tok + n_tiles * N_EXPERTS * SUBLANES) // rows) + N_EXPERTS
    starts = jnp.arange(n_blocks, dtype=I32) * rows
    block_e = jnp.minimum(jnp.sum((starts[:, None] >= pend[None, :]).astype(I32), axis=1),
                          N_EXPERTS - 1)
    first = jnp.concatenate([jnp.ones((1,), I32),
                             (block_e[1:] != block_e[:-1]).astype(I32)])
    n_used = (pend[-1] // rows).astype(I32).reshape(1)
    seg = tuple(a.reshape(-1).astype(I32) for a in (seg_cnt, seg_off, seg_dst))
    tails = ((padded - counts).astype(I32), (pstart + counts).astype(I32))
    return lidx, seg, tails, block_e, first, n_used, n_blocks * rows


def _moe(x2d, route, g_ffn, w_gate, w_up, w_down, layer, g_final, final_norm, tm,
         rows=ROUTE_ROWS):
    e2, gates2, rank2, cnt = route
    lidx, seg, tails, block_e, first, n_used, n_rows = _route_plan(
        e2, rank2, cnt[:, :, 0].astype(I32), rows, tm)
    xs = _dispatch(x2d, g_ffn, lidx, seg, tails, n_used, n_rows, rows, tm)
    ys = _expert_ffn(xs, block_e, first, n_used, w_gate, w_up, w_down, layer, rows)
    return _combine(x2d, ys, lidx.T, gates2.T, seg, g_final, final_norm, tm)


def kernel(x, norm_mix, norm_ffn, norm_final, ab_w_in, rw_mu, rw_w0, rw_w2, rw_a0, rw_a2, rw_g2, rw_k_k, rw_k_a, rw_r_k, rw_ln_w, rw_ln_b, lru_conv_w, lru_conv_b, lru_w_a, lru_b_a, lru_w_x, lru_b_x, lru_lambda, ab_w_out, c_w_in, c_lower_bound, c_norm_w, c_w_out, moe_w_group, moe_b_group, moe_w_expert, moe_b_expert, moe_w_gate, moe_w_up, moe_w_down):
    bsz, t, d = x.shape
    depth = norm_mix.shape[0]
    n = bsz * t
    lbs = jnp.cumsum(jax.nn.softmax(c_lower_bound.astype(F32), axis=0), axis=0)
    lbs = lbs - lbs[0]
    x2d = x.reshape(n, d)
    for layer in range(depth):
        j = layer // 2
        if layer % 2 == 0:
            rw_cols = rw_mu.shape[1]
            lru_cols = ab_w_in.shape[2] - rw_cols
            width = rw_w0.shape[1]
            p_a, p_b = _norm_matmul(x2d, norm_mix[layer], ab_w_in[j], (rw_cols, lru_cols),
                                    (F32, F32))
            prep = _rwkv_prep(p_a.reshape(bsz, t, rw_cols), rw_mu[j], rw_w0[j], rw_w2[j],
                              rw_a0[j], rw_a2[j], rw_g2[j], rw_k_k[j], rw_k_a[j],
                              rw_r_k[j].reshape(-1))
            ya = _rwkv_scan(*prep, rw_ln_w[j], rw_ln_b[j])
            yb = _lru(p_b.reshape(bsz, t, lru_cols), lru_conv_w[j], lru_conv_b[j], lru_w_a[j],
                      lru_b_a[j], lru_w_x[j], lru_b_x[j], lru_lambda[j])
            ys = [ya.reshape(n, width), yb.reshape(n, -1)]
            ws = [ab_w_out[j][:width], ab_w_out[j][width:]]
        else:
            hw = c_norm_w.shape[1]
            q, f, i_, g = _norm_matmul(x2d, norm_mix[layer], c_w_in[j], (hw,) * 4,
                                       (BF16, F32, BF16, BF16))
            shp = (bsz, t, hw)
            o = _gla(q.reshape(shp), f.reshape(shp), i_.reshape(shp), g.reshape(shp),
                     lbs[layer], c_norm_w[j])
            ys = [o.reshape(n, hw)]
            ws = [c_w_out[j]]
        tm = min(ROUTE_TILE, n)
        x2d = _proj_residual(x2d, ys, ws)
        route = _router(x2d, norm_ffn[layer], moe_w_group[layer], moe_b_group[layer],
                        moe_w_expert[layer], moe_b_expert[layer], tm)
        x2d = _moe(x2d, route, norm_ffn[layer], moe_w_gate, moe_w_up, moe_w_down, layer,
                   norm_final, layer == depth - 1, tm)
    return x2d.reshape(bsz, t, d)
```

```python
import functools

import jax
import jax.numpy as jnp
from jax import lax
from jax.experimental import pallas as pl
from jax.experimental.pallas import tpu as pltpu

F32 = jnp.float32
BF16 = jnp.bfloat16
I32 = jnp.int32
U32 = jnp.uint32

RMS_EPS = 1e-6
RWKV_GN_EPS = 64e-5
LRU_C = 8.0
CHUNK = 64
CHUNK_SHIFT = CHUNK.bit_length() - 1
HEAD64 = 64
HEAD_SHIFT = HEAD64.bit_length() - 1
LANES = 128
SUBLANES = 8
N_GROUPS = 4
EXPERTS_PER_GROUP = 8
N_EXPERTS = N_GROUPS * EXPERTS_PER_GROUP
ROUTE_ROWS = 512
ROUTE_TILE = 512
VMEM_LIMIT = 56 * 1024 * 1024


def _cparams(sem):
    return pltpu.CompilerParams(dimension_semantics=sem, vmem_limit_bytes=VMEM_LIMIT)


def _sigmoid(x):
    return 0.5 * jnp.tanh(0.5 * x) + 0.5


def _pack_pairs(x):
    w = x.shape[1] // 2
    lo = lax.bitcast_convert_type(x[:, :w].astype(BF16).astype(F32), U32)
    hi = lax.bitcast_convert_type(x[:, w:].astype(BF16).astype(F32), U32)
    return (lo >> 16) | hi


def _unpack_pairs(p):
    lo = lax.bitcast_convert_type(p << 16, F32).astype(BF16)
    hi = lax.bitcast_convert_type(p & jnp.uint32(0xFFFF0000), F32).astype(BF16)
    return lo, hi


def _softplus(x):
    return jnp.maximum(x, 0.0) + jnp.log(1.0 + jnp.exp(-jnp.abs(x)))


def _silu(x):
    return x * _sigmoid(x)


def _gelu_tanh(x):
    return 0.5 * x * (1.0 + jnp.tanh(0.7978845608028654 * (x + 0.044715 * x * x * x)))


def _dot(a, b):
    return jnp.dot(a, b, preferred_element_type=F32)


def _dot_nt(a, b):
    return lax.dot_general(a, b, (((1,), (1,)), ((), ())), preferred_element_type=F32)


def _dot_tn(a, b):
    return lax.dot_general(a, b, (((0,), (0,)), ((), ())), preferred_element_type=F32)


def _split2(x):
    hi = x.astype(BF16)
    lo = (x - hi.astype(F32)).astype(BF16)
    return hi, lo


def _dot_exact_rhs(x, m_bf16):
    hi, lo = _split2(x)
    return _dot(hi, m_bf16) + _dot(lo, m_bf16)


def _dot_exact_lhs(m_bf16, x):
    hi, lo = _split2(x)
    return _dot(m_bf16, hi) + _dot(m_bf16, lo)


def _dot_split(a, w_hi, w_lo):
    a_hi, a_lo = _split2(a)
    return _dot(a_hi, w_hi) + (_dot(a_lo, w_hi) + _dot(a_hi, w_lo))


def _hi_lo(w):
    hi = w.astype(BF16)
    return hi, (w - hi.astype(F32)).astype(BF16)


def _iota(shape, dim):
    return lax.broadcasted_iota(I32, shape, dim)


def _norm_matmul_kernel(x_ref, g_ref, w_ref, *o_refs, splits):
    x = x_ref[...]
    ms = jnp.mean(x * x, axis=-1, keepdims=True)
    y = (x * lax.rsqrt(ms + RMS_EPS) * g_ref[...]).astype(BF16)
    off = 0
    for o_ref, n in zip(o_refs, splits):
        o_ref[...] = _dot(y, w_ref[:, off:off + n]).astype(o_ref.dtype)
        off += n


def _norm_matmul(x2d, g, w, splits, out_dtypes, tm=1024):
    n, d = x2d.shape
    tm = min(tm, n)
    ncols = w.shape[1]
    assert sum(splits) == ncols and n % tm == 0 and len(out_dtypes) == len(splits)
    return pl.pallas_call(
        functools.partial(_norm_matmul_kernel, splits=splits),
        out_shape=[jax.ShapeDtypeStruct((n, s), dt) for s, dt in zip(splits, out_dtypes)],
        grid=(n // tm,),
        in_specs=[pl.BlockSpec((tm, d), lambda i: (i, 0)),
                  pl.BlockSpec((1, d), lambda i: (0, 0)),
                  pl.BlockSpec((d, ncols), lambda i: (0, 0))],
        out_specs=[pl.BlockSpec((tm, s), lambda i: (i, 0)) for s in splits],
        compiler_params=_cparams(("parallel",)),
        name="norm_matmul",
    )(x2d, g.reshape(1, d), w.astype(BF16))


def _proj_residual_kernel(*refs, n_in):
    x_ref = refs[0]
    y_refs = refs[1:1 + n_in]
    w_refs = refs[1 + n_in:1 + 2 * n_in]
    o_ref = refs[1 + 2 * n_in]
    acc = x_ref[...]
    for y_ref, w_ref in zip(y_refs, w_refs):
        acc = acc + _dot(y_ref[...].astype(BF16), w_ref[...])
    o_ref[...] = acc


def _proj_residual(x2d, ys, ws, tm=1024):
    n, d = x2d.shape
    tm = min(tm, n)
    n_in = len(ys)
    in_specs = [pl.BlockSpec((tm, d), lambda i: (i, 0))]
    in_specs += [pl.BlockSpec((tm, y.shape[1]), lambda i: (i, 0)) for y in ys]
    in_specs += [pl.BlockSpec(w.shape, lambda i: (0, 0)) for w in ws]
    return pl.pallas_call(
        functools.partial(_proj_residual_kernel, n_in=n_in),
        out_shape=jax.ShapeDtypeStruct((n, d), F32),
        grid=(n // tm,),
        in_specs=in_specs,
        out_specs=pl.BlockSpec((tm, d), lambda i: (i, 0)),
        compiler_params=_cparams(("parallel",)),
        name="proj_residual",
    )(x2d, *ys, *[w.astype(BF16) for w in ws])


def _router_kernel(x_ref, g_ref, w_ref, b_ref, upper_ref, e_o, gate_o, rank_o, cnt_o):
    _route_tile(x_ref[...], g_ref, w_ref, b_ref, upper_ref, e_o, gate_o, rank_o, cnt_o)


def _router(x2d, g_ffn, w_group, b_group, w_expert, b_expert, tm):
    n, d = x2d.shape
    wt = jnp.zeros((d, LANES), F32)
    wt = wt.at[:, 0:N_GROUPS].set(w_group).at[:, 8:8 + N_EXPERTS].set(w_expert)
    wt = jnp.stack(_hi_lo(wt))
    bt = jnp.zeros((LANES,), F32)
    bt = bt.at[0:N_GROUPS].set(b_group).at[8:8 + N_EXPERTS].set(b_expert)
    bt = jnp.broadcast_to(bt[:, None], (LANES, LANES))
    ti = jnp.arange(tm, dtype=I32)
    upper = (ti[:, None] < ti[None, :]).astype(BF16)
    tok2 = pl.BlockSpec((2, tm), lambda i: (0, i))
    return pl.pallas_call(
        _router_kernel,
        out_shape=[jax.ShapeDtypeStruct((2, n), I32),
                   jax.ShapeDtypeStruct((2, n), F32),
                   jax.ShapeDtypeStruct((2, n), I32),
                   jax.ShapeDtypeStruct((n // tm, N_EXPERTS, LANES), F32)],
        grid=(n // tm,),
        in_specs=[pl.BlockSpec((tm, d), lambda i: (i, 0)),
                  pl.BlockSpec((1, d), lambda i: (0, 0)),
                  pl.BlockSpec((2, d, LANES), lambda i: (0, 0, 0)),
                  pl.BlockSpec((LANES, LANES), lambda i: (0, 0)),
                  pl.BlockSpec((tm, tm), lambda i: (0, 0))],
        out_specs=[tok2, tok2, tok2,
                   pl.BlockSpec((1, N_EXPERTS, LANES), lambda i: (i, 0, 0))],
        compiler_params=_cparams(("parallel",)),
        name="moe_router",
    )(x2d, g_ffn.reshape(1, d), wt, bt, upper)


def _rwkv_prep_kernel(p_ref, mu_ref, w0_ref, a0_ref, kk_s_ref, ka_ref, rk_ref,
                      wcomb_ref, g2_ref, tril_ref,
                      rt_o, kkt_o, kh_o, bh_o, kb_o, bb_o, v_o, g_o, bonus_o, gc_o,
                      prev_ref, *, width):
    tm = p_ref.shape[1]

    @pl.when(pl.program_id(1) == 0)
    def _():
        prev_ref[...] = jnp.zeros_like(prev_ref)

    p = p_ref[0]
    rolled = pltpu.roll(p, 1, axis=0)
    prev = jnp.where(_iota(p.shape, 0) == 0, prev_ref[...], rolled)
    prev_ref[...] = p[tm - 1:tm, :]
    ps = p + (prev - p) * mu_ref[...]

    r = ps[:, 0:width]
    k = ps[:, width:2 * width]
    v = ps[:, 2 * width:3 * width]
    lowrank = ps[:, 3 * width:3 * width + LANES]
    gl = ps[:, 3 * width + LANES:3 * width + 2 * LANES]

    lane = _iota(lowrank.shape, 1)
    lr_in = jnp.where(lane < HEAD64, jnp.tanh(lowrank), lowrank)
    t12 = _dot_split(lr_in, wcomb_ref[0], wcomb_ref[1])
    wlog = -_softplus(-(w0_ref[...] + t12[:, :width])) - 0.5
    lw = -jnp.exp(wlog)
    a = _sigmoid(a0_ref[...] + t12[:, width:])
    g = _dot_split(_sigmoid(gl), g2_ref[0], g2_ref[1])

    seg = jnp.where((_iota((LANES, LANES), 0) >> HEAD_SHIFT)
                    == (_iota((LANES, LANES), 1) >> HEAD_SHIFT),
                    1.0, 0.0).astype(BF16)

    def head_sums(x):
        return jnp.concatenate([_dot_exact_rhs(x[:, q * LANES:(q + 1) * LANES], seg)
                                for q in range(width // LANES)], axis=1)

    kk = k * kk_s_ref[...]
    nrm = jnp.sqrt(head_sums(kk * kk))
    kk = kk / jnp.maximum(nrm, 1e-12)
    k2 = k * (1.0 + (a - 1.0) * ka_ref[...])
    bonus = head_sums(r * k2 * rk_ref[...]) * v

    b = kk * a

    n_chunk = tm // CHUNK
    cum = _dot_exact_lhs(tril_ref[...], lw)
    cum3 = cum.reshape(n_chunk, CHUNK, width)
    cend = cum3[:, CHUNK - 1:CHUNK, :]
    e_neg = jnp.exp(-cum)
    e_end = jnp.exp(cend - cum3).reshape(tm, width)

    rt_o[0] = (r * jnp.exp(cum)).astype(BF16)
    kkt_o[0] = (kk * jnp.exp(cum - lw)).astype(BF16)
    kh_o[0] = (k2 * e_neg).astype(BF16)
    bh_o[0] = (b * e_neg).astype(BF16)
    kb_o[0] = (k2 * e_end).astype(BF16)
    bb_o[0] = (b * e_end).astype(BF16)
    v_o[0] = v.astype(BF16)
    g_o[0] = g.astype(BF16)
    bonus_o[0] = bonus.astype(BF16)
    gc_o[0] = jnp.exp(cend)


def _rwkv_prep(p_a, mu, w0, w2, a0, a2, g2, k_k, k_a, r_k, tm=512):
    b, t, cols = p_a.shape
    width = w0.shape[0]
    rank = w2.shape[0]
    assert rank == HEAD64 and a2.shape[0] == HEAD64 and g2.shape[0] == LANES
    zeros = jnp.zeros((rank, width), F32)
    wcomb = jnp.concatenate([jnp.concatenate([w2, zeros], 1),
                             jnp.concatenate([zeros, a2], 1)], 0)
    row = lambda a_: a_.reshape(1, -1)
    const = lambda shp: pl.BlockSpec(shp, lambda i, j: (0,) * len(shp))
    tok = pl.BlockSpec((1, tm, width), lambda i, j: (i, j, 0))
    n_chunk = tm // CHUNK
    ti = jnp.arange(tm, dtype=I32)
    tril_bd = (((ti[:, None] >> CHUNK_SHIFT) == (ti[None, :] >> CHUNK_SHIFT))
               & (ti[None, :] <= ti[:, None])).astype(BF16)
    return pl.pallas_call(
        functools.partial(_rwkv_prep_kernel, width=width),
        out_shape=([jax.ShapeDtypeStruct((b, t, width), BF16)] * 9
                   + [jax.ShapeDtypeStruct((b, t // CHUNK, 1, width), F32)]),
        grid=(b, t // tm),
        in_specs=[pl.BlockSpec((1, tm, cols), lambda i, j: (i, j, 0)),
                  const((1, cols)), const((1, width)), const((1, width)), const((1, width)),
                  const((1, width)), const((1, width)),
                  const((2, LANES, 2 * width)), const((2, LANES, width)), const((tm, tm))],
        out_specs=[tok] * 9 + [pl.BlockSpec((1, n_chunk, 1, width), lambda i, j: (i, j, 0, 0))],
        scratch_shapes=[pltpu.VMEM((1, cols), F32)],
        compiler_params=_cparams(("parallel", "arbitrary")),
        name="rwkv_prep",
    )(p_a, row(mu), row(w0), row(a0), row(k_k), row(k_a), row(r_k),
      jnp.stack(_hi_lo(wcomb)), jnp.stack(_hi_lo(g2)), tril_bd)


def _rwkv_scan_kernel(rt_ref, kkt_ref, kh_ref, bh_ref, kb_ref, bb_ref, v_ref, g_ref, bonus_ref,
                      gc_ref, lnw_ref, lnb_ref, o_ref, st_ref, *, n_batch, n_pairs):
    c = CHUNK

    @pl.when(pl.program_id(0) == 0)
    def _():
        st_ref[...] = jnp.zeros_like(st_ref)

    rr = _iota((LANES, LANES), 0)
    cc = _iota((LANES, LANES), 1)
    mask_bd = jnp.where((rr >> HEAD_SHIFT) == (cc >> HEAD_SHIFT), 1.0, 0.0)
    mask_bd16 = mask_bd.astype(BF16)
    t_i = _iota((c, LANES), 0)
    s_i = _iota((c, LANES), 1) & (c - 1)
    strict = jnp.where(s_i < t_i, 1.0, 0.0)
    incl = jnp.where(s_i <= t_i, 1.0, 0.0)
    eye_ss = jnp.where(s_i == t_i, 1.0, 0.0)

    def bd16(x):
        return jnp.concatenate([x, x], axis=0) * mask_bd16

    def b16(xs):
        return [x.astype(BF16) for x in xs]

    chains = [(bi, p) for bi in range(n_batch) for p in range(n_pairs)]
    n_ch = len(chains)
    idx = range(n_ch)

    def ld(ref):
        return [ref[bi, :, p * LANES:(p + 1) * LANES] for bi, p in chains]

    rt, kkt, kh, bh, kb, bb, v = (ld(ref) for ref in
                                  (rt_ref, kkt_ref, kh_ref, bh_ref, kb_ref, bb_ref, v_ref))

    lhs = [jnp.concatenate([kkt[i], rt[i]], axis=0) for i in idx]
    rhs = [jnp.concatenate([bd16(kh[i]), bd16(bh[i])], axis=0) for i in idx]
    gm = [_dot_nt(lhs[i], rhs[i]) for i in idx]
    l_kv = [gm[i][:c, :LANES] * strict for i in idx]
    xp = [-(gm[i][:c, LANES:] * strict) for i in idx]
    pm = [gm[i][c:, :LANES] * incl for i in idx]
    nqm = [-(gm[i][c:, LANES:] * incl) for i in idx]

    tinv = [eye_ss + xp[i] for i in idx]
    xp16 = b16(xp)
    xpb = [bd16(x) for x in xp16]
    for _ in range(c.bit_length() - 2):
        xp16 = b16([_dot(xp16[i], xpb[i]) for i in idx])
        xpb = [bd16(x) for x in xp16]
        t16 = b16(tinv)
        tinv = [tinv[i] + _dot(t16[i], xpb[i]) for i in idx]
    tinv16 = b16(tinv)

    v_bd = [bd16(x) for x in v]
    l_kv16 = b16(l_kv)
    w16 = b16([_dot(l_kv16[i], v_bd[i]) for i in idx])
    tkw = [_dot(tinv16[i], jnp.concatenate([bd16(kkt[i]), bd16(w16[i])], axis=1)) for i in idx]
    tk16 = b16([x[:, :LANES] for x in tkw])

    st = [st_ref[i] for i in idx]
    st16 = b16(st)
    u = [_dot_nt(tk16[i], st16[i]) + tkw[i][:, LANES:] for i in idx]
    u16 = b16(u)
    pq16 = [jnp.concatenate([pm[i], nqm[i]], axis=1).astype(BF16) for i in idx]
    vu = [jnp.concatenate([v_bd[i], bd16(u16[i])], axis=0) for i in idx]
    y = [_dot_nt(rt[i], st16[i]) + _dot(pq16[i], vu[i]) for i in idx]
    upd = [_dot_tn(jnp.concatenate([v[i], -u16[i]], axis=0),
                   jnp.concatenate([kb[i], bb[i]], axis=0)) for i in idx]
    for i, (bi, p) in enumerate(chains):
        st_ref[i] = st[i] * gc_ref[bi, 0, :, p * LANES:(p + 1) * LANES] + upd[i] * mask_bd

    ys = jnp.concatenate(y, axis=0)
    mean = _dot_exact_rhs(ys, mask_bd16) * (1.0 / HEAD64)
    d = ys - mean
    var = _dot_exact_rhs(d * d, mask_bd16) * (1.0 / HEAD64)
    dn = d * lax.rsqrt(var + RWKV_GN_EPS)
    for i, (bi, p) in enumerate(chains):
        sl = slice(p * LANES, (p + 1) * LANES)
        yn = dn[i * c:(i + 1) * c] * lnw_ref[:, sl] + lnb_ref[:, sl]
        o_ref[bi, :, sl] = ((yn + bonus_ref[bi, :, sl]) * g_ref[bi, :, sl]).astype(o_ref.dtype)


def _rwkv_scan(rt, kkt, kh, bh, kb, bb, v, g, bonus, gc, ln_w, ln_b):
    bsz, t, width = rt.shape
    n_pairs = width // LANES
    tok = pl.BlockSpec((bsz, CHUNK, width), lambda j: (0, j, 0))
    const = pl.BlockSpec((1, width), lambda j: (0, 0))
    return pl.pallas_call(
        functools.partial(_rwkv_scan_kernel, n_batch=bsz, n_pairs=n_pairs),
        out_shape=jax.ShapeDtypeStruct((bsz, t, width), BF16),
        grid=(t // CHUNK,),
        in_specs=[tok] * 9 + [pl.BlockSpec((bsz, 1, 1, width), lambda j: (0, j, 0, 0)),
                              const, const],
        out_specs=tok,
        scratch_shapes=[pltpu.VMEM((bsz * n_pairs, LANES, LANES), F32)],
        compiler_params=_cparams(("arbitrary",)),
        name="rwkv_scan",
    )(rt, kkt, kh, bh, kb, bb, v, g, bonus, gc, ln_w.reshape(1, -1), ln_b.reshape(1, -1))


def _lru_kernel(p_ref, cw_ref, cb_ref, wg_ref, ba_ref, bx_ref, lam_ref, o_ref,
                xcarry_ref, hcarry_ref, a_s, u_s, *, width):
    tm = p_ref.shape[1]

    @pl.when(pl.program_id(1) == 0)
    def _():
        xcarry_ref[...] = jnp.zeros_like(xcarry_ref)
        hcarry_ref[...] = jnp.zeros_like(hcarry_ref)

    gate = p_ref[0, :, 0:width]
    xb = p_ref[0, :, width:2 * width]
    carry8 = xcarry_ref[...]
    row8 = _iota((8, width), 0)

    def shifted(s):
        rolled = pltpu.roll(xb, s, axis=0)
        first = jnp.where(row8 < s, pltpu.roll(carry8, s, axis=0), rolled[0:8])
        return jnp.concatenate([first, rolled[8:]], axis=0)

    xc = (cw_ref[0:1, :] * shifted(3) + cw_ref[1:2, :] * shifted(2)
          + cw_ref[2:3, :] * shifted(1) + cw_ref[3:4, :] * xb + cb_ref[...])
    xcarry_ref[...] = xb[tm - 8:tm, :]

    gates = _dot(xc.astype(BF16), wg_ref[...])
    rg = _sigmoid(gates[:, :width] + ba_ref[...])
    ig = _sigmoid(gates[:, width:] + bx_ref[...])
    log_a = -LRU_C * rg * _softplus(-lam_ref[...])
    a = jnp.exp(log_a)
    a_s[...] = a
    u_s[...] = jnp.sqrt(1.0 - a * a) * ig * xc

    m1 = row8 >= 1
    m2 = row8 >= 2
    m4 = row8 >= 4

    def body(i, h):
        off = pl.multiple_of(i * 8, 8)
        a8 = a_s[pl.ds(off, 8), :]
        u8 = u_s[pl.ds(off, 8), :]
        for s, m in ((1, m1), (2, m2), (4, m4)):
            u_sh = jnp.where(m, pltpu.roll(u8, s, axis=0), 0.0)
            a_sh = jnp.where(m, pltpu.roll(a8, s, axis=0), 1.0)
            u8 = u8 + a8 * u_sh
            a8 = a8 * a_sh
        h8 = u8 + a8 * h
        u_s[pl.ds(off, 8), :] = h8
        return jnp.broadcast_to(h8[7:8, :], (8, width))

    h_last = lax.fori_loop(0, tm // 8, body, hcarry_ref[...])
    hcarry_ref[...] = h_last
    o_ref[0] = (u_s[...] * _gelu_tanh(gate)).astype(o_ref.dtype)


def _block_diag(w):
    nb, di, do = w.shape
    eye = jnp.eye(nb, dtype=w.dtype)
    return (eye[:, None, :, None] * w[:, :, None, :]).reshape(nb * di, nb * do)


def _lru(p_b, conv_w, conv_b, w_a, b_a, w_x, b_x, lam, tm=1024):
    b, t, cols = p_b.shape
    width = cols // 2
    wg = jnp.concatenate([_block_diag(w_a), _block_diag(w_x)], axis=1).astype(BF16)
    row = lambda a_: a_.reshape(1, -1)
    const = lambda shp: pl.BlockSpec(shp, lambda i, j: (0, 0))
    return pl.pallas_call(
        functools.partial(_lru_kernel, width=width),
        out_shape=jax.ShapeDtypeStruct((b, t, width), BF16),
        grid=(b, t // tm),
        in_specs=[pl.BlockSpec((1, tm, cols), lambda i, j: (i, j, 0)),
                  const(conv_w.shape), const((1, width)), const((width, 2 * width)),
                  const((1, width)), const((1, width)), const((1, width))],
        out_specs=pl.BlockSpec((1, tm, width), lambda i, j: (i, j, 0)),
        scratch_shapes=[pltpu.VMEM((8, width), F32), pltpu.VMEM((8, width), F32),
                        pltpu.VMEM((tm, width), F32), pltpu.VMEM((tm, width), F32)],
        compiler_params=_cparams(("parallel", "arbitrary")),
        name="rglru",
    )(p_b, conv_w, row(conv_b), wg, row(b_a), row(b_x), row(lam))


def _gla_kernel(q_ref, f_ref, i_ref, g_ref, lb_ref, nw_ref, o_ref, st_ref, *, n_batch, n_heads):
    c = CHUNK

    @pl.when(pl.program_id(0) == 0)
    def _():
        st_ref[...] = jnp.zeros_like(st_ref)

    trilf = jnp.where(_iota((c, c), 1) <= _iota((c, c), 0), 1.0, 0.0)
    tril16 = trilf.astype(BF16)
    mid = c // 2 - 1
    lb = lb_ref[...]

    qm, km, qg, kb, v16, gc = [], [], [], [], [], []
    for bi in range(n_batch):
        q = _silu(q_ref[bi].astype(F32))
        fg = lb + (1.0 - lb) * _sigmoid(f_ref[bi])
        k = 1.0 - fg
        cum = _dot_exact_lhs(tril16, jnp.log(fg))
        cum_c = cum[c - 1:c, :]
        cum_m = cum[mid:mid + 1, :]
        qm.append((q * jnp.exp(cum - cum_m)).astype(BF16))
        km.append((k * jnp.exp(cum_m - cum)).astype(BF16))
        qg.append((q * jnp.exp(cum)).astype(BF16))
        kb.append((k * jnp.exp(cum_c - cum)).astype(BF16))
        v16.append(i_ref[bi].astype(BF16))
        gc.append(jnp.exp(cum_c))

    chains = [(bi, h) for bi in range(n_batch) for h in range(n_heads)]
    idx = range(len(chains))

    def hs(xs):
        return [xs[bi][:, h * LANES:(h + 1) * LANES] for bi, h in chains]

    qm_c, km_c, qg_c, kb_c, v_c, gc_c = hs(qm), hs(km), hs(qg), hs(kb), hs(v16), hs(gc)
    scores = [(_dot_nt(qm_c[i], km_c[i]) * trilf).astype(BF16) for i in idx]
    st = [st_ref[i] for i in idx]
    st16 = [s.astype(BF16) for s in st]
    o = [_dot(scores[i], v_c[i]) + _dot_nt(qg_c[i], st16[i]) for i in idx]
    for i in idx:
        st_ref[i] = st[i] * gc_c[i] + _dot_tn(v_c[i], kb_c[i])

    for i, (bi, h) in enumerate(chains):
        sl = slice(h * LANES, (h + 1) * LANES)
        on = o[i] * lax.rsqrt(jnp.mean(o[i] * o[i], axis=-1, keepdims=True) + RMS_EPS)
        gate = _silu(g_ref[bi, :, sl].astype(F32))
        o_ref[bi, :, sl] = (on * nw_ref[:, sl] * gate).astype(o_ref.dtype)


def _gla(q, f, i, g, lower_bound, norm_w):
    bsz, t, width = q.shape
    n_heads = width // LANES
    tok = pl.BlockSpec((bsz, CHUNK, width), lambda j: (0, j, 0))
    const = pl.BlockSpec((1, width), lambda j: (0, 0))
    return pl.pallas_call(
        functools.partial(_gla_kernel, n_batch=bsz, n_heads=n_heads),
        out_shape=jax.ShapeDtypeStruct((bsz, t, width), BF16),
        grid=(t // CHUNK,),
        in_specs=[tok] * 4 + [const, const],
        out_specs=tok,
        scratch_shapes=[pltpu.VMEM((bsz * n_heads, LANES, LANES), F32)],
        compiler_params=_cparams(("arbitrary",)),
        name="hgrn2_gla",
    )(q, f, i, g, lower_bound.reshape(1, -1), norm_w.reshape(1, -1))


def _route_tile(x, g_ref, w_ref, b_ref, upper_ref, e_o, gate_o, rank_o, cnt_o):
    ms = jnp.mean(x * x, axis=-1, keepdims=True)
    hn = x * lax.rsqrt(ms + RMS_EPS) * g_ref[...]
    lt = _dot_split(hn, w_ref[0], w_ref[1]).T + b_ref[:, 0:1]
    tm = x.shape[0]
    gl = lt[0:8, :]
    row8 = _iota((8, tm), 0)
    gl = jnp.where(row8 < N_GROUPS, gl, -jnp.inf)
    gmax = jnp.max(gl, axis=0, keepdims=True)
    g_sel = jnp.min(jnp.where(gl == gmax, row8, 8), axis=0, keepdims=True)
    g_gate = 1.0 / jnp.sum(jnp.exp(gl - gmax), axis=0, keepdims=True)

    el = jnp.zeros((EXPERTS_PER_GROUP, tm), F32)
    for gi in range(N_GROUPS):
        lo = 8 + gi * EXPERTS_PER_GROUP
        el = jnp.where(g_sel == gi, lt[lo:lo + EXPERTS_PER_GROUP, :], el)
    m1 = jnp.max(el, axis=0, keepdims=True)
    i1 = jnp.min(jnp.where(el == m1, row8, 8), axis=0, keepdims=True)
    el2 = jnp.where(row8 == i1, -jnp.inf, el)
    m2 = jnp.max(el2, axis=0, keepdims=True)
    i2 = jnp.min(jnp.where(el2 == m2, row8, 8), axis=0, keepdims=True)
    e2 = jnp.exp(m2 - m1)
    inv = 1.0 / (1.0 + e2)
    ea = g_sel * EXPERTS_PER_GROUP + i1
    eb = g_sel * EXPERTS_PER_GROUP + i2
    e_o[...] = jnp.concatenate([ea, eb], axis=0)
    gate_o[...] = jnp.concatenate([g_gate * inv, g_gate * e2 * inv], axis=0)

    erow = _iota((N_EXPERTS, tm), 0)
    oh_a = jnp.where(erow == ea, 1.0, 0.0)
    oh_b = jnp.where(erow == eb, 1.0, 0.0)
    pre_a = _dot(oh_a.astype(BF16), upper_ref[...])
    pre_b = _dot(oh_b.astype(BF16), upper_ref[...])
    cnt_a = jnp.sum(oh_a, axis=1, keepdims=True)
    cnt_b = jnp.sum(oh_b, axis=1, keepdims=True)
    rank_a = jnp.sum(oh_a * pre_a, axis=0, keepdims=True)
    rank_b = jnp.sum(oh_b * (pre_b + cnt_a), axis=0, keepdims=True)
    rank_o[...] = jnp.concatenate([rank_a, rank_b], axis=0).astype(I32)
    cnt_o[0] = jnp.broadcast_to(cnt_a + cnt_b, (N_EXPERTS, LANES))


def _seg_local_rows(tm):
    return 2 * tm + N_EXPERTS * SUBLANES


def _segment_copies(cnt_ref, off_ref, dst_ref, base, local, remote, sem, to_remote, wait):
    def body(e, carry):
        c = pl.multiple_of(cnt_ref[base + e], SUBLANES)

        @pl.when(c > 0)
        def _():
            off = 0 if off_ref is None else pl.multiple_of(off_ref[base + e], SUBLANES)
            loc = local.at[pl.ds(off, c), :]
            rem = remote.at[pl.ds(pl.multiple_of(dst_ref[base + e], SUBLANES), c), :]
            cp = (pltpu.make_async_copy(loc, rem, sem) if to_remote
                  else pltpu.make_async_copy(rem, loc, sem))
            if wait:
                cp.wait()
            else:
                cp.start()
        return carry
    lax.fori_loop(0, N_EXPERTS, body, 0, unroll=4)


def _segment_wait(tot_ref, tile, local, remote, sem, to_remote):
    tot = pl.multiple_of(tot_ref[tile], SUBLANES)

    @pl.when(tot > 0)
    def _():
        loc = local.at[pl.ds(0, tot), :]
        rem = remote.at[pl.ds(0, tot), :]
        cp = (pltpu.make_async_copy(loc, rem, sem) if to_remote
              else pltpu.make_async_copy(rem, loc, sem))
        cp.wait()


def _dispatch_kernel(cnt_ref, off_ref, dst_ref, tot_ref, tcnt_ref, tdst_ref, nused_ref,
                     x_ref, g_ref, lidx_ref, xs_out, sbuf, zbuf, sems, zsem):
    j = pl.program_id(0)
    nb = pl.num_programs(0)
    tm = x_ref.shape[0]
    slot = lax.rem(j, 2)

    def copies(tile, s, wait):
        if wait:
            _segment_wait(tot_ref, tile, sbuf.at[s], xs_out, sems.at[s], True)
        else:
            _segment_copies(cnt_ref, off_ref, dst_ref, tile * N_EXPERTS, sbuf.at[s], xs_out,
                            sems.at[s], True, False)

    @pl.when(j == 0)
    def _():
        zbuf[...] = jnp.zeros_like(zbuf)
        rows = zbuf.shape[0]
        n_blocks = xs_out.shape[0] // rows
        for wait in (False, True):
            _segment_copies(tcnt_ref, None, tdst_ref, 0, zbuf, xs_out, zsem, True, wait)

            def body(b, carry):
                start = pl.multiple_of(b * rows, SUBLANES)
                cp = pltpu.make_async_copy(zbuf, xs_out.at[pl.ds(start, rows), :], zsem)
                if wait:
                    cp.wait()
                else:
                    cp.start()
                return carry
            lax.fori_loop(nused_ref[0], n_blocks, body, 0)

    @pl.when(j >= 2)
    def _():
        copies(j - 2, slot, True)

    x = x_ref[...]
    ms = jnp.mean(x * x, axis=-1, keepdims=True)
    hn = (x * lax.rsqrt(ms + RMS_EPS) * g_ref[...]).astype(BF16)
    r = _iota((_seg_local_rows(tm), tm), 0)
    perm = jnp.where((r == lidx_ref[0:1, :]) | (r == lidx_ref[1:2, :]), 1.0, 0.0).astype(BF16)
    sbuf[slot] = _pack_pairs(_dot(perm, hn))
    copies(j, slot, False)

    @pl.when(j == nb - 1)
    def _():
        copies(j, slot, True)

    @pl.when((j == nb - 1) & (nb >= 2))
    def _():
        copies(j - 1, 1 - slot, True)


def _dispatch(x2d, g, lidx, seg, tails, n_used, n_rows, rows, tm):
    n, d = x2d.shape
    return pl.pallas_call(
        _dispatch_kernel,
        out_shape=jax.ShapeDtypeStruct((n_rows, d // 2), U32),
        grid_spec=pltpu.PrefetchScalarGridSpec(
            num_scalar_prefetch=7,
            grid=(n // tm,),
            in_specs=[pl.BlockSpec((tm, d), lambda j, *_: (j, 0)),
                      pl.BlockSpec((1, d), lambda j, *_: (0, 0)),
                      pl.BlockSpec((2, tm), lambda j, *_: (0, j))],
            out_specs=pl.BlockSpec(memory_space=pl.ANY),
            scratch_shapes=[pltpu.VMEM((2, _seg_local_rows(tm), d // 2), U32),
                            pltpu.VMEM((rows, d // 2), U32),
                            pltpu.SemaphoreType.DMA((2,)), pltpu.SemaphoreType.DMA(())]),
        compiler_params=_cparams(("arbitrary",)),
        name="moe_dispatch",
    )(*seg, *tails, n_used, x2d, g.reshape(1, d), lidx)


def _expert_kernel(be_ref, first_ref, nused_ref, x_ref, wg_ref, wu_ref, wd_ref, o_ref,
                   wg16, wu16, wd16):
    j = pl.program_id(0)

    @pl.when(first_ref[j] == 1)
    def _():
        wg16[...] = wg_ref[0, 0].astype(BF16)
        wu16[...] = wu_ref[0, 0].astype(BF16)
        wd16[...] = wd_ref[0, 0].astype(BF16)

    @pl.when(j < nused_ref[0])
    def _():
        x_lo, x_hi = _unpack_pairs(x_ref[...])
        half = x_lo.shape[1]
        hg = _dot(x_lo, wg16[:half, :]) + _dot(x_hi, wg16[half:, :])
        hu = _dot(x_lo, wu16[:half, :]) + _dot(x_hi, wu16[half:, :])
        hid = (_silu(hg) * hu).astype(BF16)
        o_ref[...] = _pack_pairs(_dot(hid, wd16[...]))

    @pl.when(j >= nused_ref[0])
    def _():
        o_ref[...] = jnp.zeros_like(o_ref)


def _expert_ffn(xs, block_e, first_flag, n_used, w_gate, w_up, w_down, layer, rows):
    n_rows, dh = xs.shape
    n_blocks = n_rows // rows
    d, ff = w_gate.shape[-2:]
    assert d == 2 * dh
    wspec = lambda shp: pl.BlockSpec((1, 1) + shp, lambda j, be, fi, nu: (layer, be[j], 0, 0))
    return pl.pallas_call(
        _expert_kernel,
        out_shape=jax.ShapeDtypeStruct((n_rows, dh), U32),
        grid_spec=pltpu.PrefetchScalarGridSpec(
            num_scalar_prefetch=3,
            grid=(n_blocks,),
            in_specs=[pl.BlockSpec((rows, dh),
                                   lambda j, be, fi, nu: (jnp.minimum(j, nu[0] - 1), 0)),
                      wspec((d, ff)), wspec((d, ff)), wspec((ff, d))],
            out_specs=pl.BlockSpec((rows, dh), lambda j, be, fi, nu: (j, 0)),
            scratch_shapes=[pltpu.VMEM((d, ff), BF16), pltpu.VMEM((d, ff), BF16),
                            pltpu.VMEM((ff, d), BF16)]),
        compiler_params=_cparams(("arbitrary",)),
        name="moe_expert_ffn",
    )(block_e, first_flag, n_used, xs, w_gate, w_up, w_down)


def _combine_kernel(cnt_ref, off_ref, dst_ref, tot_ref, x_ref, lidx_ref, gt_ref, ys_hbm, gfin_ref,
                    o_ref, ybuf, sems, *, final_norm):
    j = pl.program_id(0)
    nb = pl.num_programs(0)
    tm = x_ref.shape[0]
    slot = lax.rem(j, 2)

    def copies(tile, s, wait):
        if wait:
            _segment_wait(tot_ref, tile, ybuf.at[s], ys_hbm, sems.at[s], False)
        else:
            _segment_copies(cnt_ref, off_ref, dst_ref, tile * N_EXPERTS, ybuf.at[s], ys_hbm,
                            sems.at[s], False, False)

    @pl.when(j == 0)
    def _():
        ybuf[...] = jnp.zeros_like(ybuf)
        copies(0, 0, False)

    @pl.when(j + 1 < nb)
    def _():
        copies(j + 1, 1 - slot, False)

    copies(j, slot, True)
    c = _iota((tm, _seg_local_rows(tm)), 1)
    li = lidx_ref[...]
    gt = gt_ref[...]
    mix = (jnp.where(c == li[:, 0:1], gt[:, 0:1], 0.0)
           + jnp.where(c == li[:, 1:2], gt[:, 1:2], 0.0)).astype(BF16)
    y_lo, y_hi = _unpack_pairs(ybuf[slot])
    y = x_ref[...] + jnp.concatenate([_dot(mix, y_lo), _dot(mix, y_hi)], axis=1)
    if final_norm:
        ms = jnp.mean(y * y, axis=-1, keepdims=True)
        y = y * lax.rsqrt(ms + RMS_EPS) * gfin_ref[...]
    o_ref[...] = y


def _combine(x2d, ys, lidx_t, gates_t, seg, g_final, final_norm, tm):
    n, d = x2d.shape
    return pl.pallas_call(
        functools.partial(_combine_kernel, final_norm=final_norm),
        out_shape=jax.ShapeDtypeStruct((n, d), F32),
        grid_spec=pltpu.PrefetchScalarGridSpec(
            num_scalar_prefetch=4,
            grid=(n // tm,),
            in_specs=[pl.BlockSpec((tm, d), lambda j, *_: (j, 0)),
                      pl.BlockSpec((tm, 2), lambda j, *_: (j, 0)),
                      pl.BlockSpec((tm, 2), lambda j, *_: (j, 0)),
                      pl.BlockSpec(memory_space=pl.ANY),
                      pl.BlockSpec((1, d), lambda j, *_: (0, 0))],
            out_specs=pl.BlockSpec((tm, d), lambda j, *_: (j, 0)),
            scratch_shapes=[pltpu.VMEM((2, _seg_local_rows(tm), d // 2), U32),
                            pltpu.SemaphoreType.DMA((2,))]),
        compiler_params=_cparams(("arbitrary",)),
        name="moe_combine",
    )(*seg, x2d, lidx_t, gates_t, ys, g_final.reshape(1, d))


def _route_plan(e2, rank2, tile_cnt, rows, tm):
    n_tok = e2.shape[1]
    n_tiles = n_tok // tm
    seg_cnt = (tile_cnt + SUBLANES - 1) // SUBLANES * SUBLANES
    counts = jnp.sum(seg_cnt, axis=0)
    padded = (counts + rows - 1) // rows * rows
    pend = jnp.cumsum(padded)
    pstart = pend - padded
    seg_dst = pstart[None, :] + jnp.cumsum(seg_cnt, axis=0) - seg_cnt
    seg_off = jnp.cumsum(seg_cnt, axis=1) - seg_cnt
    eid = jnp.arange(N_EXPERTS, dtype=I32)
    e3 = e2.reshape(2, n_tiles, tm)
    lidx = rank2.reshape(2, n_tiles, tm) + jnp.sum(
        jnp.where(e3[None] == eid[:, None, None, None], seg_off.T[:, None, :, None], 0), axis=0)
    lidx = lidx.reshape(2, n_tok).astype(I32)
    n_blocks = -(-(2 * n_tok + n_tiles * N_EXPERTS * SUBLANES) // rows) + N_EXPERTS
    starts = jnp.arange(n_blocks, dtype=I32) * rows
    block_e = jnp.minimum(jnp.sum((starts[:, None] >= pend[None, :]).astype(I32), axis=1),
                          N_EXPERTS - 1)
    first = jnp.concatenate([jnp.ones((1,), I32),
                             (block_e[1:] != block_e[:-1]).astype(I32)])
    n_used = (pend[-1] // rows).astype(I32).reshape(1)
    seg = tuple(a.reshape(-1).astype(I32)
                for a in (seg_cnt, seg_off, seg_dst, jnp.sum(seg_cnt, axis=1)))
    tails = ((padded - counts).astype(I32), (pstart + counts).astype(I32))
    return lidx, seg, tails, block_e, first, n_used, n_blocks * rows


def _moe(x2d, route, g_ffn, w_gate, w_up, w_down, layer, g_final, final_norm, tm,
         rows=ROUTE_ROWS):
    e2, gates2, rank2, cnt = route
    lidx, seg, tails, block_e, first, n_used, n_rows = _route_plan(
        e2, rank2, cnt[:, :, 0].astype(I32), rows, tm)
    xs = _dispatch(x2d, g_ffn, lidx, seg, tails, n_used, n_rows, rows, tm)
    ys = _expert_ffn(xs, block_e, first, n_used, w_gate, w_up, w_down, layer, rows)
    return _combine(x2d, ys, lidx.T, gates2.T, seg, g_final, final_norm, tm)


def kernel(x, norm_mix, norm_ffn, norm_final, ab_w_in, rw_mu, rw_w0, rw_w2, rw_a0, rw_a2, rw_g2, rw_k_k, rw_k_a, rw_r_k, rw_ln_w, rw_ln_b, lru_conv_w, lru_conv_b, lru_w_a, lru_b_a, lru_w_x, lru_b_x, lru_lambda, ab_w_out, c_w_in, c_lower_bound, c_norm_w, c_w_out, moe_w_group, moe_b_group, moe_w_expert, moe_b_expert, moe_w_gate, moe_w_up, moe_w_down):
    bsz, t, d = x.shape
    depth = norm_mix.shape[0]
    n = bsz * t
    lbs = jnp.cumsum(jax.nn.softmax(c_lower_bound.astype(F32), axis=0), axis=0)
    lbs = lbs - lbs[0]
    x2d = x.reshape(n, d)
    for layer in range(depth):
        j = layer // 2
        if layer % 2 == 0:
            rw_cols = rw_mu.shape[1]
            lru_cols = ab_w_in.shape[2] - rw_cols
            width = rw_w0.shape[1]
            p_a, p_b = _norm_matmul(x2d, norm_mix[layer], ab_w_in[j], (rw_cols, lru_cols),
                                    (F32, F32))
            prep = _rwkv_prep(p_a.reshape(bsz, t, rw_cols), rw_mu[j], rw_w0[j], rw_w2[j],
                              rw_a0[j], rw_a2[j], rw_g2[j], rw_k_k[j], rw_k_a[j],
                              rw_r_k[j].reshape(-1))
            ya = _rwkv_scan(*prep, rw_ln_w[j], rw_ln_b[j])
            yb = _lru(p_b.reshape(bsz, t, lru_cols), lru_conv_w[j], lru_conv_b[j], lru_w_a[j],
                      lru_b_a[j], lru_w_x[j], lru_b_x[j], lru_lambda[j])
            ys = [ya.reshape(n, width), yb.reshape(n, -1)]
            ws = [ab_w_out[j][:width], ab_w_out[j][width:]]
        else:
            hw = c_norm_w.shape[1]
            q, f, i_, g = _norm_matmul(x2d, norm_mix[layer], c_w_in[j], (hw,) * 4,
                                       (BF16, F32, BF16, BF16))
            shp = (bsz, t, hw)
            o = _gla(q.reshape(shp), f.reshape(shp), i_.reshape(shp), g.reshape(shp),
                     lbs[layer], c_norm_w[j])
            ys = [o.reshape(n, hw)]
            ws = [c_w_out[j]]
        tm = min(ROUTE_TILE, n)
        x2d = _proj_residual(x2d, ys, ws)
        route = _router(x2d, norm_ffn[layer], moe_w_group[layer], moe_b_group[layer],
                        moe_w_expert[layer], moe_b_expert[layer], tm)
        x2d = _moe(x2d, route, norm_ffn[layer], moe_w_gate, moe_w_up, moe_w_down, layer,
                   norm_final, layer == depth - 1, tm)
    return x2d.reshape(bsz, t, d)
```

```python
import functools

import jax
import jax.numpy as jnp
from jax import lax
from jax.experimental import pallas as pl
from jax.experimental.pallas import tpu as pltpu

F32 = jnp.float32
BF16 = jnp.bfloat16
I32 = jnp.int32
U32 = jnp.uint32

RMS_EPS = 1e-6
RWKV_GN_EPS = 64e-5
LRU_C = 8.0
CHUNK = 64
CHUNK_SHIFT = CHUNK.bit_length() - 1
HEAD64 = 64
HEAD_SHIFT = HEAD64.bit_length() - 1
LANES = 128
SUBLANES = 8
N_GROUPS = 4
EXPERTS_PER_GROUP = 8
N_EXPERTS = N_GROUPS * EXPERTS_PER_GROUP
ROUTE_ROWS = 512
ROUTE_TILE = 512
VMEM_LIMIT = 56 * 1024 * 1024


def _cparams(sem):
    return pltpu.CompilerParams(dimension_semantics=sem, vmem_limit_bytes=VMEM_LIMIT)


def _sigmoid(x):
    return 0.5 * jnp.tanh(0.5 * x) + 0.5


def _pack_pairs(x):
    w = x.shape[1] // 2
    lo = lax.bitcast_convert_type(x[:, :w].astype(BF16).astype(F32), U32)
    hi = lax.bitcast_convert_type(x[:, w:].astype(BF16).astype(F32), U32)
    return (lo >> 16) | hi


def _unpack_pairs(p):
    lo = lax.bitcast_convert_type(p << 16, F32).astype(BF16)
    hi = lax.bitcast_convert_type(p & jnp.uint32(0xFFFF0000), F32).astype(BF16)
    return lo, hi


def _softplus(x):
    return jnp.maximum(x, 0.0) + jnp.log(1.0 + jnp.exp(-jnp.abs(x)))


def _silu(x):
    return x * _sigmoid(x)


def _gelu_tanh(x):
    return 0.5 * x * (1.0 + jnp.tanh(0.7978845608028654 * (x + 0.044715 * x * x * x)))


def _dot(a, b):
    return jnp.dot(a, b, preferred_element_type=F32)


def _dot_nt(a, b):
    return lax.dot_general(a, b, (((1,), (1,)), ((), ())), preferred_element_type=F32)


def _dot_tn(a, b):
    return lax.dot_general(a, b, (((0,), (0,)), ((), ())), preferred_element_type=F32)


def _split2(x):
    hi = x.astype(BF16)
    lo = (x - hi.astype(F32)).astype(BF16)
    return hi, lo


def _dot_exact_rhs(x, m_bf16):
    hi, lo = _split2(x)
    return _dot(hi, m_bf16) + _dot(lo, m_bf16)


def _dot_exact_lhs(m_bf16, x):
    hi, lo = _split2(x)
    return _dot(m_bf16, hi) + _dot(m_bf16, lo)


def _dot_split(a, w_hi, w_lo):
    a_hi, a_lo = _split2(a)
    return _dot(a_hi, w_hi) + (_dot(a_lo, w_hi) + _dot(a_hi, w_lo))


def _hi_lo(w):
    hi = w.astype(BF16)
    return hi, (w - hi.astype(F32)).astype(BF16)


def _iota(shape, dim):
    return lax.broadcasted_iota(I32, shape, dim)


def _norm_matmul_kernel(x_ref, g_ref, w_ref, *o_refs, splits):
    x = x_ref[...]
    ms = jnp.mean(x * x, axis=-1, keepdims=True)
    y = (x * lax.rsqrt(ms + RMS_EPS) * g_ref[...]).astype(BF16)
    off = 0
    for o_ref, n in zip(o_refs, splits):
        o_ref[...] = _dot(y, w_ref[:, off:off + n]).astype(o_ref.dtype)
        off += n


def _norm_matmul(x2d, g, w, splits, out_dtypes, tm=1024):
    n, d = x2d.shape
    tm = min(tm, n)
    ncols = w.shape[1]
    assert sum(splits) == ncols and n % tm == 0 and len(out_dtypes) == len(splits)
    return pl.pallas_call(
        functools.partial(_norm_matmul_kernel, splits=splits),
        out_shape=[jax.ShapeDtypeStruct((n, s), dt) for s, dt in zip(splits, out_dtypes)],
        grid=(n // tm,),
        in_specs=[pl.BlockSpec((tm, d), lambda i: (i, 0)),
                  pl.BlockSpec((1, d), lambda i: (0, 0)),
                  pl.BlockSpec((d, ncols), lambda i: (0, 0))],
        out_specs=[pl.BlockSpec((tm, s), lambda i: (i, 0)) for s in splits],
        compiler_params=_cparams(("parallel",)),
        name="norm_matmul",
    )(x2d, g.reshape(1, d), w.astype(BF16))


def _proj_residual_kernel(*refs, n_in):
    x_ref = refs[0]
    y_refs = refs[1:1 + n_in]
    w_refs = refs[1 + n_in:1 + 2 * n_in]
    o_ref = refs[1 + 2 * n_in]
    acc = x_ref[...]
    for y_ref, w_ref in zip(y_refs, w_refs):
        acc = acc + _dot(y_ref[...].astype(BF16), w_ref[...])
    o_ref[...] = acc


def _proj_residual(x2d, ys, ws, tm=1024):
    n, d = x2d.shape
    tm = min(tm, n)
    n_in = len(ys)
    in_specs = [pl.BlockSpec((tm, d), lambda i: (i, 0))]
    in_specs += [pl.BlockSpec((tm, y.shape[1]), lambda i: (i, 0)) for y in ys]
    in_specs += [pl.BlockSpec(w.shape, lambda i: (0, 0)) for w in ws]
    return pl.pallas_call(
        functools.partial(_proj_residual_kernel, n_in=n_in),
        out_shape=jax.ShapeDtypeStruct((n, d), F32),
        grid=(n // tm,),
        in_specs=in_specs,
        out_specs=pl.BlockSpec((tm, d), lambda i: (i, 0)),
        compiler_params=_cparams(("parallel",)),
        name="proj_residual",
    )(x2d, *ys, *[w.astype(BF16) for w in ws])


def _router_kernel(x_ref, g_ref, w_ref, b_ref, upper_ref, e_o, gate_o, rank_o, cnt_o):
    _route_tile(x_ref[...], g_ref, w_ref, b_ref, upper_ref, e_o, gate_o, rank_o, cnt_o)


def _router(x2d, g_ffn, w_group, b_group, w_expert, b_expert, tm):
    n, d = x2d.shape
    wt = jnp.zeros((d, LANES), F32)
    wt = wt.at[:, 0:N_GROUPS].set(w_group).at[:, 8:8 + N_EXPERTS].set(w_expert)
    wt = jnp.stack(_hi_lo(wt))
    bt = jnp.zeros((LANES,), F32)
    bt = bt.at[0:N_GROUPS].set(b_group).at[8:8 + N_EXPERTS].set(b_expert)
    bt = jnp.broadcast_to(bt[:, None], (LANES, LANES))
    ti = jnp.arange(tm, dtype=I32)
    upper = (ti[:, None] < ti[None, :]).astype(BF16)
    tok2 = pl.BlockSpec((2, tm), lambda i: (0, i))
    return pl.pallas_call(
        _router_kernel,
        out_shape=[jax.ShapeDtypeStruct((2, n), I32),
                   jax.ShapeDtypeStruct((2, n), F32),
                   jax.ShapeDtypeStruct((2, n), I32),
                   jax.ShapeDtypeStruct((n // tm, N_EXPERTS, LANES), F32)],
        grid=(n // tm,),
        in_specs=[pl.BlockSpec((tm, d), lambda i: (i, 0)),
                  pl.BlockSpec((1, d), lambda i: (0, 0)),
                  pl.BlockSpec((2, d, LANES), lambda i: (0, 0, 0)),
                  pl.BlockSpec((LANES, LANES), lambda i: (0, 0)),
                  pl.BlockSpec((tm, tm), lambda i: (0, 0))],
        out_specs=[tok2, tok2, tok2,
                   pl.BlockSpec((1, N_EXPERTS, LANES), lambda i: (i, 0, 0))],
        compiler_params=_cparams(("parallel",)),
        name="moe_router",
    )(x2d, g_ffn.reshape(1, d), wt, bt, upper)


def _rwkv_prep_kernel(p_ref, mu_ref, w0_ref, a0_ref, kk_s_ref, ka_ref, rk_ref,
                      wcomb_ref, g2_ref, tril_ref,
                      rt_o, kkt_o, kh_o, bh_o, kb_o, bb_o, v_o, g_o, bonus_o, gc_o,
                      prev_ref, *, width):
    tm = p_ref.shape[1]

    @pl.when(pl.program_id(1) == 0)
    def _():
        prev_ref[...] = jnp.zeros_like(prev_ref)

    p = p_ref[0]
    rolled = pltpu.roll(p, 1, axis=0)
    prev = jnp.where(_iota(p.shape, 0) == 0, prev_ref[...], rolled)
    prev_ref[...] = p[tm - 1:tm, :]
    ps = p + (prev - p) * mu_ref[...]

    r = ps[:, 0:width]
    k = ps[:, width:2 * width]
    v = ps[:, 2 * width:3 * width]
    lowrank = ps[:, 3 * width:3 * width + LANES]
    gl = ps[:, 3 * width + LANES:3 * width + 2 * LANES]

    lane = _iota(lowrank.shape, 1)
    lr_in = jnp.where(lane < HEAD64, jnp.tanh(lowrank), lowrank)
    t12 = _dot_split(lr_in, wcomb_ref[0], wcomb_ref[1])
    wlog = -_softplus(-(w0_ref[...] + t12[:, :width])) - 0.5
    lw = -jnp.exp(wlog)
    a = _sigmoid(a0_ref[...] + t12[:, width:])
    g = _dot_split(_sigmoid(gl), g2_ref[0], g2_ref[1])

    seg = jnp.where((_iota((LANES, LANES), 0) >> HEAD_SHIFT)
                    == (_iota((LANES, LANES), 1) >> HEAD_SHIFT),
                    1.0, 0.0).astype(BF16)

    def head_sums(x):
        return jnp.concatenate([_dot_exact_rhs(x[:, q * LANES:(q + 1) * LANES], seg)
                                for q in range(width // LANES)], axis=1)

    kk = k * kk_s_ref[...]
    nrm = jnp.sqrt(head_sums(kk * kk))
    kk = kk / jnp.maximum(nrm, 1e-12)
    k2 = k * (1.0 + (a - 1.0) * ka_ref[...])
    bonus = head_sums(r * k2 * rk_ref[...]) * v

    b = kk * a

    n_chunk = tm // CHUNK
    cum = _dot_exact_lhs(tril_ref[...], lw)
    cum3 = cum.reshape(n_chunk, CHUNK, width)
    cend = cum3[:, CHUNK - 1:CHUNK, :]
    e_neg = jnp.exp(-cum)
    e_end = jnp.exp(cend - cum3).reshape(tm, width)

    rt_o[0] = (r * jnp.exp(cum)).astype(BF16)
    kkt_o[0] = (kk * jnp.exp(cum - lw)).astype(BF16)
    kh_o[0] = (k2 * e_neg).astype(BF16)
    bh_o[0] = (b * e_neg).astype(BF16)
    kb_o[0] = (k2 * e_end).astype(BF16)
    bb_o[0] = (b * e_end).astype(BF16)
    v_o[0] = v.astype(BF16)
    g_o[0] = g.astype(BF16)
    bonus_o[0] = bonus.astype(BF16)
    gc_o[0] = jnp.exp(cend)


def _rwkv_prep(p_a, mu, w0, w2, a0, a2, g2, k_k, k_a, r_k, tm=512):
    b, t, cols = p_a.shape
    width = w0.shape[0]
    rank = w2.shape[0]
    assert rank == HEAD64 and a2.shape[0] == HEAD64 and g2.shape[0] == LANES
    zeros = jnp.zeros((rank, width), F32)
    wcomb = jnp.concatenate([jnp.concatenate([w2, zeros], 1),
                             jnp.concatenate([zeros, a2], 1)], 0)
    row = lambda a_: a_.reshape(1, -1)
    const = lambda shp: pl.BlockSpec(shp, lambda i, j: (0,) * len(shp))
    tok = pl.BlockSpec((1, tm, width), lambda i, j: (i, j, 0))
    n_chunk = tm // CHUNK
    ti = jnp.arange(tm, dtype=I32)
    tril_bd = (((ti[:, None] >> CHUNK_SHIFT) == (ti[None, :] >> CHUNK_SHIFT))
               & (ti[None, :] <= ti[:, None])).astype(BF16)
    return pl.pallas_call(
        functools.partial(_rwkv_prep_kernel, width=width),
        out_shape=([jax.ShapeDtypeStruct((b, t, width), BF16)] * 9
                   + [jax.ShapeDtypeStruct((b, t // CHUNK, 1, width), F32)]),
        grid=(b, t // tm),
        in_specs=[pl.BlockSpec((1, tm, cols), lambda i, j: (i, j, 0)),
                  const((1, cols)), const((1, width)), const((1, width)), const((1, width)),
                  const((1, width)), const((1, width)),
                  const((2, LANES, 2 * width)), const((2, LANES, width)), const((tm, tm))],
        out_specs=[tok] * 9 + [pl.BlockSpec((1, n_chunk, 1, width), lambda i, j: (i, j, 0, 0))],
        scratch_shapes=[pltpu.VMEM((1, cols), F32)],
        compiler_params=_cparams(("parallel", "arbitrary")),
        name="rwkv_prep",
    )(p_a, row(mu), row(w0), row(a0), row(k_k), row(k_a), row(r_k),
      jnp.stack(_hi_lo(wcomb)), jnp.stack(_hi_lo(g2)), tril_bd)


def _rwkv_scan_kernel(rt_ref, kkt_ref, kh_ref, bh_ref, kb_ref, bb_ref, v_ref, g_ref, bonus_ref,
                      gc_ref, lnw_ref, lnb_ref, o_ref, st_ref, *, n_batch, n_pairs):
    c = CHUNK

    @pl.when(pl.program_id(0) == 0)
    def _():
        st_ref[...] = jnp.zeros_like(st_ref)

    rr = _iota((LANES, LANES), 0)
    cc = _iota((LANES, LANES), 1)
    mask_bd = jnp.where((rr >> HEAD_SHIFT) == (cc >> HEAD_SHIFT), 1.0, 0.0)
    mask_bd16 = mask_bd.astype(BF16)
    t_i = _iota((c, LANES), 0)
    s_i = _iota((c, LANES), 1) & (c - 1)
    strict = jnp.where(s_i < t_i, 1.0, 0.0)
    incl = jnp.where(s_i <= t_i, 1.0, 0.0)
    eye_ss = jnp.where(s_i == t_i, 1.0, 0.0)

    def bd16(x):
        return jnp.concatenate([x, x], axis=0) * mask_bd16

    def b16(xs):
        return [x.astype(BF16) for x in xs]

    chains = [(bi, p) for bi in range(n_batch) for p in range(n_pairs)]
    n_ch = len(chains)
    idx = range(n_ch)

    def ld(ref):
        return [ref[bi, :, p * LANES:(p + 1) * LANES] for bi, p in chains]

    rt, kkt, kh, bh, kb, bb, v = (ld(ref) for ref in
                                  (rt_ref, kkt_ref, kh_ref, bh_ref, kb_ref, bb_ref, v_ref))

    lhs = [jnp.concatenate([kkt[i], rt[i]], axis=0) for i in idx]
    rhs = [jnp.concatenate([bd16(kh[i]), bd16(bh[i])], axis=0) for i in idx]
    gm = [_dot_nt(lhs[i], rhs[i]) for i in idx]
    l_kv = [gm[i][:c, :LANES] * strict for i in idx]
    xp = [-(gm[i][:c, LANES:] * strict) for i in idx]
    pm = [gm[i][c:, :LANES] * incl for i in idx]
    nqm = [-(gm[i][c:, LANES:] * incl) for i in idx]

    tinv = [eye_ss + xp[i] for i in idx]
    xp16 = b16(xp)
    xpb = [bd16(x) for x in xp16]
    for _ in range(c.bit_length() - 2):
        xp16 = b16([_dot(xp16[i], xpb[i]) for i in idx])
        xpb = [bd16(x) for x in xp16]
        t16 = b16(tinv)
        tinv = [tinv[i] + _dot(t16[i], xpb[i]) for i in idx]
    tinv16 = b16(tinv)

    v_bd = [bd16(x) for x in v]
    l_kv16 = b16(l_kv)
    w16 = b16([_dot(l_kv16[i], v_bd[i]) for i in idx])
    tkw = [_dot(tinv16[i], jnp.concatenate([bd16(kkt[i]), bd16(w16[i])], axis=1)) for i in idx]
    tk16 = b16([x[:, :LANES] for x in tkw])

    st = [st_ref[i] for i in idx]
    st16 = b16(st)
    u = [_dot_nt(tk16[i], st16[i]) + tkw[i][:, LANES:] for i in idx]
    u16 = b16(u)
    pq16 = [jnp.concatenate([pm[i], nqm[i]], axis=1).astype(BF16) for i in idx]
    vu = [jnp.concatenate([v_bd[i], bd16(u16[i])], axis=0) for i in idx]
    y = [_dot_nt(rt[i], st16[i]) + _dot(pq16[i], vu[i]) for i in idx]
    upd = [_dot_tn(jnp.concatenate([v[i], -u16[i]], axis=0),
                   jnp.concatenate([kb[i], bb[i]], axis=0)) for i in idx]
    for i, (bi, p) in enumerate(chains):
        st_ref[i] = st[i] * gc_ref[bi, 0, :, p * LANES:(p + 1) * LANES] + upd[i] * mask_bd

    ys = jnp.concatenate(y, axis=0)
    mean = _dot_exact_rhs(ys, mask_bd16) * (1.0 / HEAD64)
    d = ys - mean
    var = _dot_exact_rhs(d * d, mask_bd16) * (1.0 / HEAD64)
    dn = d * lax.rsqrt(var + RWKV_GN_EPS)
    for i, (bi, p) in enumerate(chains):
        sl = slice(p * LANES, (p + 1) * LANES)
        yn = dn[i * c:(i + 1) * c] * lnw_ref[:, sl] + lnb_ref[:, sl]
        o_ref[bi, :, sl] = ((yn + bonus_ref[bi, :, sl]) * g_ref[bi, :, sl]).astype(o_ref.dtype)


def _rwkv_scan(rt, kkt, kh, bh, kb, bb, v, g, bonus, gc, ln_w, ln_b):
    bsz, t, width = rt.shape
    n_pairs = width // LANES
    tok = pl.BlockSpec((bsz, CHUNK, width), lambda j: (0, j, 0))
    const = pl.BlockSpec((1, width), lambda j: (0, 0))
    return pl.pallas_call(
        functools.partial(_rwkv_scan_kernel, n_batch=bsz, n_pairs=n_pairs),
        out_shape=jax.ShapeDtypeStruct((bsz, t, width), BF16),
        grid=(t // CHUNK,),
        in_specs=[tok] * 9 + [pl.BlockSpec((bsz, 1, 1, width), lambda j: (0, j, 0, 0)),
                              const, const],
        out_specs=tok,
        scratch_shapes=[pltpu.VMEM((bsz * n_pairs, LANES, LANES), F32)],
        compiler_params=_cparams(("arbitrary",)),
        name="rwkv_scan",
    )(rt, kkt, kh, bh, kb, bb, v, g, bonus, gc, ln_w.reshape(1, -1), ln_b.reshape(1, -1))


def _lru_kernel(p_ref, cw_ref, cb_ref, wg_ref, ba_ref, bx_ref, lam_ref, o_ref,
                xcarry_ref, hcarry_ref, a_s, u_s, *, width):
    tm = p_ref.shape[1]

    @pl.when(pl.program_id(1) == 0)
    def _():
        xcarry_ref[...] = jnp.zeros_like(xcarry_ref)
        hcarry_ref[...] = jnp.zeros_like(hcarry_ref)

    gate = p_ref[0, :, 0:width]
    xb = p_ref[0, :, width:2 * width]
    carry8 = xcarry_ref[...]
    row8 = _iota((8, width), 0)

    def shifted(s):
        rolled = pltpu.roll(xb, s, axis=0)
        first = jnp.where(row8 < s, pltpu.roll(carry8, s, axis=0), rolled[0:8])
        return jnp.concatenate([first, rolled[8:]], axis=0)

    xc = (cw_ref[0:1, :] * shifted(3) + cw_ref[1:2, :] * shifted(2)
          + cw_ref[2:3, :] * shifted(1) + cw_ref[3:4, :] * xb + cb_ref[...])
    xcarry_ref[...] = xb[tm - 8:tm, :]

    gates = _dot(xc.astype(BF16), wg_ref[...])
    rg = _sigmoid(gates[:, :width] + ba_ref[...])
    ig = _sigmoid(gates[:, width:] + bx_ref[...])
    log_a = -LRU_C * rg * _softplus(-lam_ref[...])
    a = jnp.exp(log_a)
    a_s[...] = a
    u_s[...] = jnp.sqrt(1.0 - a * a) * ig * xc

    m1 = row8 >= 1
    m2 = row8 >= 2
    m4 = row8 >= 4

    def body(i, h):
        off = pl.multiple_of(i * 8, 8)
        a8 = a_s[pl.ds(off, 8), :]
        u8 = u_s[pl.ds(off, 8), :]
        for s, m in ((1, m1), (2, m2), (4, m4)):
            u_sh = jnp.where(m, pltpu.roll(u8, s, axis=0), 0.0)
            a_sh = jnp.where(m, pltpu.roll(a8, s, axis=0), 1.0)
            u8 = u8 + a8 * u_sh
            a8 = a8 * a_sh
        h8 = u8 + a8 * h
        u_s[pl.ds(off, 8), :] = h8
        return jnp.broadcast_to(h8[7:8, :], (8, width))

    h_last = lax.fori_loop(0, tm // 8, body, hcarry_ref[...])
    hcarry_ref[...] = h_last
    o_ref[0] = (u_s[...] * _gelu_tanh(gate)).astype(o_ref.dtype)


def _block_diag(w):
    nb, di, do = w.shape
    eye = jnp.eye(nb, dtype=w.dtype)
    return (eye[:, None, :, None] * w[:, :, None, :]).reshape(nb * di, nb * do)


def _lru(p_b, conv_w, conv_b, w_a, b_a, w_x, b_x, lam, tm=1024):
    b, t, cols = p_b.shape
    width = cols // 2
    wg = jnp.concatenate([_block_diag(w_a), _block_diag(w_x)], axis=1).astype(BF16)
    row = lambda a_: a_.reshape(1, -1)
    const = lambda shp: pl.BlockSpec(shp, lambda i, j: (0, 0))
    return pl.pallas_call(
        functools.partial(_lru_kernel, width=width),
        out_shape=jax.ShapeDtypeStruct((b, t, width), BF16),
        grid=(b, t // tm),
        in_specs=[pl.BlockSpec((1, tm, cols), lambda i, j: (i, j, 0)),
                  const(conv_w.shape), const((1, width)), const((width, 2 * width)),
                  const((1, width)), const((1, width)), const((1, width))],
        out_specs=pl.BlockSpec((1, tm, width), lambda i, j: (i, j, 0)),
        scratch_shapes=[pltpu.VMEM((8, width), F32), pltpu.VMEM((8, width), F32),
                        pltpu.VMEM((tm, width), F32), pltpu.VMEM((tm, width), F32)],
        compiler_params=_cparams(("parallel", "arbitrary")),
        name="rglru",
    )(p_b, conv_w, row(conv_b), wg, row(b_a), row(b_x), row(lam))


def _gla_kernel(q_ref, f_ref, i_ref, g_ref, lb_ref, nw_ref, o_ref, st_ref, *, n_batch, n_heads):
    c = CHUNK

    @pl.when(pl.program_id(0) == 0)
    def _():
        st_ref[...] = jnp.zeros_like(st_ref)

    trilf = jnp.where(_iota((c, c), 1) <= _iota((c, c), 0), 1.0, 0.0)
    tril16 = trilf.astype(BF16)
    mid = c // 2 - 1
    lb = lb_ref[...]

    qm, km, qg, kb, v16, gc = [], [], [], [], [], []
    for bi in range(n_batch):
        q = _silu(q_ref[bi].astype(F32))
        fg = lb + (1.0 - lb) * _sigmoid(f_ref[bi])
        k = 1.0 - fg
        cum = _dot_exact_lhs(tril16, jnp.log(fg))
        cum_c = cum[c - 1:c, :]
        cum_m = cum[mid:mid + 1, :]
        qm.append((q * jnp.exp(cum - cum_m)).astype(BF16))
        km.append((k * jnp.exp(cum_m - cum)).astype(BF16))
        qg.append((q * jnp.exp(cum)).astype(BF16))
        kb.append((k * jnp.exp(cum_c - cum)).astype(BF16))
        v16.append(i_ref[bi].astype(BF16))
        gc.append(jnp.exp(cum_c))

    chains = [(bi, h) for bi in range(n_batch) for h in range(n_heads)]
    idx = range(len(chains))

    def hs(xs):
        return [xs[bi][:, h * LANES:(h + 1) * LANES] for bi, h in chains]

    qm_c, km_c, qg_c, kb_c, v_c, gc_c = hs(qm), hs(km), hs(qg), hs(kb), hs(v16), hs(gc)
    scores = [(_dot_nt(qm_c[i], km_c[i]) * trilf).astype(BF16) for i in idx]
    st = [st_ref[i] for i in idx]
    st16 = [s.astype(BF16) for s in st]
    o = [_dot(scores[i], v_c[i]) + _dot_nt(qg_c[i], st16[i]) for i in idx]
    for i in idx:
        st_ref[i] = st[i] * gc_c[i] + _dot_tn(v_c[i], kb_c[i])

    for i, (bi, h) in enumerate(chains):
        sl = slice(h * LANES, (h + 1) * LANES)
        on = o[i] * lax.rsqrt(jnp.mean(o[i] * o[i], axis=-1, keepdims=True) + RMS_EPS)
        gate = _silu(g_ref[bi, :, sl].astype(F32))
        o_ref[bi, :, sl] = (on * nw_ref[:, sl] * gate).astype(o_ref.dtype)


def _gla(q, f, i, g, lower_bound, norm_w):
    bsz, t, width = q.shape
    n_heads = width // LANES
    tok = pl.BlockSpec((bsz, CHUNK, width), lambda j: (0, j, 0))
    const = pl.BlockSpec((1, width), lambda j: (0, 0))
    return pl.pallas_call(
        functools.partial(_gla_kernel, n_batch=bsz, n_heads=n_heads),
        out_shape=jax.ShapeDtypeStruct((bsz, t, width), BF16),
        grid=(t // CHUNK,),
        in_specs=[tok] * 4 + [const, const],
        out_specs=tok,
        scratch_shapes=[pltpu.VMEM((bsz * n_heads, LANES, LANES), F32)],
        compiler_params=_cparams(("arbitrary",)),
        name="hgrn2_gla",
    )(q, f, i, g, lower_bound.reshape(1, -1), norm_w.reshape(1, -1))


def _route_tile(x, g_ref, w_ref, b_ref, upper_ref, e_o, gate_o, rank_o, cnt_o):
    ms = jnp.mean(x * x, axis=-1, keepdims=True)
    hn = x * lax.rsqrt(ms + RMS_EPS) * g_ref[...]
    lt = _dot_split(hn, w_ref[0], w_ref[1]).T + b_ref[:, 0:1]
    tm = x.shape[0]
    gl = lt[0:8, :]
    row8 = _iota((8, tm), 0)
    gl = jnp.where(row8 < N_GROUPS, gl, -jnp.inf)
    gmax = jnp.max(gl, axis=0, keepdims=True)
    g_sel = jnp.min(jnp.where(gl == gmax, row8, 8), axis=0, keepdims=True)
    g_gate = 1.0 / jnp.sum(jnp.exp(gl - gmax), axis=0, keepdims=True)

    el = jnp.zeros((EXPERTS_PER_GROUP, tm), F32)
    for gi in range(N_GROUPS):
        lo = 8 + gi * EXPERTS_PER_GROUP
        el = jnp.where(g_sel == gi, lt[lo:lo + EXPERTS_PER_GROUP, :], el)
    m1 = jnp.max(el, axis=0, keepdims=True)
    i1 = jnp.min(jnp.where(el == m1, row8, 8), axis=0, keepdims=True)
    el2 = jnp.where(row8 == i1, -jnp.inf, el)
    m2 = jnp.max(el2, axis=0, keepdims=True)
    i2 = jnp.min(jnp.where(el2 == m2, row8, 8), axis=0, keepdims=True)
    e2 = jnp.exp(m2 - m1)
    inv = 1.0 / (1.0 + e2)
    ea = g_sel * EXPERTS_PER_GROUP + i1
    eb = g_sel * EXPERTS_PER_GROUP + i2
    e_o[...] = jnp.concatenate([ea, eb], axis=0)
    gate_o[...] = jnp.concatenate([g_gate * inv, g_gate * e2 * inv], axis=0)

    erow = _iota((N_EXPERTS, tm), 0)
    oh_a = jnp.where(erow == ea, 1.0, 0.0)
    oh_b = jnp.where(erow == eb, 1.0, 0.0)
    pre_a = _dot(oh_a.astype(BF16), upper_ref[...])
    pre_b = _dot(oh_b.astype(BF16), upper_ref[...])
    cnt_a = jnp.sum(oh_a, axis=1, keepdims=True)
    cnt_b = jnp.sum(oh_b, axis=1, keepdims=True)
    rank_a = jnp.sum(oh_a * pre_a, axis=0, keepdims=True)
    rank_b = jnp.sum(oh_b * (pre_b + cnt_a), axis=0, keepdims=True)
    rank_o[...] = jnp.concatenate([rank_a, rank_b], axis=0).astype(I32)
    cnt_o[0] = jnp.broadcast_to(cnt_a + cnt_b, (N_EXPERTS, LANES))


def _seg_local_rows(tm):
    return 2 * tm + N_EXPERTS * SUBLANES


def _segment_copies(cnt_ref, off_ref, dst_ref, base, local, remote, sem, to_remote, wait):
    def body(e, carry):
        c = pl.multiple_of(cnt_ref[base + e], SUBLANES)

        @pl.when(c > 0)
        def _():
            off = 0 if off_ref is None else pl.multiple_of(off_ref[base + e], SUBLANES)
            loc = local.at[pl.ds(off, c), :]
            rem = remote.at[pl.ds(pl.multiple_of(dst_ref[base + e], SUBLANES), c), :]
            cp = (pltpu.make_async_copy(loc, rem, sem) if to_remote
                  else pltpu.make_async_copy(rem, loc, sem))
            if wait:
                cp.wait()
            else:
                cp.start()
        return carry
    lax.fori_loop(0, N_EXPERTS, body, 0, unroll=4)


def _local_index(e_ref, rank_ref, off_ref, tile):
    e = e_ref[...]
    lidx = rank_ref[...]
    for ex in range(N_EXPERTS):
        lidx = lidx + jnp.where(e == ex, off_ref[tile * N_EXPERTS + ex], 0)
    return lidx


def _segment_wait(tot_ref, tile, local, remote, sem, to_remote):
    tot = pl.multiple_of(tot_ref[tile], SUBLANES)

    @pl.when(tot > 0)
    def _():
        loc = local.at[pl.ds(0, tot), :]
        rem = remote.at[pl.ds(0, tot), :]
        cp = (pltpu.make_async_copy(loc, rem, sem) if to_remote
              else pltpu.make_async_copy(rem, loc, sem))
        cp.wait()


def _dispatch_kernel(cnt_ref, off_ref, dst_ref, tot_ref, tcnt_ref, tdst_ref, nused_ref,
                     x_ref, g_ref, e_ref, rank_ref, xs_out, sbuf, zbuf, sems, zsem):
    j = pl.program_id(0)
    nb = pl.num_programs(0)
    tm = x_ref.shape[0]
    slot = lax.rem(j, 2)

    def copies(tile, s, wait):
        if wait:
            _segment_wait(tot_ref, tile, sbuf.at[s], xs_out, sems.at[s], True)
        else:
            _segment_copies(cnt_ref, off_ref, dst_ref, tile * N_EXPERTS, sbuf.at[s], xs_out,
                            sems.at[s], True, False)

    @pl.when(j == 0)
    def _():
        zbuf[...] = jnp.zeros_like(zbuf)
        rows = zbuf.shape[0]
        n_blocks = xs_out.shape[0] // rows
        for wait in (False, True):
            _segment_copies(tcnt_ref, None, tdst_ref, 0, zbuf, xs_out, zsem, True, wait)

            def body(b, carry):
                start = pl.multiple_of(b * rows, SUBLANES)
                cp = pltpu.make_async_copy(zbuf, xs_out.at[pl.ds(start, rows), :], zsem)
                if wait:
                    cp.wait()
                else:
                    cp.start()
                return carry
            lax.fori_loop(nused_ref[0], n_blocks, body, 0)

    @pl.when(j >= 2)
    def _():
        copies(j - 2, slot, True)

    x = x_ref[...]
    ms = jnp.mean(x * x, axis=-1, keepdims=True)
    hn = (x * lax.rsqrt(ms + RMS_EPS) * g_ref[...]).astype(BF16)
    r = _iota((_seg_local_rows(tm), tm), 0)
    lidx = _local_index(e_ref, rank_ref, off_ref, j)
    perm = jnp.where((r == lidx[0:1, :]) | (r == lidx[1:2, :]), 1.0, 0.0).astype(BF16)
    sbuf[slot] = _pack_pairs(_dot(perm, hn))
    copies(j, slot, False)

    @pl.when(j == nb - 1)
    def _():
        copies(j, slot, True)

    @pl.when((j == nb - 1) & (nb >= 2))
    def _():
        copies(j - 1, 1 - slot, True)


def _dispatch(x2d, g, e2, rank2, seg, tails, n_used, n_rows, rows, tm):
    n, d = x2d.shape
    return pl.pallas_call(
        _dispatch_kernel,
        out_shape=jax.ShapeDtypeStruct((n_rows, d // 2), U32),
        grid_spec=pltpu.PrefetchScalarGridSpec(
            num_scalar_prefetch=7,
            grid=(n // tm,),
            in_specs=[pl.BlockSpec((tm, d), lambda j, *_: (j, 0)),
                      pl.BlockSpec((1, d), lambda j, *_: (0, 0)),
                      pl.BlockSpec((2, tm), lambda j, *_: (0, j)),
                      pl.BlockSpec((2, tm), lambda j, *_: (0, j))],
            out_specs=pl.BlockSpec(memory_space=pl.ANY),
            scratch_shapes=[pltpu.VMEM((2, _seg_local_rows(tm), d // 2), U32),
                            pltpu.VMEM((rows, d // 2), U32),
                            pltpu.SemaphoreType.DMA((2,)), pltpu.SemaphoreType.DMA(())]),
        compiler_params=_cparams(("arbitrary",)),
        name="moe_dispatch",
    )(*seg, *tails, n_used, x2d, g.reshape(1, d), e2, rank2)


def _expert_kernel(be_ref, first_ref, nused_ref, x_ref, wg_ref, wu_ref, wd_ref, o_ref,
                   wg16, wu16, wd16):
    j = pl.program_id(0)

    @pl.when(first_ref[j] == 1)
    def _():
        wg16[...] = wg_ref[0, 0].astype(BF16)
        wu16[...] = wu_ref[0, 0].astype(BF16)
        wd16[...] = wd_ref[0, 0].astype(BF16)

    @pl.when(j < nused_ref[0])
    def _():
        x_lo, x_hi = _unpack_pairs(x_ref[...])
        half = x_lo.shape[1]
        hg = _dot(x_lo, wg16[:half, :]) + _dot(x_hi, wg16[half:, :])
        hu = _dot(x_lo, wu16[:half, :]) + _dot(x_hi, wu16[half:, :])
        hid = (_silu(hg) * hu).astype(BF16)
        o_ref[...] = _pack_pairs(_dot(hid, wd16[...]))

    @pl.when(j >= nused_ref[0])
    def _():
        o_ref[...] = jnp.zeros_like(o_ref)


def _expert_ffn(xs, block_e, first_flag, n_used, w_gate, w_up, w_down, layer, rows):
    n_rows, dh = xs.shape
    n_blocks = n_rows // rows
    d, ff = w_gate.shape[-2:]
    assert d == 2 * dh
    wspec = lambda shp: pl.BlockSpec((1, 1) + shp, lambda j, be, fi, nu: (layer, be[j], 0, 0))
    return pl.pallas_call(
        _expert_kernel,
        out_shape=jax.ShapeDtypeStruct((n_rows, dh), U32),
        grid_spec=pltpu.PrefetchScalarGridSpec(
            num_scalar_prefetch=3,
            grid=(n_blocks,),
            in_specs=[pl.BlockSpec((rows, dh),
                                   lambda j, be, fi, nu: (jnp.minimum(j, nu[0] - 1), 0)),
                      wspec((d, ff)), wspec((d, ff)), wspec((ff, d))],
            out_specs=pl.BlockSpec((rows, dh), lambda j, be, fi, nu: (j, 0)),
            scratch_shapes=[pltpu.VMEM((d, ff), BF16), pltpu.VMEM((d, ff), BF16),
                            pltpu.VMEM((ff, d), BF16)]),
        compiler_params=_cparams(("arbitrary",)),
        name="moe_expert_ffn",
    )(block_e, first_flag, n_used, xs, w_gate, w_up, w_down)


def _combine_kernel(cnt_ref, off_ref, dst_ref, tot_ref, x_ref, e_ref, rank_ref, gt_ref, ys_hbm,
                    gfin_ref, o_ref, ybuf, sems, *, final_norm):
    j = pl.program_id(0)
    nb = pl.num_programs(0)
    tm = x_ref.shape[0]
    slot = lax.rem(j, 2)

    def copies(tile, s, wait):
        if wait:
            _segment_wait(tot_ref, tile, ybuf.at[s], ys_hbm, sems.at[s], False)
        else:
            _segment_copies(cnt_ref, off_ref, dst_ref, tile * N_EXPERTS, ybuf.at[s], ys_hbm,
                            sems.at[s], False, False)

    @pl.when(j == 0)
    def _():
        ybuf[...] = jnp.zeros_like(ybuf)
        copies(0, 0, False)

    @pl.when(j + 1 < nb)
    def _():
        copies(j + 1, 1 - slot, False)

    copies(j, slot, True)
    lidx = _local_index(e_ref, rank_ref, off_ref, j)
    hi = (lidx >> 8).astype(F32)
    lo = (lidx & 255).astype(F32)
    rows8 = jnp.concatenate([hi, lo, gt_ref[...], jnp.zeros((2, tm), F32)], axis=0).astype(BF16)
    sel = jnp.where(_iota((SUBLANES, 6 * LANES), 0)
                    == (_iota((SUBLANES, 6 * LANES), 1) >> (LANES.bit_length() - 1)),
                    1.0, 0.0).astype(BF16)
    cols = _dot_tn(rows8, sel)
    col = lambda k: cols[:, k * LANES:(k + 1) * LANES]
    la = col(0) * 256.0 + col(2)
    lb = col(1) * 256.0 + col(3)
    ga, gb = col(4), col(5)
    lane = _iota((tm, LANES), 1).astype(F32)
    mix = jnp.concatenate(
        [jnp.where(lane + float(q * LANES) == la, ga, 0.0)
         + jnp.where(lane + float(q * LANES) == lb, gb, 0.0)
         for q in range(_seg_local_rows(tm) // LANES)], axis=1).astype(BF16)
    y_lo, y_hi = _unpack_pairs(ybuf[slot])
    y = x_ref[...] + jnp.concatenate([_dot(mix, y_lo), _dot(mix, y_hi)], axis=1)
    if final_norm:
        ms = jnp.mean(y * y, axis=-1, keepdims=True)
        y = y * lax.rsqrt(ms + RMS_EPS) * gfin_ref[...]
    o_ref[...] = y


def _combine(x2d, ys, e2, rank2, gates2, seg, g_final, final_norm, tm):
    n, d = x2d.shape
    return pl.pallas_call(
        functools.partial(_combine_kernel, final_norm=final_norm),
        out_shape=jax.ShapeDtypeStruct((n, d), F32),
        grid_spec=pltpu.PrefetchScalarGridSpec(
            num_scalar_prefetch=4,
            grid=(n // tm,),
            in_specs=[pl.BlockSpec((tm, d), lambda j, *_: (j, 0)),
                      pl.BlockSpec((2, tm), lambda j, *_: (0, j)),
                      pl.BlockSpec((2, tm), lambda j, *_: (0, j)),
                      pl.BlockSpec((2, tm), lambda j, *_: (0, j)),
                      pl.BlockSpec(memory_space=pl.ANY),
                      pl.BlockSpec((1, d), lambda j, *_: (0, 0))],
            out_specs=pl.BlockSpec((tm, d), lambda j, *_: (j, 0)),
            scratch_shapes=[pltpu.VMEM((2, _seg_local_rows(tm), d // 2), U32),
                            pltpu.SemaphoreType.DMA((2,))]),
        compiler_params=_cparams(("arbitrary",)),
        name="moe_combine",
    )(*seg, x2d, e2, rank2, gates2, ys, g_final.reshape(1, d))


def _route_plan(tile_cnt, n_tok, rows, tm):
    n_tiles = n_tok // tm
    seg_cnt = (tile_cnt + SUBLANES - 1) // SUBLANES * SUBLANES
    counts = jnp.sum(seg_cnt, axis=0)
    padded = (counts + rows - 1) // rows * rows
    pend = jnp.cumsum(padded)
    pstart = pend - padded
    seg_dst = pstart[None, :] + jnp.cumsum(seg_cnt, axis=0) - seg_cnt
    seg_off = jnp.cumsum(seg_cnt, axis=1) - seg_cnt
    n_blocks = -(-(2 * n_tok + n_tiles * N_EXPERTS * SUBLANES) // rows) + N_EXPERTS
    starts = jnp.arange(n_blocks, dtype=I32) * rows
    block_e = jnp.minimum(jnp.sum((starts[:, None] >= pend[None, :]).astype(I32), axis=1),
                          N_EXPERTS - 1)
    first = jnp.concatenate([jnp.ones((1,), I32),
                             (block_e[1:] != block_e[:-1]).astype(I32)])
    n_used = (pend[-1] // rows).astype(I32).reshape(1)
    seg = tuple(a.reshape(-1).astype(I32)
                for a in (seg_cnt, seg_off, seg_dst, jnp.sum(seg_cnt, axis=1)))
    tails = ((padded - counts).astype(I32), (pstart + counts).astype(I32))
    return seg, tails, block_e, first, n_used, n_blocks * rows


def _moe(x2d, route, g_ffn, w_gate, w_up, w_down, layer, g_final, final_norm, tm,
         rows=ROUTE_ROWS):
    e2, gates2, rank2, cnt = route
    seg, tails, block_e, first, n_used, n_rows = _route_plan(
        cnt[:, :, 0].astype(I32), e2.shape[1], rows, tm)
    xs = _dispatch(x2d, g_ffn, e2, rank2, seg, tails, n_used, n_rows, rows, tm)
    ys = _expert_ffn(xs, block_e, first, n_used, w_gate, w_up, w_down, layer, rows)
    return _combine(x2d, ys, e2, rank2, gates2, seg, g_final, final_norm, tm)


def kernel(x, norm_mix, norm_ffn, norm_final, ab_w_in, rw_mu, rw_w0, rw_w2, rw_a0, rw_a2, rw_g2, rw_k_k, rw_k_a, rw_r_k, rw_ln_w, rw_ln_b, lru_conv_w, lru_conv_b, lru_w_a, lru_b_a, lru_w_x, lru_b_x, lru_lambda, ab_w_out, c_w_in, c_lower_bound, c_norm_w, c_w_out, moe_w_group, moe_b_group, moe_w_expert, moe_b_expert, moe_w_gate, moe_w_up, moe_w_down):
    bsz, t, d = x.shape
    depth = norm_mix.shape[0]
    n = bsz * t
    lbs = jnp.cumsum(jax.nn.softmax(c_lower_bound.astype(F32), axis=0), axis=0)
    lbs = lbs - lbs[0]
    x2d = x.reshape(n, d)
    for layer in range(depth):
        j = layer // 2
        if layer % 2 == 0:
            rw_cols = rw_mu.shape[1]
            lru_cols = ab_w_in.shape[2] - rw_cols
            width = rw_w0.shape[1]
            p_a, p_b = _norm_matmul(x2d, norm_mix[layer], ab_w_in[j], (rw_cols, lru_cols),
                                    (F32, F32))
            prep = _rwkv_prep(p_a.reshape(bsz, t, rw_cols), rw_mu[j], rw_w0[j], rw_w2[j],
                              rw_a0[j], rw_a2[j], rw_g2[j], rw_k_k[j], rw_k_a[j],
                              rw_r_k[j].reshape(-1))
            ya = _rwkv_scan(*prep, rw_ln_w[j], rw_ln_b[j])
            yb = _lru(p_b.reshape(bsz, t, lru_cols), lru_conv_w[j], lru_conv_b[j], lru_w_a[j],
                      lru_b_a[j], lru_w_x[j], lru_b_x[j], lru_lambda[j])
            ys = [ya.reshape(n, width), yb.reshape(n, -1)]
            ws = [ab_w_out[j][:width], ab_w_out[j][width:]]
        else:
            hw = c_norm_w.shape[1]
            q, f, i_, g = _norm_matmul(x2d, norm_mix[layer], c_w_in[j], (hw,) * 4,
                                       (BF16, F32, BF16, BF16))
            shp = (bsz, t, hw)
            o = _gla(q.reshape(shp), f.reshape(shp), i_.reshape(shp), g.reshape(shp),
                     lbs[layer], c_norm_w[j])
            ys = [o.reshape(n, hw)]
            ws = [c_w_out[j]]
        tm = min(ROUTE_TILE, n)
        x2d = _proj_residual(x2d, ys, ws)
        route = _router(x2d, norm_ffn[layer], moe_w_group[layer], moe_b_group[layer],
                        moe_w_expert[layer], moe_b_expert[layer], tm)
        x2d = _moe(x2d, route, norm_ffn[layer], moe_w_gate, moe_w_up, moe_w_down, layer,
                   norm_final, layer == depth - 1, tm)
    return x2d.reshape(bsz, t, d)
```

```python
import functools

import jax
import jax.numpy as jnp
from jax import lax
from jax.experimental import pallas as pl
from jax.experimental.pallas import tpu as pltpu

F32 = jnp.float32
BF16 = jnp.bfloat16
I32 = jnp.int32
U32 = jnp.uint32

RMS_EPS = 1e-6
RWKV_GN_EPS = 64e-5
LRU_C = 8.0
CHUNK = 64
CHUNK_SHIFT = CHUNK.bit_length() - 1
HEAD64 = 64
HEAD_SHIFT = HEAD64.bit_length() - 1
LANES = 128
SUBLANES = 8
N_GROUPS = 4
EXPERTS_PER_GROUP = 8
N_EXPERTS = N_GROUPS * EXPERTS_PER_GROUP
ROUTE_ROWS = 512
ROUTE_TILE = 512
VMEM_LIMIT = 56 * 1024 * 1024


def _cparams(sem):
    return pltpu.CompilerParams(dimension_semantics=sem, vmem_limit_bytes=VMEM_LIMIT)


def _sigmoid(x):
    return 0.5 * jnp.tanh(0.5 * x) + 0.5


def _pack_pairs(x):
    w = x.shape[1] // 2
    lo = lax.bitcast_convert_type(x[:, :w].astype(BF16).astype(F32), U32)
    hi = lax.bitcast_convert_type(x[:, w:].astype(BF16).astype(F32), U32)
    return (lo >> 16) | hi


def _unpack_pairs(p):
    lo = lax.bitcast_convert_type(p << 16, F32).astype(BF16)
    hi = lax.bitcast_convert_type(p & jnp.uint32(0xFFFF0000), F32).astype(BF16)
    return lo, hi


def _softplus(x):
    return jnp.maximum(x, 0.0) + jnp.log(1.0 + jnp.exp(-jnp.abs(x)))


def _silu(x):
    return x * _sigmoid(x)


def _gelu_tanh(x):
    return 0.5 * x * (1.0 + jnp.tanh(0.7978845608028654 * (x + 0.044715 * x * x * x)))


def _dot(a, b):
    return jnp.dot(a, b, preferred_element_type=F32)


def _dot_nt(a, b):
    return lax.dot_general(a, b, (((1,), (1,)), ((), ())), preferred_element_type=F32)


def _dot_tn(a, b):
    return lax.dot_general(a, b, (((0,), (0,)), ((), ())), preferred_element_type=F32)


def _split2(x):
    hi = x.astype(BF16)
    lo = (x - hi.astype(F32)).astype(BF16)
    return hi, lo


def _dot_exact_rhs(x, m_bf16):
    hi, lo = _split2(x)
    return _dot(hi, m_bf16) + _dot(lo, m_bf16)


def _dot_exact_lhs(m_bf16, x):
    hi, lo = _split2(x)
    return _dot(m_bf16, hi) + _dot(m_bf16, lo)


def _dot_split(a, w_hi, w_lo):
    a_hi, a_lo = _split2(a)
    return _dot(a_hi, w_hi) + (_dot(a_lo, w_hi) + _dot(a_hi, w_lo))


def _hi_lo(w):
    hi = w.astype(BF16)
    return hi, (w - hi.astype(F32)).astype(BF16)


def _iota(shape, dim):
    return lax.broadcasted_iota(I32, shape, dim)


def _norm_matmul_kernel(x_ref, g_ref, w_ref, *o_refs, splits):
    x = x_ref[...]
    ms = jnp.mean(x * x, axis=-1, keepdims=True)
    y = (x * lax.rsqrt(ms + RMS_EPS) * g_ref[...]).astype(BF16)
    off = 0
    for o_ref, n in zip(o_refs, splits):
        o_ref[...] = _dot(y, w_ref[:, off:off + n]).astype(o_ref.dtype)
        off += n


def _norm_matmul(x2d, g, w, splits, out_dtypes, tm=1024):
    n, d = x2d.shape
    tm = min(tm, n)
    ncols = w.shape[1]
    assert sum(splits) == ncols and n % tm == 0 and len(out_dtypes) == len(splits)
    return pl.pallas_call(
        functools.partial(_norm_matmul_kernel, splits=splits),
        out_shape=[jax.ShapeDtypeStruct((n, s), dt) for s, dt in zip(splits, out_dtypes)],
        grid=(n // tm,),
        in_specs=[pl.BlockSpec((tm, d), lambda i: (i, 0)),
                  pl.BlockSpec((1, d), lambda i: (0, 0)),
                  pl.BlockSpec((d, ncols), lambda i: (0, 0))],
        out_specs=[pl.BlockSpec((tm, s), lambda i: (i, 0)) for s in splits],
        compiler_params=_cparams(("parallel",)),
        name="norm_matmul",
    )(x2d, g.reshape(1, d), w.astype(BF16))


def _proj_residual_kernel(*refs, n_in):
    x_ref = refs[0]
    y_refs = refs[1:1 + n_in]
    w_refs = refs[1 + n_in:1 + 2 * n_in]
    o_ref = refs[1 + 2 * n_in]
    acc = x_ref[...]
    for y_ref, w_ref in zip(y_refs, w_refs):
        acc = acc + _dot(y_ref[...].astype(BF16), w_ref[...])
    o_ref[...] = acc


def _proj_residual(x2d, ys, ws, tm=1024):
    n, d = x2d.shape
    tm = min(tm, n)
    n_in = len(ys)
    in_specs = [pl.BlockSpec((tm, d), lambda i: (i, 0))]
    in_specs += [pl.BlockSpec((tm, y.shape[1]), lambda i: (i, 0)) for y in ys]
    in_specs += [pl.BlockSpec(w.shape, lambda i: (0, 0)) for w in ws]
    return pl.pallas_call(
        functools.partial(_proj_residual_kernel, n_in=n_in),
        out_shape=jax.ShapeDtypeStruct((n, d), F32),
        grid=(n // tm,),
        in_specs=in_specs,
        out_specs=pl.BlockSpec((tm, d), lambda i: (i, 0)),
        compiler_params=_cparams(("parallel",)),
        name="proj_residual",
    )(x2d, *ys, *[w.astype(BF16) for w in ws])


def _router_kernel(x_ref, g_ref, w_ref, b_ref, upper_ref, e_o, gate_o, rank_o, cnt_o):
    _route_tile(x_ref[...], g_ref, w_ref, b_ref, upper_ref, e_o, gate_o, rank_o, cnt_o)


def _router(x2d, g_ffn, w_group, b_group, w_expert, b_expert, tm):
    n, d = x2d.shape
    wt = jnp.zeros((d, LANES), F32)
    wt = wt.at[:, 0:N_GROUPS].set(w_group).at[:, 8:8 + N_EXPERTS].set(w_expert)
    wt = jnp.stack(_hi_lo(wt))
    bt = jnp.zeros((LANES,), F32)
    bt = bt.at[0:N_GROUPS].set(b_group).at[8:8 + N_EXPERTS].set(b_expert)
    bt = jnp.broadcast_to(bt[:, None], (LANES, LANES))
    ti = jnp.arange(tm, dtype=I32)
    upper = (ti[:, None] < ti[None, :]).astype(BF16)
    tok2 = pl.BlockSpec((2, tm), lambda i: (0, i))
    return pl.pallas_call(
        _router_kernel,
        out_shape=[jax.ShapeDtypeStruct((2, n), I32),
                   jax.ShapeDtypeStruct((2, n), F32),
                   jax.ShapeDtypeStruct((2, n), I32),
                   jax.ShapeDtypeStruct((n // tm, N_EXPERTS, LANES), F32)],
        grid=(n // tm,),
        in_specs=[pl.BlockSpec((tm, d), lambda i: (i, 0)),
                  pl.BlockSpec((1, d), lambda i: (0, 0)),
                  pl.BlockSpec((2, d, LANES), lambda i: (0, 0, 0)),
                  pl.BlockSpec((LANES, LANES), lambda i: (0, 0)),
                  pl.BlockSpec((tm, tm), lambda i: (0, 0))],
        out_specs=[tok2, tok2, tok2,
                   pl.BlockSpec((1, N_EXPERTS, LANES), lambda i: (i, 0, 0))],
        compiler_params=_cparams(("parallel",)),
        name="moe_router",
    )(x2d, g_ffn.reshape(1, d), wt, bt, upper)


def _rwkv_prep_kernel(p_ref, mu_ref, w0_ref, a0_ref, kk_s_ref, ka_ref, rk_ref,
                      wcomb_ref, g2_ref, tril_ref,
                      rt_o, kkt_o, kh_o, bh_o, v_o, g_o, bonus_o, gc_o,
                      prev_ref, *, width):
    tm = p_ref.shape[1]

    @pl.when(pl.program_id(1) == 0)
    def _():
        prev_ref[...] = jnp.zeros_like(prev_ref)

    p = p_ref[0]
    rolled = pltpu.roll(p, 1, axis=0)
    prev = jnp.where(_iota(p.shape, 0) == 0, prev_ref[...], rolled)
    prev_ref[...] = p[tm - 1:tm, :]
    ps = p + (prev - p) * mu_ref[...]

    r = ps[:, 0:width]
    k = ps[:, width:2 * width]
    v = ps[:, 2 * width:3 * width]
    lowrank = ps[:, 3 * width:3 * width + LANES]
    gl = ps[:, 3 * width + LANES:3 * width + 2 * LANES]

    lane = _iota(lowrank.shape, 1)
    lr_in = jnp.where(lane < HEAD64, jnp.tanh(lowrank), lowrank)
    t12 = _dot_split(lr_in, wcomb_ref[0], wcomb_ref[1])
    wlog = -_softplus(-(w0_ref[...] + t12[:, :width])) - 0.5
    lw = -jnp.exp(wlog)
    a = _sigmoid(a0_ref[...] + t12[:, width:])
    g = _dot_split(_sigmoid(gl), g2_ref[0], g2_ref[1])

    seg = jnp.where((_iota((LANES, LANES), 0) >> HEAD_SHIFT)
                    == (_iota((LANES, LANES), 1) >> HEAD_SHIFT),
                    1.0, 0.0).astype(BF16)

    def head_sums(x):
        return jnp.concatenate([_dot_exact_rhs(x[:, q * LANES:(q + 1) * LANES], seg)
                                for q in range(width // LANES)], axis=1)

    kk = k * kk_s_ref[...]
    nrm = jnp.sqrt(head_sums(kk * kk))
    kk = kk / jnp.maximum(nrm, 1e-12)
    k2 = k * (1.0 + (a - 1.0) * ka_ref[...])
    bonus = head_sums(r * k2 * rk_ref[...]) * v

    b = kk * a

    n_chunk = tm // CHUNK
    cum = _dot_exact_lhs(tril_ref[...], lw)
    cum3 = cum.reshape(n_chunk, CHUNK, width)
    cend = cum3[:, CHUNK - 1:CHUNK, :]
    e_neg = jnp.exp(-cum)

    rt_o[0] = (r * jnp.exp(cum)).astype(BF16)
    kkt_o[0] = (kk * jnp.exp(cum - lw)).astype(BF16)
    kh_o[0] = (k2 * e_neg).astype(BF16)
    bh_o[0] = (b * e_neg).astype(BF16)
    v_o[0] = v.astype(BF16)
    g_o[0] = g.astype(BF16)
    bonus_o[0] = bonus.astype(BF16)
    gc_o[0] = jnp.exp(cend)


def _rwkv_prep(p_a, mu, w0, w2, a0, a2, g2, k_k, k_a, r_k, tm=512):
    b, t, cols = p_a.shape
    width = w0.shape[0]
    rank = w2.shape[0]
    assert rank == HEAD64 and a2.shape[0] == HEAD64 and g2.shape[0] == LANES
    zeros = jnp.zeros((rank, width), F32)
    wcomb = jnp.concatenate([jnp.concatenate([w2, zeros], 1),
                             jnp.concatenate([zeros, a2], 1)], 0)
    row = lambda a_: a_.reshape(1, -1)
    const = lambda shp: pl.BlockSpec(shp, lambda i, j: (0,) * len(shp))
    tok = pl.BlockSpec((1, tm, width), lambda i, j: (i, j, 0))
    n_chunk = tm // CHUNK
    ti = jnp.arange(tm, dtype=I32)
    tril_bd = (((ti[:, None] >> CHUNK_SHIFT) == (ti[None, :] >> CHUNK_SHIFT))
               & (ti[None, :] <= ti[:, None])).astype(BF16)
    return pl.pallas_call(
        functools.partial(_rwkv_prep_kernel, width=width),
        out_shape=([jax.ShapeDtypeStruct((b, t, width), BF16)] * 7
                   + [jax.ShapeDtypeStruct((b, t // CHUNK, 1, width), F32)]),
        grid=(b, t // tm),
        in_specs=[pl.BlockSpec((1, tm, cols), lambda i, j: (i, j, 0)),
                  const((1, cols)), const((1, width)), const((1, width)), const((1, width)),
                  const((1, width)), const((1, width)),
                  const((2, LANES, 2 * width)), const((2, LANES, width)), const((tm, tm))],
        out_specs=[tok] * 7 + [pl.BlockSpec((1, n_chunk, 1, width), lambda i, j: (i, j, 0, 0))],
        scratch_shapes=[pltpu.VMEM((1, cols), F32)],
        compiler_params=_cparams(("parallel", "arbitrary")),
        name="rwkv_prep",
    )(p_a, row(mu), row(w0), row(a0), row(k_k), row(k_a), row(r_k),
      jnp.stack(_hi_lo(wcomb)), jnp.stack(_hi_lo(g2)), tril_bd)


def _rwkv_scan_kernel(rt_ref, kkt_ref, kh_ref, bh_ref, v_ref, g_ref, bonus_ref,
                      gc_ref, lnw_ref, lnb_ref, o_ref, st_ref, *, n_batch, n_pairs):
    c = CHUNK

    @pl.when(pl.program_id(0) == 0)
    def _():
        st_ref[...] = jnp.zeros_like(st_ref)

    rr = _iota((LANES, LANES), 0)
    cc = _iota((LANES, LANES), 1)
    mask_bd = jnp.where((rr >> HEAD_SHIFT) == (cc >> HEAD_SHIFT), 1.0, 0.0)
    mask_bd16 = mask_bd.astype(BF16)
    t_i = _iota((c, LANES), 0)
    s_i = _iota((c, LANES), 1) & (c - 1)
    strict = jnp.where(s_i < t_i, 1.0, 0.0)
    incl = jnp.where(s_i <= t_i, 1.0, 0.0)
    eye_ss = jnp.where(s_i == t_i, 1.0, 0.0)

    def bd16(x):
        return jnp.concatenate([x, x], axis=0) * mask_bd16

    def b16(xs):
        return [x.astype(BF16) for x in xs]

    chains = [(bi, p) for bi in range(n_batch) for p in range(n_pairs)]
    n_ch = len(chains)
    idx = range(n_ch)

    def ld(ref):
        return [ref[bi, :, p * LANES:(p + 1) * LANES] for bi, p in chains]

    rt, kkt, kh, bh, v = (ld(ref) for ref in (rt_ref, kkt_ref, kh_ref, bh_ref, v_ref))

    lhs = [jnp.concatenate([kkt[i], rt[i]], axis=0) for i in idx]
    rhs = [jnp.concatenate([bd16(kh[i]), bd16(bh[i])], axis=0) for i in idx]
    gm = [_dot_nt(lhs[i], rhs[i]) for i in idx]
    l_kv = [gm[i][:c, :LANES] * strict for i in idx]
    xp = [-(gm[i][:c, LANES:] * strict) for i in idx]
    pm = [gm[i][c:, :LANES] * incl for i in idx]
    nqm = [-(gm[i][c:, LANES:] * incl) for i in idx]

    tinv = [eye_ss + xp[i] for i in idx]
    xp16 = b16(xp)
    xpb = [bd16(x) for x in xp16]
    for _ in range(c.bit_length() - 2):
        xp16 = b16([_dot(xp16[i], xpb[i]) for i in idx])
        xpb = [bd16(x) for x in xp16]
        t16 = b16(tinv)
        tinv = [tinv[i] + _dot(t16[i], xpb[i]) for i in idx]
    tinv16 = b16(tinv)

    v_bd = [bd16(x) for x in v]
    l_kv16 = b16(l_kv)
    w16 = b16([_dot(l_kv16[i], v_bd[i]) for i in idx])
    tkw = [_dot(tinv16[i], jnp.concatenate([bd16(kkt[i]), bd16(w16[i])], axis=1)) for i in idx]
    tk16 = b16([x[:, :LANES] for x in tkw])

    st = [st_ref[i] for i in idx]
    st16 = b16(st)
    u = [_dot_nt(tk16[i], st16[i]) + tkw[i][:, LANES:] for i in idx]
    u16 = b16(u)
    pq16 = [jnp.concatenate([pm[i], nqm[i]], axis=1).astype(BF16) for i in idx]
    vu = [jnp.concatenate([v_bd[i], bd16(u16[i])], axis=0) for i in idx]
    y = [_dot_nt(rt[i], st16[i]) + _dot(pq16[i], vu[i]) for i in idx]
    upd = [_dot_tn(jnp.concatenate([v[i], -u16[i]], axis=0),
                   jnp.concatenate([kh[i], bh[i]], axis=0)) for i in idx]
    for i, (bi, p) in enumerate(chains):
        st_ref[i] = (st[i] + upd[i] * mask_bd) * gc_ref[bi, 0, :, p * LANES:(p + 1) * LANES]

    ys = jnp.concatenate(y, axis=0)
    mean = _dot_exact_rhs(ys, mask_bd16) * (1.0 / HEAD64)
    d = ys - mean
    var = _dot_exact_rhs(d * d, mask_bd16) * (1.0 / HEAD64)
    dn = d * lax.rsqrt(var + RWKV_GN_EPS)
    for i, (bi, p) in enumerate(chains):
        sl = slice(p * LANES, (p + 1) * LANES)
        yn = dn[i * c:(i + 1) * c] * lnw_ref[:, sl] + lnb_ref[:, sl]
        o_ref[bi, :, sl] = ((yn + bonus_ref[bi, :, sl]) * g_ref[bi, :, sl]).astype(o_ref.dtype)


def _rwkv_scan(rt, kkt, kh, bh, v, g, bonus, gc, ln_w, ln_b):
    bsz, t, width = rt.shape
    n_pairs = width // LANES
    tok = pl.BlockSpec((bsz, CHUNK, width), lambda j: (0, j, 0))
    const = pl.BlockSpec((1, width), lambda j: (0, 0))
    return pl.pallas_call(
        functools.partial(_rwkv_scan_kernel, n_batch=bsz, n_pairs=n_pairs),
        out_shape=jax.ShapeDtypeStruct((bsz, t, width), BF16),
        grid=(t // CHUNK,),
        in_specs=[tok] * 7 + [pl.BlockSpec((bsz, 1, 1, width), lambda j: (0, j, 0, 0)),
                              const, const],
        out_specs=tok,
        scratch_shapes=[pltpu.VMEM((bsz * n_pairs, LANES, LANES), F32)],
        compiler_params=_cparams(("arbitrary",)),
        name="rwkv_scan",
    )(rt, kkt, kh, bh, v, g, bonus, gc, ln_w.reshape(1, -1), ln_b.reshape(1, -1))


def _lru_kernel(p_ref, cw_ref, cb_ref, wg_ref, ba_ref, bx_ref, lam_ref, o_ref,
                xcarry_ref, hcarry_ref, a_s, u_s, *, width):
    tm = p_ref.shape[1]

    @pl.when(pl.program_id(1) == 0)
    def _():
        xcarry_ref[...] = jnp.zeros_like(xcarry_ref)
        hcarry_ref[...] = jnp.zeros_like(hcarry_ref)

    gate = p_ref[0, :, 0:width]
    xb = p_ref[0, :, width:2 * width]
    carry8 = xcarry_ref[...]
    row8 = _iota((8, width), 0)

    def shifted(s):
        rolled = pltpu.roll(xb, s, axis=0)
        first = jnp.where(row8 < s, pltpu.roll(carry8, s, axis=0), rolled[0:8])
        return jnp.concatenate([first, rolled[8:]], axis=0)

    xc = (cw_ref[0:1, :] * shifted(3) + cw_ref[1:2, :] * shifted(2)
          + cw_ref[2:3, :] * shifted(1) + cw_ref[3:4, :] * xb + cb_ref[...])
    xcarry_ref[...] = xb[tm - 8:tm, :]

    gates = _dot(xc.astype(BF16), wg_ref[...])
    rg = _sigmoid(gates[:, :width] + ba_ref[...])
    ig = _sigmoid(gates[:, width:] + bx_ref[...])
    log_a = -LRU_C * rg * _softplus(-lam_ref[...])
    a = jnp.exp(log_a)
    a_s[...] = a
    u_s[...] = jnp.sqrt(1.0 - a * a) * ig * xc

    m1 = row8 >= 1
    m2 = row8 >= 2
    m4 = row8 >= 4

    def body(i, h):
        off = pl.multiple_of(i * 8, 8)
        a8 = a_s[pl.ds(off, 8), :]
        u8 = u_s[pl.ds(off, 8), :]
        for s, m in ((1, m1), (2, m2), (4, m4)):
            u_sh = jnp.where(m, pltpu.roll(u8, s, axis=0), 0.0)
            a_sh = jnp.where(m, pltpu.roll(a8, s, axis=0), 1.0)
            u8 = u8 + a8 * u_sh
            a8 = a8 * a_sh
        h8 = u8 + a8 * h
        u_s[pl.ds(off, 8), :] = h8
        return jnp.broadcast_to(h8[7:8, :], (8, width))

    h_last = lax.fori_loop(0, tm // 8, body, hcarry_ref[...])
    hcarry_ref[...] = h_last
    o_ref[0] = (u_s[...] * _gelu_tanh(gate)).astype(o_ref.dtype)


def _block_diag(w):
    nb, di, do = w.shape
    eye = jnp.eye(nb, dtype=w.dtype)
    return (eye[:, None, :, None] * w[:, :, None, :]).reshape(nb * di, nb * do)


def _lru(p_b, conv_w, conv_b, w_a, b_a, w_x, b_x, lam, tm=1024):
    b, t, cols = p_b.shape
    width = cols // 2
    wg = jnp.concatenate([_block_diag(w_a), _block_diag(w_x)], axis=1).astype(BF16)
    row = lambda a_: a_.reshape(1, -1)
    const = lambda shp: pl.BlockSpec(shp, lambda i, j: (0, 0))
    return pl.pallas_call(
        functools.partial(_lru_kernel, width=width),
        out_shape=jax.ShapeDtypeStruct((b, t, width), BF16),
        grid=(b, t // tm),
        in_specs=[pl.BlockSpec((1, tm, cols), lambda i, j: (i, j, 0)),
                  const(conv_w.shape), const((1, width)), const((width, 2 * width)),
                  const((1, width)), const((1, width)), const((1, width))],
        out_specs=pl.BlockSpec((1, tm, width), lambda i, j: (i, j, 0)),
        scratch_shapes=[pltpu.VMEM((8, width), F32), pltpu.VMEM((8, width), F32),
                        pltpu.VMEM((tm, width), F32), pltpu.VMEM((tm, width), F32)],
        compiler_params=_cparams(("parallel", "arbitrary")),
        name="rglru",
    )(p_b, conv_w, row(conv_b), wg, row(b_a), row(b_x), row(lam))


def _gla_kernel(q_ref, f_ref, i_ref, g_ref, lb_ref, nw_ref, o_ref, st_ref, *, n_batch, n_heads):
    c = CHUNK

    @pl.when(pl.program_id(0) == 0)
    def _():
        st_ref[...] = jnp.zeros_like(st_ref)

    trilf = jnp.where(_iota((c, c), 1) <= _iota((c, c), 0), 1.0, 0.0)
    tril16 = trilf.astype(BF16)
    mid = c // 2 - 1
    lb = lb_ref[...]

    qm, km, v16, gc, em, ecm = [], [], [], [], [], []
    for bi in range(n_batch):
        q = _silu(q_ref[bi].astype(F32))
        fg = lb + (1.0 - lb) * _sigmoid(f_ref[bi])
        k = 1.0 - fg
        cum = _dot_exact_lhs(tril16, jnp.log(fg))
        cum_c = cum[c - 1:c, :]
        cum_m = cum[mid:mid + 1, :]
        qm.append((q * jnp.exp(cum - cum_m)).astype(BF16))
        km.append((k * jnp.exp(cum_m - cum)).astype(BF16))
        v16.append(i_ref[bi].astype(BF16))
        gc.append(jnp.exp(cum_c))
        em.append(jnp.exp(cum_m))
        ecm.append(jnp.exp(cum_c - cum_m))

    chains = [(bi, h) for bi in range(n_batch) for h in range(n_heads)]
    idx = range(len(chains))

    def hs(xs):
        return [xs[bi][:, h * LANES:(h + 1) * LANES] for bi, h in chains]

    qm_c, km_c, v_c, gc_c, em_c, ecm_c = hs(qm), hs(km), hs(v16), hs(gc), hs(em), hs(ecm)
    scores = [(_dot_nt(qm_c[i], km_c[i]) * trilf).astype(BF16) for i in idx]
    st = [st_ref[i] for i in idx]
    st16 = [(st[i] * em_c[i]).astype(BF16) for i in idx]
    o = [_dot(scores[i], v_c[i]) + _dot_nt(qm_c[i], st16[i]) for i in idx]
    for i in idx:
        st_ref[i] = st[i] * gc_c[i] + _dot_tn(v_c[i], km_c[i]) * ecm_c[i]

    for i, (bi, h) in enumerate(chains):
        sl = slice(h * LANES, (h + 1) * LANES)
        on = o[i] * lax.rsqrt(jnp.mean(o[i] * o[i], axis=-1, keepdims=True) + RMS_EPS)
        gate = _silu(g_ref[bi, :, sl].astype(F32))
        o_ref[bi, :, sl] = (on * nw_ref[:, sl] * gate).astype(o_ref.dtype)


def _gla(q, f, i, g, lower_bound, norm_w):
    bsz, t, width = q.shape
    n_heads = width // LANES
    tok = pl.BlockSpec((bsz, CHUNK, width), lambda j: (0, j, 0))
    const = pl.BlockSpec((1, width), lambda j: (0, 0))
    return pl.pallas_call(
        functools.partial(_gla_kernel, n_batch=bsz, n_heads=n_heads),
        out_shape=jax.ShapeDtypeStruct((bsz, t, width), BF16),
        grid=(t // CHUNK,),
        in_specs=[tok] * 4 + [const, const],
        out_specs=tok,
        scratch_shapes=[pltpu.VMEM((bsz * n_heads, LANES, LANES), F32)],
        compiler_params=_cparams(("arbitrary",)),
        name="hgrn2_gla",
    )(q, f, i, g, lower_bound.reshape(1, -1), norm_w.reshape(1, -1))


def _route_tile(x, g_ref, w_ref, b_ref, upper_ref, e_o, gate_o, rank_o, cnt_o):
    ms = jnp.mean(x * x, axis=-1, keepdims=True)
    hn = x * lax.rsqrt(ms + RMS_EPS) * g_ref[...]
    lt = _dot_split(hn, w_ref[0], w_ref[1]).T + b_ref[:, 0:1]
    tm = x.shape[0]
    gl = lt[0:8, :]
    row8 = _iota((8, tm), 0)
    gl = jnp.where(row8 < N_GROUPS, gl, -jnp.inf)
    gmax = jnp.max(gl, axis=0, keepdims=True)
    g_sel = jnp.min(jnp.where(gl == gmax, row8, 8), axis=0, keepdims=True)
    g_gate = 1.0 / jnp.sum(jnp.exp(gl - gmax), axis=0, keepdims=True)

    el = jnp.zeros((EXPERTS_PER_GROUP, tm), F32)
    for gi in range(N_GROUPS):
        lo = 8 + gi * EXPERTS_PER_GROUP
        el = jnp.where(g_sel == gi, lt[lo:lo + EXPERTS_PER_GROUP, :], el)
    m1 = jnp.max(el, axis=0, keepdims=True)
    i1 = jnp.min(jnp.where(el == m1, row8, 8), axis=0, keepdims=True)
    el2 = jnp.where(row8 == i1, -jnp.inf, el)
    m2 = jnp.max(el2, axis=0, keepdims=True)
    i2 = jnp.min(jnp.where(el2 == m2, row8, 8), axis=0, keepdims=True)
    e2 = jnp.exp(m2 - m1)
    inv = 1.0 / (1.0 + e2)
    ea = g_sel * EXPERTS_PER_GROUP + i1
    eb = g_sel * EXPERTS_PER_GROUP + i2
    e_o[...] = jnp.concatenate([ea, eb], axis=0)
    gate_o[...] = jnp.concatenate([g_gate * inv, g_gate * e2 * inv], axis=0)

    erow = _iota((N_EXPERTS, tm), 0)
    oh_a = jnp.where(erow == ea, 1.0, 0.0)
    oh_b = jnp.where(erow == eb, 1.0, 0.0)
    pre_a = _dot(oh_a.astype(BF16), upper_ref[...])
    pre_b = _dot(oh_b.astype(BF16), upper_ref[...])
    cnt_a = jnp.sum(oh_a, axis=1, keepdims=True)
    cnt_b = jnp.sum(oh_b, axis=1, keepdims=True)
    rank_a = jnp.sum(oh_a * pre_a, axis=0, keepdims=True)
    rank_b = jnp.sum(oh_b * (pre_b + cnt_a), axis=0, keepdims=True)
    rank_o[...] = jnp.concatenate([rank_a, rank_b], axis=0).astype(I32)
    cnt_o[0] = jnp.broadcast_to(cnt_a + cnt_b, (N_EXPERTS, LANES))


def _seg_local_rows(tm):
    return 2 * tm + N_EXPERTS * SUBLANES


def _segment_copies(cnt_ref, off_ref, dst_ref, base, local, remote, sem, to_remote, wait):
    def body(e, carry):
        c = pl.multiple_of(cnt_ref[base + e], SUBLANES)

        @pl.when(c > 0)
        def _():
            off = 0 if off_ref is None else pl.multiple_of(off_ref[base + e], SUBLANES)
            loc = local.at[pl.ds(off, c), :]
            rem = remote.at[pl.ds(pl.multiple_of(dst_ref[base + e], SUBLANES), c), :]
            cp = (pltpu.make_async_copy(loc, rem, sem) if to_remote
                  else pltpu.make_async_copy(rem, loc, sem))
            if wait:
                cp.wait()
            else:
                cp.start()
        return carry
    lax.fori_loop(0, N_EXPERTS, body, 0, unroll=4)


def _local_index(e_ref, rank_ref, off_ref, tile):
    e = e_ref[...]
    lidx = rank_ref[...]
    for ex in range(N_EXPERTS):
        lidx = lidx + jnp.where(e == ex, off_ref[tile * N_EXPERTS + ex], 0)
    return lidx


def _segment_wait(tot_ref, tile, local, remote, sem, to_remote):
    tot = pl.multiple_of(tot_ref[tile], SUBLANES)

    @pl.when(tot > 0)
    def _():
        loc = local.at[pl.ds(0, tot), :]
        rem = remote.at[pl.ds(0, tot), :]
        cp = (pltpu.make_async_copy(loc, rem, sem) if to_remote
              else pltpu.make_async_copy(rem, loc, sem))
        cp.wait()


def _dispatch_kernel(cnt_ref, off_ref, dst_ref, tot_ref, tcnt_ref, tdst_ref, nused_ref,
                     x_ref, g_ref, e_ref, rank_ref, xs_out, sbuf, zbuf, sems, zsem):
    j = pl.program_id(0)
    nb = pl.num_programs(0)
    tm = x_ref.shape[0]
    slot = lax.rem(j, 2)

    def copies(tile, s, wait):
        if wait:
            _segment_wait(tot_ref, tile, sbuf.at[s], xs_out, sems.at[s], True)
        else:
            _segment_copies(cnt_ref, off_ref, dst_ref, tile * N_EXPERTS, sbuf.at[s], xs_out,
                            sems.at[s], True, False)

    @pl.when(j == 0)
    def _():
        zbuf[...] = jnp.zeros_like(zbuf)
        rows = zbuf.shape[0]
        n_blocks = xs_out.shape[0] // rows
        for wait in (False, True):
            _segment_copies(tcnt_ref, None, tdst_ref, 0, zbuf, xs_out, zsem, True, wait)

            def body(b, carry):
                start = pl.multiple_of(b * rows, SUBLANES)
                cp = pltpu.make_async_copy(zbuf, xs_out.at[pl.ds(start, rows), :], zsem)
                if wait:
                    cp.wait()
                else:
                    cp.start()
                return carry
            lax.fori_loop(nused_ref[0], n_blocks, body, 0)

    @pl.when(j >= 2)
    def _():
        copies(j - 2, slot, True)

    x = x_ref[...]
    ms = jnp.mean(x * x, axis=-1, keepdims=True)
    hn = (x * lax.rsqrt(ms + RMS_EPS) * g_ref[...]).astype(BF16)
    r = _iota((_seg_local_rows(tm), tm), 0)
    lidx = _local_index(e_ref, rank_ref, off_ref, j)
    perm = jnp.where((r == lidx[0:1, :]) | (r == lidx[1:2, :]), 1.0, 0.0).astype(BF16)
    sbuf[slot] = _pack_pairs(_dot(perm, hn))
    copies(j, slot, False)

    @pl.when(j == nb - 1)
    def _():
        copies(j, slot, True)

    @pl.when((j == nb - 1) & (nb >= 2))
    def _():
        copies(j - 1, 1 - slot, True)


def _dispatch(x2d, g, e2, rank2, seg, tails, n_used, n_rows, rows, tm):
    n, d = x2d.shape
    return pl.pallas_call(
        _dispatch_kernel,
        out_shape=jax.ShapeDtypeStruct((n_rows, d // 2), U32),
        grid_spec=pltpu.PrefetchScalarGridSpec(
            num_scalar_prefetch=7,
            grid=(n // tm,),
            in_specs=[pl.BlockSpec((tm, d), lambda j, *_: (j, 0)),
                      pl.BlockSpec((1, d), lambda j, *_: (0, 0)),
                      pl.BlockSpec((2, tm), lambda j, *_: (0, j)),
                      pl.BlockSpec((2, tm), lambda j, *_: (0, j))],
            out_specs=pl.BlockSpec(memory_space=pl.ANY),
            scratch_shapes=[pltpu.VMEM((2, _seg_local_rows(tm), d // 2), U32),
                            pltpu.VMEM((rows, d // 2), U32),
                            pltpu.SemaphoreType.DMA((2,)), pltpu.SemaphoreType.DMA(())]),
        compiler_params=_cparams(("arbitrary",)),
        name="moe_dispatch",
    )(*seg, *tails, n_used, x2d, g.reshape(1, d), e2, rank2)


def _expert_kernel(be_ref, first_ref, nused_ref, x_ref, wg_ref, wu_ref, wd_ref, o_ref,
                   wg16, wu16, wd16):
    j = pl.program_id(0)

    @pl.when(first_ref[j] == 1)
    def _():
        wg16[...] = wg_ref[0, 0].astype(BF16)
        wu16[...] = wu_ref[0, 0].astype(BF16)
        wd16[...] = wd_ref[0, 0].astype(BF16)

    @pl.when(j < nused_ref[0])
    def _():
        x_lo, x_hi = _unpack_pairs(x_ref[...])
        half = x_lo.shape[1]
        hg = _dot(x_lo, wg16[:half, :]) + _dot(x_hi, wg16[half:, :])
        hu = _dot(x_lo, wu16[:half, :]) + _dot(x_hi, wu16[half:, :])
        hid = (_silu(hg) * hu).astype(BF16)
        o_ref[...] = _pack_pairs(_dot(hid, wd16[...]))

    @pl.when(j >= nused_ref[0])
    def _():
        o_ref[...] = jnp.zeros_like(o_ref)


def _expert_ffn(xs, block_e, first_flag, n_used, w_gate, w_up, w_down, layer, rows):
    n_rows, dh = xs.shape
    n_blocks = n_rows // rows
    d, ff = w_gate.shape[-2:]
    assert d == 2 * dh
    wspec = lambda shp: pl.BlockSpec((1, 1) + shp, lambda j, be, fi, nu: (layer, be[j], 0, 0))
    return pl.pallas_call(
        _expert_kernel,
        out_shape=jax.ShapeDtypeStruct((n_rows, dh), U32),
        grid_spec=pltpu.PrefetchScalarGridSpec(
            num_scalar_prefetch=3,
            grid=(n_blocks,),
            in_specs=[pl.BlockSpec((rows, dh),
                                   lambda j, be, fi, nu: (jnp.minimum(j, nu[0] - 1), 0)),
                      wspec((d, ff)), wspec((d, ff)), wspec((ff, d))],
            out_specs=pl.BlockSpec((rows, dh), lambda j, be, fi, nu: (j, 0)),
            scratch_shapes=[pltpu.VMEM((d, ff), BF16), pltpu.VMEM((d, ff), BF16),
                            pltpu.VMEM((ff, d), BF16)]),
        compiler_params=_cparams(("arbitrary",)),
        name="moe_expert_ffn",
    )(block_e, first_flag, n_used, xs, w_gate, w_up, w_down)


def _combine_kernel(cnt_ref, off_ref, dst_ref, tot_ref, x_ref, e_ref, rank_ref, gt_ref, ys_hbm,
                    gfin_ref, o_ref, ybuf, sems, *, final_norm):
    j = pl.program_id(0)
    nb = pl.num_programs(0)
    tm = x_ref.shape[0]
    slot = lax.rem(j, 2)

    def copies(tile, s, wait):
        if wait:
            _segment_wait(tot_ref, tile, ybuf.at[s], ys_hbm, sems.at[s], False)
        else:
            _segment_copies(cnt_ref, off_ref, dst_ref, tile * N_EXPERTS, ybuf.at[s], ys_hbm,
                            sems.at[s], False, False)

    @pl.when(j == 0)
    def _():
        ybuf[...] = jnp.zeros_like(ybuf)
        copies(0, 0, False)

    @pl.when(j + 1 < nb)
    def _():
        copies(j + 1, 1 - slot, False)

    copies(j, slot, True)
    lidx = _local_index(e_ref, rank_ref, off_ref, j)
    hi = (lidx >> 8).astype(F32)
    lo = (lidx & 255).astype(F32)
    rows8 = jnp.concatenate([hi, lo, gt_ref[...], jnp.zeros((2, tm), F32)], axis=0).astype(BF16)
    sel = jnp.where(_iota((SUBLANES, 6 * LANES), 0)
                    == (_iota((SUBLANES, 6 * LANES), 1) >> (LANES.bit_length() - 1)),
                    1.0, 0.0).astype(BF16)
    cols = _dot_tn(rows8, sel)
    col = lambda k: cols[:, k * LANES:(k + 1) * LANES]
    la = col(0) * 256.0 + col(2)
    lb = col(1) * 256.0 + col(3)
    ga, gb = col(4), col(5)
    lane = _iota((tm, LANES), 1).astype(F32)
    mix = jnp.concatenate(
        [jnp.where(lane + float(q * LANES) == la, ga, 0.0)
         + jnp.where(lane + float(q * LANES) == lb, gb, 0.0)
         for q in range(_seg_local_rows(tm) // LANES)], axis=1).astype(BF16)
    y_lo, y_hi = _unpack_pairs(ybuf[slot])
    y = x_ref[...] + jnp.concatenate([_dot(mix, y_lo), _dot(mix, y_hi)], axis=1)
    if final_norm:
        ms = jnp.mean(y * y, axis=-1, keepdims=True)
        y = y * lax.rsqrt(ms + RMS_EPS) * gfin_ref[...]
    o_ref[...] = y


def _combine(x2d, ys, e2, rank2, gates2, seg, g_final, final_norm, tm):
    n, d = x2d.shape
    return pl.pallas_call(
        functools.partial(_combine_kernel, final_norm=final_norm),
        out_shape=jax.ShapeDtypeStruct((n, d), F32),
        grid_spec=pltpu.PrefetchScalarGridSpec(
            num_scalar_prefetch=4,
            grid=(n // tm,),
            in_specs=[pl.BlockSpec((tm, d), lambda j, *_: (j, 0)),
                      pl.BlockSpec((2, tm), lambda j, *_: (0, j)),
                      pl.BlockSpec((2, tm), lambda j, *_: (0, j)),
                      pl.BlockSpec((2, tm), lambda j, *_: (0, j)),
                      pl.BlockSpec(memory_space=pl.ANY),
                      pl.BlockSpec((1, d), lambda j, *_: (0, 0))],
            out_specs=pl.BlockSpec((tm, d), lambda j, *_: (j, 0)),
            scratch_shapes=[pltpu.VMEM((2, _seg_local_rows(tm), d // 2), U32),
                            pltpu.SemaphoreType.DMA((2,))]),
        compiler_params=_cparams(("arbitrary",)),
        name="moe_combine",
    )(*seg, x2d, e2, rank2, gates2, ys, g_final.reshape(1, d))


def _route_plan(tile_cnt, n_tok, rows, tm):
    n_tiles = n_tok // tm
    seg_cnt = (tile_cnt + SUBLANES - 1) // SUBLANES * SUBLANES
    counts = jnp.sum(seg_cnt, axis=0)
    padded = (counts + rows - 1) // rows * rows
    pend = jnp.cumsum(padded)
    pstart = pend - padded
    seg_dst = pstart[None, :] + jnp.cumsum(seg_cnt, axis=0) - seg_cnt
    seg_off = jnp.cumsum(seg_cnt, axis=1) - seg_cnt
    n_blocks = -(-(2 * n_tok + n_tiles * N_EXPERTS * SUBLANES) // rows) + N_EXPERTS
    starts = jnp.arange(n_blocks, dtype=I32) * rows
    block_e = jnp.minimum(jnp.sum((starts[:, None] >= pend[None, :]).astype(I32), axis=1),
                          N_EXPERTS - 1)
    first = jnp.concatenate([jnp.ones((1,), I32),
                             (block_e[1:] != block_e[:-1]).astype(I32)])
    n_used = (pend[-1] // rows).astype(I32).reshape(1)
    seg = tuple(a.reshape(-1).astype(I32)
                for a in (seg_cnt, seg_off, seg_dst, jnp.sum(seg_cnt, axis=1)))
    tails = ((padded - counts).astype(I32), (pstart + counts).astype(I32))
    return seg, tails, block_e, first, n_used, n_blocks * rows


def _moe(x2d, route, g_ffn, w_gate, w_up, w_down, layer, g_final, final_norm, tm,
         rows=ROUTE_ROWS):
    e2, gates2, rank2, cnt = route
    seg, tails, block_e, first, n_used, n_rows = _route_plan(
        cnt[:, :, 0].astype(I32), e2.shape[1], rows, tm)
    xs = _dispatch(x2d, g_ffn, e2, rank2, seg, tails, n_used, n_rows, rows, tm)
    ys = _expert_ffn(xs, block_e, first, n_used, w_gate, w_up, w_down, layer, rows)
    return _combine(x2d, ys, e2, rank2, gates2, seg, g_final, final_norm, tm)


def kernel(x, norm_mix, norm_ffn, norm_final, ab_w_in, rw_mu, rw_w0, rw_w2, rw_a0, rw_a2, rw_g2, rw_k_k, rw_k_a, rw_r_k, rw_ln_w, rw_ln_b, lru_conv_w, lru_conv_b, lru_w_a, lru_b_a, lru_w_x, lru_b_x, lru_lambda, ab_w_out, c_w_in, c_lower_bound, c_norm_w, c_w_out, moe_w_group, moe_b_group, moe_w_expert, moe_b_expert, moe_w_gate, moe_w_up, moe_w_down):
    bsz, t, d = x.shape
    depth = norm_mix.shape[0]
    n = bsz * t
    lbs = jnp.cumsum(jax.nn.softmax(c_lower_bound.astype(F32), axis=0), axis=0)
    lbs = lbs - lbs[0]
    x2d = x.reshape(n, d)
    for layer in range(depth):
        j = layer // 2
        if layer % 2 == 0:
            rw_cols = rw_mu.shape[1]
            lru_cols = ab_w_in.shape[2] - rw_cols
            width = rw_w0.shape[1]
            p_a, p_b = _norm_matmul(x2d, norm_mix[layer], ab_w_in[j], (rw_cols, lru_cols),
                                    (F32, F32))
            prep = _rwkv_prep(p_a.reshape(bsz, t, rw_cols), rw_mu[j], rw_w0[j], rw_w2[j],
                              rw_a0[j], rw_a2[j], rw_g2[j], rw_k_k[j], rw_k_a[j],
                              rw_r_k[j].reshape(-1))
            ya = _rwkv_scan(*prep, rw_ln_w[j], rw_ln_b[j])
            yb = _lru(p_b.reshape(bsz, t, lru_cols), lru_conv_w[j], lru_conv_b[j], lru_w_a[j],
                      lru_b_a[j], lru_w_x[j], lru_b_x[j], lru_lambda[j])
            ys = [ya.reshape(n, width), yb.reshape(n, -1)]
            ws = [ab_w_out[j][:width], ab_w_out[j][width:]]
        else:
            hw = c_norm_w.shape[1]
            q, f, i_, g = _norm_matmul(x2d, norm_mix[layer], c_w_in[j], (hw,) * 4,
                                       (BF16, F32, BF16, BF16))
            shp = (bsz, t, hw)
            o = _gla(q.reshape(shp), f.reshape(shp), i_.reshape(shp), g.reshape(shp),
                     lbs[layer], c_norm_w[j])
            ys = [o.reshape(n, hw)]
            ws = [c_w_out[j]]
        tm = min(ROUTE_TILE, n)
        x2d = _proj_residual(x2d, ys, ws)
        route = _router(x2d, norm_ffn[layer], moe_w_group[layer], moe_b_group[layer],
                        moe_w_expert[layer], moe_b_expert[layer], tm)
        x2d = _moe(x2d, route, norm_ffn[layer], moe_w_gate, moe_w_up, moe_w_down, layer,
                   norm_final, layer == depth - 1, tm)
    return x2d.reshape(bsz, t, d)
```

```python
import functools

import jax
import jax.numpy as jnp
from jax import lax
from jax.experimental import pallas as pl
from jax.experimental.pallas import tpu as pltpu

F32 = jnp.float32
BF16 = jnp.bfloat16
I32 = jnp.int32
U32 = jnp.uint32

RMS_EPS = 1e-6
RWKV_GN_EPS = 64e-5
LRU_C = 8.0
CHUNK = 64
CHUNK_SHIFT = CHUNK.bit_length() - 1
HEAD64 = 64
HEAD_SHIFT = HEAD64.bit_length() - 1
LANES = 128
SUBLANES = 8
N_GROUPS = 4
EXPERTS_PER_GROUP = 8
N_EXPERTS = N_GROUPS * EXPERTS_PER_GROUP
ROUTE_ROWS = 512
ROUTE_TILE = 512
VMEM_LIMIT = 56 * 1024 * 1024


def _cparams(sem):
    return pltpu.CompilerParams(dimension_semantics=sem, vmem_limit_bytes=VMEM_LIMIT)


def _sigmoid(x):
    return 0.5 * jnp.tanh(0.5 * x) + 0.5


def _pack_pairs(x):
    w = x.shape[1] // 2
    lo = lax.bitcast_convert_type(x[:, :w].astype(BF16).astype(F32), U32)
    hi = lax.bitcast_convert_type(x[:, w:].astype(BF16).astype(F32), U32)
    return (lo >> 16) | hi


def _unpack_pairs(p):
    lo = lax.bitcast_convert_type(p << 16, F32).astype(BF16)
    hi = lax.bitcast_convert_type(p & jnp.uint32(0xFFFF0000), F32).astype(BF16)
    return lo, hi


def _softplus(x):
    return jnp.maximum(x, 0.0) + jnp.log(1.0 + jnp.exp(-jnp.abs(x)))


def _silu(x):
    return x * _sigmoid(x)


def _gelu_tanh(x):
    return 0.5 * x * (1.0 + jnp.tanh(0.7978845608028654 * (x + 0.044715 * x * x * x)))


def _dot(a, b):
    return jnp.dot(a, b, preferred_element_type=F32)


def _dot_nt(a, b):
    return lax.dot_general(a, b, (((1,), (1,)), ((), ())), preferred_element_type=F32)


def _dot_tn(a, b):
    return lax.dot_general(a, b, (((0,), (0,)), ((), ())), preferred_element_type=F32)


def _split2(x):
    hi = x.astype(BF16)
    lo = (x - hi.astype(F32)).astype(BF16)
    return hi, lo


def _dot_exact_rhs(x, m_bf16):
    hi, lo = _split2(x)
    return _dot(hi, m_bf16) + _dot(lo, m_bf16)


def _dot_exact_lhs(m_bf16, x):
    hi, lo = _split2(x)
    return _dot(m_bf16, hi) + _dot(m_bf16, lo)


def _dot_split(a, w_hi, w_lo):
    a_hi, a_lo = _split2(a)
    return _dot(a_hi, w_hi) + (_dot(a_lo, w_hi) + _dot(a_hi, w_lo))


def _hi_lo(w):
    hi = w.astype(BF16)
    return hi, (w - hi.astype(F32)).astype(BF16)


def _iota(shape, dim):
    return lax.broadcasted_iota(I32, shape, dim)


def _norm_matmul_kernel(x_ref, g_ref, w_ref, *o_refs, splits):
    x = x_ref[...]
    ms = jnp.mean(x * x, axis=-1, keepdims=True)
    y = (x * lax.rsqrt(ms + RMS_EPS) * g_ref[...]).astype(BF16)
    off = 0
    for o_ref, n in zip(o_refs, splits):
        o_ref[...] = _dot(y, w_ref[:, off:off + n]).astype(o_ref.dtype)
        off += n


def _norm_matmul(x2d, g, w, splits, out_dtypes, tm=1024):
    n, d = x2d.shape
    tm = min(tm, n)
    ncols = w.shape[1]
    assert sum(splits) == ncols and n % tm == 0 and len(out_dtypes) == len(splits)
    return pl.pallas_call(
        functools.partial(_norm_matmul_kernel, splits=splits),
        out_shape=[jax.ShapeDtypeStruct((n, s), dt) for s, dt in zip(splits, out_dtypes)],
        grid=(n // tm,),
        in_specs=[pl.BlockSpec((tm, d), lambda i: (i, 0)),
                  pl.BlockSpec((1, d), lambda i: (0, 0)),
                  pl.BlockSpec((d, ncols), lambda i: (0, 0))],
        out_specs=[pl.BlockSpec((tm, s), lambda i: (i, 0)) for s in splits],
        compiler_params=_cparams(("parallel",)),
        name="norm_matmul",
    )(x2d, g.reshape(1, d), w.astype(BF16))


def _proj_residual_kernel(*refs, n_in):
    x_ref = refs[0]
    y_refs = refs[1:1 + n_in]
    w_refs = refs[1 + n_in:1 + 2 * n_in]
    o_ref = refs[1 + 2 * n_in]
    acc = x_ref[...]
    for y_ref, w_ref in zip(y_refs, w_refs):
        acc = acc + _dot(y_ref[...].astype(BF16), w_ref[...])
    o_ref[...] = acc


def _proj_residual(x2d, ys, ws, tm=1024):
    n, d = x2d.shape
    tm = min(tm, n)
    n_in = len(ys)
    in_specs = [pl.BlockSpec((tm, d), lambda i: (i, 0))]
    in_specs += [pl.BlockSpec((tm, y.shape[1]), lambda i: (i, 0)) for y in ys]
    in_specs += [pl.BlockSpec(w.shape, lambda i: (0, 0)) for w in ws]
    return pl.pallas_call(
        functools.partial(_proj_residual_kernel, n_in=n_in),
        out_shape=jax.ShapeDtypeStruct((n, d), F32),
        grid=(n // tm,),
        in_specs=in_specs,
        out_specs=pl.BlockSpec((tm, d), lambda i: (i, 0)),
        compiler_params=_cparams(("parallel",)),
        name="proj_residual",
    )(x2d, *ys, *[w.astype(BF16) for w in ws])


def _router_kernel(x_ref, g_ref, w_ref, b_ref, upper_ref, e_o, gate_o, rank_o, cnt_o):
    _route_tile(x_ref[...], g_ref, w_ref, b_ref, upper_ref, e_o, gate_o, rank_o, cnt_o)


def _router(x2d, g_ffn, w_group, b_group, w_expert, b_expert, tm):
    n, d = x2d.shape
    wt = jnp.zeros((d, LANES), F32)
    wt = wt.at[:, 0:N_GROUPS].set(w_group).at[:, 8:8 + N_EXPERTS].set(w_expert)
    wt = jnp.stack(_hi_lo(wt))
    bt = jnp.zeros((LANES,), F32)
    bt = bt.at[0:N_GROUPS].set(b_group).at[8:8 + N_EXPERTS].set(b_expert)
    bt = jnp.broadcast_to(bt[:, None], (LANES, LANES))
    ti = jnp.arange(tm, dtype=I32)
    upper = (ti[:, None] < ti[None, :]).astype(BF16)
    tok2 = pl.BlockSpec((2, tm), lambda i: (0, i))
    return pl.pallas_call(
        _router_kernel,
        out_shape=[jax.ShapeDtypeStruct((2, n), I32),
                   jax.ShapeDtypeStruct((2, n), F32),
                   jax.ShapeDtypeStruct((2, n), I32),
                   jax.ShapeDtypeStruct((n // tm, N_EXPERTS, LANES), F32)],
        grid=(n // tm,),
        in_specs=[pl.BlockSpec((tm, d), lambda i: (i, 0)),
                  pl.BlockSpec((1, d), lambda i: (0, 0)),
                  pl.BlockSpec((2, d, LANES), lambda i: (0, 0, 0)),
                  pl.BlockSpec((LANES, LANES), lambda i: (0, 0)),
                  pl.BlockSpec((tm, tm), lambda i: (0, 0))],
        out_specs=[tok2, tok2, tok2,
                   pl.BlockSpec((1, N_EXPERTS, LANES), lambda i: (i, 0, 0))],
        compiler_params=_cparams(("parallel",)),
        name="moe_router",
    )(x2d, g_ffn.reshape(1, d), wt, bt, upper)


def _rwkv_prep_kernel(p_ref, mu_ref, w0_ref, a0_ref, kk_s_ref, ka_ref, rk_ref,
                      wcomb_ref, g2_ref, tril_ref,
                      rt_o, kkt_o, kh_o, bh_o, v_o, g_o, bonus_o, gc_o,
                      prev_ref, *, width):
    tm = p_ref.shape[1]

    @pl.when(pl.program_id(1) == 0)
    def _():
        prev_ref[...] = jnp.zeros_like(prev_ref)

    p = p_ref[0]
    rolled = pltpu.roll(p, 1, axis=0)
    prev = jnp.where(_iota(p.shape, 0) == 0, prev_ref[...], rolled)
    prev_ref[...] = p[tm - 1:tm, :]
    ps = p + (prev - p) * mu_ref[...]

    r = ps[:, 0:width]
    k = ps[:, width:2 * width]
    v = ps[:, 2 * width:3 * width]
    lowrank = ps[:, 3 * width:3 * width + LANES]
    gl = ps[:, 3 * width + LANES:3 * width + 2 * LANES]

    lane = _iota(lowrank.shape, 1)
    lr_in = jnp.where(lane < HEAD64, jnp.tanh(lowrank), lowrank)
    t12 = _dot_split(lr_in, wcomb_ref[0], wcomb_ref[1])
    wlog = -_softplus(-(w0_ref[...] + t12[:, :width])) - 0.5
    lw = -jnp.exp(wlog)
    a = _sigmoid(a0_ref[...] + t12[:, width:])
    g = _dot_split(_sigmoid(gl), g2_ref[0], g2_ref[1])

    seg = jnp.where((_iota((LANES, LANES), 0) >> HEAD_SHIFT)
                    == (_iota((LANES, LANES), 1) >> HEAD_SHIFT),
                    1.0, 0.0).astype(BF16)

    def head_sums(x):
        return jnp.concatenate([_dot_exact_rhs(x[:, q * LANES:(q + 1) * LANES], seg)
                                for q in range(width // LANES)], axis=1)

    kk = k * kk_s_ref[...]
    nrm = jnp.sqrt(head_sums(kk * kk))
    kk = kk / jnp.maximum(nrm, 1e-12)
    k2 = k * (1.0 + (a - 1.0) * ka_ref[...])
    bonus = head_sums(r * k2 * rk_ref[...]) * v

    b = kk * a

    n_chunk = tm // CHUNK
    cum = _dot_exact_lhs(tril_ref[...], lw)
    cum3 = cum.reshape(n_chunk, CHUNK, width)
    cend = cum3[:, CHUNK - 1:CHUNK, :]
    e_neg = jnp.exp(-cum)

    rt_o[0] = (r * jnp.exp(cum)).astype(BF16)
    kkt_o[0] = (kk * jnp.exp(cum - lw)).astype(BF16)
    kh_o[0] = (k2 * e_neg).astype(BF16)
    bh_o[0] = (b * e_neg).astype(BF16)
    v_o[0] = v.astype(BF16)
    g_o[0] = g.astype(BF16)
    bonus_o[0] = bonus.astype(BF16)
    gc_o[0] = jnp.exp(cend)


def _rwkv_prep(p_a, mu, w0, w2, a0, a2, g2, k_k, k_a, r_k, tm=512):
    b, t, cols = p_a.shape
    width = w0.shape[0]
    rank = w2.shape[0]
    assert rank == HEAD64 and a2.shape[0] == HEAD64 and g2.shape[0] == LANES
    zeros = jnp.zeros((rank, width), F32)
    wcomb = jnp.concatenate([jnp.concatenate([w2, zeros], 1),
                             jnp.concatenate([zeros, a2], 1)], 0)
    row = lambda a_: a_.reshape(1, -1)
    const = lambda shp: pl.BlockSpec(shp, lambda i, j: (0,) * len(shp))
    tok = pl.BlockSpec((1, tm, width), lambda i, j: (i, j, 0))
    n_chunk = tm // CHUNK
    ti = jnp.arange(tm, dtype=I32)
    tril_bd = (((ti[:, None] >> CHUNK_SHIFT) == (ti[None, :] >> CHUNK_SHIFT))
               & (ti[None, :] <= ti[:, None])).astype(BF16)
    return pl.pallas_call(
        functools.partial(_rwkv_prep_kernel, width=width),
        out_shape=([jax.ShapeDtypeStruct((b, t, width), BF16)] * 7
                   + [jax.ShapeDtypeStruct((b, t // CHUNK, 1, width), F32)]),
        grid=(b, t // tm),
        in_specs=[pl.BlockSpec((1, tm, cols), lambda i, j: (i, j, 0)),
                  const((1, cols)), const((1, width)), const((1, width)), const((1, width)),
                  const((1, width)), const((1, width)),
                  const((2, LANES, 2 * width)), const((2, LANES, width)), const((tm, tm))],
        out_specs=[tok] * 7 + [pl.BlockSpec((1, n_chunk, 1, width), lambda i, j: (i, j, 0, 0))],
        scratch_shapes=[pltpu.VMEM((1, cols), F32)],
        compiler_params=_cparams(("parallel", "arbitrary")),
        name="rwkv_prep",
    )(p_a, row(mu), row(w0), row(a0), row(k_k), row(k_a), row(r_k),
      jnp.stack(_hi_lo(wcomb)), jnp.stack(_hi_lo(g2)), tril_bd)


def _rwkv_scan_kernel(rt_ref, kkt_ref, kh_ref, bh_ref, v_ref, g_ref, bonus_ref,
                      gc_ref, lnw_ref, lnb_ref, o_ref, st_ref, *, n_batch, n_pairs):
    c = CHUNK

    @pl.when(pl.program_id(0) == 0)
    def _():
        st_ref[...] = jnp.zeros_like(st_ref)

    rr = _iota((LANES, LANES), 0)
    cc = _iota((LANES, LANES), 1)
    mask_bd = jnp.where((rr >> HEAD_SHIFT) == (cc >> HEAD_SHIFT), 1.0, 0.0)
    mask_bd16 = mask_bd.astype(BF16)
    t_i = _iota((c, LANES), 0)
    s_i = _iota((c, LANES), 1) & (c - 1)
    strict = jnp.where(s_i < t_i, 1.0, 0.0)
    incl = jnp.where(s_i <= t_i, 1.0, 0.0)
    eye_ss = jnp.where(s_i == t_i, 1.0, 0.0)

    def bd16(x):
        return jnp.concatenate([x, x], axis=0) * mask_bd16

    def b16(xs):
        return [x.astype(BF16) for x in xs]

    chains = [(bi, p) for bi in range(n_batch) for p in range(n_pairs)]
    n_ch = len(chains)
    idx = range(n_ch)

    def ld(ref):
        return [ref[bi, :, p * LANES:(p + 1) * LANES] for bi, p in chains]

    rt, kkt, kh, bh, v = (ld(ref) for ref in (rt_ref, kkt_ref, kh_ref, bh_ref, v_ref))

    lhs = [jnp.concatenate([kkt[i], rt[i]], axis=0) for i in idx]
    rhs = [jnp.concatenate([bd16(kh[i]), bd16(bh[i])], axis=0) for i in idx]
    gm = [_dot_nt(lhs[i], rhs[i]) for i in idx]
    l_kv = [gm[i][:c, :LANES] * strict for i in idx]
    xp = [-(gm[i][:c, LANES:] * strict) for i in idx]
    pm = [gm[i][c:, :LANES] * incl for i in idx]
    nqm = [-(gm[i][c:, LANES:] * incl) for i in idx]

    tinv = [eye_ss + xp[i] for i in idx]
    xp16 = b16(xp)
    xpb = [bd16(x) for x in xp16]
    for _ in range(c.bit_length() - 2):
        xp16 = b16([_dot(xp16[i], xpb[i]) for i in idx])
        xpb = [bd16(x) for x in xp16]
        t16 = b16(tinv)
        tinv = [tinv[i] + _dot(t16[i], xpb[i]) for i in idx]
    tinv16 = b16(tinv)

    v_bd = [bd16(x) for x in v]
    l_kv16 = b16(l_kv)
    w16 = b16([_dot(l_kv16[i], v_bd[i]) for i in idx])
    tkw = [_dot(tinv16[i], jnp.concatenate([bd16(kkt[i]), bd16(w16[i])], axis=1)) for i in idx]
    tk16 = b16([x[:, :LANES] for x in tkw])

    st = [st_ref[i] for i in idx]
    st16 = b16(st)
    u = [_dot_nt(tk16[i], st16[i]) + tkw[i][:, LANES:] for i in idx]
    u16 = b16(u)
    pq16 = [jnp.concatenate([pm[i], nqm[i]], axis=1).astype(BF16) for i in idx]
    vu = [jnp.concatenate([v_bd[i], bd16(u16[i])], axis=0) for i in idx]
    y = [_dot_nt(rt[i], st16[i]) + _dot(pq16[i], vu[i]) for i in idx]
    upd = [_dot_tn(jnp.concatenate([v[i], -u16[i]], axis=0),
                   jnp.concatenate([kh[i], bh[i]], axis=0)) for i in idx]
    for i, (bi, p) in enumerate(chains):
        st_ref[i] = (st[i] + upd[i] * mask_bd) * gc_ref[bi, 0, :, p * LANES:(p + 1) * LANES]

    ys = jnp.concatenate(y, axis=0)
    mean = _dot_exact_rhs(ys, mask_bd16) * (1.0 / HEAD64)
    d = ys - mean
    var = _dot_exact_rhs(d * d, mask_bd16) * (1.0 / HEAD64)
    dn = d * lax.rsqrt(var + RWKV_GN_EPS)
    for i, (bi, p) in enumerate(chains):
        sl = slice(p * LANES, (p + 1) * LANES)
        yn = dn[i * c:(i + 1) * c] * lnw_ref[:, sl] + lnb_ref[:, sl]
        o_ref[bi, :, sl] = ((yn + bonus_ref[bi, :, sl]) * g_ref[bi, :, sl]).astype(o_ref.dtype)


def _rwkv_scan(rt, kkt, kh, bh, v, g, bonus, gc, ln_w, ln_b):
    bsz, t, width = rt.shape
    n_pairs = width // LANES
    tok = pl.BlockSpec((bsz, CHUNK, width), lambda j: (0, j, 0))
    const = pl.BlockSpec((1, width), lambda j: (0, 0))
    return pl.pallas_call(
        functools.partial(_rwkv_scan_kernel, n_batch=bsz, n_pairs=n_pairs),
        out_shape=jax.ShapeDtypeStruct((bsz, t, width), BF16),
        grid=(t // CHUNK,),
        in_specs=[tok] * 7 + [pl.BlockSpec((bsz, 1, 1, width), lambda j: (0, j, 0, 0)),
                              const, const],
        out_specs=tok,
        scratch_shapes=[pltpu.VMEM((bsz * n_pairs, LANES, LANES), F32)],
        compiler_params=_cparams(("arbitrary",)),
        name="rwkv_scan",
    )(rt, kkt, kh, bh, v, g, bonus, gc, ln_w.reshape(1, -1), ln_b.reshape(1, -1))


def _lru_kernel(p_ref, cw_ref, cb_ref, wg_ref, ba_ref, bx_ref, lam_ref, o_ref,
                xcarry_ref, hcarry_ref, a_s, u_s, *, width):
    tm = p_ref.shape[1]

    @pl.when(pl.program_id(1) == 0)
    def _():
        xcarry_ref[...] = jnp.zeros_like(xcarry_ref)
        hcarry_ref[...] = jnp.zeros_like(hcarry_ref)

    gate = p_ref[0, :, 0:width]
    xb = p_ref[0, :, width:2 * width]
    carry8 = xcarry_ref[...]
    row8 = _iota((8, width), 0)

    def shifted(s):
        rolled = pltpu.roll(xb, s, axis=0)
        first = jnp.where(row8 < s, pltpu.roll(carry8, s, axis=0), rolled[0:8])
        return jnp.concatenate([first, rolled[8:]], axis=0)

    xc = (cw_ref[0:1, :] * shifted(3) + cw_ref[1:2, :] * shifted(2)
          + cw_ref[2:3, :] * shifted(1) + cw_ref[3:4, :] * xb + cb_ref[...])
    xcarry_ref[...] = xb[tm - 8:tm, :]

    gates = _dot(xc.astype(BF16), wg_ref[...])
    rg = _sigmoid(gates[:, :width] + ba_ref[...])
    ig = _sigmoid(gates[:, width:] + bx_ref[...])
    log_a = -LRU_C * rg * _softplus(-lam_ref[...])
    a = jnp.exp(log_a)
    a_s[...] = a
    u_s[...] = jnp.sqrt(1.0 - a * a) * ig * xc

    m1 = row8 >= 1
    m2 = row8 >= 2
    m4 = row8 >= 4

    def body(i, h):
        off = pl.multiple_of(i * 8, 8)
        a8 = a_s[pl.ds(off, 8), :]
        u8 = u_s[pl.ds(off, 8), :]
        for s, m in ((1, m1), (2, m2), (4, m4)):
            u_sh = jnp.where(m, pltpu.roll(u8, s, axis=0), 0.0)
            a_sh = jnp.where(m, pltpu.roll(a8, s, axis=0), 1.0)
            u8 = u8 + a8 * u_sh
            a8 = a8 * a_sh
        h8 = u8 + a8 * h
        u_s[pl.ds(off, 8), :] = h8
        return jnp.broadcast_to(h8[7:8, :], (8, width))

    h_last = lax.fori_loop(0, tm // 8, body, hcarry_ref[...])
    hcarry_ref[...] = h_last
    o_ref[0] = (u_s[...] * _gelu_tanh(gate)).astype(o_ref.dtype)


def _block_diag(w):
    nb, di, do = w.shape
    eye = jnp.eye(nb, dtype=w.dtype)
    return (eye[:, None, :, None] * w[:, :, None, :]).reshape(nb * di, nb * do)


def _lru(p_b, conv_w, conv_b, w_a, b_a, w_x, b_x, lam, tm=1024):
    b, t, cols = p_b.shape
    width = cols // 2
    wg = jnp.concatenate([_block_diag(w_a), _block_diag(w_x)], axis=1).astype(BF16)
    row = lambda a_: a_.reshape(1, -1)
    const = lambda shp: pl.BlockSpec(shp, lambda i, j: (0, 0))
    return pl.pallas_call(
        functools.partial(_lru_kernel, width=width),
        out_shape=jax.ShapeDtypeStruct((b, t, width), BF16),
        grid=(b, t // tm),
        in_specs=[pl.BlockSpec((1, tm, cols), lambda i, j: (i, j, 0)),
                  const(conv_w.shape), const((1, width)), const((width, 2 * width)),
                  const((1, width)), const((1, width)), const((1, width))],
        out_specs=pl.BlockSpec((1, tm, width), lambda i, j: (i, j, 0)),
        scratch_shapes=[pltpu.VMEM((8, width), F32), pltpu.VMEM((8, width), F32),
                        pltpu.VMEM((tm, width), F32), pltpu.VMEM((tm, width), F32)],
        compiler_params=_cparams(("parallel", "arbitrary")),
        name="rglru",
    )(p_b, conv_w, row(conv_b), wg, row(b_a), row(b_x), row(lam))


def _gla_kernel(q_ref, f_ref, i_ref, g_ref, lb_ref, nw_ref, o_ref, st_ref, *, n_batch, n_heads):
    c = CHUNK

    @pl.when(pl.program_id(0) == 0)
    def _():
        st_ref[...] = jnp.zeros_like(st_ref)

    trilf = jnp.where(_iota((c, c), 1) <= _iota((c, c), 0), 1.0, 0.0)
    tril16 = trilf.astype(BF16)
    mid = c // 2 - 1
    lb = lb_ref[...]

    qm, km, v16, gc, em, ecm = [], [], [], [], [], []
    for bi in range(n_batch):
        q = _silu(q_ref[bi].astype(F32))
        fg = lb + (1.0 - lb) * _sigmoid(f_ref[bi])
        k = 1.0 - fg
        cum = _dot_exact_lhs(tril16, jnp.log(fg))
        cum_c = cum[c - 1:c, :]
        cum_m = cum[mid:mid + 1, :]
        qm.append((q * jnp.exp(cum - cum_m)).astype(BF16))
        km.append((k * jnp.exp(cum_m - cum)).astype(BF16))
        v16.append(i_ref[bi].astype(BF16))
        gc.append(jnp.exp(cum_c))
        em.append(jnp.exp(cum_m))
        ecm.append(jnp.exp(cum_c - cum_m))

    chains = [(bi, h) for bi in range(n_batch) for h in range(n_heads)]
    idx = range(len(chains))

    def hs(xs):
        return [xs[bi][:, h * LANES:(h + 1) * LANES] for bi, h in chains]

    qm_c, km_c, v_c, gc_c, em_c, ecm_c = hs(qm), hs(km), hs(v16), hs(gc), hs(em), hs(ecm)
    scores = [(_dot_nt(qm_c[i], km_c[i]) * trilf).astype(BF16) for i in idx]
    st = [st_ref[i] for i in idx]
    st16 = [(st[i] * em_c[i]).astype(BF16) for i in idx]
    o = [_dot(scores[i], v_c[i]) + _dot_nt(qm_c[i], st16[i]) for i in idx]
    for i in idx:
        st_ref[i] = st[i] * gc_c[i] + _dot_tn(v_c[i], km_c[i]) * ecm_c[i]

    for i, (bi, h) in enumerate(chains):
        sl = slice(h * LANES, (h + 1) * LANES)
        on = o[i] * lax.rsqrt(jnp.mean(o[i] * o[i], axis=-1, keepdims=True) + RMS_EPS)
        gate = _silu(g_ref[bi, :, sl].astype(F32))
        o_ref[bi, :, sl] = (on * nw_ref[:, sl] * gate).astype(o_ref.dtype)


def _gla(q, f, i, g, lower_bound, norm_w):
    bsz, t, width = q.shape
    n_heads = width // LANES
    tok = pl.BlockSpec((bsz, CHUNK, width), lambda j: (0, j, 0))
    const = pl.BlockSpec((1, width), lambda j: (0, 0))
    return pl.pallas_call(
        functools.partial(_gla_kernel, n_batch=bsz, n_heads=n_heads),
        out_shape=jax.ShapeDtypeStruct((bsz, t, width), BF16),
        grid=(t // CHUNK,),
        in_specs=[tok] * 4 + [const, const],
        out_specs=tok,
        scratch_shapes=[pltpu.VMEM((bsz * n_heads, LANES, LANES), F32)],
        compiler_params=_cparams(("arbitrary",)),
        name="hgrn2_gla",
    )(q, f, i, g, lower_bound.reshape(1, -1), norm_w.reshape(1, -1))


def _route_tile(x, g_ref, w_ref, b_ref, upper_ref, e_o, gate_o, rank_o, cnt_o):
    ms = jnp.mean(x * x, axis=-1, keepdims=True)
    hn = x * lax.rsqrt(ms + RMS_EPS) * g_ref[...]
    lt = _dot_split(hn, w_ref[0], w_ref[1]).T + b_ref[:, 0:1]
    tm = x.shape[0]
    gl = lt[0:8, :]
    row8 = _iota((8, tm), 0)
    gl = jnp.where(row8 < N_GROUPS, gl, -jnp.inf)
    gmax = jnp.max(gl, axis=0, keepdims=True)
    g_sel = jnp.min(jnp.where(gl == gmax, row8, 8), axis=0, keepdims=True)
    g_gate = 1.0 / jnp.sum(jnp.exp(gl - gmax), axis=0, keepdims=True)

    el = jnp.zeros((EXPERTS_PER_GROUP, tm), F32)
    for gi in range(N_GROUPS):
        lo = 8 + gi * EXPERTS_PER_GROUP
        el = jnp.where(g_sel == gi, lt[lo:lo + EXPERTS_PER_GROUP, :], el)
    m1 = jnp.max(el, axis=0, keepdims=True)
    i1 = jnp.min(jnp.where(el == m1, row8, 8), axis=0, keepdims=True)
    el2 = jnp.where(row8 == i1, -jnp.inf, el)
    m2 = jnp.max(el2, axis=0, keepdims=True)
    i2 = jnp.min(jnp.where(el2 == m2, row8, 8), axis=0, keepdims=True)
    e2 = jnp.exp(m2 - m1)
    inv = 1.0 / (1.0 + e2)
    ea = g_sel * EXPERTS_PER_GROUP + i1
    eb = g_sel * EXPERTS_PER_GROUP + i2
    e_o[...] = jnp.concatenate([ea, eb], axis=0)
    gate_o[...] = jnp.concatenate([g_gate * inv, g_gate * e2 * inv], axis=0)

    erow = _iota((N_EXPERTS, tm), 0)
    oh_a = jnp.where(erow == ea, 1.0, 0.0)
    oh_b = jnp.where(erow == eb, 1.0, 0.0)
    pre_a = _dot(oh_a.astype(BF16), upper_ref[...])
    pre_b = _dot(oh_b.astype(BF16), upper_ref[...])
    cnt_a = jnp.sum(oh_a, axis=1, keepdims=True)
    cnt_b = jnp.sum(oh_b, axis=1, keepdims=True)
    rank_a = jnp.sum(oh_a * pre_a, axis=0, keepdims=True)
    rank_b = jnp.sum(oh_b * (pre_b + cnt_a), axis=0, keepdims=True)
    rank_o[...] = jnp.concatenate([rank_a, rank_b], axis=0).astype(I32)
    cnt_o[0] = jnp.broadcast_to(cnt_a + cnt_b, (N_EXPERTS, LANES))


def _seg_local_rows(tm):
    return 2 * tm + N_EXPERTS * SUBLANES


def _segment_copies(cnt_ref, off_ref, dst_ref, base, local, remote, sem, to_remote, wait):
    def body(e, carry):
        c = pl.multiple_of(cnt_ref[base + e], SUBLANES)

        @pl.when(c > 0)
        def _():
            off = 0 if off_ref is None else pl.multiple_of(off_ref[base + e], SUBLANES)
            loc = local.at[pl.ds(off, c), :]
            rem = remote.at[pl.ds(pl.multiple_of(dst_ref[base + e], SUBLANES), c), :]
            cp = (pltpu.make_async_copy(loc, rem, sem) if to_remote
                  else pltpu.make_async_copy(rem, loc, sem))
            if wait:
                cp.wait()
            else:
                cp.start()
        return carry
    lax.fori_loop(0, N_EXPERTS, body, 0, unroll=4)


def _local_index(e_ref, rank_ref, off_ref, tile):
    e = e_ref[...]
    lidx = rank_ref[...]
    for ex in range(N_EXPERTS):
        lidx = lidx + jnp.where(e == ex, off_ref[tile * N_EXPERTS + ex], 0)
    return lidx


def _segment_wait(tot_ref, tile, local, remote, sem, to_remote):
    tot = pl.multiple_of(tot_ref[tile], SUBLANES)

    @pl.when(tot > 0)
    def _():
        loc = local.at[pl.ds(0, tot), :]
        rem = remote.at[pl.ds(0, tot), :]
        cp = (pltpu.make_async_copy(loc, rem, sem) if to_remote
              else pltpu.make_async_copy(rem, loc, sem))
        cp.wait()


def _dispatch_kernel(cnt_ref, off_ref, dst_ref, tot_ref, tcnt_ref, tdst_ref, nused_ref,
                     x_ref, g_ref, e_ref, rank_ref, xs_out, sbuf, zbuf, sems, zsem):
    j = pl.program_id(0)
    nb = pl.num_programs(0)
    tm = x_ref.shape[0]
    slot = lax.rem(j, 2)

    def copies(tile, s, wait):
        if wait:
            _segment_wait(tot_ref, tile, sbuf.at[s], xs_out, sems.at[s], True)
        else:
            _segment_copies(cnt_ref, off_ref, dst_ref, tile * N_EXPERTS, sbuf.at[s], xs_out,
                            sems.at[s], True, False)

    @pl.when(j == 0)
    def _():
        zbuf[...] = jnp.zeros_like(zbuf)
        rows = zbuf.shape[0]
        n_blocks = xs_out.shape[0] // rows
        for wait in (False, True):
            _segment_copies(tcnt_ref, None, tdst_ref, 0, zbuf, xs_out, zsem, True, wait)

            def body(b, carry):
                start = pl.multiple_of(b * rows, SUBLANES)
                cp = pltpu.make_async_copy(zbuf, xs_out.at[pl.ds(start, rows), :], zsem)
                if wait:
                    cp.wait()
                else:
                    cp.start()
                return carry
            lax.fori_loop(nused_ref[0], n_blocks, body, 0)

    @pl.when(j >= 2)
    def _():
        copies(j - 2, slot, True)

    x = x_ref[...]
    ms = jnp.mean(x * x, axis=-1, keepdims=True)
    hn = (x * lax.rsqrt(ms + RMS_EPS) * g_ref[...]).astype(BF16)
    r = _iota((_seg_local_rows(tm), tm), 0)
    lidx = _local_index(e_ref, rank_ref, off_ref, j)
    perm = jnp.where((r == lidx[0:1, :]) | (r == lidx[1:2, :]), 1.0, 0.0).astype(BF16)
    sbuf[slot] = _pack_pairs(_dot(perm, hn))
    copies(j, slot, False)

    @pl.when(j == nb - 1)
    def _():
        copies(j, slot, True)

    @pl.when((j == nb - 1) & (nb >= 2))
    def _():
        copies(j - 1, 1 - slot, True)


def _dispatch(x2d, g, e2, rank2, seg, tails, n_used, n_rows, rows, tm):
    n, d = x2d.shape
    return pl.pallas_call(
        _dispatch_kernel,
        out_shape=jax.ShapeDtypeStruct((n_rows, d // 2), U32),
        grid_spec=pltpu.PrefetchScalarGridSpec(
            num_scalar_prefetch=7,
            grid=(n // tm,),
            in_specs=[pl.BlockSpec((tm, d), lambda j, *_: (j, 0)),
                      pl.BlockSpec((1, d), lambda j, *_: (0, 0)),
                      pl.BlockSpec((2, tm), lambda j, *_: (0, j)),
                      pl.BlockSpec((2, tm), lambda j, *_: (0, j))],
            out_specs=pl.BlockSpec(memory_space=pl.ANY),
            scratch_shapes=[pltpu.VMEM((2, _seg_local_rows(tm), d // 2), U32),
                            pltpu.VMEM((rows, d // 2), U32),
                            pltpu.SemaphoreType.DMA((2,)), pltpu.SemaphoreType.DMA(())]),
        compiler_params=_cparams(("arbitrary",)),
        name="moe_dispatch",
    )(*seg, *tails, n_used, x2d, g.reshape(1, d), e2, rank2)


def _expert_kernel(be_ref, first_ref, nused_ref, slot_ref, next_ref, x_ref, wg_hbm, wu_hbm,
                   wd_hbm, o_ref, wg32, wu32, wd32, wg16, wu16, wd16, wsem, *, layer):
    j = pl.program_id(0)

    def weight_copies(e, s):
        return [pltpu.make_async_copy(hbm.at[layer, e], buf.at[s], wsem.at[s])
                for hbm, buf in ((wg_hbm, wg32), (wu_hbm, wu32), (wd_hbm, wd32))]

    @pl.when((first_ref[j] == 1) & (j < nused_ref[0]))
    def _():
        s = slot_ref[j]

        @pl.when(j == 0)
        def _():
            for cp in weight_copies(be_ref[0], 0):
                cp.start()

        for cp in weight_copies(be_ref[j], s):
            cp.wait()

        @pl.when(next_ref[j] >= 0)
        def _():
            for cp in weight_copies(next_ref[j], 1 - s):
                cp.start()

        wg16[...] = wg32[s].astype(BF16)
        wu16[...] = wu32[s].astype(BF16)
        wd16[...] = wd32[s].astype(BF16)

    @pl.when(j < nused_ref[0])
    def _():
        x_lo, x_hi = _unpack_pairs(x_ref[...])
        half = x_lo.shape[1]
        hg = _dot(x_lo, wg16[:half, :]) + _dot(x_hi, wg16[half:, :])
        hu = _dot(x_lo, wu16[:half, :]) + _dot(x_hi, wu16[half:, :])
        hid = (_silu(hg) * hu).astype(BF16)
        o_ref[...] = _pack_pairs(_dot(hid, wd16[...]))

    @pl.when(j >= nused_ref[0])
    def _():
        o_ref[...] = jnp.zeros_like(o_ref)


def _expert_ffn(xs, blocks, w_gate, w_up, w_down, layer, rows):
    n_rows, dh = xs.shape
    n_blocks = n_rows // rows
    d, ff = w_gate.shape[-2:]
    assert d == 2 * dh
    any_spec = pl.BlockSpec(memory_space=pl.ANY)
    return pl.pallas_call(
        functools.partial(_expert_kernel, layer=layer),
        out_shape=jax.ShapeDtypeStruct((n_rows, dh), U32),
        grid_spec=pltpu.PrefetchScalarGridSpec(
            num_scalar_prefetch=5,
            grid=(n_blocks,),
            in_specs=[pl.BlockSpec((rows, dh),
                                   lambda j, be, fi, nu, *_: (jnp.minimum(j, nu[0] - 1), 0)),
                      any_spec, any_spec, any_spec],
            out_specs=pl.BlockSpec((rows, dh), lambda j, *_: (j, 0)),
            scratch_shapes=[pltpu.VMEM((2, d, ff), F32), pltpu.VMEM((2, d, ff), F32),
                            pltpu.VMEM((2, ff, d), F32),
                            pltpu.VMEM((d, ff), BF16), pltpu.VMEM((d, ff), BF16),
                            pltpu.VMEM((ff, d), BF16), pltpu.SemaphoreType.DMA((2,))]),
        compiler_params=_cparams(("arbitrary",)),
        name="moe_expert_ffn",
    )(*blocks, xs, w_gate, w_up, w_down)


def _combine_kernel(cnt_ref, off_ref, dst_ref, tot_ref, x_ref, e_ref, rank_ref, gt_ref, ys_hbm,
                    gfin_ref, o_ref, ybuf, sems, *, final_norm):
    j = pl.program_id(0)
    nb = pl.num_programs(0)
    tm = x_ref.shape[0]
    slot = lax.rem(j, 2)

    def copies(tile, s, wait):
        if wait:
            _segment_wait(tot_ref, tile, ybuf.at[s], ys_hbm, sems.at[s], False)
        else:
            _segment_copies(cnt_ref, off_ref, dst_ref, tile * N_EXPERTS, ybuf.at[s], ys_hbm,
                            sems.at[s], False, False)

    @pl.when(j == 0)
    def _():
        ybuf[...] = jnp.zeros_like(ybuf)
        copies(0, 0, False)

    @pl.when(j + 1 < nb)
    def _():
        copies(j + 1, 1 - slot, False)

    copies(j, slot, True)
    lidx = _local_index(e_ref, rank_ref, off_ref, j)
    hi = (lidx >> 8).astype(F32)
    lo = (lidx & 255).astype(F32)
    rows8 = jnp.concatenate([hi, lo, gt_ref[...], jnp.zeros((2, tm), F32)], axis=0).astype(BF16)
    sel = jnp.where(_iota((SUBLANES, 6 * LANES), 0)
                    == (_iota((SUBLANES, 6 * LANES), 1) >> (LANES.bit_length() - 1)),
                    1.0, 0.0).astype(BF16)
    cols = _dot_tn(rows8, sel)
    col = lambda k: cols[:, k * LANES:(k + 1) * LANES]
    la = col(0) * 256.0 + col(2)
    lb = col(1) * 256.0 + col(3)
    ga, gb = col(4), col(5)
    lane = _iota((tm, LANES), 1).astype(F32)
    mix = jnp.concatenate(
        [jnp.where(lane + float(q * LANES) == la, ga, 0.0)
         + jnp.where(lane + float(q * LANES) == lb, gb, 0.0)
         for q in range(_seg_local_rows(tm) // LANES)], axis=1).astype(BF16)
    y_lo, y_hi = _unpack_pairs(ybuf[slot])
    y = x_ref[...] + jnp.concatenate([_dot(mix, y_lo), _dot(mix, y_hi)], axis=1)
    if final_norm:
        ms = jnp.mean(y * y, axis=-1, keepdims=True)
        y = y * lax.rsqrt(ms + RMS_EPS) * gfin_ref[...]
    o_ref[...] = y


def _combine(x2d, ys, e2, rank2, gates2, seg, g_final, final_norm, tm):
    n, d = x2d.shape
    return pl.pallas_call(
        functools.partial(_combine_kernel, final_norm=final_norm),
        out_shape=jax.ShapeDtypeStruct((n, d), F32),
        grid_spec=pltpu.PrefetchScalarGridSpec(
            num_scalar_prefetch=4,
            grid=(n // tm,),
            in_specs=[pl.BlockSpec((tm, d), lambda j, *_: (j, 0)),
                      pl.BlockSpec((2, tm), lambda j, *_: (0, j)),
                      pl.BlockSpec((2, tm), lambda j, *_: (0, j)),
                      pl.BlockSpec((2, tm), lambda j, *_: (0, j)),
                      pl.BlockSpec(memory_space=pl.ANY),
                      pl.BlockSpec((1, d), lambda j, *_: (0, 0))],
            out_specs=pl.BlockSpec((tm, d), lambda j, *_: (j, 0)),
            scratch_shapes=[pltpu.VMEM((2, _seg_local_rows(tm), d // 2), U32),
                            pltpu.SemaphoreType.DMA((2,))]),
        compiler_params=_cparams(("arbitrary",)),
        name="moe_combine",
    )(*seg, x2d, e2, rank2, gates2, ys, g_final.reshape(1, d))


def _route_plan(tile_cnt, n_tok, rows, tm):
    n_tiles = n_tok // tm
    seg_cnt = (tile_cnt + SUBLANES - 1) // SUBLANES * SUBLANES
    counts = jnp.sum(seg_cnt, axis=0)
    padded = (counts + rows - 1) // rows * rows
    pend = jnp.cumsum(padded)
    pstart = pend - padded
    seg_dst = pstart[None, :] + jnp.cumsum(seg_cnt, axis=0) - seg_cnt
    seg_off = jnp.cumsum(seg_cnt, axis=1) - seg_cnt
    n_blocks = -(-(2 * n_tok + n_tiles * N_EXPERTS * SUBLANES) // rows) + N_EXPERTS
    starts = jnp.arange(n_blocks, dtype=I32) * rows
    block_e = jnp.minimum(jnp.sum((starts[:, None] >= pend[None, :]).astype(I32), axis=1),
                          N_EXPERTS - 1)
    first = jnp.concatenate([jnp.ones((1,), I32),
                             (block_e[1:] != block_e[:-1]).astype(I32)])
    n_used = (pend[-1] // rows).astype(I32).reshape(1)
    bidx = jnp.arange(n_blocks, dtype=I32)
    w_slot = (jnp.cumsum(first) - 1) % 2
    later_first = (bidx[None, :] > bidx[:, None]) & (first[None, :] == 1) & (bidx[None, :] < n_used)
    next_e = jnp.where(jnp.any(later_first, axis=1),
                       block_e[jnp.argmax(later_first, axis=1)], -1)
    blocks = (block_e.astype(I32), first, n_used, w_slot.astype(I32), next_e.astype(I32))
    seg = tuple(a.reshape(-1).astype(I32)
                for a in (seg_cnt, seg_off, seg_dst, jnp.sum(seg_cnt, axis=1)))
    tails = ((padded - counts).astype(I32), (pstart + counts).astype(I32))
    return seg, tails, blocks, n_blocks * rows


def _moe(x2d, route, g_ffn, w_gate, w_up, w_down, layer, g_final, final_norm, tm,
         rows=ROUTE_ROWS):
    e2, gates2, rank2, cnt = route
    seg, tails, blocks, n_rows = _route_plan(cnt[:, :, 0].astype(I32), e2.shape[1], rows, tm)
    xs = _dispatch(x2d, g_ffn, e2, rank2, seg, tails, blocks[2], n_rows, rows, tm)
    ys = _expert_ffn(xs, blocks, w_gate, w_up, w_down, layer, rows)
    return _combine(x2d, ys, e2, rank2, gates2, seg, g_final, final_norm, tm)


def kernel(x, norm_mix, norm_ffn, norm_final, ab_w_in, rw_mu, rw_w0, rw_w2, rw_a0, rw_a2, rw_g2, rw_k_k, rw_k_a, rw_r_k, rw_ln_w, rw_ln_b, lru_conv_w, lru_conv_b, lru_w_a, lru_b_a, lru_w_x, lru_b_x, lru_lambda, ab_w_out, c_w_in, c_lower_bound, c_norm_w, c_w_out, moe_w_group, moe_b_group, moe_w_expert, moe_b_expert, moe_w_gate, moe_w_up, moe_w_down):
    bsz, t, d = x.shape
    depth = norm_mix.shape[0]
    n = bsz * t
    lbs = jnp.cumsum(jax.nn.softmax(c_lower_bound.astype(F32), axis=0), axis=0)
    lbs = lbs - lbs[0]
    x2d = x.reshape(n, d)
    for layer in range(depth):
        j = layer // 2
        if layer % 2 == 0:
            rw_cols = rw_mu.shape[1]
            lru_cols = ab_w_in.shape[2] - rw_cols
            width = rw_w0.shape[1]
            p_a, p_b = _norm_matmul(x2d, norm_mix[layer], ab_w_in[j], (rw_cols, lru_cols),
                                    (F32, F32))
            prep = _rwkv_prep(p_a.reshape(bsz, t, rw_cols), rw_mu[j], rw_w0[j], rw_w2[j],
                              rw_a0[j], rw_a2[j], rw_g2[j], rw_k_k[j], rw_k_a[j],
                              rw_r_k[j].reshape(-1))
            ya = _rwkv_scan(*prep, rw_ln_w[j], rw_ln_b[j])
            yb = _lru(p_b.reshape(bsz, t, lru_cols), lru_conv_w[j], lru_conv_b[j], lru_w_a[j],
                      lru_b_a[j], lru_w_x[j], lru_b_x[j], lru_lambda[j])
            ys = [ya.reshape(n, width), yb.reshape(n, -1)]
            ws = [ab_w_out[j][:width], ab_w_out[j][width:]]
        else:
            hw = c_norm_w.shape[1]
            q, f, i_, g = _norm_matmul(x2d, norm_mix[layer], c_w_in[j], (hw,) * 4,
                                       (BF16, F32, BF16, BF16))
            shp = (bsz, t, hw)
            o = _gla(q.reshape(shp), f.reshape(shp), i_.reshape(shp), g.reshape(shp),
                     lbs[layer], c_norm_w[j])
            ys = [o.reshape(n, hw)]
            ws = [c_w_out[j]]
        tm = min(ROUTE_TILE, n)
        x2d = _proj_residual(x2d, ys, ws)
        route = _router(x2d, norm_ffn[layer], moe_w_group[layer], moe_b_group[layer],
                        moe_w_expert[layer], moe_b_expert[layer], tm)
        x2d = _moe(x2d, route, norm_ffn[layer], moe_w_gate, moe_w_up, moe_w_down, layer,
                   norm_final, layer == depth - 1, tm)
    return x2d.reshape(bsz, t, d)
```

```python
import functools

import jax
import jax.numpy as jnp
from jax import lax
from jax.experimental import pallas as pl
from jax.experimental.pallas import tpu as pltpu

F32 = jnp.float32
BF16 = jnp.bfloat16
I32 = jnp.int32
U32 = jnp.uint32

RMS_EPS = 1e-6
RWKV_GN_EPS = 64e-5
LRU_C = 8.0
CHUNK = 64
CHUNK_SHIFT = CHUNK.bit_length() - 1
HEAD64 = 64
HEAD_SHIFT = HEAD64.bit_length() - 1
LANES = 128
SUBLANES = 8
N_GROUPS = 4
EXPERTS_PER_GROUP = 8
N_EXPERTS = N_GROUPS * EXPERTS_PER_GROUP
ROUTE_ROWS = 512
ROUTE_TILE = 512
DIGIT_BITS = 8
VMEM_LIMIT = 56 * 1024 * 1024


def _cparams(sem):
    return pltpu.CompilerParams(dimension_semantics=sem, vmem_limit_bytes=VMEM_LIMIT)


def _sigmoid(x):
    return 0.5 * jnp.tanh(0.5 * x) + 0.5


def _pack_pairs(x):
    w = x.shape[1] // 2
    lo = lax.bitcast_convert_type(x[:, :w].astype(BF16).astype(F32), U32)
    hi = lax.bitcast_convert_type(x[:, w:].astype(BF16).astype(F32), U32)
    return (lo >> 16) | hi


def _unpack_pairs(p):
    lo = lax.bitcast_convert_type(p << 16, F32).astype(BF16)
    hi = lax.bitcast_convert_type(p & jnp.uint32(0xFFFF0000), F32).astype(BF16)
    return lo, hi


def _softplus(x):
    return jnp.maximum(x, 0.0) + jnp.log(1.0 + jnp.exp(-jnp.abs(x)))


def _silu(x):
    return x * _sigmoid(x)


def _gelu_tanh(x):
    return 0.5 * x * (1.0 + jnp.tanh(0.7978845608028654 * (x + 0.044715 * x * x * x)))


def _dot(a, b):
    return jnp.dot(a, b, preferred_element_type=F32)


def _dot_nt(a, b):
    return lax.dot_general(a, b, (((1,), (1,)), ((), ())), preferred_element_type=F32)


def _dot_tn(a, b):
    return lax.dot_general(a, b, (((0,), (0,)), ((), ())), preferred_element_type=F32)


def _split2(x):
    hi = x.astype(BF16)
    lo = (x - hi.astype(F32)).astype(BF16)
    return hi, lo


def _dot_exact_rhs(x, m_bf16):
    hi, lo = _split2(x)
    return _dot(hi, m_bf16) + _dot(lo, m_bf16)


def _dot_exact_lhs(m_bf16, x):
    hi, lo = _split2(x)
    return _dot(m_bf16, hi) + _dot(m_bf16, lo)


def _dot_split(a, w_hi, w_lo):
    a_hi, a_lo = _split2(a)
    return _dot(a_hi, w_hi) + (_dot(a_lo, w_hi) + _dot(a_hi, w_lo))


def _hi_lo(w):
    hi = w.astype(BF16)
    return hi, (w - hi.astype(F32)).astype(BF16)


def _iota(shape, dim):
    return lax.broadcasted_iota(I32, shape, dim)


def _norm_matmul_kernel(x_ref, g_ref, w_ref, *o_refs, splits):
    x = x_ref[...]
    ms = jnp.mean(x * x, axis=-1, keepdims=True)
    y = (x * lax.rsqrt(ms + RMS_EPS) * g_ref[...]).astype(BF16)
    off = 0
    for o_ref, n in zip(o_refs, splits):
        o_ref[...] = _dot(y, w_ref[:, off:off + n]).astype(o_ref.dtype)
        off += n


def _norm_matmul(x2d, g, w, splits, out_dtypes, tm=1024):
    n, d = x2d.shape
    tm = min(tm, n)
    ncols = w.shape[1]
    assert sum(splits) == ncols and n % tm == 0 and len(out_dtypes) == len(splits)
    return pl.pallas_call(
        functools.partial(_norm_matmul_kernel, splits=splits),
        out_shape=[jax.ShapeDtypeStruct((n, s), dt) for s, dt in zip(splits, out_dtypes)],
        grid=(n // tm,),
        in_specs=[pl.BlockSpec((tm, d), lambda i: (i, 0)),
                  pl.BlockSpec((1, d), lambda i: (0, 0)),
                  pl.BlockSpec((d, ncols), lambda i: (0, 0))],
        out_specs=[pl.BlockSpec((tm, s), lambda i: (i, 0)) for s in splits],
        compiler_params=_cparams(("parallel",)),
        name="norm_matmul",
    )(x2d, g.reshape(1, d), w.astype(BF16))


def _proj_residual_kernel(*refs, n_in):
    x_ref = refs[0]
    y_refs = refs[1:1 + n_in]
    w_refs = refs[1 + n_in:1 + 2 * n_in]
    o_ref = refs[1 + 2 * n_in]
    acc = x_ref[...]
    for y_ref, w_ref in zip(y_refs, w_refs):
        acc = acc + _dot(y_ref[...].astype(BF16), w_ref[...])
    o_ref[...] = acc


def _proj_residual(x2d, ys, ws, tm=1024):
    n, d = x2d.shape
    tm = min(tm, n)
    n_in = len(ys)
    in_specs = [pl.BlockSpec((tm, d), lambda i: (i, 0))]
    in_specs += [pl.BlockSpec((tm, y.shape[1]), lambda i: (i, 0)) for y in ys]
    in_specs += [pl.BlockSpec(w.shape, lambda i: (0, 0)) for w in ws]
    return pl.pallas_call(
        functools.partial(_proj_residual_kernel, n_in=n_in),
        out_shape=jax.ShapeDtypeStruct((n, d), F32),
        grid=(n // tm,),
        in_specs=in_specs,
        out_specs=pl.BlockSpec((tm, d), lambda i: (i, 0)),
        compiler_params=_cparams(("parallel",)),
        name="proj_residual",
    )(x2d, *ys, *[w.astype(BF16) for w in ws])


def _router_kernel(x_ref, g_ref, w_ref, b_ref, upper_ref, e_o, gate_o, rank_o, cnt_o, hn_o):
    _route_tile(x_ref[...], g_ref, w_ref, b_ref, upper_ref, e_o, gate_o, rank_o, cnt_o, hn_o)


def _router(x2d, g_ffn, w_group, b_group, w_expert, b_expert, tm):
    n, d = x2d.shape
    wt = jnp.zeros((d, LANES), F32)
    wt = wt.at[:, 0:N_GROUPS].set(w_group).at[:, 8:8 + N_EXPERTS].set(w_expert)
    wt = jnp.stack(_hi_lo(wt))
    bt = jnp.zeros((LANES,), F32)
    bt = bt.at[0:N_GROUPS].set(b_group).at[8:8 + N_EXPERTS].set(b_expert)
    bt = jnp.broadcast_to(bt[:, None], (LANES, LANES))
    ti = jnp.arange(tm, dtype=I32)
    upper = (ti[:, None] < ti[None, :]).astype(BF16)
    tok2 = pl.BlockSpec((2, tm), lambda i: (0, i))
    return pl.pallas_call(
        _router_kernel,
        out_shape=[jax.ShapeDtypeStruct((2, n), I32),
                   jax.ShapeDtypeStruct((2, n), F32),
                   jax.ShapeDtypeStruct((2, n), I32),
                   jax.ShapeDtypeStruct((n // tm, N_EXPERTS, LANES), F32),
                   jax.ShapeDtypeStruct((n, d), BF16)],
        grid=(n // tm,),
        in_specs=[pl.BlockSpec((tm, d), lambda i: (i, 0)),
                  pl.BlockSpec((1, d), lambda i: (0, 0)),
                  pl.BlockSpec((2, d, LANES), lambda i: (0, 0, 0)),
                  pl.BlockSpec((LANES, LANES), lambda i: (0, 0)),
                  pl.BlockSpec((tm, tm), lambda i: (0, 0))],
        out_specs=[tok2, tok2, tok2,
                   pl.BlockSpec((1, N_EXPERTS, LANES), lambda i: (i, 0, 0)),
                   pl.BlockSpec((tm, d), lambda i: (i, 0))],
        compiler_params=_cparams(("parallel",)),
        name="moe_router",
    )(x2d, g_ffn.reshape(1, d), wt, bt, upper)


def _rwkv_prep_kernel(p_ref, mu_ref, w0_ref, a0_ref, kk_s_ref, ka_ref, rk_ref,
                      wcomb_ref, g2_ref, tril_ref,
                      rt_o, kkt_o, kh_o, bh_o, v_o, g_o, bonus_o, gc_o,
                      prev_ref, *, width):
    tm = p_ref.shape[1]

    @pl.when(pl.program_id(1) == 0)
    def _():
        prev_ref[...] = jnp.zeros_like(prev_ref)

    p = p_ref[0]
    rolled = pltpu.roll(p, 1, axis=0)
    prev = jnp.where(_iota(p.shape, 0) == 0, prev_ref[...], rolled)
    prev_ref[...] = p[tm - 1:tm, :]
    ps = p + (prev - p) * mu_ref[...]

    r = ps[:, 0:width]
    k = ps[:, width:2 * width]
    v = ps[:, 2 * width:3 * width]
    lowrank = ps[:, 3 * width:3 * width + LANES]
    gl = ps[:, 3 * width + LANES:3 * width + 2 * LANES]

    lane = _iota(lowrank.shape, 1)
    lr_in = jnp.where(lane < HEAD64, jnp.tanh(lowrank), lowrank)
    t12 = _dot_split(lr_in, wcomb_ref[0], wcomb_ref[1])
    wlog = -_softplus(-(w0_ref[...] + t12[:, :width])) - 0.5
    lw = -jnp.exp(wlog)
    a = _sigmoid(a0_ref[...] + t12[:, width:])
    g = _dot_split(_sigmoid(gl), g2_ref[0], g2_ref[1])

    seg = jnp.where((_iota((LANES, LANES), 0) >> HEAD_SHIFT)
                    == (_iota((LANES, LANES), 1) >> HEAD_SHIFT),
                    1.0, 0.0).astype(BF16)

    def head_sums(x):
        return jnp.concatenate([_dot_exact_rhs(x[:, q * LANES:(q + 1) * LANES], seg)
                                for q in range(width // LANES)], axis=1)

    kk = k * kk_s_ref[...]
    nrm = jnp.sqrt(head_sums(kk * kk))
    kk = kk / jnp.maximum(nrm, 1e-12)
    k2 = k * (1.0 + (a - 1.0) * ka_ref[...])
    bonus = head_sums(r * k2 * rk_ref[...]) * v

    b = kk * a

    n_chunk = tm // CHUNK
    cum = _dot_exact_lhs(tril_ref[...], lw)
    cum3 = cum.reshape(n_chunk, CHUNK, width)
    cend = cum3[:, CHUNK - 1:CHUNK, :]
    e_neg = jnp.exp(-cum)

    rt_o[0] = (r * jnp.exp(cum)).astype(BF16)
    kkt_o[0] = (kk * jnp.exp(cum - lw)).astype(BF16)
    kh_o[0] = (k2 * e_neg).astype(BF16)
    bh_o[0] = (b * e_neg).astype(BF16)
    v_o[0] = v.astype(BF16)
    g_o[0] = g.astype(BF16)
    bonus_o[0] = bonus.astype(BF16)
    gc_o[0] = jnp.exp(cend)


def _rwkv_prep(p_a, mu, w0, w2, a0, a2, g2, k_k, k_a, r_k, tm=512):
    b, t, cols = p_a.shape
    width = w0.shape[0]
    rank = w2.shape[0]
    assert rank == HEAD64 and a2.shape[0] == HEAD64 and g2.shape[0] == LANES
    zeros = jnp.zeros((rank, width), F32)
    wcomb = jnp.concatenate([jnp.concatenate([w2, zeros], 1),
                             jnp.concatenate([zeros, a2], 1)], 0)
    row = lambda a_: a_.reshape(1, -1)
    const = lambda shp: pl.BlockSpec(shp, lambda i, j: (0,) * len(shp))
    tok = pl.BlockSpec((1, tm, width), lambda i, j: (i, j, 0))
    n_chunk = tm // CHUNK
    ti = jnp.arange(tm, dtype=I32)
    tril_bd = (((ti[:, None] >> CHUNK_SHIFT) == (ti[None, :] >> CHUNK_SHIFT))
               & (ti[None, :] <= ti[:, None])).astype(BF16)
    return pl.pallas_call(
        functools.partial(_rwkv_prep_kernel, width=width),
        out_shape=([jax.ShapeDtypeStruct((b, t, width), BF16)] * 7
                   + [jax.ShapeDtypeStruct((b, t // CHUNK, 1, width), F32)]),
        grid=(b, t // tm),
        in_specs=[pl.BlockSpec((1, tm, cols), lambda i, j: (i, j, 0)),
                  const((1, cols)), const((1, width)), const((1, width)), const((1, width)),
                  const((1, width)), const((1, width)),
                  const((2, LANES, 2 * width)), const((2, LANES, width)), const((tm, tm))],
        out_specs=[tok] * 7 + [pl.BlockSpec((1, n_chunk, 1, width), lambda i, j: (i, j, 0, 0))],
        scratch_shapes=[pltpu.VMEM((1, cols), F32)],
        compiler_params=_cparams(("parallel", "arbitrary")),
        name="rwkv_prep",
    )(p_a, row(mu), row(w0), row(a0), row(k_k), row(k_a), row(r_k),
      jnp.stack(_hi_lo(wcomb)), jnp.stack(_hi_lo(g2)), tril_bd)


def _rwkv_scan_kernel(rt_ref, kkt_ref, kh_ref, bh_ref, v_ref, g_ref, bonus_ref,
                      gc_ref, lnw_ref, lnb_ref, o_ref, st_ref, *, n_batch, n_pairs):
    c = CHUNK

    @pl.when(pl.program_id(0) == 0)
    def _():
        st_ref[...] = jnp.zeros_like(st_ref)

    rr = _iota((LANES, LANES), 0)
    cc = _iota((LANES, LANES), 1)
    mask_bd = jnp.where((rr >> HEAD_SHIFT) == (cc >> HEAD_SHIFT), 1.0, 0.0)
    mask_bd16 = mask_bd.astype(BF16)
    t_i = _iota((c, LANES), 0)
    s_i = _iota((c, LANES), 1) & (c - 1)
    strict = jnp.where(s_i < t_i, 1.0, 0.0)
    incl = jnp.where(s_i <= t_i, 1.0, 0.0)
    eye_ss = jnp.where(s_i == t_i, 1.0, 0.0)

    def bd16(x):
        return jnp.concatenate([x, x], axis=0) * mask_bd16

    def b16(xs):
        return [x.astype(BF16) for x in xs]

    chains = [(bi, p) for bi in range(n_batch) for p in range(n_pairs)]
    n_ch = len(chains)
    idx = range(n_ch)

    def ld(ref):
        return [ref[bi, :, p * LANES:(p + 1) * LANES] for bi, p in chains]

    rt, kkt, kh, bh, v = (ld(ref) for ref in (rt_ref, kkt_ref, kh_ref, bh_ref, v_ref))

    lhs = [jnp.concatenate([kkt[i], rt[i]], axis=0) for i in idx]
    rhs = [jnp.concatenate([bd16(kh[i]), bd16(bh[i])], axis=0) for i in idx]
    gm = [_dot_nt(lhs[i], rhs[i]) for i in idx]
    l_kv = [gm[i][:c, :LANES] * strict for i in idx]
    xp = [-(gm[i][:c, LANES:] * strict) for i in idx]
    pm = [gm[i][c:, :LANES] * incl for i in idx]
    nqm = [-(gm[i][c:, LANES:] * incl) for i in idx]

    tinv = [eye_ss + xp[i] for i in idx]
    xp16 = b16(xp)
    xpb = [bd16(x) for x in xp16]
    for _ in range(c.bit_length() - 2):
        xp16 = b16([_dot(xp16[i], xpb[i]) for i in idx])
        xpb = [bd16(x) for x in xp16]
        t16 = b16(tinv)
        tinv = [tinv[i] + _dot(t16[i], xpb[i]) for i in idx]
    tinv16 = b16(tinv)

    v_bd = [bd16(x) for x in v]
    l_kv16 = b16(l_kv)
    w16 = b16([_dot(l_kv16[i], v_bd[i]) for i in idx])
    tkw = [_dot(tinv16[i], jnp.concatenate([bd16(kkt[i]), bd16(w16[i])], axis=1)) for i in idx]
    tk16 = b16([x[:, :LANES] for x in tkw])

    st = [st_ref[i] for i in idx]
    st16 = b16(st)
    u = [_dot_nt(tk16[i], st16[i]) + tkw[i][:, LANES:] for i in idx]
    u16 = b16(u)
    pq16 = [jnp.concatenate([pm[i], nqm[i]], axis=1).astype(BF16) for i in idx]
    vu = [jnp.concatenate([v_bd[i], bd16(u16[i])], axis=0) for i in idx]
    y = [_dot_nt(rt[i], st16[i]) + _dot(pq16[i], vu[i]) for i in idx]
    upd = [_dot_tn(jnp.concatenate([v[i], -u16[i]], axis=0),
                   jnp.concatenate([kh[i], bh[i]], axis=0)) for i in idx]
    for i, (bi, p) in enumerate(chains):
        st_ref[i] = (st[i] + upd[i] * mask_bd) * gc_ref[bi, 0, :, p * LANES:(p + 1) * LANES]

    ys = jnp.concatenate(y, axis=0)
    mean = _dot_exact_rhs(ys, mask_bd16) * (1.0 / HEAD64)
    d = ys - mean
    var = _dot_exact_rhs(d * d, mask_bd16) * (1.0 / HEAD64)
    dn = d * lax.rsqrt(var + RWKV_GN_EPS)
    for i, (bi, p) in enumerate(chains):
        sl = slice(p * LANES, (p + 1) * LANES)
        yn = dn[i * c:(i + 1) * c] * lnw_ref[:, sl] + lnb_ref[:, sl]
        o_ref[bi, :, sl] = ((yn + bonus_ref[bi, :, sl]) * g_ref[bi, :, sl]).astype(o_ref.dtype)


def _rwkv_scan(rt, kkt, kh, bh, v, g, bonus, gc, ln_w, ln_b):
    bsz, t, width = rt.shape
    n_pairs = width // LANES
    tok = pl.BlockSpec((bsz, CHUNK, width), lambda j: (0, j, 0))
    const = pl.BlockSpec((1, width), lambda j: (0, 0))
    return pl.pallas_call(
        functools.partial(_rwkv_scan_kernel, n_batch=bsz, n_pairs=n_pairs),
        out_shape=jax.ShapeDtypeStruct((bsz, t, width), BF16),
        grid=(t // CHUNK,),
        in_specs=[tok] * 7 + [pl.BlockSpec((bsz, 1, 1, width), lambda j: (0, j, 0, 0)),
                              const, const],
        out_specs=tok,
        scratch_shapes=[pltpu.VMEM((bsz * n_pairs, LANES, LANES), F32)],
        compiler_params=_cparams(("arbitrary",)),
        name="rwkv_scan",
    )(rt, kkt, kh, bh, v, g, bonus, gc, ln_w.reshape(1, -1), ln_b.reshape(1, -1))


def _lru_kernel(p_ref, cw_ref, cb_ref, wg_ref, ba_ref, bx_ref, lam_ref, o_ref,
                xcarry_ref, hcarry_ref, a_s, u_s, *, width):
    tm = p_ref.shape[1]

    @pl.when(pl.program_id(1) == 0)
    def _():
        xcarry_ref[...] = jnp.zeros_like(xcarry_ref)
        hcarry_ref[...] = jnp.zeros_like(hcarry_ref)

    gate = p_ref[0, :, 0:width]
    xb = p_ref[0, :, width:2 * width]
    carry8 = xcarry_ref[...]
    row8 = _iota((8, width), 0)

    def shifted(s):
        rolled = pltpu.roll(xb, s, axis=0)
        first = jnp.where(row8 < s, pltpu.roll(carry8, s, axis=0), rolled[0:8])
        return jnp.concatenate([first, rolled[8:]], axis=0)

    xc = (cw_ref[0:1, :] * shifted(3) + cw_ref[1:2, :] * shifted(2)
          + cw_ref[2:3, :] * shifted(1) + cw_ref[3:4, :] * xb + cb_ref[...])
    xcarry_ref[...] = xb[tm - 8:tm, :]

    gates = _dot(xc.astype(BF16), wg_ref[...])
    rg = _sigmoid(gates[:, :width] + ba_ref[...])
    ig = _sigmoid(gates[:, width:] + bx_ref[...])
    log_a = -LRU_C * rg * _softplus(-lam_ref[...])
    a = jnp.exp(log_a)
    a_s[...] = a
    u_s[...] = jnp.sqrt(1.0 - a * a) * ig * xc

    m1 = row8 >= 1
    m2 = row8 >= 2
    m4 = row8 >= 4

    def body(i, h):
        off = pl.multiple_of(i * 8, 8)
        a8 = a_s[pl.ds(off, 8), :]
        u8 = u_s[pl.ds(off, 8), :]
        for s, m in ((1, m1), (2, m2), (4, m4)):
            u_sh = jnp.where(m, pltpu.roll(u8, s, axis=0), 0.0)
            a_sh = jnp.where(m, pltpu.roll(a8, s, axis=0), 1.0)
            u8 = u8 + a8 * u_sh
            a8 = a8 * a_sh
        h8 = u8 + a8 * h
        u_s[pl.ds(off, 8), :] = h8
        return jnp.broadcast_to(h8[7:8, :], (8, width))

    h_last = lax.fori_loop(0, tm // 8, body, hcarry_ref[...])
    hcarry_ref[...] = h_last
    o_ref[0] = (u_s[...] * _gelu_tanh(gate)).astype(o_ref.dtype)


def _block_diag(w):
    nb, di, do = w.shape
    eye = jnp.eye(nb, dtype=w.dtype)
    return (eye[:, None, :, None] * w[:, :, None, :]).reshape(nb * di, nb * do)


def _lru(p_b, conv_w, conv_b, w_a, b_a, w_x, b_x, lam, tm=1024):
    b, t, cols = p_b.shape
    width = cols // 2
    wg = jnp.concatenate([_block_diag(w_a), _block_diag(w_x)], axis=1).astype(BF16)
    row = lambda a_: a_.reshape(1, -1)
    const = lambda shp: pl.BlockSpec(shp, lambda i, j: (0, 0))
    return pl.pallas_call(
        functools.partial(_lru_kernel, width=width),
        out_shape=jax.ShapeDtypeStruct((b, t, width), BF16),
        grid=(b, t // tm),
        in_specs=[pl.BlockSpec((1, tm, cols), lambda i, j: (i, j, 0)),
                  const(conv_w.shape), const((1, width)), const((width, 2 * width)),
                  const((1, width)), const((1, width)), const((1, width))],
        out_specs=pl.BlockSpec((1, tm, width), lambda i, j: (i, j, 0)),
        scratch_shapes=[pltpu.VMEM((8, width), F32), pltpu.VMEM((8, width), F32),
                        pltpu.VMEM((tm, width), F32), pltpu.VMEM((tm, width), F32)],
        compiler_params=_cparams(("parallel", "arbitrary")),
        name="rglru",
    )(p_b, conv_w, row(conv_b), wg, row(b_a), row(b_x), row(lam))


def _gla_kernel(q_ref, f_ref, i_ref, g_ref, lb_ref, nw_ref, o_ref, st_ref, *, n_batch, n_heads):
    c = CHUNK

    @pl.when(pl.program_id(0) == 0)
    def _():
        st_ref[...] = jnp.zeros_like(st_ref)

    trilf = jnp.where(_iota((c, c), 1) <= _iota((c, c), 0), 1.0, 0.0)
    tril16 = trilf.astype(BF16)
    mid = c // 2 - 1
    lb = lb_ref[...]

    qm, km, v16, gc, em, ecm = [], [], [], [], [], []
    for bi in range(n_batch):
        q = _silu(q_ref[bi].astype(F32))
        fg = lb + (1.0 - lb) * _sigmoid(f_ref[bi])
        k = 1.0 - fg
        cum = _dot_exact_lhs(tril16, jnp.log(fg))
        cum_c = cum[c - 1:c, :]
        cum_m = cum[mid:mid + 1, :]
        qm.append((q * jnp.exp(cum - cum_m)).astype(BF16))
        km.append((k * jnp.exp(cum_m - cum)).astype(BF16))
        v16.append(i_ref[bi].astype(BF16))
        gc.append(jnp.exp(cum_c))
        em.append(jnp.exp(cum_m))
        ecm.append(jnp.exp(cum_c - cum_m))

    chains = [(bi, h) for bi in range(n_batch) for h in range(n_heads)]
    idx = range(len(chains))

    def hs(xs):
        return [xs[bi][:, h * LANES:(h + 1) * LANES] for bi, h in chains]

    qm_c, km_c, v_c, gc_c, em_c, ecm_c = hs(qm), hs(km), hs(v16), hs(gc), hs(em), hs(ecm)
    scores = [(_dot_nt(qm_c[i], km_c[i]) * trilf).astype(BF16) for i in idx]
    st = [st_ref[i] for i in idx]
    st16 = [(st[i] * em_c[i]).astype(BF16) for i in idx]
    o = [_dot(scores[i], v_c[i]) + _dot_nt(qm_c[i], st16[i]) for i in idx]
    for i in idx:
        st_ref[i] = st[i] * gc_c[i] + _dot_tn(v_c[i], km_c[i]) * ecm_c[i]

    for i, (bi, h) in enumerate(chains):
        sl = slice(h * LANES, (h + 1) * LANES)
        on = o[i] * lax.rsqrt(jnp.mean(o[i] * o[i], axis=-1, keepdims=True) + RMS_EPS)
        gate = _silu(g_ref[bi, :, sl].astype(F32))
        o_ref[bi, :, sl] = (on * nw_ref[:, sl] * gate).astype(o_ref.dtype)


def _gla(q, f, i, g, lower_bound, norm_w):
    bsz, t, width = q.shape
    n_heads = width // LANES
    tok = pl.BlockSpec((bsz, CHUNK, width), lambda j: (0, j, 0))
    const = pl.BlockSpec((1, width), lambda j: (0, 0))
    return pl.pallas_call(
        functools.partial(_gla_kernel, n_batch=bsz, n_heads=n_heads),
        out_shape=jax.ShapeDtypeStruct((bsz, t, width), BF16),
        grid=(t // CHUNK,),
        in_specs=[tok] * 4 + [const, const],
        out_specs=tok,
        scratch_shapes=[pltpu.VMEM((bsz * n_heads, LANES, LANES), F32)],
        compiler_params=_cparams(("arbitrary",)),
        name="hgrn2_gla",
    )(q, f, i, g, lower_bound.reshape(1, -1), norm_w.reshape(1, -1))


def _route_tile(x, g_ref, w_ref, b_ref, upper_ref, e_o, gate_o, rank_o, cnt_o, hn_o):
    ms = jnp.mean(x * x, axis=-1, keepdims=True)
    hn = x * lax.rsqrt(ms + RMS_EPS) * g_ref[...]
    hn_o[...] = hn.astype(BF16)
    lt = _dot_split(hn, w_ref[0], w_ref[1]).T + b_ref[:, 0:1]
    tm = x.shape[0]
    gl = lt[0:8, :]
    row8 = _iota((8, tm), 0)
    gl = jnp.where(row8 < N_GROUPS, gl, -jnp.inf)
    gmax = jnp.max(gl, axis=0, keepdims=True)
    g_sel = jnp.min(jnp.where(gl == gmax, row8, 8), axis=0, keepdims=True)
    g_gate = 1.0 / jnp.sum(jnp.exp(gl - gmax), axis=0, keepdims=True)

    el = jnp.zeros((EXPERTS_PER_GROUP, tm), F32)
    for gi in range(N_GROUPS):
        lo = 8 + gi * EXPERTS_PER_GROUP
        el = jnp.where(g_sel == gi, lt[lo:lo + EXPERTS_PER_GROUP, :], el)
    m1 = jnp.max(el, axis=0, keepdims=True)
    i1 = jnp.min(jnp.where(el == m1, row8, 8), axis=0, keepdims=True)
    el2 = jnp.where(row8 == i1, -jnp.inf, el)
    m2 = jnp.max(el2, axis=0, keepdims=True)
    i2 = jnp.min(jnp.where(el2 == m2, row8, 8), axis=0, keepdims=True)
    e2 = jnp.exp(m2 - m1)
    inv = 1.0 / (1.0 + e2)
    ea = g_sel * EXPERTS_PER_GROUP + i1
    eb = g_sel * EXPERTS_PER_GROUP + i2
    e_o[...] = jnp.concatenate([ea, eb], axis=0)
    gate_o[...] = jnp.concatenate([g_gate * inv, g_gate * e2 * inv], axis=0)

    erow = _iota((N_EXPERTS, tm), 0)
    oh_a = jnp.where(erow == ea, 1.0, 0.0)
    oh_b = jnp.where(erow == eb, 1.0, 0.0)
    pre_a = _dot(oh_a.astype(BF16), upper_ref[...])
    pre_b = _dot(oh_b.astype(BF16), upper_ref[...])
    cnt_a = jnp.sum(oh_a, axis=1, keepdims=True)
    cnt_b = jnp.sum(oh_b, axis=1, keepdims=True)
    rank_a = jnp.sum(oh_a * pre_a, axis=0, keepdims=True)
    rank_b = jnp.sum(oh_b * (pre_b + cnt_a), axis=0, keepdims=True)
    rank_o[...] = jnp.concatenate([rank_a, rank_b], axis=0).astype(I32)
    cnt_o[0] = jnp.broadcast_to(cnt_a + cnt_b, (N_EXPERTS, LANES))


def _seg_local_rows(tm):
    return 2 * tm + N_EXPERTS * SUBLANES


def _segment_copies(cnt_ref, off_ref, dst_ref, base, local, remote, sem, to_remote, wait):
    def body(e, carry):
        c = pl.multiple_of(cnt_ref[base + e], SUBLANES)

        @pl.when(c > 0)
        def _():
            off = 0 if off_ref is None else pl.multiple_of(off_ref[base + e], SUBLANES)
            loc = local.at[pl.ds(off, c), :]
            rem = remote.at[pl.ds(pl.multiple_of(dst_ref[base + e], SUBLANES), c), :]
            cp = (pltpu.make_async_copy(loc, rem, sem) if to_remote
                  else pltpu.make_async_copy(rem, loc, sem))
            if wait:
                cp.wait()
            else:
                cp.start()
        return carry
    lax.fori_loop(0, N_EXPERTS, body, 0, unroll=4)


def _local_index(e_ref, rank_ref, off_ref, tile):
    e = e_ref[...]
    lidx = rank_ref[...]
    for ex in range(N_EXPERTS):
        lidx = lidx + jnp.where(e == ex, off_ref[tile * N_EXPERTS + ex], 0)
    return lidx


def _segment_wait(tot_ref, tile, local, remote, sem, to_remote):
    tot = pl.multiple_of(tot_ref[tile], SUBLANES)

    @pl.when(tot > 0)
    def _():
        loc = local.at[pl.ds(0, tot), :]
        rem = remote.at[pl.ds(0, tot), :]
        cp = (pltpu.make_async_copy(loc, rem, sem) if to_remote
              else pltpu.make_async_copy(rem, loc, sem))
        cp.wait()


def _dispatch_kernel(cnt_ref, off_ref, dst_ref, tot_ref, tcnt_ref, tdst_ref, nused_ref,
                     hn_ref, e_ref, rank_ref, xs_out, sbuf, zbuf, sems, zsem):
    j = pl.program_id(0)
    nb = pl.num_programs(0)
    tm = hn_ref.shape[0]
    slot = lax.rem(j, 2)

    def copies(tile, s, wait):
        if wait:
            _segment_wait(tot_ref, tile, sbuf.at[s], xs_out, sems.at[s], True)
        else:
            _segment_copies(cnt_ref, off_ref, dst_ref, tile * N_EXPERTS, sbuf.at[s], xs_out,
                            sems.at[s], True, False)

    @pl.when(j == 0)
    def _():
        zbuf[...] = jnp.zeros_like(zbuf)
        rows = zbuf.shape[0]
        n_blocks = xs_out.shape[0] // rows
        for wait in (False, True):
            _segment_copies(tcnt_ref, None, tdst_ref, 0, zbuf, xs_out, zsem, True, wait)

            def body(b, carry):
                start = pl.multiple_of(b * rows, SUBLANES)
                cp = pltpu.make_async_copy(zbuf, xs_out.at[pl.ds(start, rows), :], zsem)
                if wait:
                    cp.wait()
                else:
                    cp.start()
                return carry
            lax.fori_loop(nused_ref[0], n_blocks, body, 0)

    @pl.when(j >= 2)
    def _():
        copies(j - 2, slot, True)

    r = _iota((_seg_local_rows(tm), tm), 0)
    lidx = _local_index(e_ref, rank_ref, off_ref, j)
    perm = jnp.where((r == lidx[0:1, :]) | (r == lidx[1:2, :]), 1.0, 0.0).astype(BF16)
    sbuf[slot] = _pack_pairs(_dot(perm, hn_ref[...]))
    copies(j, slot, False)

    @pl.when(j == nb - 1)
    def _():
        copies(j, slot, True)

    @pl.when((j == nb - 1) & (nb >= 2))
    def _():
        copies(j - 1, 1 - slot, True)


def _dispatch(hn16, e2, rank2, seg, tails, n_used, n_rows, rows, tm):
    n, d = hn16.shape
    return pl.pallas_call(
        _dispatch_kernel,
        out_shape=jax.ShapeDtypeStruct((n_rows, d // 2), U32),
        grid_spec=pltpu.PrefetchScalarGridSpec(
            num_scalar_prefetch=7,
            grid=(n // tm,),
            in_specs=[pl.BlockSpec((tm, d), lambda j, *_: (j, 0)),
                      pl.BlockSpec((2, tm), lambda j, *_: (0, j)),
                      pl.BlockSpec((2, tm), lambda j, *_: (0, j))],
            out_specs=pl.BlockSpec(memory_space=pl.ANY),
            scratch_shapes=[pltpu.VMEM((2, _seg_local_rows(tm), d // 2), U32),
                            pltpu.VMEM((rows, d // 2), U32),
                            pltpu.SemaphoreType.DMA((2,)), pltpu.SemaphoreType.DMA(())]),
        compiler_params=_cparams(("arbitrary",)),
        name="moe_dispatch",
    )(*seg, *tails, n_used, hn16, e2, rank2)


def _expert_kernel(be_ref, first_ref, nused_ref, slot_ref, next_ref, x_ref, wg_hbm, wu_hbm,
                   wd_hbm, o_ref, wg32, wu32, wd32, wg16, wu16, wd16, wsem, *, layer):
    j = pl.program_id(0)

    def weight_copies(e, s):
        return [pltpu.make_async_copy(hbm.at[layer, e], buf.at[s], wsem.at[s])
                for hbm, buf in ((wg_hbm, wg32), (wu_hbm, wu32), (wd_hbm, wd32))]

    @pl.when((first_ref[j] == 1) & (j < nused_ref[0]))
    def _():
        s = slot_ref[j]

        @pl.when(j == 0)
        def _():
            for cp in weight_copies(be_ref[0], 0):
                cp.start()

        for cp in weight_copies(be_ref[j], s):
            cp.wait()

        @pl.when(next_ref[j] >= 0)
        def _():
            for cp in weight_copies(next_ref[j], 1 - s):
                cp.start()

        wg16[...] = wg32[s].astype(BF16)
        wu16[...] = wu32[s].astype(BF16)
        wd16[...] = wd32[s].astype(BF16)

    @pl.when(j < nused_ref[0])
    def _():
        x_lo, x_hi = _unpack_pairs(x_ref[...])
        half = x_lo.shape[1]
        hg = _dot(x_lo, wg16[:half, :]) + _dot(x_hi, wg16[half:, :])
        hu = _dot(x_lo, wu16[:half, :]) + _dot(x_hi, wu16[half:, :])
        hid = (_silu(hg) * hu).astype(BF16)
        o_ref[...] = _pack_pairs(_dot(hid, wd16[...]))

    @pl.when(j >= nused_ref[0])
    def _():
        o_ref[...] = jnp.zeros_like(o_ref)


def _expert_ffn(xs, blocks, w_gate, w_up, w_down, layer, rows):
    n_rows, dh = xs.shape
    n_blocks = n_rows // rows
    d, ff = w_gate.shape[-2:]
    assert d == 2 * dh
    any_spec = pl.BlockSpec(memory_space=pl.ANY)
    return pl.pallas_call(
        functools.partial(_expert_kernel, layer=layer),
        out_shape=jax.ShapeDtypeStruct((n_rows, dh), U32),
        grid_spec=pltpu.PrefetchScalarGridSpec(
            num_scalar_prefetch=5,
            grid=(n_blocks,),
            in_specs=[pl.BlockSpec((rows, dh),
                                   lambda j, be, fi, nu, *_: (
                                       jnp.maximum(jnp.minimum(j, nu[0] - 1), 0), 0)),
                      any_spec, any_spec, any_spec],
            out_specs=pl.BlockSpec((rows, dh), lambda j, *_: (j, 0)),
            scratch_shapes=[pltpu.VMEM((2, d, ff), F32), pltpu.VMEM((2, d, ff), F32),
                            pltpu.VMEM((2, ff, d), F32),
                            pltpu.VMEM((d, ff), BF16), pltpu.VMEM((d, ff), BF16),
                            pltpu.VMEM((ff, d), BF16), pltpu.SemaphoreType.DMA((2,))]),
        compiler_params=_cparams(("arbitrary",)),
        name="moe_expert_ffn",
    )(*blocks, xs, w_gate, w_up, w_down)


def _combine_kernel(cnt_ref, off_ref, dst_ref, tot_ref, x_ref, e_ref, rank_ref, gt_ref, ys_hbm,
                    gfin_ref, o_ref, ybuf, sems, *, final_norm):
    j = pl.program_id(0)
    nb = pl.num_programs(0)
    tm = x_ref.shape[0]
    slot = lax.rem(j, 2)

    def copies(tile, s, wait):
        if wait:
            _segment_wait(tot_ref, tile, ybuf.at[s], ys_hbm, sems.at[s], False)
        else:
            _segment_copies(cnt_ref, off_ref, dst_ref, tile * N_EXPERTS, ybuf.at[s], ys_hbm,
                            sems.at[s], False, False)

    @pl.when(j == 0)
    def _():
        ybuf[...] = jnp.zeros_like(ybuf)
        copies(0, 0, False)

    @pl.when(j + 1 < nb)
    def _():
        copies(j + 1, 1 - slot, False)

    copies(j, slot, True)
    lidx = _local_index(e_ref, rank_ref, off_ref, j)
    hi = (lidx >> DIGIT_BITS).astype(F32)
    lo = (lidx & ((1 << DIGIT_BITS) - 1)).astype(F32)
    rows8 = jnp.concatenate([hi, lo, gt_ref[...], jnp.zeros((2, tm), F32)], axis=0).astype(BF16)
    sel = jnp.where(_iota((SUBLANES, 6 * LANES), 0)
                    == (_iota((SUBLANES, 6 * LANES), 1) >> (LANES.bit_length() - 1)),
                    1.0, 0.0).astype(BF16)
    cols = _dot_tn(rows8, sel)
    col = lambda k: cols[:, k * LANES:(k + 1) * LANES]
    la = col(0) * float(1 << DIGIT_BITS) + col(2)
    lb = col(1) * float(1 << DIGIT_BITS) + col(3)
    ga, gb = col(4), col(5)
    lane = _iota((tm, LANES), 1).astype(F32)
    mix = jnp.concatenate(
        [jnp.where(lane + float(q * LANES) == la, ga, 0.0)
         + jnp.where(lane + float(q * LANES) == lb, gb, 0.0)
         for q in range(_seg_local_rows(tm) // LANES)], axis=1).astype(BF16)
    y_lo, y_hi = _unpack_pairs(ybuf[slot])
    y = x_ref[...] + jnp.concatenate([_dot(mix, y_lo), _dot(mix, y_hi)], axis=1)
    if final_norm:
        ms = jnp.mean(y * y, axis=-1, keepdims=True)
        y = y * lax.rsqrt(ms + RMS_EPS) * gfin_ref[...]
    o_ref[...] = y


def _combine(x2d, ys, e2, rank2, gates2, seg, g_final, final_norm, tm):
    n, d = x2d.shape
    return pl.pallas_call(
        functools.partial(_combine_kernel, final_norm=final_norm),
        out_shape=jax.ShapeDtypeStruct((n, d), F32),
        grid_spec=pltpu.PrefetchScalarGridSpec(
            num_scalar_prefetch=4,
            grid=(n // tm,),
            in_specs=[pl.BlockSpec((tm, d), lambda j, *_: (j, 0)),
                      pl.BlockSpec((2, tm), lambda j, *_: (0, j)),
                      pl.BlockSpec((2, tm), lambda j, *_: (0, j)),
                      pl.BlockSpec((2, tm), lambda j, *_: (0, j)),
                      pl.BlockSpec(memory_space=pl.ANY),
                      pl.BlockSpec((1, d), lambda j, *_: (0, 0))],
            out_specs=pl.BlockSpec((tm, d), lambda j, *_: (j, 0)),
            scratch_shapes=[pltpu.VMEM((2, _seg_local_rows(tm), d // 2), U32),
                            pltpu.SemaphoreType.DMA((2,))]),
        compiler_params=_cparams(("arbitrary",)),
        name="moe_combine",
    )(*seg, x2d, e2, rank2, gates2, ys, g_final.reshape(1, d))


def _route_plan(tile_cnt, n_tok, rows, tm):
    n_tiles = n_tok // tm
    seg_cnt = (tile_cnt + SUBLANES - 1) // SUBLANES * SUBLANES
    counts = jnp.sum(seg_cnt, axis=0)
    padded = (counts + rows - 1) // rows * rows
    pend = jnp.cumsum(padded)
    pstart = pend - padded
    seg_dst = pstart[None, :] + jnp.cumsum(seg_cnt, axis=0) - seg_cnt
    seg_off = jnp.cumsum(seg_cnt, axis=1) - seg_cnt
    n_blocks = -(-(2 * n_tok + n_tiles * N_EXPERTS * SUBLANES) // rows) + N_EXPERTS
    starts = jnp.arange(n_blocks, dtype=I32) * rows
    block_e = jnp.minimum(jnp.sum((starts[:, None] >= pend[None, :]).astype(I32), axis=1),
                          N_EXPERTS - 1)
    first = jnp.concatenate([jnp.ones((1,), I32),
                             (block_e[1:] != block_e[:-1]).astype(I32)])
    n_used = (pend[-1] // rows).astype(I32).reshape(1)
    bidx = jnp.arange(n_blocks, dtype=I32)
    w_slot = (jnp.cumsum(first) - 1) % 2
    later_first = (bidx[None, :] > bidx[:, None]) & (first[None, :] == 1) & (bidx[None, :] < n_used)
    next_e = jnp.where(jnp.any(later_first, axis=1),
                       block_e[jnp.argmax(later_first, axis=1)], -1)
    blocks = (block_e.astype(I32), first, n_used, w_slot.astype(I32), next_e.astype(I32))
    seg = tuple(a.reshape(-1).astype(I32)
                for a in (seg_cnt, seg_off, seg_dst, jnp.sum(seg_cnt, axis=1)))
    tails = ((padded - counts).astype(I32), (pstart + counts).astype(I32))
    return seg, tails, blocks, n_blocks * rows


def _moe(x2d, route, w_gate, w_up, w_down, layer, g_final, final_norm, tm, rows=ROUTE_ROWS):
    e2, gates2, rank2, cnt, hn16 = route
    seg, tails, blocks, n_rows = _route_plan(cnt[:, :, 0].astype(I32), e2.shape[1], rows, tm)
    xs = _dispatch(hn16, e2, rank2, seg, tails, blocks[2], n_rows, rows, tm)
    ys = _expert_ffn(xs, blocks, w_gate, w_up, w_down, layer, rows)
    return _combine(x2d, ys, e2, rank2, gates2, seg, g_final, final_norm, tm)


def kernel(x, norm_mix, norm_ffn, norm_final, ab_w_in, rw_mu, rw_w0, rw_w2, rw_a0, rw_a2, rw_g2, rw_k_k, rw_k_a, rw_r_k, rw_ln_w, rw_ln_b, lru_conv_w, lru_conv_b, lru_w_a, lru_b_a, lru_w_x, lru_b_x, lru_lambda, ab_w_out, c_w_in, c_lower_bound, c_norm_w, c_w_out, moe_w_group, moe_b_group, moe_w_expert, moe_b_expert, moe_w_gate, moe_w_up, moe_w_down):
    bsz, t, d = x.shape
    depth = norm_mix.shape[0]
    n = bsz * t
    lbs = jnp.cumsum(jax.nn.softmax(c_lower_bound.astype(F32), axis=0), axis=0)
    lbs = lbs - lbs[0]
    x2d = x.reshape(n, d)
    for layer in range(depth):
        j = layer // 2
        if layer % 2 == 0:
            rw_cols = rw_mu.shape[1]
            lru_cols = ab_w_in.shape[2] - rw_cols
            width = rw_w0.shape[1]
            p_a, p_b = _norm_matmul(x2d, norm_mix[layer], ab_w_in[j], (rw_cols, lru_cols),
                                    (F32, F32))
            prep = _rwkv_prep(p_a.reshape(bsz, t, rw_cols), rw_mu[j], rw_w0[j], rw_w2[j],
                              rw_a0[j], rw_a2[j], rw_g2[j], rw_k_k[j], rw_k_a[j],
                              rw_r_k[j].reshape(-1))
            ya = _rwkv_scan(*prep, rw_ln_w[j], rw_ln_b[j])
            yb = _lru(p_b.reshape(bsz, t, lru_cols), lru_conv_w[j], lru_conv_b[j], lru_w_a[j],
                      lru_b_a[j], lru_w_x[j], lru_b_x[j], lru_lambda[j])
            ys = [ya.reshape(n, width), yb.reshape(n, -1)]
            ws = [ab_w_out[j][:width], ab_w_out[j][width:]]
        else:
            hw = c_norm_w.shape[1]
            q, f, i_, g = _norm_matmul(x2d, norm_mix[layer], c_w_in[j], (hw,) * 4,
                                       (BF16, F32, BF16, BF16))
            shp = (bsz, t, hw)
            o = _gla(q.reshape(shp), f.reshape(shp), i_.reshape(shp), g.reshape(shp),
                     lbs[layer], c_norm_w[j])
            ys = [o.reshape(n, hw)]
            ws = [c_w_out[j]]
        tm = min(ROUTE_TILE, n)
        x2d = _proj_residual(x2d, ys, ws)
        route = _router(x2d, norm_ffn[layer], moe_w_group[layer], moe_b_group[layer],
                        moe_w_expert[layer], moe_b_expert[layer], tm)
        x2d = _moe(x2d, route, moe_w_gate, moe_w_up, moe_w_down, layer, norm_final,
                   layer == depth - 1, tm)
    return x2d.reshape(bsz, t, d)
```

```python
import functools

import jax
import jax.numpy as jnp
from jax import lax
from jax.experimental import pallas as pl
from jax.experimental.pallas import tpu as pltpu

F32 = jnp.float32
BF16 = jnp.bfloat16
I32 = jnp.int32
U32 = jnp.uint32

RMS_EPS = 1e-6
RWKV_GN_EPS = 64e-5
LRU_C = 8.0
CHUNK = 64
CHUNK_SHIFT = CHUNK.bit_length() - 1
HEAD64 = 64
HEAD_SHIFT = HEAD64.bit_length() - 1
LANES = 128
SUBLANES = 8
N_GROUPS = 4
EXPERTS_PER_GROUP = 8
N_EXPERTS = N_GROUPS * EXPERTS_PER_GROUP
ROUTE_ROWS = 512
ROUTE_TILE = 512
DIGIT_BITS = 8
VMEM_LIMIT = 56 * 1024 * 1024


def _cparams(sem):
    return pltpu.CompilerParams(dimension_semantics=sem, vmem_limit_bytes=VMEM_LIMIT)


def _sigmoid(x):
    return 0.5 * jnp.tanh(0.5 * x) + 0.5


def _pack_pairs(x):
    w = x.shape[1] // 2
    lo = lax.bitcast_convert_type(x[:, :w].astype(BF16).astype(F32), U32)
    hi = lax.bitcast_convert_type(x[:, w:].astype(BF16).astype(F32), U32)
    return (lo >> 16) | hi


def _unpack_pairs(p):
    lo = lax.bitcast_convert_type(p << 16, F32).astype(BF16)
    hi = lax.bitcast_convert_type(p & jnp.uint32(0xFFFF0000), F32).astype(BF16)
    return lo, hi


def _softplus(x):
    return jnp.maximum(x, 0.0) + jnp.log(1.0 + jnp.exp(-jnp.abs(x)))


def _silu(x):
    h = 0.5 * x
    return h * jnp.tanh(h) + h


def _gelu_tanh(x):
    return 0.5 * x * (1.0 + jnp.tanh(0.7978845608028654 * (x + 0.044715 * x * x * x)))


def _dot(a, b):
    return jnp.dot(a, b, preferred_element_type=F32)


def _dot_nt(a, b):
    return lax.dot_general(a, b, (((1,), (1,)), ((), ())), preferred_element_type=F32)


def _dot_tn(a, b):
    return lax.dot_general(a, b, (((0,), (0,)), ((), ())), preferred_element_type=F32)


def _split2(x):
    hi = x.astype(BF16)
    lo = (x - hi.astype(F32)).astype(BF16)
    return hi, lo


def _dot_exact_rhs(x, m_bf16):
    hi, lo = _split2(x)
    return _dot(hi, m_bf16) + _dot(lo, m_bf16)


def _dot_exact_lhs(m_bf16, x):
    hi, lo = _split2(x)
    return _dot(m_bf16, hi) + _dot(m_bf16, lo)


def _dot_split(a, w_hi, w_lo):
    a_hi, a_lo = _split2(a)
    return _dot(a_hi, w_hi) + (_dot(a_lo, w_hi) + _dot(a_hi, w_lo))


def _hi_lo(w):
    hi = w.astype(BF16)
    return hi, (w - hi.astype(F32)).astype(BF16)


def _iota(shape, dim):
    return lax.broadcasted_iota(I32, shape, dim)


def _norm_matmul_kernel(x_ref, g_ref, w_ref, *o_refs, splits):
    x = x_ref[...]
    ms = jnp.mean(x * x, axis=-1, keepdims=True)
    y = (x * lax.rsqrt(ms + RMS_EPS) * g_ref[...]).astype(BF16)
    off = 0
    for o_ref, n in zip(o_refs, splits):
        o_ref[...] = _dot(y, w_ref[:, off:off + n]).astype(o_ref.dtype)
        off += n


def _norm_matmul(x2d, g, w, splits, out_dtypes, tm=1024):
    n, d = x2d.shape
    tm = min(tm, n)
    ncols = w.shape[1]
    assert sum(splits) == ncols and n % tm == 0 and len(out_dtypes) == len(splits)
    return pl.pallas_call(
        functools.partial(_norm_matmul_kernel, splits=splits),
        out_shape=[jax.ShapeDtypeStruct((n, s), dt) for s, dt in zip(splits, out_dtypes)],
        grid=(n // tm,),
        in_specs=[pl.BlockSpec((tm, d), lambda i: (i, 0)),
                  pl.BlockSpec((1, d), lambda i: (0, 0)),
                  pl.BlockSpec((d, ncols), lambda i: (0, 0))],
        out_specs=[pl.BlockSpec((tm, s), lambda i: (i, 0)) for s in splits],
        compiler_params=_cparams(("parallel",)),
        name="norm_matmul",
    )(x2d, g.reshape(1, d), w.astype(BF16))


def _proj_residual_kernel(*refs, n_in):
    x_ref = refs[0]
    y_refs = refs[1:1 + n_in]
    w_refs = refs[1 + n_in:1 + 2 * n_in]
    o_ref = refs[1 + 2 * n_in]
    acc = x_ref[...]
    for y_ref, w_ref in zip(y_refs, w_refs):
        acc = acc + _dot(y_ref[...].astype(BF16), w_ref[...])
    o_ref[...] = acc


def _proj_residual(x2d, ys, ws, tm=1024):
    n, d = x2d.shape
    tm = min(tm, n)
    n_in = len(ys)
    in_specs = [pl.BlockSpec((tm, d), lambda i: (i, 0))]
    in_specs += [pl.BlockSpec((tm, y.shape[1]), lambda i: (i, 0)) for y in ys]
    in_specs += [pl.BlockSpec(w.shape, lambda i: (0, 0)) for w in ws]
    return pl.pallas_call(
        functools.partial(_proj_residual_kernel, n_in=n_in),
        out_shape=jax.ShapeDtypeStruct((n, d), F32),
        grid=(n // tm,),
        in_specs=in_specs,
        out_specs=pl.BlockSpec((tm, d), lambda i: (i, 0)),
        compiler_params=_cparams(("parallel",)),
        name="proj_residual",
    )(x2d, *ys, *[w.astype(BF16) for w in ws])


def _router_kernel(x_ref, g_ref, w_ref, b_ref, upper_ref, e_o, gate_o, rank_o, cnt_o, hn_o):
    _route_tile(x_ref[...], g_ref, w_ref, b_ref, upper_ref, e_o, gate_o, rank_o, cnt_o, hn_o)


def _router(x2d, g_ffn, w_group, b_group, w_expert, b_expert, tm):
    n, d = x2d.shape
    wt = jnp.zeros((d, LANES), F32)
    wt = wt.at[:, 0:N_GROUPS].set(w_group).at[:, 8:8 + N_EXPERTS].set(w_expert)
    wt = jnp.stack(_hi_lo(wt))
    bt = jnp.zeros((LANES,), F32)
    bt = bt.at[0:N_GROUPS].set(b_group).at[8:8 + N_EXPERTS].set(b_expert)
    bt = jnp.broadcast_to(bt[:, None], (LANES, LANES))
    ti = jnp.arange(tm, dtype=I32)
    upper = (ti[:, None] < ti[None, :]).astype(BF16)
    tok2 = pl.BlockSpec((2, tm), lambda i: (0, i))
    return pl.pallas_call(
        _router_kernel,
        out_shape=[jax.ShapeDtypeStruct((2, n), I32),
                   jax.ShapeDtypeStruct((2, n), F32),
                   jax.ShapeDtypeStruct((2, n), I32),
                   jax.ShapeDtypeStruct((n // tm, N_EXPERTS, LANES), F32),
                   jax.ShapeDtypeStruct((n, d), BF16)],
        grid=(n // tm,),
        in_specs=[pl.BlockSpec((tm, d), lambda i: (i, 0)),
                  pl.BlockSpec((1, d), lambda i: (0, 0)),
                  pl.BlockSpec((2, d, LANES), lambda i: (0, 0, 0)),
                  pl.BlockSpec((LANES, LANES), lambda i: (0, 0)),
                  pl.BlockSpec((tm, tm), lambda i: (0, 0))],
        out_specs=[tok2, tok2, tok2,
                   pl.BlockSpec((1, N_EXPERTS, LANES), lambda i: (i, 0, 0)),
                   pl.BlockSpec((tm, d), lambda i: (i, 0))],
        compiler_params=_cparams(("parallel",)),
        name="moe_router",
    )(x2d, g_ffn.reshape(1, d), wt, bt, upper)


def _rwkv_prep_kernel(p_ref, mu_ref, w0_ref, a0_ref, kk_s_ref, ka_ref, rk_ref,
                      wcomb_ref, g2_ref, tril_ref,
                      rt_o, kkt_o, kh_o, bh_o, v_o, g_o, bonus_o, gc_o,
                      prev_ref, *, width):
    tm = p_ref.shape[1]

    @pl.when(pl.program_id(1) == 0)
    def _():
        prev_ref[...] = jnp.zeros_like(prev_ref)

    p = p_ref[0]
    rolled = pltpu.roll(p, 1, axis=0)
    prev = jnp.where(_iota(p.shape, 0) == 0, prev_ref[...], rolled)
    prev_ref[...] = p[tm - 1:tm, :]
    ps = p + (prev - p) * mu_ref[...]

    r = ps[:, 0:width]
    k = ps[:, width:2 * width]
    v = ps[:, 2 * width:3 * width]
    lowrank = ps[:, 3 * width:3 * width + LANES]
    gl = ps[:, 3 * width + LANES:3 * width + 2 * LANES]

    lane = _iota(lowrank.shape, 1)
    lr_in = jnp.where(lane < HEAD64, jnp.tanh(lowrank), lowrank)
    t12 = _dot_split(lr_in, wcomb_ref[0], wcomb_ref[1])
    wlog = -_softplus(-(w0_ref[...] + t12[:, :width])) - 0.5
    lw = -jnp.exp(wlog)
    a = _sigmoid(a0_ref[...] + t12[:, width:])
    g = _dot_split(_sigmoid(gl), g2_ref[0], g2_ref[1])

    seg = jnp.where((_iota((LANES, LANES), 0) >> HEAD_SHIFT)
                    == (_iota((LANES, LANES), 1) >> HEAD_SHIFT),
                    1.0, 0.0).astype(BF16)

    def head_sums(x):
        return jnp.concatenate([_dot_exact_rhs(x[:, q * LANES:(q + 1) * LANES], seg)
                                for q in range(width // LANES)], axis=1)

    kk = k * kk_s_ref[...]
    nrm = jnp.sqrt(head_sums(kk * kk))
    kk = kk / jnp.maximum(nrm, 1e-12)
    k2 = k * (1.0 + (a - 1.0) * ka_ref[...])
    bonus = head_sums(r * k2 * rk_ref[...]) * v

    b = kk * a

    n_chunk = tm // CHUNK
    cum = _dot_exact_lhs(tril_ref[...], lw)
    cum3 = cum.reshape(n_chunk, CHUNK, width)
    cend = cum3[:, CHUNK - 1:CHUNK, :]
    e_neg = jnp.exp(-cum)

    rt_o[0] = (r * jnp.exp(cum)).astype(BF16)
    kkt_o[0] = (kk * jnp.exp(cum - lw)).astype(BF16)
    kh_o[0] = (k2 * e_neg).astype(BF16)
    bh_o[0] = (b * e_neg).astype(BF16)
    v_o[0] = v.astype(BF16)
    g_o[0] = g.astype(BF16)
    bonus_o[0] = bonus.astype(BF16)
    gc_o[0] = jnp.exp(cend)


def _rwkv_prep(p_a, mu, w0, w2, a0, a2, g2, k_k, k_a, r_k, tm=512):
    b, t, cols = p_a.shape
    width = w0.shape[0]
    rank = w2.shape[0]
    assert rank == HEAD64 and a2.shape[0] == HEAD64 and g2.shape[0] == LANES
    zeros = jnp.zeros((rank, width), F32)
    wcomb = jnp.concatenate([jnp.concatenate([w2, zeros], 1),
                             jnp.concatenate([zeros, a2], 1)], 0)
    row = lambda a_: a_.reshape(1, -1)
    const = lambda shp: pl.BlockSpec(shp, lambda i, j: (0,) * len(shp))
    tok = pl.BlockSpec((1, tm, width), lambda i, j: (i, j, 0))
    n_chunk = tm // CHUNK
    ti = jnp.arange(tm, dtype=I32)
    tril_bd = (((ti[:, None] >> CHUNK_SHIFT) == (ti[None, :] >> CHUNK_SHIFT))
               & (ti[None, :] <= ti[:, None])).astype(BF16)
    return pl.pallas_call(
        functools.partial(_rwkv_prep_kernel, width=width),
        out_shape=([jax.ShapeDtypeStruct((b, t, width), BF16)] * 7
                   + [jax.ShapeDtypeStruct((b, t // CHUNK, 1, width), F32)]),
        grid=(b, t // tm),
        in_specs=[pl.BlockSpec((1, tm, cols), lambda i, j: (i, j, 0)),
                  const((1, cols)), const((1, width)), const((1, width)), const((1, width)),
                  const((1, width)), const((1, width)),
                  const((2, LANES, 2 * width)), const((2, LANES, width)), const((tm, tm))],
        out_specs=[tok] * 7 + [pl.BlockSpec((1, n_chunk, 1, width), lambda i, j: (i, j, 0, 0))],
        scratch_shapes=[pltpu.VMEM((1, cols), F32)],
        compiler_params=_cparams(("parallel", "arbitrary")),
        name="rwkv_prep",
    )(p_a, row(mu), row(w0), row(a0), row(k_k), row(k_a), row(r_k),
      jnp.stack(_hi_lo(wcomb)), jnp.stack(_hi_lo(g2)), tril_bd)


def _rwkv_scan_kernel(rt_ref, kkt_ref, kh_ref, bh_ref, v_ref, g_ref, bonus_ref,
                      gc_ref, lnw_ref, lnb_ref, o_ref, st_ref, *, n_batch, n_pairs):
    c = CHUNK

    @pl.when(pl.program_id(0) == 0)
    def _():
        st_ref[...] = jnp.zeros_like(st_ref)

    rr = _iota((LANES, LANES), 0)
    cc = _iota((LANES, LANES), 1)
    mask_bd = jnp.where((rr >> HEAD_SHIFT) == (cc >> HEAD_SHIFT), 1.0, 0.0)
    mask_bd16 = mask_bd.astype(BF16)
    t_i = _iota((c, LANES), 0)
    s_i = _iota((c, LANES), 1) & (c - 1)
    strict = jnp.where(s_i < t_i, 1.0, 0.0)
    incl = jnp.where(s_i <= t_i, 1.0, 0.0)
    eye_ss = jnp.where(s_i == t_i, 1.0, 0.0)

    def bd16(x):
        return jnp.concatenate([x, x], axis=0) * mask_bd16

    def b16(xs):
        return [x.astype(BF16) for x in xs]

    chains = [(bi, p) for bi in range(n_batch) for p in range(n_pairs)]
    n_ch = len(chains)
    idx = range(n_ch)

    def ld(ref):
        return [ref[bi, :, p * LANES:(p + 1) * LANES] for bi, p in chains]

    rt, kkt, kh, bh, v = (ld(ref) for ref in (rt_ref, kkt_ref, kh_ref, bh_ref, v_ref))

    lhs = [jnp.concatenate([kkt[i], rt[i]], axis=0) for i in idx]
    rhs = [jnp.concatenate([bd16(kh[i]), bd16(bh[i])], axis=0) for i in idx]
    gm = [_dot_nt(lhs[i], rhs[i]) for i in idx]
    l_kv = [gm[i][:c, :LANES] * strict for i in idx]
    xp = [-(gm[i][:c, LANES:] * strict) for i in idx]
    pm = [gm[i][c:, :LANES] * incl for i in idx]
    nqm = [-(gm[i][c:, LANES:] * incl) for i in idx]

    tinv = [eye_ss + xp[i] for i in idx]
    xp16 = b16(xp)
    xpb = [bd16(x) for x in xp16]
    for _ in range(c.bit_length() - 2):
        xp16 = b16([_dot(xp16[i], xpb[i]) for i in idx])
        xpb = [bd16(x) for x in xp16]
        t16 = b16(tinv)
        tinv = [tinv[i] + _dot(t16[i], xpb[i]) for i in idx]
    tinv16 = b16(tinv)

    v_bd = [bd16(x) for x in v]
    l_kv16 = b16(l_kv)
    w16 = b16([_dot(l_kv16[i], v_bd[i]) for i in idx])
    tkw = [_dot(tinv16[i], jnp.concatenate([bd16(kkt[i]), bd16(w16[i])], axis=1)) for i in idx]
    tk16 = b16([x[:, :LANES] for x in tkw])

    st = [st_ref[i] for i in idx]
    st16 = b16(st)
    u = [_dot_nt(tk16[i], st16[i]) + tkw[i][:, LANES:] for i in idx]
    u16 = b16(u)
    pq16 = [jnp.concatenate([pm[i], nqm[i]], axis=1).astype(BF16) for i in idx]
    vu = [jnp.concatenate([v_bd[i], bd16(u16[i])], axis=0) for i in idx]
    y = [_dot_nt(rt[i], st16[i]) + _dot(pq16[i], vu[i]) for i in idx]
    upd = [_dot_tn(jnp.concatenate([v[i], -u16[i]], axis=0),
                   jnp.concatenate([kh[i], bh[i]], axis=0)) for i in idx]
    for i, (bi, p) in enumerate(chains):
        st_ref[i] = (st[i] + upd[i] * mask_bd) * gc_ref[bi, 0, :, p * LANES:(p + 1) * LANES]

    ys = jnp.concatenate(y, axis=0)
    mean = _dot_exact_rhs(ys, mask_bd16) * (1.0 / HEAD64)
    d = ys - mean
    var = _dot_exact_rhs(d * d, mask_bd16) * (1.0 / HEAD64)
    dn = d * lax.rsqrt(var + RWKV_GN_EPS)
    for i, (bi, p) in enumerate(chains):
        sl = slice(p * LANES, (p + 1) * LANES)
        yn = dn[i * c:(i + 1) * c] * lnw_ref[:, sl] + lnb_ref[:, sl]
        o_ref[bi, :, sl] = ((yn + bonus_ref[bi, :, sl]) * g_ref[bi, :, sl]).astype(o_ref.dtype)


def _rwkv_scan(rt, kkt, kh, bh, v, g, bonus, gc, ln_w, ln_b):
    bsz, t, width = rt.shape
    n_pairs = width // LANES
    tok = pl.BlockSpec((bsz, CHUNK, width), lambda j: (0, j, 0))
    const = pl.BlockSpec((1, width), lambda j: (0, 0))
    return pl.pallas_call(
        functools.partial(_rwkv_scan_kernel, n_batch=bsz, n_pairs=n_pairs),
        out_shape=jax.ShapeDtypeStruct((bsz, t, width), BF16),
        grid=(t // CHUNK,),
        in_specs=[tok] * 7 + [pl.BlockSpec((bsz, 1, 1, width), lambda j: (0, j, 0, 0)),
                              const, const],
        out_specs=tok,
        scratch_shapes=[pltpu.VMEM((bsz * n_pairs, LANES, LANES), F32)],
        compiler_params=_cparams(("arbitrary",)),
        name="rwkv_scan",
    )(rt, kkt, kh, bh, v, g, bonus, gc, ln_w.reshape(1, -1), ln_b.reshape(1, -1))


def _lru_kernel(p_ref, cw_ref, cb_ref, wg_ref, ba_ref, bx_ref, lam_ref, o_ref,
                xcarry_ref, hcarry_ref, a_s, u_s, *, width):
    tm = p_ref.shape[1]

    @pl.when(pl.program_id(1) == 0)
    def _():
        xcarry_ref[...] = jnp.zeros_like(xcarry_ref)
        hcarry_ref[...] = jnp.zeros_like(hcarry_ref)

    gate = p_ref[0, :, 0:width]
    xb = p_ref[0, :, width:2 * width]
    carry8 = xcarry_ref[...]
    row8 = _iota((8, width), 0)

    def shifted(s):
        rolled = pltpu.roll(xb, s, axis=0)
        first = jnp.where(row8 < s, pltpu.roll(carry8, s, axis=0), rolled[0:8])
        return jnp.concatenate([first, rolled[8:]], axis=0)

    xc = (cw_ref[0:1, :] * shifted(3) + cw_ref[1:2, :] * shifted(2)
          + cw_ref[2:3, :] * shifted(1) + cw_ref[3:4, :] * xb + cb_ref[...])
    xcarry_ref[...] = xb[tm - 8:tm, :]

    gates = _dot(xc.astype(BF16), wg_ref[...])
    ig = _sigmoid(gates[:, width:] + bx_ref[...])
    half_rate = (-0.5 * LRU_C) * _softplus(-lam_ref[...])
    log_a = half_rate * jnp.tanh(0.5 * (gates[:, :width] + ba_ref[...])) + half_rate
    a = jnp.exp(log_a)
    a_s[...] = a
    u_s[...] = jnp.sqrt(1.0 - a * a) * ig * xc

    m1 = row8 >= 1
    m2 = row8 >= 2
    m4 = row8 >= 4

    def body(i, h):
        off = pl.multiple_of(i * 8, 8)
        a8 = a_s[pl.ds(off, 8), :]
        u8 = u_s[pl.ds(off, 8), :]
        for s, m in ((1, m1), (2, m2), (4, m4)):
            u_sh = jnp.where(m, pltpu.roll(u8, s, axis=0), 0.0)
            a_sh = jnp.where(m, pltpu.roll(a8, s, axis=0), 1.0)
            u8 = u8 + a8 * u_sh
            a8 = a8 * a_sh
        h8 = u8 + a8 * h
        u_s[pl.ds(off, 8), :] = h8
        return jnp.broadcast_to(h8[7:8, :], (8, width))

    h_last = lax.fori_loop(0, tm // 8, body, hcarry_ref[...])
    hcarry_ref[...] = h_last
    o_ref[0] = (u_s[...] * _gelu_tanh(gate)).astype(o_ref.dtype)


def _block_diag(w):
    nb, di, do = w.shape
    eye = jnp.eye(nb, dtype=w.dtype)
    return (eye[:, None, :, None] * w[:, :, None, :]).reshape(nb * di, nb * do)


def _lru(p_b, conv_w, conv_b, w_a, b_a, w_x, b_x, lam, tm=1024):
    b, t, cols = p_b.shape
    width = cols // 2
    wg = jnp.concatenate([_block_diag(w_a), _block_diag(w_x)], axis=1).astype(BF16)
    row = lambda a_: a_.reshape(1, -1)
    const = lambda shp: pl.BlockSpec(shp, lambda i, j: (0, 0))
    return pl.pallas_call(
        functools.partial(_lru_kernel, width=width),
        out_shape=jax.ShapeDtypeStruct((b, t, width), BF16),
        grid=(b, t // tm),
        in_specs=[pl.BlockSpec((1, tm, cols), lambda i, j: (i, j, 0)),
                  const(conv_w.shape), const((1, width)), const((width, 2 * width)),
                  const((1, width)), const((1, width)), const((1, width))],
        out_specs=pl.BlockSpec((1, tm, width), lambda i, j: (i, j, 0)),
        scratch_shapes=[pltpu.VMEM((8, width), F32), pltpu.VMEM((8, width), F32),
                        pltpu.VMEM((tm, width), F32), pltpu.VMEM((tm, width), F32)],
        compiler_params=_cparams(("parallel", "arbitrary")),
        name="rglru",
    )(p_b, conv_w, row(conv_b), wg, row(b_a), row(b_x), row(lam))


def _gla_kernel(q_ref, f_ref, i_ref, g_ref, lb_ref, nw_ref, o_ref, st_ref, *, n_batch, n_heads):
    c = CHUNK

    @pl.when(pl.program_id(0) == 0)
    def _():
        st_ref[...] = jnp.zeros_like(st_ref)

    trilf = jnp.where(_iota((c, c), 1) <= _iota((c, c), 0), 1.0, 0.0)
    tril16 = trilf.astype(BF16)
    mid = c // 2 - 1
    lb = lb_ref[...]
    fg_c0 = 0.5 + 0.5 * lb
    fg_c1 = 0.5 - 0.5 * lb

    qm, km, v16, gc, em, ecm = [], [], [], [], [], []
    for bi in range(n_batch):
        q = _silu(q_ref[bi].astype(F32))
        fg = fg_c0 + fg_c1 * jnp.tanh(0.5 * f_ref[bi])
        k = 1.0 - fg
        cum = _dot_exact_lhs(tril16, jnp.log(fg))
        cum_c = cum[c - 1:c, :]
        cum_m = cum[mid:mid + 1, :]
        qm.append((q * jnp.exp(cum - cum_m)).astype(BF16))
        km.append((k * jnp.exp(cum_m - cum)).astype(BF16))
        v16.append(i_ref[bi].astype(BF16))
        gc.append(jnp.exp(cum_c))
        em.append(jnp.exp(cum_m))
        ecm.append(jnp.exp(cum_c - cum_m))

    chains = [(bi, h) for bi in range(n_batch) for h in range(n_heads)]
    idx = range(len(chains))

    def hs(xs):
        return [xs[bi][:, h * LANES:(h + 1) * LANES] for bi, h in chains]

    qm_c, km_c, v_c, gc_c, em_c, ecm_c = hs(qm), hs(km), hs(v16), hs(gc), hs(em), hs(ecm)
    scores = [(_dot_nt(qm_c[i], km_c[i]) * trilf).astype(BF16) for i in idx]
    st = [st_ref[i] for i in idx]
    st16 = [(st[i] * em_c[i]).astype(BF16) for i in idx]
    o = [_dot(scores[i], v_c[i]) + _dot_nt(qm_c[i], st16[i]) for i in idx]
    for i in idx:
        st_ref[i] = st[i] * gc_c[i] + _dot_tn(v_c[i], km_c[i]) * ecm_c[i]

    for i, (bi, h) in enumerate(chains):
        sl = slice(h * LANES, (h + 1) * LANES)
        on = o[i] * lax.rsqrt(jnp.mean(o[i] * o[i], axis=-1, keepdims=True) + RMS_EPS)
        gate = _silu(g_ref[bi, :, sl].astype(F32))
        o_ref[bi, :, sl] = (on * nw_ref[:, sl] * gate).astype(o_ref.dtype)


def _gla(q, f, i, g, lower_bound, norm_w):
    bsz, t, width = q.shape
    n_heads = width // LANES
    tok = pl.BlockSpec((bsz, CHUNK, width), lambda j: (0, j, 0))
    const = pl.BlockSpec((1, width), lambda j: (0, 0))
    return pl.pallas_call(
        functools.partial(_gla_kernel, n_batch=bsz, n_heads=n_heads),
        out_shape=jax.ShapeDtypeStruct((bsz, t, width), BF16),
        grid=(t // CHUNK,),
        in_specs=[tok] * 4 + [const, const],
        out_specs=tok,
        scratch_shapes=[pltpu.VMEM((bsz * n_heads, LANES, LANES), F32)],
        compiler_params=_cparams(("arbitrary",)),
        name="hgrn2_gla",
    )(q, f, i, g, lower_bound.reshape(1, -1), norm_w.reshape(1, -1))


def _route_tile(x, g_ref, w_ref, b_ref, upper_ref, e_o, gate_o, rank_o, cnt_o, hn_o):
    ms = jnp.mean(x * x, axis=-1, keepdims=True)
    hn = x * lax.rsqrt(ms + RMS_EPS) * g_ref[...]
    hn_o[...] = hn.astype(BF16)
    lt = _dot_split(hn, w_ref[0], w_ref[1]).T + b_ref[:, 0:1]
    tm = x.shape[0]
    gl = lt[0:8, :]
    row8 = _iota((8, tm), 0)
    gl = jnp.where(row8 < N_GROUPS, gl, -jnp.inf)
    gmax = jnp.max(gl, axis=0, keepdims=True)
    g_sel = jnp.min(jnp.where(gl == gmax, row8, 8), axis=0, keepdims=True)
    g_gate = 1.0 / jnp.sum(jnp.exp(gl - gmax), axis=0, keepdims=True)

    el = jnp.zeros((EXPERTS_PER_GROUP, tm), F32)
    for gi in range(N_GROUPS):
        lo = 8 + gi * EXPERTS_PER_GROUP
        el = jnp.where(g_sel == gi, lt[lo:lo + EXPERTS_PER_GROUP, :], el)
    m1 = jnp.max(el, axis=0, keepdims=True)
    i1 = jnp.min(jnp.where(el == m1, row8, 8), axis=0, keepdims=True)
    el2 = jnp.where(row8 == i1, -jnp.inf, el)
    m2 = jnp.max(el2, axis=0, keepdims=True)
    i2 = jnp.min(jnp.where(el2 == m2, row8, 8), axis=0, keepdims=True)
    e2 = jnp.exp(m2 - m1)
    inv = 1.0 / (1.0 + e2)
    ea = g_sel * EXPERTS_PER_GROUP + i1
    eb = g_sel * EXPERTS_PER_GROUP + i2
    e_o[...] = jnp.concatenate([ea, eb], axis=0)
    gate_o[...] = jnp.concatenate([g_gate * inv, g_gate * e2 * inv], axis=0)

    erow = _iota((N_EXPERTS, tm), 0)
    oh_a = jnp.where(erow == ea, 1.0, 0.0)
    oh_b = jnp.where(erow == eb, 1.0, 0.0)
    pre_a = _dot(oh_a.astype(BF16), upper_ref[...])
    pre_b = _dot(oh_b.astype(BF16), upper_ref[...])
    cnt_a = jnp.sum(oh_a, axis=1, keepdims=True)
    cnt_b = jnp.sum(oh_b, axis=1, keepdims=True)
    rank_a = jnp.sum(oh_a * pre_a, axis=0, keepdims=True)
    rank_b = jnp.sum(oh_b * (pre_b + cnt_a), axis=0, keepdims=True)
    rank_o[...] = jnp.concatenate([rank_a, rank_b], axis=0).astype(I32)
    cnt_o[0] = jnp.broadcast_to(cnt_a + cnt_b, (N_EXPERTS, LANES))


def _seg_local_rows(tm):
    return 2 * tm + N_EXPERTS * SUBLANES


def _segment_copies(cnt_ref, off_ref, dst_ref, base, local, remote, sem, to_remote, wait):
    def body(e, carry):
        c = pl.multiple_of(cnt_ref[base + e], SUBLANES)

        @pl.when(c > 0)
        def _():
            off = 0 if off_ref is None else pl.multiple_of(off_ref[base + e], SUBLANES)
            loc = local.at[pl.ds(off, c), :]
            rem = remote.at[pl.ds(pl.multiple_of(dst_ref[base + e], SUBLANES), c), :]
            cp = (pltpu.make_async_copy(loc, rem, sem) if to_remote
                  else pltpu.make_async_copy(rem, loc, sem))
            if wait:
                cp.wait()
            else:
                cp.start()
        return carry
    lax.fori_loop(0, N_EXPERTS, body, 0, unroll=4)


def _local_index(e_ref, rank_ref, off_ref, tile):
    e = e_ref[...]
    lidx = rank_ref[...]
    for ex in range(N_EXPERTS):
        lidx = lidx + jnp.where(e == ex, off_ref[tile * N_EXPERTS + ex], 0)
    return lidx


def _segment_wait(tot_ref, tile, local, remote, sem, to_remote):
    tot = pl.multiple_of(tot_ref[tile], SUBLANES)

    @pl.when(tot > 0)
    def _():
        loc = local.at[pl.ds(0, tot), :]
        rem = remote.at[pl.ds(0, tot), :]
        cp = (pltpu.make_async_copy(loc, rem, sem) if to_remote
              else pltpu.make_async_copy(rem, loc, sem))
        cp.wait()


def _dispatch_kernel(cnt_ref, off_ref, dst_ref, tot_ref, tcnt_ref, tdst_ref, nused_ref,
                     hn_ref, e_ref, rank_ref, xs_out, sbuf, zbuf, sems, zsem):
    j = pl.program_id(0)
    nb = pl.num_programs(0)
    tm = hn_ref.shape[0]
    slot = lax.rem(j, 2)

    def copies(tile, s, wait):
        if wait:
            _segment_wait(tot_ref, tile, sbuf.at[s], xs_out, sems.at[s], True)
        else:
            _segment_copies(cnt_ref, off_ref, dst_ref, tile * N_EXPERTS, sbuf.at[s], xs_out,
                            sems.at[s], True, False)

    @pl.when(j == 0)
    def _():
        zbuf[...] = jnp.zeros_like(zbuf)
        rows = zbuf.shape[0]
        n_blocks = xs_out.shape[0] // rows
        for wait in (False, True):
            _segment_copies(tcnt_ref, None, tdst_ref, 0, zbuf, xs_out, zsem, True, wait)

            def body(b, carry):
                start = pl.multiple_of(b * rows, SUBLANES)
                cp = pltpu.make_async_copy(zbuf, xs_out.at[pl.ds(start, rows), :], zsem)
                if wait:
                    cp.wait()
                else:
                    cp.start()
                return carry
            lax.fori_loop(nused_ref[0], n_blocks, body, 0)

    @pl.when(j >= 2)
    def _():
        copies(j - 2, slot, True)

    r = _iota((_seg_local_rows(tm), tm), 0)
    lidx = _local_index(e_ref, rank_ref, off_ref, j)
    perm = jnp.where((r == lidx[0:1, :]) | (r == lidx[1:2, :]), 1.0, 0.0).astype(BF16)
    sbuf[slot] = _pack_pairs(_dot(perm, hn_ref[...]))
    copies(j, slot, False)

    @pl.when(j == nb - 1)
    def _():
        copies(j, slot, True)

    @pl.when((j == nb - 1) & (nb >= 2))
    def _():
        copies(j - 1, 1 - slot, True)


def _dispatch(hn16, e2, rank2, seg, tails, n_used, n_rows, rows, tm):
    n, d = hn16.shape
    return pl.pallas_call(
        _dispatch_kernel,
        out_shape=jax.ShapeDtypeStruct((n_rows, d // 2), U32),
        grid_spec=pltpu.PrefetchScalarGridSpec(
            num_scalar_prefetch=7,
            grid=(n // tm,),
            in_specs=[pl.BlockSpec((tm, d), lambda j, *_: (j, 0)),
                      pl.BlockSpec((2, tm), lambda j, *_: (0, j)),
                      pl.BlockSpec((2, tm), lambda j, *_: (0, j))],
            out_specs=pl.BlockSpec(memory_space=pl.ANY),
            scratch_shapes=[pltpu.VMEM((2, _seg_local_rows(tm), d // 2), U32),
                            pltpu.VMEM((rows, d // 2), U32),
                            pltpu.SemaphoreType.DMA((2,)), pltpu.SemaphoreType.DMA(())]),
        compiler_params=_cparams(("arbitrary",)),
        name="moe_dispatch",
    )(*seg, *tails, n_used, hn16, e2, rank2)


def _expert_kernel(be_ref, first_ref, nused_ref, slot_ref, next_ref, x_ref, wg_hbm, wu_hbm,
                   wd_hbm, o_ref, wg32, wu32, wd32, wg16, wu16, wd16, wsem, *, layer):
    j = pl.program_id(0)

    def weight_copies(e, s):
        return [pltpu.make_async_copy(hbm.at[layer, e], buf.at[s], wsem.at[s])
                for hbm, buf in ((wg_hbm, wg32), (wu_hbm, wu32), (wd_hbm, wd32))]

    @pl.when((first_ref[j] == 1) & (j < nused_ref[0]))
    def _():
        s = slot_ref[j]

        @pl.when(j == 0)
        def _():
            for cp in weight_copies(be_ref[0], 0):
                cp.start()

        for cp in weight_copies(be_ref[j], s):
            cp.wait()

        @pl.when(next_ref[j] >= 0)
        def _():
            for cp in weight_copies(next_ref[j], 1 - s):
                cp.start()

        wg16[...] = wg32[s].astype(BF16)
        wu16[...] = wu32[s].astype(BF16)
        wd16[...] = wd32[s].astype(BF16)

    @pl.when(j < nused_ref[0])
    def _():
        x_lo, x_hi = _unpack_pairs(x_ref[...])
        half = x_lo.shape[1]
        hg = _dot(x_lo, wg16[:half, :]) + _dot(x_hi, wg16[half:, :])
        hu = _dot(x_lo, wu16[:half, :]) + _dot(x_hi, wu16[half:, :])
        hid = (_silu(hg) * hu).astype(BF16)
        o_ref[...] = _pack_pairs(_dot(hid, wd16[...]))

    @pl.when(j >= nused_ref[0])
    def _():
        o_ref[...] = jnp.zeros_like(o_ref)


def _expert_ffn(xs, blocks, w_gate, w_up, w_down, layer, rows):
    n_rows, dh = xs.shape
    n_blocks = n_rows // rows
    d, ff = w_gate.shape[-2:]
    assert d == 2 * dh
    any_spec = pl.BlockSpec(memory_space=pl.ANY)
    return pl.pallas_call(
        functools.partial(_expert_kernel, layer=layer),
        out_shape=jax.ShapeDtypeStruct((n_rows, dh), U32),
        grid_spec=pltpu.PrefetchScalarGridSpec(
            num_scalar_prefetch=5,
            grid=(n_blocks,),
            in_specs=[pl.BlockSpec((rows, dh),
                                   lambda j, be, fi, nu, *_: (
                                       jnp.maximum(jnp.minimum(j, nu[0] - 1), 0), 0)),
                      any_spec, any_spec, any_spec],
            out_specs=pl.BlockSpec((rows, dh), lambda j, *_: (j, 0)),
            scratch_shapes=[pltpu.VMEM((2, d, ff), F32), pltpu.VMEM((2, d, ff), F32),
                            pltpu.VMEM((2, ff, d), F32),
                            pltpu.VMEM((d, ff), BF16), pltpu.VMEM((d, ff), BF16),
                            pltpu.VMEM((ff, d), BF16), pltpu.SemaphoreType.DMA((2,))]),
        compiler_params=_cparams(("arbitrary",)),
        name="moe_expert_ffn",
    )(*blocks, xs, w_gate, w_up, w_down)


def _combine_kernel(cnt_ref, off_ref, dst_ref, tot_ref, x_ref, e_ref, rank_ref, gt_ref, ys_hbm,
                    gfin_ref, o_ref, ybuf, sems, *, final_norm):
    j = pl.program_id(0)
    nb = pl.num_programs(0)
    tm = x_ref.shape[0]
    slot = lax.rem(j, 2)

    def copies(tile, s, wait):
        if wait:
            _segment_wait(tot_ref, tile, ybuf.at[s], ys_hbm, sems.at[s], False)
        else:
            _segment_copies(cnt_ref, off_ref, dst_ref, tile * N_EXPERTS, ybuf.at[s], ys_hbm,
                            sems.at[s], False, False)

    @pl.when(j == 0)
    def _():
        ybuf[...] = jnp.zeros_like(ybuf)
        copies(0, 0, False)

    @pl.when(j + 1 < nb)
    def _():
        copies(j + 1, 1 - slot, False)

    copies(j, slot, True)
    lidx = _local_index(e_ref, rank_ref, off_ref, j)
    hi = (lidx >> DIGIT_BITS).astype(F32)
    lo = (lidx & ((1 << DIGIT_BITS) - 1)).astype(F32)
    rows8 = jnp.concatenate([hi, lo, gt_ref[...], jnp.zeros((2, tm), F32)], axis=0).astype(BF16)
    sel = jnp.where(_iota((SUBLANES, 6 * LANES), 0)
                    == (_iota((SUBLANES, 6 * LANES), 1) >> (LANES.bit_length() - 1)),
                    1.0, 0.0).astype(BF16)
    cols = _dot_tn(rows8, sel)
    col = lambda k: cols[:, k * LANES:(k + 1) * LANES]
    la = col(0) * float(1 << DIGIT_BITS) + col(2)
    lb = col(1) * float(1 << DIGIT_BITS) + col(3)
    ga, gb = col(4), col(5)
    lane = _iota((tm, LANES), 1).astype(F32)
    mix = jnp.concatenate(
        [jnp.where(lane + float(q * LANES) == la, ga, 0.0)
         + jnp.where(lane + float(q * LANES) == lb, gb, 0.0)
         for q in range(_seg_local_rows(tm) // LANES)], axis=1).astype(BF16)
    y_lo, y_hi = _unpack_pairs(ybuf[slot])
    y = x_ref[...] + jnp.concatenate([_dot(mix, y_lo), _dot(mix, y_hi)], axis=1)
    if final_norm:
        ms = jnp.mean(y * y, axis=-1, keepdims=True)
        y = y * lax.rsqrt(ms + RMS_EPS) * gfin_ref[...]
    o_ref[...] = y


def _combine(x2d, ys, e2, rank2, gates2, seg, g_final, final_norm, tm):
    n, d = x2d.shape
    return pl.pallas_call(
        functools.partial(_combine_kernel, final_norm=final_norm),
        out_shape=jax.ShapeDtypeStruct((n, d), F32),
        grid_spec=pltpu.PrefetchScalarGridSpec(
            num_scalar_prefetch=4,
            grid=(n // tm,),
            in_specs=[pl.BlockSpec((tm, d), lambda j, *_: (j, 0)),
                      pl.BlockSpec((2, tm), lambda j, *_: (0, j)),
                      pl.BlockSpec((2, tm), lambda j, *_: (0, j)),
                      pl.BlockSpec((2, tm), lambda j, *_: (0, j)),
                      pl.BlockSpec(memory_space=pl.ANY),
                      pl.BlockSpec((1, d), lambda j, *_: (0, 0))],
            out_specs=pl.BlockSpec((tm, d), lambda j, *_: (j, 0)),
            scratch_shapes=[pltpu.VMEM((2, _seg_local_rows(tm), d // 2), U32),
                            pltpu.SemaphoreType.DMA((2,))]),
        compiler_params=_cparams(("arbitrary",)),
        name="moe_combine",
    )(*seg, x2d, e2, rank2, gates2, ys, g_final.reshape(1, d))


def _route_plan(tile_cnt, n_tok, rows, tm):
    n_tiles = n_tok // tm
    seg_cnt = (tile_cnt + SUBLANES - 1) // SUBLANES * SUBLANES
    counts = jnp.sum(seg_cnt, axis=0)
    padded = (counts + rows - 1) // rows * rows
    pend = jnp.cumsum(padded)
    pstart = pend - padded
    seg_dst = pstart[None, :] + jnp.cumsum(seg_cnt, axis=0) - seg_cnt
    seg_off = jnp.cumsum(seg_cnt, axis=1) - seg_cnt
    n_blocks = -(-(2 * n_tok + n_tiles * N_EXPERTS * SUBLANES) // rows) + N_EXPERTS
    starts = jnp.arange(n_blocks, dtype=I32) * rows
    block_e = jnp.minimum(jnp.sum((starts[:, None] >= pend[None, :]).astype(I32), axis=1),
                          N_EXPERTS - 1)
    first = jnp.concatenate([jnp.ones((1,), I32),
                             (block_e[1:] != block_e[:-1]).astype(I32)])
    n_used = (pend[-1] // rows).astype(I32).reshape(1)
    bidx = jnp.arange(n_blocks, dtype=I32)
    w_slot = (jnp.cumsum(first) - 1) % 2
    later_first = (bidx[None, :] > bidx[:, None]) & (first[None, :] == 1) & (bidx[None, :] < n_used)
    next_e = jnp.where(jnp.any(later_first, axis=1),
                       block_e[jnp.argmax(later_first, axis=1)], -1)
    blocks = (block_e.astype(I32), first, n_used, w_slot.astype(I32), next_e.astype(I32))
    seg = tuple(a.reshape(-1).astype(I32)
                for a in (seg_cnt, seg_off, seg_dst, jnp.sum(seg_cnt, axis=1)))
    tails = ((padded - counts).astype(I32), (pstart + counts).astype(I32))
    return seg, tails, blocks, n_blocks * rows


def _moe(x2d, route, w_gate, w_up, w_down, layer, g_final, final_norm, tm, rows=ROUTE_ROWS):
    e2, gates2, rank2, cnt, hn16 = route
    seg, tails, blocks, n_rows = _route_plan(cnt[:, :, 0].astype(I32), e2.shape[1], rows, tm)
    xs = _dispatch(hn16, e2, rank2, seg, tails, blocks[2], n_rows, rows, tm)
    ys = _expert_ffn(xs, blocks, w_gate, w_up, w_down, layer, rows)
    return _combine(x2d, ys, e2, rank2, gates2, seg, g_final, final_norm, tm)


def kernel(x, norm_mix, norm_ffn, norm_final, ab_w_in, rw_mu, rw_w0, rw_w2, rw_a0, rw_a2, rw_g2, rw_k_k, rw_k_a, rw_r_k, rw_ln_w, rw_ln_b, lru_conv_w, lru_conv_b, lru_w_a, lru_b_a, lru_w_x, lru_b_x, lru_lambda, ab_w_out, c_w_in, c_lower_bound, c_norm_w, c_w_out, moe_w_group, moe_b_group, moe_w_expert, moe_b_expert, moe_w_gate, moe_w_up, moe_w_down):
    bsz, t, d = x.shape
    depth = norm_mix.shape[0]
    n = bsz * t
    lbs = jnp.cumsum(jax.nn.softmax(c_lower_bound.astype(F32), axis=0), axis=0)
    lbs = lbs - lbs[0]
    x2d = x.reshape(n, d)
    for layer in range(depth):
        j = layer // 2
        if layer % 2 == 0:
            rw_cols = rw_mu.shape[1]
            lru_cols = ab_w_in.shape[2] - rw_cols
            width = rw_w0.shape[1]
            p_a, p_b = _norm_matmul(x2d, norm_mix[layer], ab_w_in[j], (rw_cols, lru_cols),
                                    (F32, F32))
            prep = _rwkv_prep(p_a.reshape(bsz, t, rw_cols), rw_mu[j], rw_w0[j], rw_w2[j],
                              rw_a0[j], rw_a2[j], rw_g2[j], rw_k_k[j], rw_k_a[j],
                              rw_r_k[j].reshape(-1))
            ya = _rwkv_scan(*prep, rw_ln_w[j], rw_ln_b[j])
            yb = _lru(p_b.reshape(bsz, t, lru_cols), lru_conv_w[j], lru_conv_b[j], lru_w_a[j],
                      lru_b_a[j], lru_w_x[j], lru_b_x[j], lru_lambda[j])
            ys = [ya.reshape(n, width), yb.reshape(n, -1)]
            ws = [ab_w_out[j][:width], ab_w_out[j][width:]]
        else:
            hw = c_norm_w.shape[1]
            q, f, i_, g = _norm_matmul(x2d, norm_mix[layer], c_w_in[j], (hw,) * 4,
                                       (BF16, F32, BF16, BF16))
            shp = (bsz, t, hw)
            o = _gla(q.reshape(shp), f.reshape(shp), i_.reshape(shp), g.reshape(shp),
                     lbs[layer], c_norm_w[j])
            ys = [o.reshape(n, hw)]
            ws = [c_w_out[j]]
        tm = min(ROUTE_TILE, n)
        x2d = _proj_residual(x2d, ys, ws)
        route = _router(x2d, norm_ffn[layer], moe_w_group[layer], moe_b_group[layer],
                        moe_w_expert[layer], moe_b_expert[layer], tm)
        x2d = _moe(x2d, route, moe_w_gate, moe_w_up, moe_w_down, layer, norm_final,
                   layer == depth - 1, tm)
    return x2d.reshape(bsz, t, d)
```

```python
import functools

import jax
import jax.numpy as jnp
from jax import lax
from jax.experimental import pallas as pl
from jax.experimental.pallas import tpu as pltpu

F32 = jnp.float32
BF16 = jnp.bfloat16
I32 = jnp.int32
U32 = jnp.uint32

RMS_EPS = 1e-6
RWKV_GN_EPS = 64e-5
LRU_C = 8.0
CHUNK = 64
CHUNK_SHIFT = CHUNK.bit_length() - 1
HEAD64 = 64
HEAD_SHIFT = HEAD64.bit_length() - 1
LANES = 128
SUBLANES = 8
N_GROUPS = 4
EXPERTS_PER_GROUP = 8
N_EXPERTS = N_GROUPS * EXPERTS_PER_GROUP
ROUTE_ROWS = 512
ROUTE_TILE = 512
DIGIT_BITS = 8
VMEM_LIMIT = 56 * 1024 * 1024


def _cparams(sem):
    return pltpu.CompilerParams(dimension_semantics=sem, vmem_limit_bytes=VMEM_LIMIT)


def _sigmoid(x):
    return 0.5 * jnp.tanh(0.5 * x) + 0.5


def _pack_pairs(x):
    w = x.shape[1] // 2
    lo = lax.bitcast_convert_type(x[:, :w].astype(BF16).astype(F32), U32)
    hi = lax.bitcast_convert_type(x[:, w:].astype(BF16).astype(F32), U32)
    return (lo >> 16) | hi


def _unpack_pairs(p):
    lo = lax.bitcast_convert_type(p << 16, F32).astype(BF16)
    hi = lax.bitcast_convert_type(p & jnp.uint32(0xFFFF0000), F32).astype(BF16)
    return lo, hi


def _softplus(x):
    return jnp.maximum(x, 0.0) + jnp.log(1.0 + jnp.exp(-jnp.abs(x)))


def _silu(x):
    h = 0.5 * x
    return h * jnp.tanh(h) + h


def _gelu_tanh(x):
    return 0.5 * x * (1.0 + jnp.tanh(0.7978845608028654 * (x + 0.044715 * x * x * x)))


def _dot(a, b):
    return jnp.dot(a, b, preferred_element_type=F32)


def _dot_nt(a, b):
    return lax.dot_general(a, b, (((1,), (1,)), ((), ())), preferred_element_type=F32)


def _dot_tn(a, b):
    return lax.dot_general(a, b, (((0,), (0,)), ((), ())), preferred_element_type=F32)


def _split2(x):
    hi = x.astype(BF16)
    lo = (x - hi.astype(F32)).astype(BF16)
    return hi, lo


def _dot_exact_rhs(x, m_bf16):
    hi, lo = _split2(x)
    return _dot(hi, m_bf16) + _dot(lo, m_bf16)


def _dot_exact_lhs(m_bf16, x):
    hi, lo = _split2(x)
    return _dot(m_bf16, hi) + _dot(m_bf16, lo)


def _dot_split(a, w_hi, w_lo):
    a_hi, a_lo = _split2(a)
    return _dot(a_hi, w_hi) + (_dot(a_lo, w_hi) + _dot(a_hi, w_lo))


def _hi_lo(w):
    hi = w.astype(BF16)
    return hi, (w - hi.astype(F32)).astype(BF16)


def _iota(shape, dim):
    return lax.broadcasted_iota(I32, shape, dim)


def _norm_matmul_kernel(x_ref, g_ref, w_ref, *o_refs, splits):
    x = x_ref[...]
    ms = jnp.mean(x * x, axis=-1, keepdims=True)
    y = (x * lax.rsqrt(ms + RMS_EPS) * g_ref[...]).astype(BF16)
    off = 0
    for o_ref, n in zip(o_refs, splits):
        o_ref[...] = _dot(y, w_ref[:, off:off + n]).astype(o_ref.dtype)
        off += n


def _norm_matmul(x2d, g, w, splits, out_dtypes, tm=1024):
    n, d = x2d.shape
    tm = min(tm, n)
    ncols = w.shape[1]
    assert sum(splits) == ncols and n % tm == 0 and len(out_dtypes) == len(splits)
    return pl.pallas_call(
        functools.partial(_norm_matmul_kernel, splits=splits),
        out_shape=[jax.ShapeDtypeStruct((n, s), dt) for s, dt in zip(splits, out_dtypes)],
        grid=(n // tm,),
        in_specs=[pl.BlockSpec((tm, d), lambda i: (i, 0)),
                  pl.BlockSpec((1, d), lambda i: (0, 0)),
                  pl.BlockSpec((d, ncols), lambda i: (0, 0))],
        out_specs=[pl.BlockSpec((tm, s), lambda i: (i, 0)) for s in splits],
        compiler_params=_cparams(("parallel",)),
        name="norm_matmul",
    )(x2d, g.reshape(1, d), w.astype(BF16))


def _proj_residual_kernel(*refs, n_in):
    x_ref = refs[0]
    y_refs = refs[1:1 + n_in]
    w_refs = refs[1 + n_in:1 + 2 * n_in]
    o_ref = refs[1 + 2 * n_in]
    acc = x_ref[...]
    for y_ref, w_ref in zip(y_refs, w_refs):
        acc = acc + _dot(y_ref[...].astype(BF16), w_ref[...])
    o_ref[...] = acc


def _proj_residual(x2d, ys, ws, tm=1024):
    n, d = x2d.shape
    tm = min(tm, n)
    n_in = len(ys)
    in_specs = [pl.BlockSpec((tm, d), lambda i: (i, 0))]
    in_specs += [pl.BlockSpec((tm, y.shape[1]), lambda i: (i, 0)) for y in ys]
    in_specs += [pl.BlockSpec(w.shape, lambda i: (0, 0)) for w in ws]
    return pl.pallas_call(
        functools.partial(_proj_residual_kernel, n_in=n_in),
        out_shape=jax.ShapeDtypeStruct((n, d), F32),
        grid=(n // tm,),
        in_specs=in_specs,
        out_specs=pl.BlockSpec((tm, d), lambda i: (i, 0)),
        compiler_params=_cparams(("parallel",)),
        name="proj_residual",
    )(x2d, *ys, *[w.astype(BF16) for w in ws])


def _router_kernel(x_ref, g_ref, w_ref, b_ref, upper_ref, e_o, gate_o, rank_o, cnt_o, hn_o):
    _route_tile(x_ref[...], g_ref, w_ref, b_ref, upper_ref, e_o, gate_o, rank_o, cnt_o, hn_o)


def _router(x2d, g_ffn, w_group, b_group, w_expert, b_expert, tm):
    n, d = x2d.shape
    wt = jnp.zeros((d, LANES), F32)
    wt = wt.at[:, 0:N_GROUPS].set(w_group).at[:, 8:8 + N_EXPERTS].set(w_expert)
    wt = jnp.stack(_hi_lo(wt))
    bt = jnp.zeros((LANES,), F32)
    bt = bt.at[0:N_GROUPS].set(b_group).at[8:8 + N_EXPERTS].set(b_expert)
    bt = jnp.broadcast_to(bt[:, None], (LANES, LANES))
    ti = jnp.arange(tm, dtype=I32)
    upper = (ti[:, None] < ti[None, :]).astype(BF16)
    tok2 = pl.BlockSpec((2, tm), lambda i: (0, i))
    return pl.pallas_call(
        _router_kernel,
        out_shape=[jax.ShapeDtypeStruct((2, n), I32),
                   jax.ShapeDtypeStruct((2, n), F32),
                   jax.ShapeDtypeStruct((2, n), I32),
                   jax.ShapeDtypeStruct((n // tm, N_EXPERTS, LANES), F32),
                   jax.ShapeDtypeStruct((n, d), BF16)],
        grid=(n // tm,),
        in_specs=[pl.BlockSpec((tm, d), lambda i: (i, 0)),
                  pl.BlockSpec((1, d), lambda i: (0, 0)),
                  pl.BlockSpec((2, d, LANES), lambda i: (0, 0, 0)),
                  pl.BlockSpec((LANES, LANES), lambda i: (0, 0)),
                  pl.BlockSpec((tm, tm), lambda i: (0, 0))],
        out_specs=[tok2, tok2, tok2,
                   pl.BlockSpec((1, N_EXPERTS, LANES), lambda i: (i, 0, 0)),
                   pl.BlockSpec((tm, d), lambda i: (i, 0))],
        compiler_params=_cparams(("parallel",)),
        name="moe_router",
    )(x2d, g_ffn.reshape(1, d), wt, bt, upper)


def _rwkv_prep_kernel(p_ref, mu_ref, w0_ref, a0_ref, kk_s_ref, ka_ref, rk_ref,
                      wcomb_ref, g2_ref, tril_ref,
                      rt_o, kkt_o, kh_o, bh_o, v_o, g_o, bonus_o, gc_o,
                      prev_ref, *, width):
    tm = p_ref.shape[1]

    @pl.when(pl.program_id(1) == 0)
    def _():
        prev_ref[...] = jnp.zeros_like(prev_ref)

    p = p_ref[0]
    rolled = pltpu.roll(p, 1, axis=0)
    prev = jnp.where(_iota(p.shape, 0) == 0, prev_ref[...], rolled)
    prev_ref[...] = p[tm - 1:tm, :]
    ps = p + (prev - p) * mu_ref[...]

    r = ps[:, 0:width]
    k = ps[:, width:2 * width]
    v = ps[:, 2 * width:3 * width]
    lowrank = ps[:, 3 * width:3 * width + LANES]
    gl = ps[:, 3 * width + LANES:3 * width + 2 * LANES]

    lane = _iota(lowrank.shape, 1)
    lr_in = jnp.where(lane < HEAD64, jnp.tanh(lowrank), lowrank)
    t12 = _dot_split(lr_in, wcomb_ref[0], wcomb_ref[1])
    wlog = -_softplus(-(w0_ref[...] + t12[:, :width])) - 0.5
    lw = -jnp.exp(wlog)
    a = _sigmoid(a0_ref[...] + t12[:, width:])
    g = _dot_split(_sigmoid(gl), g2_ref[0], g2_ref[1])

    seg = jnp.where((_iota((LANES, LANES), 0) >> HEAD_SHIFT)
                    == (_iota((LANES, LANES), 1) >> HEAD_SHIFT),
                    1.0, 0.0).astype(BF16)

    def head_sums(x):
        return jnp.concatenate([_dot_exact_rhs(x[:, q * LANES:(q + 1) * LANES], seg)
                                for q in range(width // LANES)], axis=1)

    kk = k * kk_s_ref[...]
    nrm = jnp.sqrt(head_sums(kk * kk))
    kk = kk / jnp.maximum(nrm, 1e-12)
    k2 = k * (1.0 + (a - 1.0) * ka_ref[...])
    bonus = head_sums(r * k2 * rk_ref[...]) * v

    b = kk * a

    n_chunk = tm // CHUNK
    cum = _dot_exact_lhs(tril_ref[...], lw)
    cum3 = cum.reshape(n_chunk, CHUNK, width)
    cend = cum3[:, CHUNK - 1:CHUNK, :]
    e_neg = jnp.exp(-cum)

    rt_o[0] = (r * jnp.exp(cum)).astype(BF16)
    kkt_o[0] = (kk * jnp.exp(cum - lw)).astype(BF16)
    kh_o[0] = (k2 * e_neg).astype(BF16)
    bh_o[0] = (b * e_neg).astype(BF16)
    v_o[0] = v.astype(BF16)
    g_o[0] = g.astype(BF16)
    bonus_o[0] = bonus.astype(BF16)
    gc_o[0] = jnp.exp(cend)


def _rwkv_prep(p_a, mu, w0, w2, a0, a2, g2, k_k, k_a, r_k, tm=512):
    b, t, cols = p_a.shape
    width = w0.shape[0]
    rank = w2.shape[0]
    assert rank == HEAD64 and a2.shape[0] == HEAD64 and g2.shape[0] == LANES
    zeros = jnp.zeros((rank, width), F32)
    wcomb = jnp.concatenate([jnp.concatenate([w2, zeros], 1),
                             jnp.concatenate([zeros, a2], 1)], 0)
    row = lambda a_: a_.reshape(1, -1)
    const = lambda shp: pl.BlockSpec(shp, lambda i, j: (0,) * len(shp))
    tok = pl.BlockSpec((1, tm, width), lambda i, j: (i, j, 0))
    n_chunk = tm // CHUNK
    ti = jnp.arange(tm, dtype=I32)
    tril_bd = (((ti[:, None] >> CHUNK_SHIFT) == (ti[None, :] >> CHUNK_SHIFT))
               & (ti[None, :] <= ti[:, None])).astype(BF16)
    return pl.pallas_call(
        functools.partial(_rwkv_prep_kernel, width=width),
        out_shape=([jax.ShapeDtypeStruct((b, t, width), BF16)] * 7
                   + [jax.ShapeDtypeStruct((b, t // CHUNK, 1, width), F32)]),
        grid=(b, t // tm),
        in_specs=[pl.BlockSpec((1, tm, cols), lambda i, j: (i, j, 0)),
                  const((1, cols)), const((1, width)), const((1, width)), const((1, width)),
                  const((1, width)), const((1, width)),
                  const((2, LANES, 2 * width)), const((2, LANES, width)), const((tm, tm))],
        out_specs=[tok] * 7 + [pl.BlockSpec((1, n_chunk, 1, width), lambda i, j: (i, j, 0, 0))],
        scratch_shapes=[pltpu.VMEM((1, cols), F32)],
        compiler_params=_cparams(("parallel", "arbitrary")),
        name="rwkv_prep",
    )(p_a, row(mu), row(w0), row(a0), row(k_k), row(k_a), row(r_k),
      jnp.stack(_hi_lo(wcomb)), jnp.stack(_hi_lo(g2)), tril_bd)


def _rwkv_scan_kernel(rt_ref, kkt_ref, kh_ref, bh_ref, v_ref, g_ref, bonus_ref,
                      gc_ref, lnw_ref, lnb_ref, o_ref, st_ref, *, n_batch, n_pairs):
    c = CHUNK

    @pl.when(pl.program_id(0) == 0)
    def _():
        st_ref[...] = jnp.zeros_like(st_ref)

    rr = _iota((LANES, LANES), 0)
    cc = _iota((LANES, LANES), 1)
    mask_bd = jnp.where((rr >> HEAD_SHIFT) == (cc >> HEAD_SHIFT), 1.0, 0.0)
    mask_bd16 = mask_bd.astype(BF16)
    t_i = _iota((c, LANES), 0)
    s_i = _iota((c, LANES), 1) & (c - 1)
    strict = jnp.where(s_i < t_i, 1.0, 0.0)
    incl = jnp.where(s_i <= t_i, 1.0, 0.0)
    eye_ss = jnp.where(s_i == t_i, 1.0, 0.0)

    def bd16(x):
        return jnp.concatenate([x, x], axis=0) * mask_bd16

    def b16(xs):
        return [x.astype(BF16) for x in xs]

    chains = [(bi, p) for bi in range(n_batch) for p in range(n_pairs)]
    n_ch = len(chains)
    idx = range(n_ch)

    def ld(ref):
        return [ref[bi, :, p * LANES:(p + 1) * LANES] for bi, p in chains]

    rt, kkt, kh, bh, v = (ld(ref) for ref in (rt_ref, kkt_ref, kh_ref, bh_ref, v_ref))

    lhs = [jnp.concatenate([kkt[i], rt[i]], axis=0) for i in idx]
    rhs = [jnp.concatenate([bd16(kh[i]), bd16(bh[i])], axis=0) for i in idx]
    gm = [_dot_nt(lhs[i], rhs[i]) for i in idx]
    l_kv = [gm[i][:c, :LANES] * strict for i in idx]
    xp = [-(gm[i][:c, LANES:] * strict) for i in idx]
    pm = [gm[i][c:, :LANES] * incl for i in idx]
    nqm = [-(gm[i][c:, LANES:] * incl) for i in idx]

    tinv = [eye_ss + xp[i] for i in idx]
    xp16 = b16(xp)
    xpb = [bd16(x) for x in xp16]
    for _ in range(c.bit_length() - 2):
        xp16 = b16([_dot(xp16[i], xpb[i]) for i in idx])
        xpb = [bd16(x) for x in xp16]
        t16 = b16(tinv)
        tinv = [tinv[i] + _dot(t16[i], xpb[i]) for i in idx]
    tinv16 = b16(tinv)

    v_bd = [bd16(x) for x in v]
    l_kv16 = b16(l_kv)
    w16 = b16([_dot(l_kv16[i], v_bd[i]) for i in idx])
    tkw = [_dot(tinv16[i], jnp.concatenate([bd16(kkt[i]), bd16(w16[i])], axis=1)) for i in idx]
    tk16 = b16([x[:, :LANES] for x in tkw])

    st = [st_ref[i] for i in idx]
    st16 = b16(st)
    u = [_dot_nt(tk16[i], st16[i]) + tkw[i][:, LANES:] for i in idx]
    u16 = b16(u)
    pq16 = [jnp.concatenate([pm[i], nqm[i]], axis=1).astype(BF16) for i in idx]
    vu = [jnp.concatenate([v_bd[i], bd16(u16[i])], axis=0) for i in idx]
    y = [_dot_nt(rt[i], st16[i]) + _dot(pq16[i], vu[i]) for i in idx]
    upd = [_dot_tn(jnp.concatenate([v[i], -u16[i]], axis=0),
                   jnp.concatenate([kh[i], bh[i]], axis=0)) for i in idx]
    for i, (bi, p) in enumerate(chains):
        st_ref[i] = (st[i] + upd[i] * mask_bd) * gc_ref[bi, 0, :, p * LANES:(p + 1) * LANES]

    ys = jnp.concatenate(y, axis=0)
    mean = _dot_exact_rhs(ys, mask_bd16) * (1.0 / HEAD64)
    d = ys - mean
    var = _dot_exact_rhs(d * d, mask_bd16) * (1.0 / HEAD64)
    dn = d * lax.rsqrt(var + RWKV_GN_EPS)
    for i, (bi, p) in enumerate(chains):
        sl = slice(p * LANES, (p + 1) * LANES)
        yn = dn[i * c:(i + 1) * c] * lnw_ref[:, sl] + lnb_ref[:, sl]
        o_ref[bi, :, sl] = ((yn + bonus_ref[bi, :, sl]) * g_ref[bi, :, sl]).astype(o_ref.dtype)


def _rwkv_scan(rt, kkt, kh, bh, v, g, bonus, gc, ln_w, ln_b):
    bsz, t, width = rt.shape
    n_pairs = width // LANES
    tok = pl.BlockSpec((bsz, CHUNK, width), lambda j: (0, j, 0))
    const = pl.BlockSpec((1, width), lambda j: (0, 0))
    return pl.pallas_call(
        functools.partial(_rwkv_scan_kernel, n_batch=bsz, n_pairs=n_pairs),
        out_shape=jax.ShapeDtypeStruct((bsz, t, width), BF16),
        grid=(t // CHUNK,),
        in_specs=[tok] * 7 + [pl.BlockSpec((bsz, 1, 1, width), lambda j: (0, j, 0, 0)),
                              const, const],
        out_specs=tok,
        scratch_shapes=[pltpu.VMEM((bsz * n_pairs, LANES, LANES), F32)],
        compiler_params=_cparams(("arbitrary",)),
        name="rwkv_scan",
    )(rt, kkt, kh, bh, v, g, bonus, gc, ln_w.reshape(1, -1), ln_b.reshape(1, -1))


def _proj_lru_kernel(x_ref, gn_ref, w_ref, cw_ref, cb_ref, wg_ref, ba_ref, bx_ref, lam_ref,
                     pa_o, o_ref, xcarry_ref, hcarry_ref, a_s, u_s, *, width, rw_cols):
    tm = x_ref.shape[1]

    @pl.when(pl.program_id(1) == 0)
    def _():
        xcarry_ref[...] = jnp.zeros_like(xcarry_ref)
        hcarry_ref[...] = jnp.zeros_like(hcarry_ref)

    x = x_ref[0]
    ms = jnp.mean(x * x, axis=-1, keepdims=True)
    hn = (x * lax.rsqrt(ms + RMS_EPS) * gn_ref[...]).astype(BF16)
    p_b = _dot(hn, w_ref[:, rw_cols:])
    pa_o[0] = _dot(hn, w_ref[:, :rw_cols])

    gate = p_b[:, 0:width]
    xb = p_b[:, width:2 * width]
    carry8 = xcarry_ref[...]
    row8 = _iota((8, width), 0)

    def shifted(s):
        rolled = pltpu.roll(xb, s, axis=0)
        first = jnp.where(row8 < s, pltpu.roll(carry8, s, axis=0), rolled[0:8])
        return jnp.concatenate([first, rolled[8:]], axis=0)

    xc = (cw_ref[0:1, :] * shifted(3) + cw_ref[1:2, :] * shifted(2)
          + cw_ref[2:3, :] * shifted(1) + cw_ref[3:4, :] * xb + cb_ref[...])
    xcarry_ref[...] = xb[tm - 8:tm, :]

    gates = _dot(xc.astype(BF16), wg_ref[...])
    ig = _sigmoid(gates[:, width:] + bx_ref[...])
    half_rate = (-0.5 * LRU_C) * _softplus(-lam_ref[...])
    log_a = half_rate * jnp.tanh(0.5 * (gates[:, :width] + ba_ref[...])) + half_rate
    a = jnp.exp(log_a)
    a_s[...] = a
    u_s[...] = jnp.sqrt(1.0 - a * a) * ig * xc

    m1 = row8 >= 1
    m2 = row8 >= 2
    m4 = row8 >= 4

    def body(i, h):
        off = pl.multiple_of(i * 8, 8)
        a8 = a_s[pl.ds(off, 8), :]
        u8 = u_s[pl.ds(off, 8), :]
        for s, m in ((1, m1), (2, m2), (4, m4)):
            u_sh = jnp.where(m, pltpu.roll(u8, s, axis=0), 0.0)
            a_sh = jnp.where(m, pltpu.roll(a8, s, axis=0), 1.0)
            u8 = u8 + a8 * u_sh
            a8 = a8 * a_sh
        h8 = u8 + a8 * h
        u_s[pl.ds(off, 8), :] = h8
        return jnp.broadcast_to(h8[7:8, :], (8, width))

    h_last = lax.fori_loop(0, tm // 8, body, hcarry_ref[...])
    hcarry_ref[...] = h_last
    o_ref[0] = (u_s[...] * _gelu_tanh(gate)).astype(o_ref.dtype)


def _block_diag(w):
    nb, di, do = w.shape
    eye = jnp.eye(nb, dtype=w.dtype)
    return (eye[:, None, :, None] * w[:, :, None, :]).reshape(nb * di, nb * do)


def _proj_lru(x3d, g_norm, w_in, rw_cols, conv_w, conv_b, w_a, b_a, w_x, b_x, lam, tm=512):
    b, t, d = x3d.shape
    tm = min(tm, t)
    cols = w_in.shape[1]
    width = (cols - rw_cols) // 2
    wg = jnp.concatenate([_block_diag(w_a), _block_diag(w_x)], axis=1).astype(BF16)
    row = lambda a_: a_.reshape(1, -1)
    const = lambda shp: pl.BlockSpec(shp, lambda i, j: (0, 0))
    return pl.pallas_call(
        functools.partial(_proj_lru_kernel, width=width, rw_cols=rw_cols),
        out_shape=[jax.ShapeDtypeStruct((b, t, rw_cols), F32),
                   jax.ShapeDtypeStruct((b, t, width), BF16)],
        grid=(b, t // tm),
        in_specs=[pl.BlockSpec((1, tm, d), lambda i, j: (i, j, 0)),
                  const((1, d)), const((d, cols)),
                  const(conv_w.shape), const((1, width)), const((width, 2 * width)),
                  const((1, width)), const((1, width)), const((1, width))],
        out_specs=[pl.BlockSpec((1, tm, rw_cols), lambda i, j: (i, j, 0)),
                   pl.BlockSpec((1, tm, width), lambda i, j: (i, j, 0))],
        scratch_shapes=[pltpu.VMEM((8, width), F32), pltpu.VMEM((8, width), F32),
                        pltpu.VMEM((tm, width), F32), pltpu.VMEM((tm, width), F32)],
        compiler_params=_cparams(("parallel", "arbitrary")),
        name="proj_rglru",
    )(x3d, row(g_norm), w_in.astype(BF16), conv_w, row(conv_b), wg, row(b_a), row(b_x), row(lam))


def _gla_kernel(q_ref, f_ref, i_ref, g_ref, lb_ref, nw_ref, o_ref, st_ref, *, n_batch, n_heads):
    c = CHUNK

    @pl.when(pl.program_id(0) == 0)
    def _():
        st_ref[...] = jnp.zeros_like(st_ref)

    trilf = jnp.where(_iota((c, c), 1) <= _iota((c, c), 0), 1.0, 0.0)
    tril16 = trilf.astype(BF16)
    mid = c // 2 - 1
    lb = lb_ref[...]
    fg_c0 = 0.5 + 0.5 * lb
    fg_c1 = 0.5 - 0.5 * lb

    qm, km, v16, gc, em, ecm = [], [], [], [], [], []
    for bi in range(n_batch):
        q = _silu(q_ref[bi].astype(F32))
        fg = fg_c0 + fg_c1 * jnp.tanh(0.5 * f_ref[bi])
        k = 1.0 - fg
        cum = _dot_exact_lhs(tril16, jnp.log(fg))
        cum_c = cum[c - 1:c, :]
        cum_m = cum[mid:mid + 1, :]
        qm.append((q * jnp.exp(cum - cum_m)).astype(BF16))
        km.append((k * jnp.exp(cum_m - cum)).astype(BF16))
        v16.append(i_ref[bi].astype(BF16))
        gc.append(jnp.exp(cum_c))
        em.append(jnp.exp(cum_m))
        ecm.append(jnp.exp(cum_c - cum_m))

    chains = [(bi, h) for bi in range(n_batch) for h in range(n_heads)]
    idx = range(len(chains))

    def hs(xs):
        return [xs[bi][:, h * LANES:(h + 1) * LANES] for bi, h in chains]

    qm_c, km_c, v_c, gc_c, em_c, ecm_c = hs(qm), hs(km), hs(v16), hs(gc), hs(em), hs(ecm)
    scores = [(_dot_nt(qm_c[i], km_c[i]) * trilf).astype(BF16) for i in idx]
    st = [st_ref[i] for i in idx]
    st16 = [(st[i] * em_c[i]).astype(BF16) for i in idx]
    o = [_dot(scores[i], v_c[i]) + _dot_nt(qm_c[i], st16[i]) for i in idx]
    for i in idx:
        st_ref[i] = st[i] * gc_c[i] + _dot_tn(v_c[i], km_c[i]) * ecm_c[i]

    for i, (bi, h) in enumerate(chains):
        sl = slice(h * LANES, (h + 1) * LANES)
        on = o[i] * lax.rsqrt(jnp.mean(o[i] * o[i], axis=-1, keepdims=True) + RMS_EPS)
        gate = _silu(g_ref[bi, :, sl].astype(F32))
        o_ref[bi, :, sl] = (on * nw_ref[:, sl] * gate).astype(o_ref.dtype)


def _gla(q, f, i, g, lower_bound, norm_w):
    bsz, t, width = q.shape
    n_heads = width // LANES
    tok = pl.BlockSpec((bsz, CHUNK, width), lambda j: (0, j, 0))
    const = pl.BlockSpec((1, width), lambda j: (0, 0))
    return pl.pallas_call(
        functools.partial(_gla_kernel, n_batch=bsz, n_heads=n_heads),
        out_shape=jax.ShapeDtypeStruct((bsz, t, width), BF16),
        grid=(t // CHUNK,),
        in_specs=[tok] * 4 + [const, const],
        out_specs=tok,
        scratch_shapes=[pltpu.VMEM((bsz * n_heads, LANES, LANES), F32)],
        compiler_params=_cparams(("arbitrary",)),
        name="hgrn2_gla",
    )(q, f, i, g, lower_bound.reshape(1, -1), norm_w.reshape(1, -1))


def _route_tile(x, g_ref, w_ref, b_ref, upper_ref, e_o, gate_o, rank_o, cnt_o, hn_o):
    ms = jnp.mean(x * x, axis=-1, keepdims=True)
    hn = x * lax.rsqrt(ms + RMS_EPS) * g_ref[...]
    hn_o[...] = hn.astype(BF16)
    lt = _dot_split(hn, w_ref[0], w_ref[1]).T + b_ref[:, 0:1]
    tm = x.shape[0]
    gl = lt[0:8, :]
    row8 = _iota((8, tm), 0)
    gl = jnp.where(row8 < N_GROUPS, gl, -jnp.inf)
    gmax = jnp.max(gl, axis=0, keepdims=True)
    g_sel = jnp.min(jnp.where(gl == gmax, row8, 8), axis=0, keepdims=True)
    g_gate = 1.0 / jnp.sum(jnp.exp(gl - gmax), axis=0, keepdims=True)

    el = jnp.zeros((EXPERTS_PER_GROUP, tm), F32)
    for gi in range(N_GROUPS):
        lo = 8 + gi * EXPERTS_PER_GROUP
        el = jnp.where(g_sel == gi, lt[lo:lo + EXPERTS_PER_GROUP, :], el)
    m1 = jnp.max(el, axis=0, keepdims=True)
    i1 = jnp.min(jnp.where(el == m1, row8, 8), axis=0, keepdims=True)
    el2 = jnp.where(row8 == i1, -jnp.inf, el)
    m2 = jnp.max(el2, axis=0, keepdims=True)
    i2 = jnp.min(jnp.where(el2 == m2, row8, 8), axis=0, keepdims=True)
    e2 = jnp.exp(m2 - m1)
    inv = 1.0 / (1.0 + e2)
    ea = g_sel * EXPERTS_PER_GROUP + i1
    eb = g_sel * EXPERTS_PER_GROUP + i2
    e_o[...] = jnp.concatenate([ea, eb], axis=0)
    gate_o[...] = jnp.concatenate([g_gate * inv, g_gate * e2 * inv], axis=0)

    erow = _iota((N_EXPERTS, tm), 0)
    oh_a = jnp.where(erow == ea, 1.0, 0.0)
    oh_b = jnp.where(erow == eb, 1.0, 0.0)
    pre_a = _dot(oh_a.astype(BF16), upper_ref[...])
    pre_b = _dot(oh_b.astype(BF16), upper_ref[...])
    cnt_a = jnp.sum(oh_a, axis=1, keepdims=True)
    cnt_b = jnp.sum(oh_b, axis=1, keepdims=True)
    rank_a = jnp.sum(oh_a * pre_a, axis=0, keepdims=True)
    rank_b = jnp.sum(oh_b * (pre_b + cnt_a), axis=0, keepdims=True)
    rank_o[...] = jnp.concatenate([rank_a, rank_b], axis=0).astype(I32)
    cnt_o[0] = jnp.broadcast_to(cnt_a + cnt_b, (N_EXPERTS, LANES))


def _seg_local_rows(tm):
    return 2 * tm + N_EXPERTS * SUBLANES


def _segment_copies(cnt_ref, off_ref, dst_ref, base, local, remote, sem, to_remote, wait):
    def body(e, carry):
        c = pl.multiple_of(cnt_ref[base + e], SUBLANES)

        @pl.when(c > 0)
        def _():
            off = 0 if off_ref is None else pl.multiple_of(off_ref[base + e], SUBLANES)
            loc = local.at[pl.ds(off, c), :]
            rem = remote.at[pl.ds(pl.multiple_of(dst_ref[base + e], SUBLANES), c), :]
            cp = (pltpu.make_async_copy(loc, rem, sem) if to_remote
                  else pltpu.make_async_copy(rem, loc, sem))
            if wait:
                cp.wait()
            else:
                cp.start()
        return carry
    lax.fori_loop(0, N_EXPERTS, body, 0, unroll=4)


def _local_index(e_ref, rank_ref, off_ref, tile):
    e = e_ref[...]
    lidx = rank_ref[...]
    for ex in range(N_EXPERTS):
        lidx = lidx + jnp.where(e == ex, off_ref[tile * N_EXPERTS + ex], 0)
    return lidx


def _segment_wait(tot_ref, tile, local, remote, sem, to_remote):
    tot = pl.multiple_of(tot_ref[tile], SUBLANES)

    @pl.when(tot > 0)
    def _():
        loc = local.at[pl.ds(0, tot), :]
        rem = remote.at[pl.ds(0, tot), :]
        cp = (pltpu.make_async_copy(loc, rem, sem) if to_remote
              else pltpu.make_async_copy(rem, loc, sem))
        cp.wait()


def _dispatch_kernel(cnt_ref, off_ref, dst_ref, tot_ref, tcnt_ref, tdst_ref, nused_ref,
                     hn_ref, e_ref, rank_ref, xs_out, sbuf, zbuf, sems, zsem):
    j = pl.program_id(0)
    nb = pl.num_programs(0)
    tm = hn_ref.shape[0]
    slot = lax.rem(j, 2)

    def copies(tile, s, wait):
        if wait:
            _segment_wait(tot_ref, tile, sbuf.at[s], xs_out, sems.at[s], True)
        else:
            _segment_copies(cnt_ref, off_ref, dst_ref, tile * N_EXPERTS, sbuf.at[s], xs_out,
                            sems.at[s], True, False)

    @pl.when(j == 0)
    def _():
        zbuf[...] = jnp.zeros_like(zbuf)
        rows = zbuf.shape[0]
        n_blocks = xs_out.shape[0] // rows
        for wait in (False, True):
            _segment_copies(tcnt_ref, None, tdst_ref, 0, zbuf, xs_out, zsem, True, wait)

            def body(b, carry):
                start = pl.multiple_of(b * rows, SUBLANES)
                cp = pltpu.make_async_copy(zbuf, xs_out.at[pl.ds(start, rows), :], zsem)
                if wait:
                    cp.wait()
                else:
                    cp.start()
                return carry
            lax.fori_loop(nused_ref[0], n_blocks, body, 0)

    @pl.when(j >= 2)
    def _():
        copies(j - 2, slot, True)

    r = _iota((_seg_local_rows(tm), tm), 0)
    lidx = _local_index(e_ref, rank_ref, off_ref, j)
    perm = jnp.where((r == lidx[0:1, :]) | (r == lidx[1:2, :]), 1.0, 0.0).astype(BF16)
    sbuf[slot] = _pack_pairs(_dot(perm, hn_ref[...]))
    copies(j, slot, False)

    @pl.when(j == nb - 1)
    def _():
        copies(j, slot, True)

    @pl.when((j == nb - 1) & (nb >= 2))
    def _():
        copies(j - 1, 1 - slot, True)


def _dispatch(hn16, e2, rank2, seg, tails, n_used, n_rows, rows, tm):
    n, d = hn16.shape
    return pl.pallas_call(
        _dispatch_kernel,
        out_shape=jax.ShapeDtypeStruct((n_rows, d // 2), U32),
        grid_spec=pltpu.PrefetchScalarGridSpec(
            num_scalar_prefetch=7,
            grid=(n // tm,),
            in_specs=[pl.BlockSpec((tm, d), lambda j, *_: (j, 0)),
                      pl.BlockSpec((2, tm), lambda j, *_: (0, j)),
                      pl.BlockSpec((2, tm), lambda j, *_: (0, j))],
            out_specs=pl.BlockSpec(memory_space=pl.ANY),
            scratch_shapes=[pltpu.VMEM((2, _seg_local_rows(tm), d // 2), U32),
                            pltpu.VMEM((rows, d // 2), U32),
                            pltpu.SemaphoreType.DMA((2,)), pltpu.SemaphoreType.DMA(())]),
        compiler_params=_cparams(("arbitrary",)),
        name="moe_dispatch",
    )(*seg, *tails, n_used, hn16, e2, rank2)


def _expert_kernel(be_ref, first_ref, nused_ref, slot_ref, next_ref, x_ref, wg_hbm, wu_hbm,
                   wd_hbm, o_ref, wg32, wu32, wd32, wg16, wu16, wd16, wsem, *, layer):
    j = pl.program_id(0)

    def weight_copies(e, s):
        return [pltpu.make_async_copy(hbm.at[layer, e], buf.at[s], wsem.at[s])
                for hbm, buf in ((wg_hbm, wg32), (wu_hbm, wu32), (wd_hbm, wd32))]

    @pl.when((first_ref[j] == 1) & (j < nused_ref[0]))
    def _():
        s = slot_ref[j]

        @pl.when(j == 0)
        def _():
            for cp in weight_copies(be_ref[0], 0):
                cp.start()

        for cp in weight_copies(be_ref[j], s):
            cp.wait()

        @pl.when(next_ref[j] >= 0)
        def _():
            for cp in weight_copies(next_ref[j], 1 - s):
                cp.start()

        wg16[...] = wg32[s].astype(BF16)
        wu16[...] = wu32[s].astype(BF16)
        wd16[...] = wd32[s].astype(BF16)

    @pl.when(j < nused_ref[0])
    def _():
        x_lo, x_hi = _unpack_pairs(x_ref[...])
        half = x_lo.shape[1]
        hg = _dot(x_lo, wg16[:half, :]) + _dot(x_hi, wg16[half:, :])
        hu = _dot(x_lo, wu16[:half, :]) + _dot(x_hi, wu16[half:, :])
        hid = (_silu(hg) * hu).astype(BF16)
        o_ref[...] = _pack_pairs(_dot(hid, wd16[...]))

    @pl.when(j >= nused_ref[0])
    def _():
        o_ref[...] = jnp.zeros_like(o_ref)


def _expert_ffn(xs, blocks, w_gate, w_up, w_down, layer, rows):
    n_rows, dh = xs.shape
    n_blocks = n_rows // rows
    d, ff = w_gate.shape[-2:]
    assert d == 2 * dh
    any_spec = pl.BlockSpec(memory_space=pl.ANY)
    return pl.pallas_call(
        functools.partial(_expert_kernel, layer=layer),
        out_shape=jax.ShapeDtypeStruct((n_rows, dh), U32),
        grid_spec=pltpu.PrefetchScalarGridSpec(
            num_scalar_prefetch=5,
            grid=(n_blocks,),
            in_specs=[pl.BlockSpec((rows, dh),
                                   lambda j, be, fi, nu, *_: (
                                       jnp.maximum(jnp.minimum(j, nu[0] - 1), 0), 0)),
                      any_spec, any_spec, any_spec],
            out_specs=pl.BlockSpec((rows, dh), lambda j, *_: (j, 0)),
            scratch_shapes=[pltpu.VMEM((2, d, ff), F32), pltpu.VMEM((2, d, ff), F32),
                            pltpu.VMEM((2, ff, d), F32),
                            pltpu.VMEM((d, ff), BF16), pltpu.VMEM((d, ff), BF16),
                            pltpu.VMEM((ff, d), BF16), pltpu.SemaphoreType.DMA((2,))]),
        compiler_params=_cparams(("arbitrary",)),
        name="moe_expert_ffn",
    )(*blocks, xs, w_gate, w_up, w_down)


def _combine_kernel(cnt_ref, off_ref, dst_ref, tot_ref, x_ref, e_ref, rank_ref, gt_ref, ys_hbm,
                    gfin_ref, o_ref, ybuf, sems, *, final_norm):
    j = pl.program_id(0)
    nb = pl.num_programs(0)
    tm = x_ref.shape[0]
    slot = lax.rem(j, 2)

    def copies(tile, s, wait):
        if wait:
            _segment_wait(tot_ref, tile, ybuf.at[s], ys_hbm, sems.at[s], False)
        else:
            _segment_copies(cnt_ref, off_ref, dst_ref, tile * N_EXPERTS, ybuf.at[s], ys_hbm,
                            sems.at[s], False, False)

    @pl.when(j == 0)
    def _():
        ybuf[...] = jnp.zeros_like(ybuf)
        copies(0, 0, False)

    @pl.when(j + 1 < nb)
    def _():
        copies(j + 1, 1 - slot, False)

    copies(j, slot, True)
    lidx = _local_index(e_ref, rank_ref, off_ref, j)
    hi = (lidx >> DIGIT_BITS).astype(F32)
    lo = (lidx & ((1 << DIGIT_BITS) - 1)).astype(F32)
    rows8 = jnp.concatenate([hi, lo, gt_ref[...], jnp.zeros((2, tm), F32)], axis=0).astype(BF16)
    sel = jnp.where(_iota((SUBLANES, 6 * LANES), 0)
                    == (_iota((SUBLANES, 6 * LANES), 1) >> (LANES.bit_length() - 1)),
                    1.0, 0.0).astype(BF16)
    cols = _dot_tn(rows8, sel)
    col = lambda k: cols[:, k * LANES:(k + 1) * LANES]
    la = col(0) * float(1 << DIGIT_BITS) + col(2)
    lb = col(1) * float(1 << DIGIT_BITS) + col(3)
    ga, gb = col(4), col(5)
    lane = _iota((tm, LANES), 1).astype(F32)
    mix = jnp.concatenate(
        [jnp.where(lane + float(q * LANES) == la, ga, 0.0)
         + jnp.where(lane + float(q * LANES) == lb, gb, 0.0)
         for q in range(_seg_local_rows(tm) // LANES)], axis=1).astype(BF16)
    y_lo, y_hi = _unpack_pairs(ybuf[slot])
    y = x_ref[...] + jnp.concatenate([_dot(mix, y_lo), _dot(mix, y_hi)], axis=1)
    if final_norm:
        ms = jnp.mean(y * y, axis=-1, keepdims=True)
        y = y * lax.rsqrt(ms + RMS_EPS) * gfin_ref[...]
    o_ref[...] = y


def _combine(x2d, ys, e2, rank2, gates2, seg, g_final, final_norm, tm):
    n, d = x2d.shape
    return pl.pallas_call(
        functools.partial(_combine_kernel, final_norm=final_norm),
        out_shape=jax.ShapeDtypeStruct((n, d), F32),
        grid_spec=pltpu.PrefetchScalarGridSpec(
            num_scalar_prefetch=4,
            grid=(n // tm,),
            in_specs=[pl.BlockSpec((tm, d), lambda j, *_: (j, 0)),
                      pl.BlockSpec((2, tm), lambda j, *_: (0, j)),
                      pl.BlockSpec((2, tm), lambda j, *_: (0, j)),
                      pl.BlockSpec((2, tm), lambda j, *_: (0, j)),
                      pl.BlockSpec(memory_space=pl.ANY),
                      pl.BlockSpec((1, d), lambda j, *_: (0, 0))],
            out_specs=pl.BlockSpec((tm, d), lambda j, *_: (j, 0)),
            scratch_shapes=[pltpu.VMEM((2, _seg_local_rows(tm), d // 2), U32),
                            pltpu.SemaphoreType.DMA((2,))]),
        compiler_params=_cparams(("arbitrary",)),
        name="moe_combine",
    )(*seg, x2d, e2, rank2, gates2, ys, g_final.reshape(1, d))


def _route_plan(tile_cnt, n_tok, rows, tm):
    n_tiles = n_tok // tm
    seg_cnt = (tile_cnt + SUBLANES - 1) // SUBLANES * SUBLANES
    counts = jnp.sum(seg_cnt, axis=0)
    padded = (counts + rows - 1) // rows * rows
    pend = jnp.cumsum(padded)
    pstart = pend - padded
    seg_dst = pstart[None, :] + jnp.cumsum(seg_cnt, axis=0) - seg_cnt
    seg_off = jnp.cumsum(seg_cnt, axis=1) - seg_cnt
    n_blocks = -(-(2 * n_tok + n_tiles * N_EXPERTS * SUBLANES) // rows) + N_EXPERTS
    starts = jnp.arange(n_blocks, dtype=I32) * rows
    block_e = jnp.minimum(jnp.sum((starts[:, None] >= pend[None, :]).astype(I32), axis=1),
                          N_EXPERTS - 1)
    first = jnp.concatenate([jnp.ones((1,), I32),
                             (block_e[1:] != block_e[:-1]).astype(I32)])
    n_used = (pend[-1] // rows).astype(I32).reshape(1)
    bidx = jnp.arange(n_blocks, dtype=I32)
    w_slot = (jnp.cumsum(first) - 1) % 2
    later_first = (bidx[None, :] > bidx[:, None]) & (first[None, :] == 1) & (bidx[None, :] < n_used)
    next_e = jnp.where(jnp.any(later_first, axis=1),
                       block_e[jnp.argmax(later_first, axis=1)], -1)
    blocks = (block_e.astype(I32), first, n_used, w_slot.astype(I32), next_e.astype(I32))
    seg = tuple(a.reshape(-1).astype(I32)
                for a in (seg_cnt, seg_off, seg_dst, jnp.sum(seg_cnt, axis=1)))
    tails = ((padded - counts).astype(I32), (pstart + counts).astype(I32))
    return seg, tails, blocks, n_blocks * rows


def _moe(x2d, route, w_gate, w_up, w_down, layer, g_final, final_norm, tm, rows=ROUTE_ROWS):
    e2, gates2, rank2, cnt, hn16 = route
    seg, tails, blocks, n_rows = _route_plan(cnt[:, :, 0].astype(I32), e2.shape[1], rows, tm)
    xs = _dispatch(hn16, e2, rank2, seg, tails, blocks[2], n_rows, rows, tm)
    ys = _expert_ffn(xs, blocks, w_gate, w_up, w_down, layer, rows)
    return _combine(x2d, ys, e2, rank2, gates2, seg, g_final, final_norm, tm)


def kernel(x, norm_mix, norm_ffn, norm_final, ab_w_in, rw_mu, rw_w0, rw_w2, rw_a0, rw_a2, rw_g2, rw_k_k, rw_k_a, rw_r_k, rw_ln_w, rw_ln_b, lru_conv_w, lru_conv_b, lru_w_a, lru_b_a, lru_w_x, lru_b_x, lru_lambda, ab_w_out, c_w_in, c_lower_bound, c_norm_w, c_w_out, moe_w_group, moe_b_group, moe_w_expert, moe_b_expert, moe_w_gate, moe_w_up, moe_w_down):
    bsz, t, d = x.shape
    depth = norm_mix.shape[0]
    n = bsz * t
    lbs = jnp.cumsum(jax.nn.softmax(c_lower_bound.astype(F32), axis=0), axis=0)
    lbs = lbs - lbs[0]
    x2d = x.reshape(n, d)
    for layer in range(depth):
        j = layer // 2
        if layer % 2 == 0:
            rw_cols = rw_mu.shape[1]
            width = rw_w0.shape[1]
            p_a, yb = _proj_lru(x2d.reshape(bsz, t, d), norm_mix[layer], ab_w_in[j], rw_cols,
                                lru_conv_w[j], lru_conv_b[j], lru_w_a[j], lru_b_a[j],
                                lru_w_x[j], lru_b_x[j], lru_lambda[j])
            prep = _rwkv_prep(p_a, rw_mu[j], rw_w0[j], rw_w2[j], rw_a0[j], rw_a2[j], rw_g2[j],
                              rw_k_k[j], rw_k_a[j], rw_r_k[j].reshape(-1))
            ya = _rwkv_scan(*prep, rw_ln_w[j], rw_ln_b[j])
            ys = [ya.reshape(n, width), yb.reshape(n, -1)]
            ws = [ab_w_out[j][:width], ab_w_out[j][width:]]
        else:
            hw = c_norm_w.shape[1]
            q, f, i_, g = _norm_matmul(x2d, norm_mix[layer], c_w_in[j], (hw,) * 4,
                                       (BF16, F32, BF16, BF16))
            shp = (bsz, t, hw)
            o = _gla(q.reshape(shp), f.reshape(shp), i_.reshape(shp), g.reshape(shp),
                     lbs[layer], c_norm_w[j])
            ys = [o.reshape(n, hw)]
            ws = [c_w_out[j]]
        tm = min(ROUTE_TILE, n)
        x2d = _proj_residual(x2d, ys, ws)
        route = _router(x2d, norm_ffn[layer], moe_w_group[layer], moe_b_group[layer],
                        moe_w_expert[layer], moe_b_expert[layer], tm)
        x2d = _moe(x2d, route, moe_w_gate, moe_w_up, moe_w_down, layer, norm_final,
                   layer == depth - 1, tm)
    return x2d.reshape(bsz, t, d)
```

```python
import functools

import jax
import jax.numpy as jnp
from jax import lax
from jax.experimental import pallas as pl
from jax.experimental.pallas import tpu as pltpu

F32 = jnp.float32
BF16 = jnp.bfloat16
I32 = jnp.int32
U32 = jnp.uint32

RMS_EPS = 1e-6
RWKV_GN_EPS = 64e-5
LRU_C = 8.0
CHUNK = 64
CHUNK_SHIFT = CHUNK.bit_length() - 1
HEAD64 = 64
HEAD_SHIFT = HEAD64.bit_length() - 1
LANES = 128
SUBLANES = 8
N_GROUPS = 4
EXPERTS_PER_GROUP = 8
N_EXPERTS = N_GROUPS * EXPERTS_PER_GROUP
ROUTE_ROWS = 512
ROUTE_TILE = 512
DIGIT_BITS = 8
VMEM_LIMIT = 56 * 1024 * 1024


def _cparams(sem):
    return pltpu.CompilerParams(dimension_semantics=sem, vmem_limit_bytes=VMEM_LIMIT)


def _sigmoid(x):
    return 0.5 * jnp.tanh(0.5 * x) + 0.5


def _pack_pairs(x):
    w = x.shape[1] // 2
    lo = lax.bitcast_convert_type(x[:, :w].astype(BF16).astype(F32), U32)
    hi = lax.bitcast_convert_type(x[:, w:].astype(BF16).astype(F32), U32)
    return (lo >> 16) | hi


def _unpack_pairs(p):
    lo = lax.bitcast_convert_type(p << 16, F32).astype(BF16)
    hi = lax.bitcast_convert_type(p & jnp.uint32(0xFFFF0000), F32).astype(BF16)
    return lo, hi


def _softplus(x):
    return jnp.maximum(x, 0.0) + jnp.log(1.0 + jnp.exp(-jnp.abs(x)))


def _silu(x):
    h = 0.5 * x
    return h * jnp.tanh(h) + h


def _gelu_tanh(x):
    return 0.5 * x * (1.0 + jnp.tanh(0.7978845608028654 * (x + 0.044715 * x * x * x)))


def _dot(a, b):
    return jnp.dot(a, b, preferred_element_type=F32)


def _dot_nt(a, b):
    return lax.dot_general(a, b, (((1,), (1,)), ((), ())), preferred_element_type=F32)


def _dot_tn(a, b):
    return lax.dot_general(a, b, (((0,), (0,)), ((), ())), preferred_element_type=F32)


def _split2(x):
    hi = x.astype(BF16)
    lo = (x - hi.astype(F32)).astype(BF16)
    return hi, lo


def _dot_exact_rhs(x, m_bf16):
    hi, lo = _split2(x)
    return _dot(hi, m_bf16) + _dot(lo, m_bf16)


def _dot_exact_lhs(m_bf16, x):
    hi, lo = _split2(x)
    return _dot(m_bf16, hi) + _dot(m_bf16, lo)


def _dot_split(a, w_hi, w_lo):
    a_hi, a_lo = _split2(a)
    return _dot(a_hi, w_hi) + (_dot(a_lo, w_hi) + _dot(a_hi, w_lo))


def _hi_lo(w):
    hi = w.astype(BF16)
    return hi, (w - hi.astype(F32)).astype(BF16)


def _iota(shape, dim):
    return lax.broadcasted_iota(I32, shape, dim)


def _norm_matmul_kernel(x_ref, g_ref, w_ref, *o_refs, splits):
    x = x_ref[...]
    ms = jnp.mean(x * x, axis=-1, keepdims=True)
    y = (x * lax.rsqrt(ms + RMS_EPS) * g_ref[...]).astype(BF16)
    off = 0
    for o_ref, n in zip(o_refs, splits):
        o_ref[...] = _dot(y, w_ref[:, off:off + n]).astype(o_ref.dtype)
        off += n


def _norm_matmul(x2d, g, w, splits, out_dtypes, tm=1024):
    n, d = x2d.shape
    tm = min(tm, n)
    ncols = w.shape[1]
    assert sum(splits) == ncols and n % tm == 0 and len(out_dtypes) == len(splits)
    return pl.pallas_call(
        functools.partial(_norm_matmul_kernel, splits=splits),
        out_shape=[jax.ShapeDtypeStruct((n, s), dt) for s, dt in zip(splits, out_dtypes)],
        grid=(n // tm,),
        in_specs=[pl.BlockSpec((tm, d), lambda i: (i, 0)),
                  pl.BlockSpec((1, d), lambda i: (0, 0)),
                  pl.BlockSpec((d, ncols), lambda i: (0, 0))],
        out_specs=[pl.BlockSpec((tm, s), lambda i: (i, 0)) for s in splits],
        compiler_params=_cparams(("parallel",)),
        name="norm_matmul",
    )(x2d, g.reshape(1, d), w.astype(BF16))


def _proj_residual_kernel(*refs, n_in):
    x_ref = refs[0]
    y_refs = refs[1:1 + n_in]
    w_refs = refs[1 + n_in:1 + 2 * n_in]
    o_ref = refs[1 + 2 * n_in]
    acc = x_ref[...]
    for y_ref, w_ref in zip(y_refs, w_refs):
        acc = acc + _dot(y_ref[...].astype(BF16), w_ref[...])
    o_ref[...] = acc


def _proj_residual(x2d, ys, ws, tm=1024):
    n, d = x2d.shape
    tm = min(tm, n)
    n_in = len(ys)
    in_specs = [pl.BlockSpec((tm, d), lambda i: (i, 0))]
    in_specs += [pl.BlockSpec((tm, y.shape[1]), lambda i: (i, 0)) for y in ys]
    in_specs += [pl.BlockSpec(w.shape, lambda i: (0, 0)) for w in ws]
    return pl.pallas_call(
        functools.partial(_proj_residual_kernel, n_in=n_in),
        out_shape=jax.ShapeDtypeStruct((n, d), F32),
        grid=(n // tm,),
        in_specs=in_specs,
        out_specs=pl.BlockSpec((tm, d), lambda i: (i, 0)),
        compiler_params=_cparams(("parallel",)),
        name="proj_residual",
    )(x2d, *ys, *[w.astype(BF16) for w in ws])


def _router_kernel(x_ref, g_ref, w_ref, b_ref, upper_ref, e_o, gate_o, rank_o, cnt_o, hn_o):
    _route_tile(x_ref[...], g_ref, w_ref, b_ref, upper_ref, e_o, gate_o, rank_o, cnt_o, hn_o)


def _router(x2d, g_ffn, w_group, b_group, w_expert, b_expert, tm):
    n, d = x2d.shape
    wt = jnp.zeros((d, LANES), F32)
    wt = wt.at[:, 0:N_GROUPS].set(w_group).at[:, 8:8 + N_EXPERTS].set(w_expert)
    wt = jnp.stack(_hi_lo(wt))
    bt = jnp.zeros((LANES,), F32)
    bt = bt.at[0:N_GROUPS].set(b_group).at[8:8 + N_EXPERTS].set(b_expert)
    bt = jnp.broadcast_to(bt[:, None], (LANES, LANES))
    ti = jnp.arange(tm, dtype=I32)
    upper = (ti[:, None] < ti[None, :]).astype(BF16)
    tok2 = pl.BlockSpec((2, tm), lambda i: (0, i))
    return pl.pallas_call(
        _router_kernel,
        out_shape=[jax.ShapeDtypeStruct((2, n), I32),
                   jax.ShapeDtypeStruct((2, n), F32),
                   jax.ShapeDtypeStruct((2, n), I32),
                   jax.ShapeDtypeStruct((n // tm, N_EXPERTS, LANES), F32),
                   jax.ShapeDtypeStruct((n, d), BF16)],
        grid=(n // tm,),
        in_specs=[pl.BlockSpec((tm, d), lambda i: (i, 0)),
                  pl.BlockSpec((1, d), lambda i: (0, 0)),
                  pl.BlockSpec((2, d, LANES), lambda i: (0, 0, 0)),
                  pl.BlockSpec((LANES, LANES), lambda i: (0, 0)),
                  pl.BlockSpec((tm, tm), lambda i: (0, 0))],
        out_specs=[tok2, tok2, tok2,
                   pl.BlockSpec((1, N_EXPERTS, LANES), lambda i: (i, 0, 0)),
                   pl.BlockSpec((tm, d), lambda i: (i, 0))],
        compiler_params=_cparams(("parallel",)),
        name="moe_router",
    )(x2d, g_ffn.reshape(1, d), wt, bt, upper)


def _rwkv_prep_kernel(p_ref, mu_ref, w0_ref, a0_ref, kk_s_ref, ka_ref, rk_ref,
                      wcomb_ref, g2_ref, tril_ref,
                      rt_o, kkt_o, kh_o, bh_o, v_o, g_o, bonus_o, gc_o,
                      prev_ref, *, width):
    tm = p_ref.shape[1]

    @pl.when(pl.program_id(1) == 0)
    def _():
        prev_ref[...] = jnp.zeros_like(prev_ref)

    p = p_ref[0]
    rolled = pltpu.roll(p, 1, axis=0)
    head = jnp.where(_iota((SUBLANES, p.shape[1]), 0) == 0, prev_ref[...], rolled[0:SUBLANES])
    prev = jnp.concatenate([head, rolled[SUBLANES:]], axis=0)
    prev_ref[...] = p[tm - 1:tm, :]
    ps = p + (prev - p) * mu_ref[...]

    r = ps[:, 0:width]
    k = ps[:, width:2 * width]
    v = ps[:, 2 * width:3 * width]
    lowrank = ps[:, 3 * width:3 * width + LANES]
    gl = ps[:, 3 * width + LANES:3 * width + 2 * LANES]

    lane = _iota(lowrank.shape, 1)
    lr_in = jnp.where(lane < HEAD64, jnp.tanh(lowrank), lowrank)
    t12 = _dot_split(lr_in, wcomb_ref[0], wcomb_ref[1])
    wlog = -_softplus(-(w0_ref[...] + t12[:, :width])) - 0.5
    lw = -jnp.exp(wlog)
    a = _sigmoid(a0_ref[...] + t12[:, width:])
    g = _dot_split(_sigmoid(gl), g2_ref[0], g2_ref[1])

    seg = jnp.where((_iota((LANES, LANES), 0) >> HEAD_SHIFT)
                    == (_iota((LANES, LANES), 1) >> HEAD_SHIFT),
                    1.0, 0.0).astype(BF16)

    def head_sums(x):
        return jnp.concatenate([_dot_exact_rhs(x[:, q * LANES:(q + 1) * LANES], seg)
                                for q in range(width // LANES)], axis=1)

    kk = k * kk_s_ref[...]
    nrm = jnp.sqrt(head_sums(kk * kk))
    kk = kk / jnp.maximum(nrm, 1e-12)
    k2 = k * (1.0 + (a - 1.0) * ka_ref[...])
    bonus = head_sums(r * k2 * rk_ref[...]) * v

    b = kk * a

    n_chunk = tm // CHUNK
    cum = _dot_exact_lhs(tril_ref[...], lw)
    cum3 = cum.reshape(n_chunk, CHUNK, width)
    cend = cum3[:, CHUNK - 1:CHUNK, :]
    e_neg = jnp.exp(-cum)

    rt_o[0] = (r * jnp.exp(cum)).astype(BF16)
    kkt_o[0] = (kk * jnp.exp(cum - lw)).astype(BF16)
    kh_o[0] = (k2 * e_neg).astype(BF16)
    bh_o[0] = (b * e_neg).astype(BF16)
    v_o[0] = v.astype(BF16)
    g_o[0] = g.astype(BF16)
    bonus_o[0] = bonus.astype(BF16)
    gc_o[0] = jnp.exp(cend)


def _rwkv_prep(p_a, mu, w0, w2, a0, a2, g2, k_k, k_a, r_k, tm=512):
    b, t, cols = p_a.shape
    width = w0.shape[0]
    rank = w2.shape[0]
    assert rank == HEAD64 and a2.shape[0] == HEAD64 and g2.shape[0] == LANES
    zeros = jnp.zeros((rank, width), F32)
    wcomb = jnp.concatenate([jnp.concatenate([w2, zeros], 1),
                             jnp.concatenate([zeros, a2], 1)], 0)
    row = lambda a_: a_.reshape(1, -1)
    const = lambda shp: pl.BlockSpec(shp, lambda i, j: (0,) * len(shp))
    tok = pl.BlockSpec((1, tm, width), lambda i, j: (i, j, 0))
    n_chunk = tm // CHUNK
    ti = jnp.arange(tm, dtype=I32)
    tril_bd = (((ti[:, None] >> CHUNK_SHIFT) == (ti[None, :] >> CHUNK_SHIFT))
               & (ti[None, :] <= ti[:, None])).astype(BF16)
    return pl.pallas_call(
        functools.partial(_rwkv_prep_kernel, width=width),
        out_shape=([jax.ShapeDtypeStruct((b, t, width), BF16)] * 7
                   + [jax.ShapeDtypeStruct((b, t // CHUNK, 1, width), F32)]),
        grid=(b, t // tm),
        in_specs=[pl.BlockSpec((1, tm, cols), lambda i, j: (i, j, 0)),
                  const((1, cols)), const((1, width)), const((1, width)), const((1, width)),
                  const((1, width)), const((1, width)),
                  const((2, LANES, 2 * width)), const((2, LANES, width)), const((tm, tm))],
        out_specs=[tok] * 7 + [pl.BlockSpec((1, n_chunk, 1, width), lambda i, j: (i, j, 0, 0))],
        scratch_shapes=[pltpu.VMEM((1, cols), F32)],
        compiler_params=_cparams(("parallel", "arbitrary")),
        name="rwkv_prep",
    )(p_a, row(mu), row(w0), row(a0), row(k_k), row(k_a), row(r_k),
      jnp.stack(_hi_lo(wcomb)), jnp.stack(_hi_lo(g2)), tril_bd)


def _rwkv_scan_kernel(rt_ref, kkt_ref, kh_ref, bh_ref, v_ref, g_ref, bonus_ref,
                      gc_ref, lnw_ref, lnb_ref, o_ref, st_ref, *, n_batch, n_pairs):
    c = CHUNK

    @pl.when(pl.program_id(0) == 0)
    def _():
        st_ref[...] = jnp.zeros_like(st_ref)

    rr = _iota((LANES, LANES), 0)
    cc = _iota((LANES, LANES), 1)
    mask_bd = jnp.where((rr >> HEAD_SHIFT) == (cc >> HEAD_SHIFT), 1.0, 0.0)
    mask_bd16 = mask_bd.astype(BF16)
    t_i = _iota((c, LANES), 0)
    s_i = _iota((c, LANES), 1) & (c - 1)
    strict = jnp.where(s_i < t_i, 1.0, 0.0)
    incl = jnp.where(s_i <= t_i, 1.0, 0.0)
    eye_ss = jnp.where(s_i == t_i, 1.0, 0.0)

    def bd16(x):
        return jnp.concatenate([x, x], axis=0) * mask_bd16

    def b16(xs):
        return [x.astype(BF16) for x in xs]

    chains = [(bi, p) for bi in range(n_batch) for p in range(n_pairs)]
    n_ch = len(chains)
    idx = range(n_ch)

    def ld(ref):
        return [ref[bi, :, p * LANES:(p + 1) * LANES] for bi, p in chains]

    rt, kkt, kh, bh, v = (ld(ref) for ref in (rt_ref, kkt_ref, kh_ref, bh_ref, v_ref))

    lhs = [jnp.concatenate([kkt[i], rt[i]], axis=0) for i in idx]
    rhs = [jnp.concatenate([bd16(kh[i]), bd16(bh[i])], axis=0) for i in idx]
    gm = [_dot_nt(lhs[i], rhs[i]) for i in idx]
    l_kv = [gm[i][:c, :LANES] * strict for i in idx]
    xp = [-(gm[i][:c, LANES:] * strict) for i in idx]
    pm = [gm[i][c:, :LANES] * incl for i in idx]
    nqm = [-(gm[i][c:, LANES:] * incl) for i in idx]

    tinv = [eye_ss + xp[i] for i in idx]
    xp16 = b16(xp)
    xpb = [bd16(x) for x in xp16]
    for _ in range(c.bit_length() - 2):
        xp16 = b16([_dot(xp16[i], xpb[i]) for i in idx])
        xpb = [bd16(x) for x in xp16]
        t16 = b16(tinv)
        tinv = [tinv[i] + _dot(t16[i], xpb[i]) for i in idx]
    tinv16 = b16(tinv)

    v_bd = [bd16(x) for x in v]
    l_kv16 = b16(l_kv)
    w16 = b16([_dot(l_kv16[i], v_bd[i]) for i in idx])
    tkw = [_dot(tinv16[i], jnp.concatenate([bd16(kkt[i]), bd16(w16[i])], axis=1)) for i in idx]
    tk16 = b16([x[:, :LANES] for x in tkw])

    st = [st_ref[i] for i in idx]
    st16 = b16(st)
    u = [_dot_nt(tk16[i], st16[i]) + tkw[i][:, LANES:] for i in idx]
    u16 = b16(u)
    pq16 = [jnp.concatenate([pm[i], nqm[i]], axis=1).astype(BF16) for i in idx]
    vu = [jnp.concatenate([v_bd[i], bd16(u16[i])], axis=0) for i in idx]
    y = [_dot_nt(rt[i], st16[i]) + _dot(pq16[i], vu[i]) for i in idx]
    upd = [_dot_tn(jnp.concatenate([v[i], -u16[i]], axis=0),
                   jnp.concatenate([kh[i], bh[i]], axis=0)) for i in idx]
    for i, (bi, p) in enumerate(chains):
        st_ref[i] = (st[i] + upd[i] * mask_bd) * gc_ref[bi, 0, :, p * LANES:(p + 1) * LANES]

    ys = jnp.concatenate(y, axis=0)
    mean = _dot_exact_rhs(ys, mask_bd16) * (1.0 / HEAD64)
    d = ys - mean
    var = _dot_exact_rhs(d * d, mask_bd16) * (1.0 / HEAD64)
    dn = d * lax.rsqrt(var + RWKV_GN_EPS)
    for i, (bi, p) in enumerate(chains):
        sl = slice(p * LANES, (p + 1) * LANES)
        yn = dn[i * c:(i + 1) * c] * lnw_ref[:, sl] + lnb_ref[:, sl]
        o_ref[bi, :, sl] = ((yn + bonus_ref[bi, :, sl]) * g_ref[bi, :, sl]).astype(o_ref.dtype)


def _rwkv_scan(rt, kkt, kh, bh, v, g, bonus, gc, ln_w, ln_b):
    bsz, t, width = rt.shape
    n_pairs = width // LANES
    tok = pl.BlockSpec((bsz, CHUNK, width), lambda j: (0, j, 0))
    const = pl.BlockSpec((1, width), lambda j: (0, 0))
    return pl.pallas_call(
        functools.partial(_rwkv_scan_kernel, n_batch=bsz, n_pairs=n_pairs),
        out_shape=jax.ShapeDtypeStruct((bsz, t, width), BF16),
        grid=(t // CHUNK,),
        in_specs=[tok] * 7 + [pl.BlockSpec((bsz, 1, 1, width), lambda j: (0, j, 0, 0)),
                              const, const],
        out_specs=tok,
        scratch_shapes=[pltpu.VMEM((bsz * n_pairs, LANES, LANES), F32)],
        compiler_params=_cparams(("arbitrary",)),
        name="rwkv_scan",
    )(rt, kkt, kh, bh, v, g, bonus, gc, ln_w.reshape(1, -1), ln_b.reshape(1, -1))


def _proj_lru_kernel(x_ref, gn_ref, w_ref, cw_ref, cb_ref, wg_ref, ba_ref, bx_ref, lam_ref,
                     pa_o, o_ref, xcarry_ref, hcarry_ref, a_s, u_s, *, width, rw_cols):
    tm = x_ref.shape[1]

    @pl.when(pl.program_id(1) == 0)
    def _():
        xcarry_ref[...] = jnp.zeros_like(xcarry_ref)
        hcarry_ref[...] = jnp.zeros_like(hcarry_ref)

    x = x_ref[0]
    ms = jnp.mean(x * x, axis=-1, keepdims=True)
    hn = (x * lax.rsqrt(ms + RMS_EPS) * gn_ref[...]).astype(BF16)
    p_b = _dot(hn, w_ref[:, rw_cols:])
    pa_o[0] = _dot(hn, w_ref[:, :rw_cols])

    gate = p_b[:, 0:width]
    xb = p_b[:, width:2 * width]
    carry8 = xcarry_ref[...]
    row8 = _iota((8, width), 0)

    def shifted(s):
        rolled = pltpu.roll(xb, s, axis=0)
        first = jnp.where(row8 < s, pltpu.roll(carry8, s, axis=0), rolled[0:8])
        return jnp.concatenate([first, rolled[8:]], axis=0)

    xc = (cw_ref[0:1, :] * shifted(3) + cw_ref[1:2, :] * shifted(2)
          + cw_ref[2:3, :] * shifted(1) + cw_ref[3:4, :] * xb + cb_ref[...])
    xcarry_ref[...] = xb[tm - 8:tm, :]

    gates = _dot(xc.astype(BF16), wg_ref[...])
    ig = _sigmoid(gates[:, width:] + bx_ref[...])
    half_rate = (-0.5 * LRU_C) * _softplus(-lam_ref[...])
    log_a = half_rate * jnp.tanh(0.5 * (gates[:, :width] + ba_ref[...])) + half_rate
    a = jnp.exp(log_a)
    a_s[...] = a
    u_s[...] = jnp.sqrt(1.0 - a * a) * ig * xc

    m1 = row8 >= 1
    m2 = row8 >= 2
    m4 = row8 >= 4

    def body(i, h):
        off = pl.multiple_of(i * 8, 8)
        a8 = a_s[pl.ds(off, 8), :]
        u8 = u_s[pl.ds(off, 8), :]
        for s, m in ((1, m1), (2, m2), (4, m4)):
            u_sh = jnp.where(m, pltpu.roll(u8, s, axis=0), 0.0)
            a_sh = jnp.where(m, pltpu.roll(a8, s, axis=0), 1.0)
            u8 = u8 + a8 * u_sh
            a8 = a8 * a_sh
        h8 = u8 + a8 * h
        u_s[pl.ds(off, 8), :] = h8
        return jnp.broadcast_to(h8[7:8, :], (8, width))

    h_last = lax.fori_loop(0, tm // 8, body, hcarry_ref[...])
    hcarry_ref[...] = h_last
    o_ref[0] = (u_s[...] * _gelu_tanh(gate)).astype(o_ref.dtype)


def _block_diag(w):
    nb, di, do = w.shape
    eye = jnp.eye(nb, dtype=w.dtype)
    return (eye[:, None, :, None] * w[:, :, None, :]).reshape(nb * di, nb * do)


def _proj_lru(x3d, g_norm, w_in, rw_cols, conv_w, conv_b, w_a, b_a, w_x, b_x, lam, tm=512):
    b, t, d = x3d.shape
    tm = min(tm, t)
    cols = w_in.shape[1]
    width = (cols - rw_cols) // 2
    wg = jnp.concatenate([_block_diag(w_a), _block_diag(w_x)], axis=1).astype(BF16)
    row = lambda a_: a_.reshape(1, -1)
    const = lambda shp: pl.BlockSpec(shp, lambda i, j: (0, 0))
    return pl.pallas_call(
        functools.partial(_proj_lru_kernel, width=width, rw_cols=rw_cols),
        out_shape=[jax.ShapeDtypeStruct((b, t, rw_cols), F32),
                   jax.ShapeDtypeStruct((b, t, width), BF16)],
        grid=(b, t // tm),
        in_specs=[pl.BlockSpec((1, tm, d), lambda i, j: (i, j, 0)),
                  const((1, d)), const((d, cols)),
                  const(conv_w.shape), const((1, width)), const((width, 2 * width)),
                  const((1, width)), const((1, width)), const((1, width))],
        out_specs=[pl.BlockSpec((1, tm, rw_cols), lambda i, j: (i, j, 0)),
                   pl.BlockSpec((1, tm, width), lambda i, j: (i, j, 0))],
        scratch_shapes=[pltpu.VMEM((8, width), F32), pltpu.VMEM((8, width), F32),
                        pltpu.VMEM((tm, width), F32), pltpu.VMEM((tm, width), F32)],
        compiler_params=_cparams(("parallel", "arbitrary")),
        name="proj_rglru",
    )(x3d, row(g_norm), w_in.astype(BF16), conv_w, row(conv_b), wg, row(b_a), row(b_x), row(lam))


def _gla_kernel(q_ref, f_ref, i_ref, g_ref, lb_ref, nw_ref, o_ref, st_ref, *, n_batch, n_heads):
    c = CHUNK

    @pl.when(pl.program_id(0) == 0)
    def _():
        st_ref[...] = jnp.zeros_like(st_ref)

    trilf = jnp.where(_iota((c, c), 1) <= _iota((c, c), 0), 1.0, 0.0)
    tril16 = trilf.astype(BF16)
    mid = c // 2 - 1
    lb = lb_ref[...]
    fg_c0 = 0.5 + 0.5 * lb
    fg_c1 = 0.5 - 0.5 * lb

    qm, km, v16, gc, em, ecm = [], [], [], [], [], []
    for bi in range(n_batch):
        q = _silu(q_ref[bi].astype(F32))
        fg = fg_c0 + fg_c1 * jnp.tanh(0.5 * f_ref[bi])
        k = 1.0 - fg
        cum = _dot_exact_lhs(tril16, jnp.log(fg))
        cum_c = cum[c - 1:c, :]
        cum_m = cum[mid:mid + 1, :]
        qm.append((q * jnp.exp(cum - cum_m)).astype(BF16))
        km.append((k * jnp.exp(cum_m - cum)).astype(BF16))
        v16.append(i_ref[bi].astype(BF16))
        gc.append(jnp.exp(cum_c))
        em.append(jnp.exp(cum_m))
        ecm.append(jnp.exp(cum_c - cum_m))

    chains = [(bi, h) for bi in range(n_batch) for h in range(n_heads)]
    idx = range(len(chains))

    def hs(xs):
        return [xs[bi][:, h * LANES:(h + 1) * LANES] for bi, h in chains]

    qm_c, km_c, v_c, gc_c, em_c, ecm_c = hs(qm), hs(km), hs(v16), hs(gc), hs(em), hs(ecm)
    scores = [(_dot_nt(qm_c[i], km_c[i]) * trilf).astype(BF16) for i in idx]
    st = [st_ref[i] for i in idx]
    st16 = [(st[i] * em_c[i]).astype(BF16) for i in idx]
    o = [_dot(scores[i], v_c[i]) + _dot_nt(qm_c[i], st16[i]) for i in idx]
    for i in idx:
        st_ref[i] = st[i] * gc_c[i] + _dot_tn(v_c[i], km_c[i]) * ecm_c[i]

    for i, (bi, h) in enumerate(chains):
        sl = slice(h * LANES, (h + 1) * LANES)
        on = o[i] * lax.rsqrt(jnp.mean(o[i] * o[i], axis=-1, keepdims=True) + RMS_EPS)
        gate = _silu(g_ref[bi, :, sl].astype(F32))
        o_ref[bi, :, sl] = (on * nw_ref[:, sl] * gate).astype(o_ref.dtype)


def _gla(q, f, i, g, lower_bound, norm_w):
    bsz, t, width = q.shape
    n_heads = width // LANES
    tok = pl.BlockSpec((bsz, CHUNK, width), lambda j: (0, j, 0))
    const = pl.BlockSpec((1, width), lambda j: (0, 0))
    return pl.pallas_call(
        functools.partial(_gla_kernel, n_batch=bsz, n_heads=n_heads),
        out_shape=jax.ShapeDtypeStruct((bsz, t, width), BF16),
        grid=(t // CHUNK,),
        in_specs=[tok] * 4 + [const, const],
        out_specs=tok,
        scratch_shapes=[pltpu.VMEM((bsz * n_heads, LANES, LANES), F32)],
        compiler_params=_cparams(("arbitrary",)),
        name="hgrn2_gla",
    )(q, f, i, g, lower_bound.reshape(1, -1), norm_w.reshape(1, -1))


def _route_tile(x, g_ref, w_ref, b_ref, upper_ref, e_o, gate_o, rank_o, cnt_o, hn_o):
    ms = jnp.mean(x * x, axis=-1, keepdims=True)
    hn = x * lax.rsqrt(ms + RMS_EPS) * g_ref[...]
    hn_o[...] = hn.astype(BF16)
    lt = _dot_split(hn, w_ref[0], w_ref[1]).T + b_ref[:, 0:1]
    tm = x.shape[0]
    gl = lt[0:8, :]
    row8 = _iota((8, tm), 0)
    gl = jnp.where(row8 < N_GROUPS, gl, -jnp.inf)
    gmax = jnp.max(gl, axis=0, keepdims=True)
    g_sel = jnp.min(jnp.where(gl == gmax, row8, 8), axis=0, keepdims=True)
    g_gate = 1.0 / jnp.sum(jnp.exp(gl - gmax), axis=0, keepdims=True)

    el = jnp.zeros((EXPERTS_PER_GROUP, tm), F32)
    for gi in range(N_GROUPS):
        lo = 8 + gi * EXPERTS_PER_GROUP
        el = jnp.where(g_sel == gi, lt[lo:lo + EXPERTS_PER_GROUP, :], el)
    m1 = jnp.max(el, axis=0, keepdims=True)
    i1 = jnp.min(jnp.where(el == m1, row8, 8), axis=0, keepdims=True)
    el2 = jnp.where(row8 == i1, -jnp.inf, el)
    m2 = jnp.max(el2, axis=0, keepdims=True)
    i2 = jnp.min(jnp.where(el2 == m2, row8, 8), axis=0, keepdims=True)
    e2 = jnp.exp(m2 - m1)
    inv = 1.0 / (1.0 + e2)
    ea = g_sel * EXPERTS_PER_GROUP + i1
    eb = g_sel * EXPERTS_PER_GROUP + i2
    e_o[...] = jnp.concatenate([ea, eb], axis=0)
    gate_o[...] = jnp.concatenate([g_gate * inv, g_gate * e2 * inv], axis=0)

    erow = _iota((N_EXPERTS, tm), 0)
    oh_a = jnp.where(erow == ea, 1.0, 0.0)
    oh_b = jnp.where(erow == eb, 1.0, 0.0)
    pre_a = _dot(oh_a.astype(BF16), upper_ref[...])
    pre_b = _dot(oh_b.astype(BF16), upper_ref[...])
    cnt_a = jnp.sum(oh_a, axis=1, keepdims=True)
    cnt_b = jnp.sum(oh_b, axis=1, keepdims=True)
    rank_a = jnp.sum(oh_a * pre_a, axis=0, keepdims=True)
    rank_b = jnp.sum(oh_b * (pre_b + cnt_a), axis=0, keepdims=True)
    rank_o[...] = jnp.concatenate([rank_a, rank_b], axis=0).astype(I32)
    cnt_o[0] = jnp.broadcast_to(cnt_a + cnt_b, (N_EXPERTS, LANES))


def _seg_local_rows(tm):
    return 2 * tm + N_EXPERTS * SUBLANES


def _segment_copies(cnt_ref, off_ref, dst_ref, base, local, remote, sem, to_remote, wait):
    def body(e, carry):
        c = pl.multiple_of(cnt_ref[base + e], SUBLANES)

        @pl.when(c > 0)
        def _():
            off = 0 if off_ref is None else pl.multiple_of(off_ref[base + e], SUBLANES)
            loc = local.at[pl.ds(off, c), :]
            rem = remote.at[pl.ds(pl.multiple_of(dst_ref[base + e], SUBLANES), c), :]
            cp = (pltpu.make_async_copy(loc, rem, sem) if to_remote
                  else pltpu.make_async_copy(rem, loc, sem))
            if wait:
                cp.wait()
            else:
                cp.start()
        return carry
    lax.fori_loop(0, N_EXPERTS, body, 0, unroll=4)


def _local_index(e_ref, rank_ref, off_ref, tile):
    e = e_ref[...]
    lidx = rank_ref[...]
    for ex in range(N_EXPERTS):
        lidx = lidx + jnp.where(e == ex, off_ref[tile * N_EXPERTS + ex], 0)
    return lidx


def _segment_wait(tot_ref, tile, local, remote, sem, to_remote):
    tot = pl.multiple_of(tot_ref[tile], SUBLANES)

    @pl.when(tot > 0)
    def _():
        loc = local.at[pl.ds(0, tot), :]
        rem = remote.at[pl.ds(0, tot), :]
        cp = (pltpu.make_async_copy(loc, rem, sem) if to_remote
              else pltpu.make_async_copy(rem, loc, sem))
        cp.wait()


def _dispatch_kernel(cnt_ref, off_ref, dst_ref, tot_ref, tcnt_ref, tdst_ref, nused_ref,
                     hn_ref, e_ref, rank_ref, xs_out, sbuf, zbuf, sems, zsem):
    j = pl.program_id(0)
    nb = pl.num_programs(0)
    tm = hn_ref.shape[0]
    slot = lax.rem(j, 2)

    def copies(tile, s, wait):
        if wait:
            _segment_wait(tot_ref, tile, sbuf.at[s], xs_out, sems.at[s], True)
        else:
            _segment_copies(cnt_ref, off_ref, dst_ref, tile * N_EXPERTS, sbuf.at[s], xs_out,
                            sems.at[s], True, False)

    @pl.when(j == 0)
    def _():
        zbuf[...] = jnp.zeros_like(zbuf)
        rows = zbuf.shape[0]
        n_blocks = xs_out.shape[0] // rows
        for wait in (False, True):
            _segment_copies(tcnt_ref, None, tdst_ref, 0, zbuf, xs_out, zsem, True, wait)

            def body(b, carry):
                start = pl.multiple_of(b * rows, SUBLANES)
                cp = pltpu.make_async_copy(zbuf, xs_out.at[pl.ds(start, rows), :], zsem)
                if wait:
                    cp.wait()
                else:
                    cp.start()
                return carry
            lax.fori_loop(nused_ref[0], n_blocks, body, 0)

    @pl.when(j >= 2)
    def _():
        copies(j - 2, slot, True)

    r = _iota((_seg_local_rows(tm), tm), 0)
    lidx = _local_index(e_ref, rank_ref, off_ref, j)
    perm = jnp.where((r == lidx[0:1, :]) | (r == lidx[1:2, :]), 1.0, 0.0).astype(BF16)
    sbuf[slot] = _pack_pairs(_dot(perm, hn_ref[...]))
    copies(j, slot, False)

    @pl.when(j == nb - 1)
    def _():
        copies(j, slot, True)

    @pl.when((j == nb - 1) & (nb >= 2))
    def _():
        copies(j - 1, 1 - slot, True)


def _dispatch(hn16, e2, rank2, seg, tails, n_used, n_rows, rows, tm):
    n, d = hn16.shape
    return pl.pallas_call(
        _dispatch_kernel,
        out_shape=jax.ShapeDtypeStruct((n_rows, d // 2), U32),
        grid_spec=pltpu.PrefetchScalarGridSpec(
            num_scalar_prefetch=7,
            grid=(n // tm,),
            in_specs=[pl.BlockSpec((tm, d), lambda j, *_: (j, 0)),
                      pl.BlockSpec((2, tm), lambda j, *_: (0, j)),
                      pl.BlockSpec((2, tm), lambda j, *_: (0, j))],
            out_specs=pl.BlockSpec(memory_space=pl.ANY),
            scratch_shapes=[pltpu.VMEM((2, _seg_local_rows(tm), d // 2), U32),
                            pltpu.VMEM((rows, d // 2), U32),
                            pltpu.SemaphoreType.DMA((2,)), pltpu.SemaphoreType.DMA(())]),
        compiler_params=_cparams(("arbitrary",)),
        name="moe_dispatch",
    )(*seg, *tails, n_used, hn16, e2, rank2)


def _expert_kernel(be_ref, first_ref, nused_ref, slot_ref, next_ref, x_ref, wg_hbm, wu_hbm,
                   wd_hbm, o_ref, wg32, wu32, wd32, wg16, wu16, wd16, wsem, *, layer):
    j = pl.program_id(0)

    def weight_copies(e, s):
        return [pltpu.make_async_copy(hbm.at[layer, e], buf.at[s], wsem.at[s])
                for hbm, buf in ((wg_hbm, wg32), (wu_hbm, wu32), (wd_hbm, wd32))]

    @pl.when((first_ref[j] == 1) & (j < nused_ref[0]))
    def _():
        s = slot_ref[j]

        @pl.when(j == 0)
        def _():
            for cp in weight_copies(be_ref[0], 0):
                cp.start()

        for cp in weight_copies(be_ref[j], s):
            cp.wait()

        @pl.when(next_ref[j] >= 0)
        def _():
            for cp in weight_copies(next_ref[j], 1 - s):
                cp.start()

        wg16[...] = wg32[s].astype(BF16)
        wu16[...] = wu32[s].astype(BF16)
        wd16[...] = wd32[s].astype(BF16)

    @pl.when(j < nused_ref[0])
    def _():
        x_lo, x_hi = _unpack_pairs(x_ref[...])
        half = x_lo.shape[1]
        hg = _dot(x_lo, wg16[:half, :]) + _dot(x_hi, wg16[half:, :])
        hu = _dot(x_lo, wu16[:half, :]) + _dot(x_hi, wu16[half:, :])
        hid = (_silu(hg) * hu).astype(BF16)
        o_ref[...] = _pack_pairs(_dot(hid, wd16[...]))

    @pl.when(j >= nused_ref[0])
    def _():
        o_ref[...] = jnp.zeros_like(o_ref)


def _expert_ffn(xs, blocks, w_gate, w_up, w_down, layer, rows):
    n_rows, dh = xs.shape
    n_blocks = n_rows // rows
    d, ff = w_gate.shape[-2:]
    assert d == 2 * dh
    any_spec = pl.BlockSpec(memory_space=pl.ANY)
    return pl.pallas_call(
        functools.partial(_expert_kernel, layer=layer),
        out_shape=jax.ShapeDtypeStruct((n_rows, dh), U32),
        grid_spec=pltpu.PrefetchScalarGridSpec(
            num_scalar_prefetch=5,
            grid=(n_blocks,),
            in_specs=[pl.BlockSpec((rows, dh),
                                   lambda j, be, fi, nu, *_: (
                                       jnp.maximum(jnp.minimum(j, nu[0] - 1), 0), 0)),
                      any_spec, any_spec, any_spec],
            out_specs=pl.BlockSpec((rows, dh), lambda j, *_: (j, 0)),
            scratch_shapes=[pltpu.VMEM((2, d, ff), F32), pltpu.VMEM((2, d, ff), F32),
                            pltpu.VMEM((2, ff, d), F32),
                            pltpu.VMEM((d, ff), BF16), pltpu.VMEM((d, ff), BF16),
                            pltpu.VMEM((ff, d), BF16), pltpu.SemaphoreType.DMA((2,))]),
        compiler_params=_cparams(("arbitrary",)),
        name="moe_expert_ffn",
    )(*blocks, xs, w_gate, w_up, w_down)


def _combine_kernel(cnt_ref, off_ref, dst_ref, tot_ref, x_ref, e_ref, rank_ref, gt_ref, ys_hbm,
                    gfin_ref, o_ref, ybuf, sems, *, final_norm):
    j = pl.program_id(0)
    nb = pl.num_programs(0)
    tm = x_ref.shape[0]
    slot = lax.rem(j, 2)

    def copies(tile, s, wait):
        if wait:
            _segment_wait(tot_ref, tile, ybuf.at[s], ys_hbm, sems.at[s], False)
        else:
            _segment_copies(cnt_ref, off_ref, dst_ref, tile * N_EXPERTS, ybuf.at[s], ys_hbm,
                            sems.at[s], False, False)

    @pl.when(j == 0)
    def _():
        ybuf[...] = jnp.zeros_like(ybuf)
        copies(0, 0, False)

    @pl.when(j + 1 < nb)
    def _():
        copies(j + 1, 1 - slot, False)

    copies(j, slot, True)
    lidx = _local_index(e_ref, rank_ref, off_ref, j)
    hi = (lidx >> DIGIT_BITS).astype(F32)
    lo = (lidx & ((1 << DIGIT_BITS) - 1)).astype(F32)
    rows8 = jnp.concatenate([hi, lo, gt_ref[...], jnp.zeros((2, tm), F32)], axis=0).astype(BF16)
    sel = jnp.where(_iota((SUBLANES, 6 * LANES), 0)
                    == (_iota((SUBLANES, 6 * LANES), 1) >> (LANES.bit_length() - 1)),
                    1.0, 0.0).astype(BF16)
    cols = _dot_tn(rows8, sel)
    col = lambda k: cols[:, k * LANES:(k + 1) * LANES]
    la = col(0) * float(1 << DIGIT_BITS) + col(2)
    lb = col(1) * float(1 << DIGIT_BITS) + col(3)
    ga, gb = col(4), col(5)
    lane = _iota((tm, LANES), 1).astype(F32)
    mix = jnp.concatenate(
        [jnp.where(lane + float(q * LANES) == la, ga, 0.0)
         + jnp.where(lane + float(q * LANES) == lb, gb, 0.0)
         for q in range(_seg_local_rows(tm) // LANES)], axis=1).astype(BF16)
    y_lo, y_hi = _unpack_pairs(ybuf[slot])
    y = x_ref[...] + jnp.concatenate([_dot(mix, y_lo), _dot(mix, y_hi)], axis=1)
    if final_norm:
        ms = jnp.mean(y * y, axis=-1, keepdims=True)
        y = y * lax.rsqrt(ms + RMS_EPS) * gfin_ref[...]
    o_ref[...] = y


def _combine(x2d, ys, e2, rank2, gates2, seg, g_final, final_norm, tm):
    n, d = x2d.shape
    return pl.pallas_call(
        functools.partial(_combine_kernel, final_norm=final_norm),
        out_shape=jax.ShapeDtypeStruct((n, d), F32),
        grid_spec=pltpu.PrefetchScalarGridSpec(
            num_scalar_prefetch=4,
            grid=(n // tm,),
            in_specs=[pl.BlockSpec((tm, d), lambda j, *_: (j, 0)),
                      pl.BlockSpec((2, tm), lambda j, *_: (0, j)),
                      pl.BlockSpec((2, tm), lambda j, *_: (0, j)),
                      pl.BlockSpec((2, tm), lambda j, *_: (0, j)),
                      pl.BlockSpec(memory_space=pl.ANY),
                      pl.BlockSpec((1, d), lambda j, *_: (0, 0))],
            out_specs=pl.BlockSpec((tm, d), lambda j, *_: (j, 0)),
            scratch_shapes=[pltpu.VMEM((2, _seg_local_rows(tm), d // 2), U32),
                            pltpu.SemaphoreType.DMA((2,))]),
        compiler_params=_cparams(("arbitrary",)),
        name="moe_combine",
    )(*seg, x2d, e2, rank2, gates2, ys, g_final.reshape(1, d))


def _route_plan(tile_cnt, n_tok, rows, tm):
    n_tiles = n_tok // tm
    seg_cnt = (tile_cnt + SUBLANES - 1) // SUBLANES * SUBLANES
    counts = jnp.sum(seg_cnt, axis=0)
    padded = (counts + rows - 1) // rows * rows
    pend = jnp.cumsum(padded)
    pstart = pend - padded
    seg_dst = pstart[None, :] + jnp.cumsum(seg_cnt, axis=0) - seg_cnt
    seg_off = jnp.cumsum(seg_cnt, axis=1) - seg_cnt
    n_blocks = -(-(2 * n_tok + n_tiles * N_EXPERTS * SUBLANES) // rows) + N_EXPERTS
    starts = jnp.arange(n_blocks, dtype=I32) * rows
    block_e = jnp.minimum(jnp.sum((starts[:, None] >= pend[None, :]).astype(I32), axis=1),
                          N_EXPERTS - 1)
    first = jnp.concatenate([jnp.ones((1,), I32),
                             (block_e[1:] != block_e[:-1]).astype(I32)])
    n_used = (pend[-1] // rows).astype(I32).reshape(1)
    bidx = jnp.arange(n_blocks, dtype=I32)
    w_slot = (jnp.cumsum(first) - 1) % 2
    later_first = (bidx[None, :] > bidx[:, None]) & (first[None, :] == 1) & (bidx[None, :] < n_used)
    next_e = jnp.where(jnp.any(later_first, axis=1),
                       block_e[jnp.argmax(later_first, axis=1)], -1)
    blocks = (block_e.astype(I32), first, n_used, w_slot.astype(I32), next_e.astype(I32))
    seg = tuple(a.reshape(-1).astype(I32)
                for a in (seg_cnt, seg_off, seg_dst, jnp.sum(seg_cnt, axis=1)))
    tails = ((padded - counts).astype(I32), (pstart + counts).astype(I32))
    return seg, tails, blocks, n_blocks * rows


def _moe(x2d, route, w_gate, w_up, w_down, layer, g_final, final_norm, tm, rows=ROUTE_ROWS):
    e2, gates2, rank2, cnt, hn16 = route
    seg, tails, blocks, n_rows = _route_plan(cnt[:, :, 0].astype(I32), e2.shape[1], rows, tm)
    xs = _dispatch(hn16, e2, rank2, seg, tails, blocks[2], n_rows, rows, tm)
    ys = _expert_ffn(xs, blocks, w_gate, w_up, w_down, layer, rows)
    return _combine(x2d, ys, e2, rank2, gates2, seg, g_final, final_norm, tm)


def kernel(x, norm_mix, norm_ffn, norm_final, ab_w_in, rw_mu, rw_w0, rw_w2, rw_a0, rw_a2, rw_g2, rw_k_k, rw_k_a, rw_r_k, rw_ln_w, rw_ln_b, lru_conv_w, lru_conv_b, lru_w_a, lru_b_a, lru_w_x, lru_b_x, lru_lambda, ab_w_out, c_w_in, c_lower_bound, c_norm_w, c_w_out, moe_w_group, moe_b_group, moe_w_expert, moe_b_expert, moe_w_gate, moe_w_up, moe_w_down):
    bsz, t, d = x.shape
    depth = norm_mix.shape[0]
    n = bsz * t
    lbs = jnp.cumsum(jax.nn.softmax(c_lower_bound.astype(F32), axis=0), axis=0)
    lbs = lbs - lbs[0]
    x2d = x.reshape(n, d)
    for layer in range(depth):
        j = layer // 2
        if layer % 2 == 0:
            rw_cols = rw_mu.shape[1]
            width = rw_w0.shape[1]
            p_a, yb = _proj_lru(x2d.reshape(bsz, t, d), norm_mix[layer], ab_w_in[j], rw_cols,
                                lru_conv_w[j], lru_conv_b[j], lru_w_a[j], lru_b_a[j],
                                lru_w_x[j], lru_b_x[j], lru_lambda[j])
            prep = _rwkv_prep(p_a, rw_mu[j], rw_w0[j], rw_w2[j], rw_a0[j], rw_a2[j], rw_g2[j],
                              rw_k_k[j], rw_k_a[j], rw_r_k[j].reshape(-1))
            ya = _rwkv_scan(*prep, rw_ln_w[j], rw_ln_b[j])
            ys = [ya.reshape(n, width), yb.reshape(n, -1)]
            ws = [ab_w_out[j][:width], ab_w_out[j][width:]]
        else:
            hw = c_norm_w.shape[1]
            q, f, i_, g = _norm_matmul(x2d, norm_mix[layer], c_w_in[j], (hw,) * 4,
                                       (BF16, F32, BF16, BF16))
            shp = (bsz, t, hw)
            o = _gla(q.reshape(shp), f.reshape(shp), i_.reshape(shp), g.reshape(shp),
                     lbs[layer], c_norm_w[j])
            ys = [o.reshape(n, hw)]
            ws = [c_w_out[j]]
        tm = min(ROUTE_TILE, n)
        x2d = _proj_residual(x2d, ys, ws)
        route = _router(x2d, norm_ffn[layer], moe_w_group[layer], moe_b_group[layer],
                        moe_w_expert[layer], moe_b_expert[layer], tm)
        x2d = _moe(x2d, route, moe_w_gate, moe_w_up, moe_w_down, layer, norm_final,
                   layer == depth - 1, tm)
    return x2d.reshape(bsz, t, d)
```

```python
import functools

import jax
import jax.numpy as jnp
from jax import lax
from jax.experimental import pallas as pl
from jax.experimental.pallas import tpu as pltpu

F32 = jnp.float32
BF16 = jnp.bfloat16
I32 = jnp.int32
U32 = jnp.uint32

RMS_EPS = 1e-6
RWKV_GN_EPS = 64e-5
LRU_C = 8.0
CHUNK = 64
CHUNK_SHIFT = CHUNK.bit_length() - 1
HEAD64 = 64
HEAD_SHIFT = HEAD64.bit_length() - 1
LANES = 128
SUBLANES = 8
N_GROUPS = 4
EXPERTS_PER_GROUP = 8
N_EXPERTS = N_GROUPS * EXPERTS_PER_GROUP
ROUTE_ROWS = 512
ROUTE_TILE = 512
DIGIT_BITS = 8
VMEM_LIMIT = 56 * 1024 * 1024


def _cparams(sem):
    return pltpu.CompilerParams(dimension_semantics=sem, vmem_limit_bytes=VMEM_LIMIT)


def _sigmoid(x):
    return 0.5 * jnp.tanh(0.5 * x) + 0.5


def _pack_pairs(x):
    w = x.shape[1] // 2
    lo = lax.bitcast_convert_type(x[:, :w].astype(BF16).astype(F32), U32)
    hi = lax.bitcast_convert_type(x[:, w:].astype(BF16).astype(F32), U32)
    return (lo >> 16) | hi


def _unpack_pairs(p):
    lo = lax.bitcast_convert_type(p << 16, F32).astype(BF16)
    hi = lax.bitcast_convert_type(p & jnp.uint32(0xFFFF0000), F32).astype(BF16)
    return lo, hi


def _softplus(x):
    return jnp.maximum(x, 0.0) + jnp.log(1.0 + jnp.exp(-jnp.abs(x)))


def _silu(x):
    h = 0.5 * x
    return h * jnp.tanh(h) + h


def _gelu_tanh(x):
    return 0.5 * x * (1.0 + jnp.tanh(0.7978845608028654 * (x + 0.044715 * x * x * x)))


def _dot(a, b):
    return jnp.dot(a, b, preferred_element_type=F32)


def _dot_nt(a, b):
    return lax.dot_general(a, b, (((1,), (1,)), ((), ())), preferred_element_type=F32)


def _dot_tn(a, b):
    return lax.dot_general(a, b, (((0,), (0,)), ((), ())), preferred_element_type=F32)


def _split2(x):
    hi = x.astype(BF16)
    lo = (x - hi.astype(F32)).astype(BF16)
    return hi, lo


def _dot_exact_rhs(x, m_bf16):
    hi, lo = _split2(x)
    return _dot(hi, m_bf16) + _dot(lo, m_bf16)


def _dot_exact_lhs(m_bf16, x):
    hi, lo = _split2(x)
    return _dot(m_bf16, hi) + _dot(m_bf16, lo)


def _dot_split(a, w_hi, w_lo):
    a_hi, a_lo = _split2(a)
    return _dot(a_hi, w_hi) + (_dot(a_lo, w_hi) + _dot(a_hi, w_lo))


def _hi_lo(w):
    hi = w.astype(BF16)
    return hi, (w - hi.astype(F32)).astype(BF16)


def _iota(shape, dim):
    return lax.broadcasted_iota(I32, shape, dim)


def _norm_matmul_kernel(x_ref, g_ref, w_ref, *o_refs, splits):
    x = x_ref[...]
    ms = jnp.mean(x * x, axis=-1, keepdims=True)
    y = (x * lax.rsqrt(ms + RMS_EPS) * g_ref[...]).astype(BF16)
    off = 0
    for o_ref, n in zip(o_refs, splits):
        o_ref[...] = _dot(y, w_ref[:, off:off + n]).astype(o_ref.dtype)
        off += n


def _norm_matmul(x2d, g, w, splits, out_dtypes, tm=1024):
    n, d = x2d.shape
    tm = min(tm, n)
    ncols = w.shape[1]
    assert sum(splits) == ncols and n % tm == 0 and len(out_dtypes) == len(splits)
    return pl.pallas_call(
        functools.partial(_norm_matmul_kernel, splits=splits),
        out_shape=[jax.ShapeDtypeStruct((n, s), dt) for s, dt in zip(splits, out_dtypes)],
        grid=(n // tm,),
        in_specs=[pl.BlockSpec((tm, d), lambda i: (i, 0)),
                  pl.BlockSpec((1, d), lambda i: (0, 0)),
                  pl.BlockSpec((d, ncols), lambda i: (0, 0))],
        out_specs=[pl.BlockSpec((tm, s), lambda i: (i, 0)) for s in splits],
        compiler_params=_cparams(("parallel",)),
        name="norm_matmul",
    )(x2d, g.reshape(1, d), w.astype(BF16))


def _proj_residual_kernel(*refs, n_in):
    x_ref = refs[0]
    y_refs = refs[1:1 + n_in]
    w_refs = refs[1 + n_in:1 + 2 * n_in]
    o_ref = refs[1 + 2 * n_in]
    acc = x_ref[...]
    for y_ref, w_ref in zip(y_refs, w_refs):
        acc = acc + _dot(y_ref[...].astype(BF16), w_ref[...])
    o_ref[...] = acc


def _proj_residual(x2d, ys, ws, tm=1024):
    n, d = x2d.shape
    tm = min(tm, n)
    n_in = len(ys)
    in_specs = [pl.BlockSpec((tm, d), lambda i: (i, 0))]
    in_specs += [pl.BlockSpec((tm, y.shape[1]), lambda i: (i, 0)) for y in ys]
    in_specs += [pl.BlockSpec(w.shape, lambda i: (0, 0)) for w in ws]
    return pl.pallas_call(
        functools.partial(_proj_residual_kernel, n_in=n_in),
        out_shape=jax.ShapeDtypeStruct((n, d), F32),
        grid=(n // tm,),
        in_specs=in_specs,
        out_specs=pl.BlockSpec((tm, d), lambda i: (i, 0)),
        compiler_params=_cparams(("parallel",)),
        name="proj_residual",
    )(x2d, *ys, *[w.astype(BF16) for w in ws])


def _router_kernel(x_ref, g_ref, w_ref, b_ref, upper_ref, e_o, gate_o, rank_o, cnt_o, hn_o):
    _route_tile(x_ref[...], g_ref, w_ref, b_ref, upper_ref, e_o, gate_o, rank_o, cnt_o, hn_o)


def _router(x2d, g_ffn, w_group, b_group, w_expert, b_expert, tm):
    n, d = x2d.shape
    wt = jnp.zeros((d, LANES), F32)
    wt = wt.at[:, 0:N_GROUPS].set(w_group).at[:, 8:8 + N_EXPERTS].set(w_expert)
    wt = jnp.stack(_hi_lo(wt))
    bt = jnp.zeros((LANES,), F32)
    bt = bt.at[0:N_GROUPS].set(b_group).at[8:8 + N_EXPERTS].set(b_expert)
    bt = jnp.broadcast_to(bt[:, None], (LANES, LANES))
    ti = jnp.arange(tm, dtype=I32)
    upper = (ti[:, None] < ti[None, :]).astype(BF16)
    tok2 = pl.BlockSpec((2, tm), lambda i: (0, i))
    return pl.pallas_call(
        _router_kernel,
        out_shape=[jax.ShapeDtypeStruct((2, n), I32),
                   jax.ShapeDtypeStruct((2, n), F32),
                   jax.ShapeDtypeStruct((2, n), I32),
                   jax.ShapeDtypeStruct((n // tm, N_EXPERTS, LANES), F32),
                   jax.ShapeDtypeStruct((n, d), BF16)],
        grid=(n // tm,),
        in_specs=[pl.BlockSpec((tm, d), lambda i: (i, 0)),
                  pl.BlockSpec((1, d), lambda i: (0, 0)),
                  pl.BlockSpec((2, d, LANES), lambda i: (0, 0, 0)),
                  pl.BlockSpec((LANES, LANES), lambda i: (0, 0)),
                  pl.BlockSpec((tm, tm), lambda i: (0, 0))],
        out_specs=[tok2, tok2, tok2,
                   pl.BlockSpec((1, N_EXPERTS, LANES), lambda i: (i, 0, 0)),
                   pl.BlockSpec((tm, d), lambda i: (i, 0))],
        compiler_params=_cparams(("parallel",)),
        name="moe_router",
    )(x2d, g_ffn.reshape(1, d), wt, bt, upper)


def _rwkv_prep_kernel(p_ref, mu_ref, w0_ref, a0_ref, kk_s_ref, ka_ref, rk_ref,
                      wcomb_ref, g2_ref, tril_ref,
                      rt_o, kkt_o, kh_o, bh_o, v_o, g_o, bonus_o, gc_o,
                      prev_ref, *, width):
    tm = p_ref.shape[1]

    @pl.when(pl.program_id(1) == 0)
    def _():
        prev_ref[...] = jnp.zeros_like(prev_ref)

    p = p_ref[0]
    rolled = pltpu.roll(p, 1, axis=0)
    head = jnp.where(_iota((SUBLANES, p.shape[1]), 0) == 0, prev_ref[...], rolled[0:SUBLANES])
    prev = jnp.concatenate([head, rolled[SUBLANES:]], axis=0)
    prev_ref[...] = p[tm - 1:tm, :]
    ps = p + (prev - p) * mu_ref[...]

    r = ps[:, 0:width]
    k = ps[:, width:2 * width]
    v = ps[:, 2 * width:3 * width]
    lowrank = ps[:, 3 * width:3 * width + LANES]
    gl = ps[:, 3 * width + LANES:3 * width + 2 * LANES]

    lane = _iota(lowrank.shape, 1)
    lr_in = jnp.where(lane < HEAD64, jnp.tanh(lowrank), lowrank)
    t12 = _dot_split(lr_in, wcomb_ref[0], wcomb_ref[1])
    wlog = -_softplus(-(w0_ref[...] + t12[:, :width])) - 0.5
    lw = -jnp.exp(wlog)
    a = _sigmoid(a0_ref[...] + t12[:, width:])
    g = _dot_split(_sigmoid(gl), g2_ref[0], g2_ref[1])

    seg = jnp.where((_iota((LANES, LANES), 0) >> HEAD_SHIFT)
                    == (_iota((LANES, LANES), 1) >> HEAD_SHIFT),
                    1.0, 0.0).astype(BF16)

    def head_sums(x):
        return jnp.concatenate([_dot_exact_rhs(x[:, q * LANES:(q + 1) * LANES], seg)
                                for q in range(width // LANES)], axis=1)

    kk = k * kk_s_ref[...]
    nrm = jnp.sqrt(head_sums(kk * kk))
    kk = kk / jnp.maximum(nrm, 1e-12)
    k2 = k * (1.0 + (a - 1.0) * ka_ref[...])
    bonus = head_sums(r * k2 * rk_ref[...]) * v

    b = kk * a

    n_chunk = tm // CHUNK
    cum = _dot_exact_lhs(tril_ref[...], lw)
    cum3 = cum.reshape(n_chunk, CHUNK, width)
    cend = cum3[:, CHUNK - 1:CHUNK, :]
    e_neg = jnp.exp(-cum)

    rt_o[0] = (r * jnp.exp(cum)).astype(BF16)
    kkt_o[0] = (kk * jnp.exp(cum - lw)).astype(BF16)
    kh_o[0] = (k2 * e_neg).astype(BF16)
    bh_o[0] = (b * e_neg).astype(BF16)
    v_o[0] = v.astype(BF16)
    g_o[0] = g.astype(BF16)
    bonus_o[0] = bonus.astype(BF16)
    gc_o[0] = jnp.exp(cend)


def _rwkv_prep(p_a, mu, w0, w2, a0, a2, g2, k_k, k_a, r_k, tm=512):
    b, t, cols = p_a.shape
    width = w0.shape[0]
    rank = w2.shape[0]
    assert rank == HEAD64 and a2.shape[0] == HEAD64 and g2.shape[0] == LANES
    zeros = jnp.zeros((rank, width), F32)
    wcomb = jnp.concatenate([jnp.concatenate([w2, zeros], 1),
                             jnp.concatenate([zeros, a2], 1)], 0)
    row = lambda a_: a_.reshape(1, -1)
    const = lambda shp: pl.BlockSpec(shp, lambda i, j: (0,) * len(shp))
    tok = pl.BlockSpec((1, tm, width), lambda i, j: (i, j, 0))
    n_chunk = tm // CHUNK
    ti = jnp.arange(tm, dtype=I32)
    tril_bd = (((ti[:, None] >> CHUNK_SHIFT) == (ti[None, :] >> CHUNK_SHIFT))
               & (ti[None, :] <= ti[:, None])).astype(BF16)
    return pl.pallas_call(
        functools.partial(_rwkv_prep_kernel, width=width),
        out_shape=([jax.ShapeDtypeStruct((b, t, width), BF16)] * 7
                   + [jax.ShapeDtypeStruct((b, t // CHUNK, 1, width), F32)]),
        grid=(b, t // tm),
        in_specs=[pl.BlockSpec((1, tm, cols), lambda i, j: (i, j, 0)),
                  const((1, cols)), const((1, width)), const((1, width)), const((1, width)),
                  const((1, width)), const((1, width)),
                  const((2, LANES, 2 * width)), const((2, LANES, width)), const((tm, tm))],
        out_specs=[tok] * 7 + [pl.BlockSpec((1, n_chunk, 1, width), lambda i, j: (i, j, 0, 0))],
        scratch_shapes=[pltpu.VMEM((1, cols), F32)],
        compiler_params=_cparams(("parallel", "arbitrary")),
        name="rwkv_prep",
    )(p_a, row(mu), row(w0), row(a0), row(k_k), row(k_a), row(r_k),
      jnp.stack(_hi_lo(wcomb)), jnp.stack(_hi_lo(g2)), tril_bd)


def _rwkv_scan_kernel(rt_ref, kkt_ref, kh_ref, bh_ref, v_ref, g_ref, bonus_ref,
                      gc_ref, lnw_ref, lnb_ref, o_ref, st_ref, *, n_batch, n_pairs):
    c = CHUNK

    @pl.when(pl.program_id(0) == 0)
    def _():
        st_ref[...] = jnp.zeros_like(st_ref)

    rr = _iota((LANES, LANES), 0)
    cc = _iota((LANES, LANES), 1)
    mask_bd = jnp.where((rr >> HEAD_SHIFT) == (cc >> HEAD_SHIFT), 1.0, 0.0)
    mask_bd16 = mask_bd.astype(BF16)
    t_i = _iota((c, LANES), 0)
    s_i = _iota((c, LANES), 1) & (c - 1)
    strict = jnp.where(s_i < t_i, 1.0, 0.0)
    incl = jnp.where(s_i <= t_i, 1.0, 0.0)
    eye_ss = jnp.where(s_i == t_i, 1.0, 0.0)

    def bd16(x):
        return jnp.concatenate([x, x], axis=0) * mask_bd16

    def b16(xs):
        return [x.astype(BF16) for x in xs]

    chains = [(bi, p) for bi in range(n_batch) for p in range(n_pairs)]
    n_ch = len(chains)
    idx = range(n_ch)

    def ld(ref):
        return [ref[bi, :, p * LANES:(p + 1) * LANES] for bi, p in chains]

    rt, kkt, kh, bh, v = (ld(ref) for ref in (rt_ref, kkt_ref, kh_ref, bh_ref, v_ref))

    lhs = [jnp.concatenate([kkt[i], rt[i]], axis=0) for i in idx]
    rhs = [jnp.concatenate([bd16(kh[i]), bd16(bh[i])], axis=0) for i in idx]
    gm = [_dot_nt(lhs[i], rhs[i]) for i in idx]
    l_kv = [gm[i][:c, :LANES] * strict for i in idx]
    xp = [-(gm[i][:c, LANES:] * strict) for i in idx]
    pm = [gm[i][c:, :LANES] * incl for i in idx]
    nqm = [-(gm[i][c:, LANES:] * incl) for i in idx]

    tinv = [eye_ss + xp[i] for i in idx]
    xp16 = b16(xp)
    xpb = [bd16(x) for x in xp16]
    xp16 = b16([_dot(xp16[i], xpb[i]) for i in idx])
    for _ in range(c.bit_length() - 3):
        xpb = [bd16(x) for x in xp16]
        both = [_dot(jnp.concatenate([tinv[i].astype(BF16), xp16[i]], axis=0), xpb[i])
                for i in idx]
        tinv = [tinv[i] + both[i][:c] for i in idx]
        xp16 = b16([both[i][c:] for i in idx])
    xpb = [bd16(x) for x in xp16]
    t16 = b16(tinv)
    tinv = [tinv[i] + _dot(t16[i], xpb[i]) for i in idx]
    tinv16 = b16(tinv)

    v_bd = [bd16(x) for x in v]
    l_kv16 = b16(l_kv)
    w16 = b16([_dot(l_kv16[i], v_bd[i]) for i in idx])
    tkw = [_dot(tinv16[i], jnp.concatenate([bd16(kkt[i]), bd16(w16[i])], axis=1)) for i in idx]
    tk16 = b16([x[:, :LANES] for x in tkw])

    st = [st_ref[i] for i in idx]
    st16 = b16(st)
    u = [_dot_nt(tk16[i], st16[i]) + tkw[i][:, LANES:] for i in idx]
    u16 = b16(u)
    pq16 = [jnp.concatenate([pm[i], nqm[i]], axis=1).astype(BF16) for i in idx]
    vu = [jnp.concatenate([v_bd[i], bd16(u16[i])], axis=0) for i in idx]
    y = [_dot_nt(rt[i], st16[i]) + _dot(pq16[i], vu[i]) for i in idx]
    upd = [_dot_tn(jnp.concatenate([v[i], -u16[i]], axis=0),
                   jnp.concatenate([kh[i], bh[i]], axis=0)) for i in idx]
    for i, (bi, p) in enumerate(chains):
        st_ref[i] = (st[i] + upd[i] * mask_bd) * gc_ref[bi, 0, :, p * LANES:(p + 1) * LANES]

    ys = jnp.concatenate(y, axis=0)
    mean = _dot_exact_rhs(ys, mask_bd16) * (1.0 / HEAD64)
    d = ys - mean
    var = _dot_exact_rhs(d * d, mask_bd16) * (1.0 / HEAD64)
    dn = d * lax.rsqrt(var + RWKV_GN_EPS)
    for i, (bi, p) in enumerate(chains):
        sl = slice(p * LANES, (p + 1) * LANES)
        yn = dn[i * c:(i + 1) * c] * lnw_ref[:, sl] + lnb_ref[:, sl]
        o_ref[bi, :, sl] = ((yn + bonus_ref[bi, :, sl]) * g_ref[bi, :, sl]).astype(o_ref.dtype)


def _rwkv_scan(rt, kkt, kh, bh, v, g, bonus, gc, ln_w, ln_b):
    bsz, t, width = rt.shape
    n_pairs = width // LANES
    tok = pl.BlockSpec((bsz, CHUNK, width), lambda j: (0, j, 0))
    const = pl.BlockSpec((1, width), lambda j: (0, 0))
    return pl.pallas_call(
        functools.partial(_rwkv_scan_kernel, n_batch=bsz, n_pairs=n_pairs),
        out_shape=jax.ShapeDtypeStruct((bsz, t, width), BF16),
        grid=(t // CHUNK,),
        in_specs=[tok] * 7 + [pl.BlockSpec((bsz, 1, 1, width), lambda j: (0, j, 0, 0)),
                              const, const],
        out_specs=tok,
        scratch_shapes=[pltpu.VMEM((bsz * n_pairs, LANES, LANES), F32)],
        compiler_params=_cparams(("arbitrary",)),
        name="rwkv_scan",
    )(rt, kkt, kh, bh, v, g, bonus, gc, ln_w.reshape(1, -1), ln_b.reshape(1, -1))


def _proj_lru_kernel(x_ref, gn_ref, w_ref, cw_ref, cb_ref, wg_ref, ba_ref, bx_ref, lam_ref,
                     pa_o, o_ref, xcarry_ref, hcarry_ref, a_s, u_s, *, width, rw_cols):
    tm = x_ref.shape[1]

    @pl.when(pl.program_id(1) == 0)
    def _():
        xcarry_ref[...] = jnp.zeros_like(xcarry_ref)
        hcarry_ref[...] = jnp.zeros_like(hcarry_ref)

    x = x_ref[0]
    ms = jnp.mean(x * x, axis=-1, keepdims=True)
    hn = (x * lax.rsqrt(ms + RMS_EPS) * gn_ref[...]).astype(BF16)
    p_b = _dot(hn, w_ref[:, rw_cols:])
    pa_o[0] = _dot(hn, w_ref[:, :rw_cols])

    gate = p_b[:, 0:width]
    xb = p_b[:, width:2 * width]
    carry8 = xcarry_ref[...]
    row8 = _iota((8, width), 0)

    def shifted(s):
        rolled = pltpu.roll(xb, s, axis=0)
        first = jnp.where(row8 < s, pltpu.roll(carry8, s, axis=0), rolled[0:8])
        return jnp.concatenate([first, rolled[8:]], axis=0)

    xc = (cw_ref[0:1, :] * shifted(3) + cw_ref[1:2, :] * shifted(2)
          + cw_ref[2:3, :] * shifted(1) + cw_ref[3:4, :] * xb + cb_ref[...])
    xcarry_ref[...] = xb[tm - 8:tm, :]

    gates = _dot(xc.astype(BF16), wg_ref[...])
    ig = _sigmoid(gates[:, width:] + bx_ref[...])
    half_rate = (-0.5 * LRU_C) * _softplus(-lam_ref[...])
    log_a = half_rate * jnp.tanh(0.5 * (gates[:, :width] + ba_ref[...])) + half_rate
    a = jnp.exp(log_a)
    a_s[...] = a
    u_s[...] = jnp.sqrt(1.0 - a * a) * ig * xc

    m1 = row8 >= 1
    m2 = row8 >= 2
    m4 = row8 >= 4

    def body(i, h):
        off = pl.multiple_of(i * 8, 8)
        a8 = a_s[pl.ds(off, 8), :]
        u8 = u_s[pl.ds(off, 8), :]
        for s, m in ((1, m1), (2, m2), (4, m4)):
            u_sh = jnp.where(m, pltpu.roll(u8, s, axis=0), 0.0)
            a_sh = jnp.where(m, pltpu.roll(a8, s, axis=0), 1.0)
            u8 = u8 + a8 * u_sh
            a8 = a8 * a_sh
        h8 = u8 + a8 * h
        u_s[pl.ds(off, 8), :] = h8
        return jnp.broadcast_to(h8[7:8, :], (8, width))

    h_last = lax.fori_loop(0, tm // 8, body, hcarry_ref[...])
    hcarry_ref[...] = h_last
    o_ref[0] = (u_s[...] * _gelu_tanh(gate)).astype(o_ref.dtype)


def _block_diag(w):
    nb, di, do = w.shape
    eye = jnp.eye(nb, dtype=w.dtype)
    return (eye[:, None, :, None] * w[:, :, None, :]).reshape(nb * di, nb * do)


def _proj_lru(x3d, g_norm, w_in, rw_cols, conv_w, conv_b, w_a, b_a, w_x, b_x, lam, tm=512):
    b, t, d = x3d.shape
    tm = min(tm, t)
    cols = w_in.shape[1]
    width = (cols - rw_cols) // 2
    wg = jnp.concatenate([_block_diag(w_a), _block_diag(w_x)], axis=1).astype(BF16)
    row = lambda a_: a_.reshape(1, -1)
    const = lambda shp: pl.BlockSpec(shp, lambda i, j: (0, 0))
    return pl.pallas_call(
        functools.partial(_proj_lru_kernel, width=width, rw_cols=rw_cols),
        out_shape=[jax.ShapeDtypeStruct((b, t, rw_cols), F32),
                   jax.ShapeDtypeStruct((b, t, width), BF16)],
        grid=(b, t // tm),
        in_specs=[pl.BlockSpec((1, tm, d), lambda i, j: (i, j, 0)),
                  const((1, d)), const((d, cols)),
                  const(conv_w.shape), const((1, width)), const((width, 2 * width)),
                  const((1, width)), const((1, width)), const((1, width))],
        out_specs=[pl.BlockSpec((1, tm, rw_cols), lambda i, j: (i, j, 0)),
                   pl.BlockSpec((1, tm, width), lambda i, j: (i, j, 0))],
        scratch_shapes=[pltpu.VMEM((8, width), F32), pltpu.VMEM((8, width), F32),
                        pltpu.VMEM((tm, width), F32), pltpu.VMEM((tm, width), F32)],
        compiler_params=_cparams(("parallel", "arbitrary")),
        name="proj_rglru",
    )(x3d, row(g_norm), w_in.astype(BF16), conv_w, row(conv_b), wg, row(b_a), row(b_x), row(lam))


def _gla_kernel(q_ref, f_ref, i_ref, g_ref, lb_ref, nw_ref, o_ref, st_ref, *, n_batch, n_heads):
    c = CHUNK

    @pl.when(pl.program_id(0) == 0)
    def _():
        st_ref[...] = jnp.zeros_like(st_ref)

    trilf = jnp.where(_iota((c, c), 1) <= _iota((c, c), 0), 1.0, 0.0)
    tril16 = trilf.astype(BF16)
    mid = c // 2 - 1
    lb = lb_ref[...]
    fg_c0 = 0.5 + 0.5 * lb
    fg_c1 = 0.5 - 0.5 * lb

    qm, km, v16, gc, em, ecm = [], [], [], [], [], []
    for bi in range(n_batch):
        q = _silu(q_ref[bi].astype(F32))
        fg = fg_c0 + fg_c1 * jnp.tanh(0.5 * f_ref[bi])
        k = 1.0 - fg
        cum = _dot_exact_lhs(tril16, jnp.log(fg))
        cum_c = cum[c - 1:c, :]
        cum_m = cum[mid:mid + 1, :]
        qm.append((q * jnp.exp(cum - cum_m)).astype(BF16))
        km.append((k * jnp.exp(cum_m - cum)).astype(BF16))
        v16.append(i_ref[bi].astype(BF16))
        gc.append(jnp.exp(cum_c))
        em.append(jnp.exp(cum_m))
        ecm.append(jnp.exp(cum_c - cum_m))

    chains = [(bi, h) for bi in range(n_batch) for h in range(n_heads)]
    idx = range(len(chains))

    def hs(xs):
        return [xs[bi][:, h * LANES:(h + 1) * LANES] for bi, h in chains]

    qm_c, km_c, v_c, gc_c, em_c, ecm_c = hs(qm), hs(km), hs(v16), hs(gc), hs(em), hs(ecm)
    scores = [(_dot_nt(qm_c[i], km_c[i]) * trilf).astype(BF16) for i in idx]
    st = [st_ref[i] for i in idx]
    st16 = [(st[i] * em_c[i]).astype(BF16) for i in idx]
    o = [_dot(scores[i], v_c[i]) + _dot_nt(qm_c[i], st16[i]) for i in idx]
    for i in idx:
        st_ref[i] = st[i] * gc_c[i] + _dot_tn(v_c[i], km_c[i]) * ecm_c[i]

    for i, (bi, h) in enumerate(chains):
        sl = slice(h * LANES, (h + 1) * LANES)
        on = o[i] * lax.rsqrt(jnp.mean(o[i] * o[i], axis=-1, keepdims=True) + RMS_EPS)
        gate = _silu(g_ref[bi, :, sl].astype(F32))
        o_ref[bi, :, sl] = (on * nw_ref[:, sl] * gate).astype(o_ref.dtype)


def _gla(q, f, i, g, lower_bound, norm_w):
    bsz, t, width = q.shape
    n_heads = width // LANES
    tok = pl.BlockSpec((bsz, CHUNK, width), lambda j: (0, j, 0))
    const = pl.BlockSpec((1, width), lambda j: (0, 0))
    return pl.pallas_call(
        functools.partial(_gla_kernel, n_batch=bsz, n_heads=n_heads),
        out_shape=jax.ShapeDtypeStruct((bsz, t, width), BF16),
        grid=(t // CHUNK,),
        in_specs=[tok] * 4 + [const, const],
        out_specs=tok,
        scratch_shapes=[pltpu.VMEM((bsz * n_heads, LANES, LANES), F32)],
        compiler_params=_cparams(("arbitrary",)),
        name="hgrn2_gla",
    )(q, f, i, g, lower_bound.reshape(1, -1), norm_w.reshape(1, -1))


def _route_tile(x, g_ref, w_ref, b_ref, upper_ref, e_o, gate_o, rank_o, cnt_o, hn_o):
    ms = jnp.mean(x * x, axis=-1, keepdims=True)
    hn = x * lax.rsqrt(ms + RMS_EPS) * g_ref[...]
    hn_o[...] = hn.astype(BF16)
    lt = _dot_split(hn, w_ref[0], w_ref[1]).T + b_ref[:, 0:1]
    tm = x.shape[0]
    gl = lt[0:8, :]
    row8 = _iota((8, tm), 0)
    gl = jnp.where(row8 < N_GROUPS, gl, -jnp.inf)
    gmax = jnp.max(gl, axis=0, keepdims=True)
    g_sel = jnp.min(jnp.where(gl == gmax, row8, 8), axis=0, keepdims=True)
    g_gate = 1.0 / jnp.sum(jnp.exp(gl - gmax), axis=0, keepdims=True)

    el = jnp.zeros((EXPERTS_PER_GROUP, tm), F32)
    for gi in range(N_GROUPS):
        lo = 8 + gi * EXPERTS_PER_GROUP
        el = jnp.where(g_sel == gi, lt[lo:lo + EXPERTS_PER_GROUP, :], el)
    m1 = jnp.max(el, axis=0, keepdims=True)
    i1 = jnp.min(jnp.where(el == m1, row8, 8), axis=0, keepdims=True)
    el2 = jnp.where(row8 == i1, -jnp.inf, el)
    m2 = jnp.max(el2, axis=0, keepdims=True)
    i2 = jnp.min(jnp.where(el2 == m2, row8, 8), axis=0, keepdims=True)
    e2 = jnp.exp(m2 - m1)
    inv = 1.0 / (1.0 + e2)
    ea = g_sel * EXPERTS_PER_GROUP + i1
    eb = g_sel * EXPERTS_PER_GROUP + i2
    e_o[...] = jnp.concatenate([ea, eb], axis=0)
    gate_o[...] = jnp.concatenate([g_gate * inv, g_gate * e2 * inv], axis=0)

    erow = _iota((N_EXPERTS, tm), 0)
    oh_a = jnp.where(erow == ea, 1.0, 0.0)
    oh_b = jnp.where(erow == eb, 1.0, 0.0)
    pre_a = _dot(oh_a.astype(BF16), upper_ref[...])
    pre_b = _dot(oh_b.astype(BF16), upper_ref[...])
    cnt_a = jnp.sum(oh_a, axis=1, keepdims=True)
    cnt_b = jnp.sum(oh_b, axis=1, keepdims=True)
    rank_a = jnp.sum(oh_a * pre_a, axis=0, keepdims=True)
    rank_b = jnp.sum(oh_b * (pre_b + cnt_a), axis=0, keepdims=True)
    rank_o[...] = jnp.concatenate([rank_a, rank_b], axis=0).astype(I32)
    cnt_o[0] = jnp.broadcast_to(cnt_a + cnt_b, (N_EXPERTS, LANES))


def _seg_local_rows(tm):
    return 2 * tm + N_EXPERTS * SUBLANES


def _segment_copies(cnt_ref, off_ref, dst_ref, base, local, remote, sem, to_remote, wait):
    def body(e, carry):
        c = pl.multiple_of(cnt_ref[base + e], SUBLANES)

        @pl.when(c > 0)
        def _():
            off = 0 if off_ref is None else pl.multiple_of(off_ref[base + e], SUBLANES)
            loc = local.at[pl.ds(off, c), :]
            rem = remote.at[pl.ds(pl.multiple_of(dst_ref[base + e], SUBLANES), c), :]
            cp = (pltpu.make_async_copy(loc, rem, sem) if to_remote
                  else pltpu.make_async_copy(rem, loc, sem))
            if wait:
                cp.wait()
            else:
                cp.start()
        return carry
    lax.fori_loop(0, N_EXPERTS, body, 0, unroll=4)


def _local_index(e_ref, rank_ref, off_ref, tile):
    e = e_ref[...]
    lidx = rank_ref[...]
    for ex in range(N_EXPERTS):
        lidx = lidx + jnp.where(e == ex, off_ref[tile * N_EXPERTS + ex], 0)
    return lidx


def _segment_wait(tot_ref, tile, local, remote, sem, to_remote):
    tot = pl.multiple_of(tot_ref[tile], SUBLANES)

    @pl.when(tot > 0)
    def _():
        loc = local.at[pl.ds(0, tot), :]
        rem = remote.at[pl.ds(0, tot), :]
        cp = (pltpu.make_async_copy(loc, rem, sem) if to_remote
              else pltpu.make_async_copy(rem, loc, sem))
        cp.wait()


def _dispatch_kernel(cnt_ref, off_ref, dst_ref, tot_ref, tcnt_ref, tdst_ref, nused_ref,
                     hn_ref, e_ref, rank_ref, xs_out, sbuf, zbuf, sems, zsem):
    j = pl.program_id(0)
    nb = pl.num_programs(0)
    tm = hn_ref.shape[0]
    slot = lax.rem(j, 2)

    def copies(tile, s, wait):
        if wait:
            _segment_wait(tot_ref, tile, sbuf.at[s], xs_out, sems.at[s], True)
        else:
            _segment_copies(cnt_ref, off_ref, dst_ref, tile * N_EXPERTS, sbuf.at[s], xs_out,
                            sems.at[s], True, False)

    @pl.when(j == 0)
    def _():
        zbuf[...] = jnp.zeros_like(zbuf)
        rows = zbuf.shape[0]
        n_blocks = xs_out.shape[0] // rows
        for wait in (False, True):
            _segment_copies(tcnt_ref, None, tdst_ref, 0, zbuf, xs_out, zsem, True, wait)

            def body(b, carry):
                start = pl.multiple_of(b * rows, SUBLANES)
                cp = pltpu.make_async_copy(zbuf, xs_out.at[pl.ds(start, rows), :], zsem)
                if wait:
                    cp.wait()
                else:
                    cp.start()
                return carry
            lax.fori_loop(nused_ref[0], n_blocks, body, 0)

    @pl.when(j >= 2)
    def _():
        copies(j - 2, slot, True)

    r = _iota((_seg_local_rows(tm), tm), 0)
    lidx = _local_index(e_ref, rank_ref, off_ref, j)
    perm = jnp.where((r == lidx[0:1, :]) | (r == lidx[1:2, :]), 1.0, 0.0).astype(BF16)
    sbuf[slot] = _pack_pairs(_dot(perm, hn_ref[...]))
    copies(j, slot, False)

    @pl.when(j == nb - 1)
    def _():
        copies(j, slot, True)

    @pl.when((j == nb - 1) & (nb >= 2))
    def _():
        copies(j - 1, 1 - slot, True)


def _dispatch(hn16, e2, rank2, seg, tails, n_used, n_rows, rows, tm):
    n, d = hn16.shape
    return pl.pallas_call(
        _dispatch_kernel,
        out_shape=jax.ShapeDtypeStruct((n_rows, d // 2), U32),
        grid_spec=pltpu.PrefetchScalarGridSpec(
            num_scalar_prefetch=7,
            grid=(n // tm,),
            in_specs=[pl.BlockSpec((tm, d), lambda j, *_: (j, 0)),
                      pl.BlockSpec((2, tm), lambda j, *_: (0, j)),
                      pl.BlockSpec((2, tm), lambda j, *_: (0, j))],
            out_specs=pl.BlockSpec(memory_space=pl.ANY),
            scratch_shapes=[pltpu.VMEM((2, _seg_local_rows(tm), d // 2), U32),
                            pltpu.VMEM((rows, d // 2), U32),
                            pltpu.SemaphoreType.DMA((2,)), pltpu.SemaphoreType.DMA(())]),
        compiler_params=_cparams(("arbitrary",)),
        name="moe_dispatch",
    )(*seg, *tails, n_used, hn16, e2, rank2)


def _expert_kernel(be_ref, first_ref, nused_ref, slot_ref, next_ref, x_ref, wg_hbm, wu_hbm,
                   wd_hbm, o_ref, wg32, wu32, wd32, wg16, wu16, wd16, wsem, *, layer):
    j = pl.program_id(0)

    def weight_copies(e, s):
        return [pltpu.make_async_copy(hbm.at[layer, e], buf.at[s], wsem.at[s])
                for hbm, buf in ((wg_hbm, wg32), (wu_hbm, wu32), (wd_hbm, wd32))]

    @pl.when((first_ref[j] == 1) & (j < nused_ref[0]))
    def _():
        s = slot_ref[j]

        @pl.when(j == 0)
        def _():
            for cp in weight_copies(be_ref[0], 0):
                cp.start()

        for cp in weight_copies(be_ref[j], s):
            cp.wait()

        @pl.when(next_ref[j] >= 0)
        def _():
            for cp in weight_copies(next_ref[j], 1 - s):
                cp.start()

        wg16[...] = wg32[s].astype(BF16)
        wu16[...] = wu32[s].astype(BF16)
        wd16[...] = wd32[s].astype(BF16)

    @pl.when(j < nused_ref[0])
    def _():
        x_lo, x_hi = _unpack_pairs(x_ref[...])
        half = x_lo.shape[1]
        hg = _dot(x_lo, wg16[:half, :]) + _dot(x_hi, wg16[half:, :])
        hu = _dot(x_lo, wu16[:half, :]) + _dot(x_hi, wu16[half:, :])
        hid = (_silu(hg) * hu).astype(BF16)
        o_ref[...] = _pack_pairs(_dot(hid, wd16[...]))

    @pl.when(j >= nused_ref[0])
    def _():
        o_ref[...] = jnp.zeros_like(o_ref)


def _expert_ffn(xs, blocks, w_gate, w_up, w_down, layer, rows):
    n_rows, dh = xs.shape
    n_blocks = n_rows // rows
    d, ff = w_gate.shape[-2:]
    assert d == 2 * dh
    any_spec = pl.BlockSpec(memory_space=pl.ANY)
    return pl.pallas_call(
        functools.partial(_expert_kernel, layer=layer),
        out_shape=jax.ShapeDtypeStruct((n_rows, dh), U32),
        grid_spec=pltpu.PrefetchScalarGridSpec(
            num_scalar_prefetch=5,
            grid=(n_blocks,),
            in_specs=[pl.BlockSpec((rows, dh),
                                   lambda j, be, fi, nu, *_: (
                                       jnp.maximum(jnp.minimum(j, nu[0] - 1), 0), 0)),
                      any_spec, any_spec, any_spec],
            out_specs=pl.BlockSpec((rows, dh), lambda j, *_: (j, 0)),
            scratch_shapes=[pltpu.VMEM((2, d, ff), F32), pltpu.VMEM((2, d, ff), F32),
                            pltpu.VMEM((2, ff, d), F32),
                            pltpu.VMEM((d, ff), BF16), pltpu.VMEM((d, ff), BF16),
                            pltpu.VMEM((ff, d), BF16), pltpu.SemaphoreType.DMA((2,))]),
        compiler_params=_cparams(("arbitrary",)),
        name="moe_expert_ffn",
    )(*blocks, xs, w_gate, w_up, w_down)


def _combine_kernel(cnt_ref, off_ref, dst_ref, tot_ref, x_ref, e_ref, rank_ref, gt_ref, ys_hbm,
                    gfin_ref, o_ref, ybuf, sems, *, final_norm):
    j = pl.program_id(0)
    nb = pl.num_programs(0)
    tm = x_ref.shape[0]
    slot = lax.rem(j, 2)

    def copies(tile, s, wait):
        if wait:
            _segment_wait(tot_ref, tile, ybuf.at[s], ys_hbm, sems.at[s], False)
        else:
            _segment_copies(cnt_ref, off_ref, dst_ref, tile * N_EXPERTS, ybuf.at[s], ys_hbm,
                            sems.at[s], False, False)

    @pl.when(j == 0)
    def _():
        ybuf[...] = jnp.zeros_like(ybuf)
        copies(0, 0, False)

    @pl.when(j + 1 < nb)
    def _():
        copies(j + 1, 1 - slot, False)

    copies(j, slot, True)
    lidx = _local_index(e_ref, rank_ref, off_ref, j)
    hi = (lidx >> DIGIT_BITS).astype(F32)
    lo = (lidx & ((1 << DIGIT_BITS) - 1)).astype(F32)
    rows8 = jnp.concatenate([hi, lo, gt_ref[...], jnp.zeros((2, tm), F32)], axis=0).astype(BF16)
    sel = jnp.where(_iota((SUBLANES, 6 * LANES), 0)
                    == (_iota((SUBLANES, 6 * LANES), 1) >> (LANES.bit_length() - 1)),
                    1.0, 0.0).astype(BF16)
    cols = _dot_tn(rows8, sel)
    col = lambda k: cols[:, k * LANES:(k + 1) * LANES]
    la = col(0) * float(1 << DIGIT_BITS) + col(2)
    lb = col(1) * float(1 << DIGIT_BITS) + col(3)
    ga, gb = col(4), col(5)
    lane = _iota((tm, LANES), 1).astype(F32)
    mix = jnp.concatenate(
        [jnp.where(lane + float(q * LANES) == la, ga, 0.0)
         + jnp.where(lane + float(q * LANES) == lb, gb, 0.0)
         for q in range(_seg_local_rows(tm) // LANES)], axis=1).astype(BF16)
    y_lo, y_hi = _unpack_pairs(ybuf[slot])
    y = x_ref[...] + jnp.concatenate([_dot(mix, y_lo), _dot(mix, y_hi)], axis=1)
    if final_norm:
        ms = jnp.mean(y * y, axis=-1, keepdims=True)
        y = y * lax.rsqrt(ms + RMS_EPS) * gfin_ref[...]
    o_ref[...] = y


def _combine(x2d, ys, e2, rank2, gates2, seg, g_final, final_norm, tm):
    n, d = x2d.shape
    return pl.pallas_call(
        functools.partial(_combine_kernel, final_norm=final_norm),
        out_shape=jax.ShapeDtypeStruct((n, d), F32),
        grid_spec=pltpu.PrefetchScalarGridSpec(
            num_scalar_prefetch=4,
            grid=(n // tm,),
            in_specs=[pl.BlockSpec((tm, d), lambda j, *_: (j, 0)),
                      pl.BlockSpec((2, tm), lambda j, *_: (0, j)),
                      pl.BlockSpec((2, tm), lambda j, *_: (0, j)),
                      pl.BlockSpec((2, tm), lambda j, *_: (0, j)),
                      pl.BlockSpec(memory_space=pl.ANY),
                      pl.BlockSpec((1, d), lambda j, *_: (0, 0))],
            out_specs=pl.BlockSpec((tm, d), lambda j, *_: (j, 0)),
            scratch_shapes=[pltpu.VMEM((2, _seg_local_rows(tm), d // 2), U32),
                            pltpu.SemaphoreType.DMA((2,))]),
        compiler_params=_cparams(("arbitrary",)),
        name="moe_combine",
    )(*seg, x2d, e2, rank2, gates2, ys, g_final.reshape(1, d))


def _route_plan(tile_cnt, n_tok, rows, tm):
    n_tiles = n_tok // tm
    seg_cnt = (tile_cnt + SUBLANES - 1) // SUBLANES * SUBLANES
    counts = jnp.sum(seg_cnt, axis=0)
    padded = (counts + rows - 1) // rows * rows
    pend = jnp.cumsum(padded)
    pstart = pend - padded
    seg_dst = pstart[None, :] + jnp.cumsum(seg_cnt, axis=0) - seg_cnt
    seg_off = jnp.cumsum(seg_cnt, axis=1) - seg_cnt
    n_blocks = -(-(2 * n_tok + n_tiles * N_EXPERTS * SUBLANES) // rows) + N_EXPERTS
    starts = jnp.arange(n_blocks, dtype=I32) * rows
    block_e = jnp.minimum(jnp.sum((starts[:, None] >= pend[None, :]).astype(I32), axis=1),
                          N_EXPERTS - 1)
    first = jnp.concatenate([jnp.ones((1,), I32),
                             (block_e[1:] != block_e[:-1]).astype(I32)])
    n_used = (pend[-1] // rows).astype(I32).reshape(1)
    bidx = jnp.arange(n_blocks, dtype=I32)
    w_slot = (jnp.cumsum(first) - 1) % 2
    later_first = (bidx[None, :] > bidx[:, None]) & (first[None, :] == 1) & (bidx[None, :] < n_used)
    next_e = jnp.where(jnp.any(later_first, axis=1),
                       block_e[jnp.argmax(later_first, axis=1)], -1)
    blocks = (block_e.astype(I32), first, n_used, w_slot.astype(I32), next_e.astype(I32))
    seg = tuple(a.reshape(-1).astype(I32)
                for a in (seg_cnt, seg_off, seg_dst, jnp.sum(seg_cnt, axis=1)))
    tails = ((padded - counts).astype(I32), (pstart + counts).astype(I32))
    return seg, tails, blocks, n_blocks * rows


def _moe(x2d, route, w_gate, w_up, w_down, layer, g_final, final_norm, tm, rows=ROUTE_ROWS):
    e2, gates2, rank2, cnt, hn16 = route
    seg, tails, blocks, n_rows = _route_plan(cnt[:, :, 0].astype(I32), e2.shape[1], rows, tm)
    xs = _dispatch(hn16, e2, rank2, seg, tails, blocks[2], n_rows, rows, tm)
    ys = _expert_ffn(xs, blocks, w_gate, w_up, w_down, layer, rows)
    return _combine(x2d, ys, e2, rank2, gates2, seg, g_final, final_norm, tm)


def kernel(x, norm_mix, norm_ffn, norm_final, ab_w_in, rw_mu, rw_w0, rw_w2, rw_a0, rw_a2, rw_g2, rw_k_k, rw_k_a, rw_r_k, rw_ln_w, rw_ln_b, lru_conv_w, lru_conv_b, lru_w_a, lru_b_a, lru_w_x, lru_b_x, lru_lambda, ab_w_out, c_w_in, c_lower_bound, c_norm_w, c_w_out, moe_w_group, moe_b_group, moe_w_expert, moe_b_expert, moe_w_gate, moe_w_up, moe_w_down):
    bsz, t, d = x.shape
    depth = norm_mix.shape[0]
    n = bsz * t
    lbs = jnp.cumsum(jax.nn.softmax(c_lower_bound.astype(F32), axis=0), axis=0)
    lbs = lbs - lbs[0]
    x2d = x.reshape(n, d)
    for layer in range(depth):
        j = layer // 2
        if layer % 2 == 0:
            rw_cols = rw_mu.shape[1]
            width = rw_w0.shape[1]
            p_a, yb = _proj_lru(x2d.reshape(bsz, t, d), norm_mix[layer], ab_w_in[j], rw_cols,
                                lru_conv_w[j], lru_conv_b[j], lru_w_a[j], lru_b_a[j],
                                lru_w_x[j], lru_b_x[j], lru_lambda[j])
            prep = _rwkv_prep(p_a, rw_mu[j], rw_w0[j], rw_w2[j], rw_a0[j], rw_a2[j], rw_g2[j],
                              rw_k_k[j], rw_k_a[j], rw_r_k[j].reshape(-1))
            ya = _rwkv_scan(*prep, rw_ln_w[j], rw_ln_b[j])
            ys = [ya.reshape(n, width), yb.reshape(n, -1)]
            ws = [ab_w_out[j][:width], ab_w_out[j][width:]]
        else:
            hw = c_norm_w.shape[1]
            q, f, i_, g = _norm_matmul(x2d, norm_mix[layer], c_w_in[j], (hw,) * 4,
                                       (BF16, F32, BF16, BF16))
            shp = (bsz, t, hw)
            o = _gla(q.reshape(shp), f.reshape(shp), i_.reshape(shp), g.reshape(shp),
                     lbs[layer], c_norm_w[j])
            ys = [o.reshape(n, hw)]
            ws = [c_w_out[j]]
        tm = min(ROUTE_TILE, n)
        x2d = _proj_residual(x2d, ys, ws)
        route = _router(x2d, norm_ffn[layer], moe_w_group[layer], moe_b_group[layer],
                        moe_w_expert[layer], moe_b_expert[layer], tm)
        x2d = _moe(x2d, route, moe_w_gate, moe_w_up, moe_w_down, layer, norm_final,
                   layer == depth - 1, tm)
    return x2d.reshape(bsz, t, d)
```

```python
import functools

import jax
import jax.numpy as jnp
from jax import lax
from jax.experimental import pallas as pl
from jax.experimental.pallas import tpu as pltpu

F32 = jnp.float32
BF16 = jnp.bfloat16
I32 = jnp.int32
U32 = jnp.uint32

RMS_EPS = 1e-6
RWKV_GN_EPS = 64e-5
LRU_C = 8.0
CHUNK = 64
CHUNK_SHIFT = CHUNK.bit_length() - 1
HEAD64 = 64
HEAD_SHIFT = HEAD64.bit_length() - 1
LANES = 128
SUBLANES = 8
N_GROUPS = 4
EXPERTS_PER_GROUP = 8
N_EXPERTS = N_GROUPS * EXPERTS_PER_GROUP
ROUTE_ROWS = 512
ROUTE_TILE = 512
DIGIT_BITS = 8
VMEM_LIMIT = 56 * 1024 * 1024


def _cparams(sem):
    return pltpu.CompilerParams(dimension_semantics=sem, vmem_limit_bytes=VMEM_LIMIT)


def _sigmoid(x):
    return 0.5 * jnp.tanh(0.5 * x) + 0.5


def _pack_pairs(x):
    w = x.shape[1] // 2
    lo = lax.bitcast_convert_type(x[:, :w].astype(BF16).astype(F32), U32)
    hi = lax.bitcast_convert_type(x[:, w:].astype(BF16).astype(F32), U32)
    return (lo >> 16) | hi


def _unpack_pairs(p):
    lo = lax.bitcast_convert_type(p << 16, F32).astype(BF16)
    hi = lax.bitcast_convert_type(p & jnp.uint32(0xFFFF0000), F32).astype(BF16)
    return lo, hi


def _softplus(x):
    return jnp.maximum(x, 0.0) + jnp.log(1.0 + jnp.exp(-jnp.abs(x)))


def _silu(x):
    h = 0.5 * x
    return h * jnp.tanh(h) + h


def _gelu_tanh(x):
    return 0.5 * x * (1.0 + jnp.tanh(0.7978845608028654 * (x + 0.044715 * x * x * x)))


def _dot(a, b):
    return jnp.dot(a, b, preferred_element_type=F32)


def _dot_nt(a, b):
    return lax.dot_general(a, b, (((1,), (1,)), ((), ())), preferred_element_type=F32)


def _dot_tn(a, b):
    return lax.dot_general(a, b, (((0,), (0,)), ((), ())), preferred_element_type=F32)


def _split2(x):
    hi = x.astype(BF16)
    lo = (x - hi.astype(F32)).astype(BF16)
    return hi, lo


def _dot_exact_rhs(x, m_bf16):
    hi, lo = _split2(x)
    return _dot(hi, m_bf16) + _dot(lo, m_bf16)


def _dot_exact_lhs(m_bf16, x):
    hi, lo = _split2(x)
    return _dot(m_bf16, hi) + _dot(m_bf16, lo)


def _dot_split(a, w_hi, w_lo):
    a_hi, a_lo = _split2(a)
    return _dot(a_hi, w_hi) + (_dot(a_lo, w_hi) + _dot(a_hi, w_lo))


def _hi_lo(w):
    hi = w.astype(BF16)
    return hi, (w - hi.astype(F32)).astype(BF16)


def _iota(shape, dim):
    return lax.broadcasted_iota(I32, shape, dim)


def _norm_matmul_kernel(x_ref, g_ref, w_ref, *o_refs, splits):
    x = x_ref[...]
    ms = jnp.mean(x * x, axis=-1, keepdims=True)
    y = (x * lax.rsqrt(ms + RMS_EPS) * g_ref[...]).astype(BF16)
    off = 0
    for o_ref, n in zip(o_refs, splits):
        o_ref[...] = _dot(y, w_ref[:, off:off + n]).astype(o_ref.dtype)
        off += n


def _norm_matmul(x2d, g, w, splits, out_dtypes, tm=1024):
    n, d = x2d.shape
    tm = min(tm, n)
    ncols = w.shape[1]
    assert sum(splits) == ncols and n % tm == 0 and len(out_dtypes) == len(splits)
    return pl.pallas_call(
        functools.partial(_norm_matmul_kernel, splits=splits),
        out_shape=[jax.ShapeDtypeStruct((n, s), dt) for s, dt in zip(splits, out_dtypes)],
        grid=(n // tm,),
        in_specs=[pl.BlockSpec((tm, d), lambda i: (i, 0)),
                  pl.BlockSpec((1, d), lambda i: (0, 0)),
                  pl.BlockSpec((d, ncols), lambda i: (0, 0))],
        out_specs=[pl.BlockSpec((tm, s), lambda i: (i, 0)) for s in splits],
        compiler_params=_cparams(("parallel",)),
        name="norm_matmul",
    )(x2d, g.reshape(1, d), w.astype(BF16))


def _proj_residual_kernel(*refs, n_in):
    x_ref = refs[0]
    y_refs = refs[1:1 + n_in]
    w_refs = refs[1 + n_in:1 + 2 * n_in]
    o_ref = refs[1 + 2 * n_in]
    acc = x_ref[...]
    for y_ref, w_ref in zip(y_refs, w_refs):
        acc = acc + _dot(y_ref[...].astype(BF16), w_ref[...])
    o_ref[...] = acc


def _proj_residual(x2d, ys, ws, tm=1024):
    n, d = x2d.shape
    tm = min(tm, n)
    n_in = len(ys)
    in_specs = [pl.BlockSpec((tm, d), lambda i: (i, 0))]
    in_specs += [pl.BlockSpec((tm, y.shape[1]), lambda i: (i, 0)) for y in ys]
    in_specs += [pl.BlockSpec(w.shape, lambda i: (0, 0)) for w in ws]
    return pl.pallas_call(
        functools.partial(_proj_residual_kernel, n_in=n_in),
        out_shape=jax.ShapeDtypeStruct((n, d), F32),
        grid=(n // tm,),
        in_specs=in_specs,
        out_specs=pl.BlockSpec((tm, d), lambda i: (i, 0)),
        compiler_params=_cparams(("parallel",)),
        name="proj_residual",
    )(x2d, *ys, *[w.astype(BF16) for w in ws])


def _router_kernel(x_ref, g_ref, w_ref, b_ref, upper_ref, e_o, gate_o, rank_o, cnt_o, hn_o):
    _route_tile(x_ref[...], g_ref, w_ref, b_ref, upper_ref, e_o, gate_o, rank_o, cnt_o, hn_o)


def _router(x2d, g_ffn, w_group, b_group, w_expert, b_expert, tm):
    n, d = x2d.shape
    wt = jnp.zeros((d, LANES), F32)
    wt = wt.at[:, 0:N_GROUPS].set(w_group).at[:, 8:8 + N_EXPERTS].set(w_expert)
    wt = jnp.stack(_hi_lo(wt))
    bt = jnp.zeros((LANES,), F32)
    bt = bt.at[0:N_GROUPS].set(b_group).at[8:8 + N_EXPERTS].set(b_expert)
    bt = jnp.broadcast_to(bt[:, None], (LANES, LANES))
    ti = jnp.arange(tm, dtype=I32)
    upper = (ti[:, None] < ti[None, :]).astype(BF16)
    tok2 = pl.BlockSpec((2, tm), lambda i: (0, i))
    return pl.pallas_call(
        _router_kernel,
        out_shape=[jax.ShapeDtypeStruct((2, n), I32),
                   jax.ShapeDtypeStruct((2, n), F32),
                   jax.ShapeDtypeStruct((2, n), I32),
                   jax.ShapeDtypeStruct((n // tm, N_EXPERTS, LANES), F32),
                   jax.ShapeDtypeStruct((n, d), BF16)],
        grid=(n // tm,),
        in_specs=[pl.BlockSpec((tm, d), lambda i: (i, 0)),
                  pl.BlockSpec((1, d), lambda i: (0, 0)),
                  pl.BlockSpec((2, d, LANES), lambda i: (0, 0, 0)),
                  pl.BlockSpec((LANES, LANES), lambda i: (0, 0)),
                  pl.BlockSpec((tm, tm), lambda i: (0, 0))],
        out_specs=[tok2, tok2, tok2,
                   pl.BlockSpec((1, N_EXPERTS, LANES), lambda i: (i, 0, 0)),
                   pl.BlockSpec((tm, d), lambda i: (i, 0))],
        compiler_params=_cparams(("parallel",)),
        name="moe_router",
    )(x2d, g_ffn.reshape(1, d), wt, bt, upper)


def _rwkv_prep_kernel(p_ref, mu_ref, w0_ref, a0_ref, kk_s_ref, ka_ref, rk_ref,
                      wcomb_ref, g2_ref, tril_ref,
                      rt_o, kkt_o, kh_o, bh_o, v_o, g_o, bonus_o, gc_o,
                      prev_ref, *, width):
    tm = p_ref.shape[1]

    @pl.when(pl.program_id(1) == 0)
    def _():
        prev_ref[...] = jnp.zeros_like(prev_ref)

    p = p_ref[0]
    rolled = pltpu.roll(p, 1, axis=0)
    head = jnp.where(_iota((SUBLANES, p.shape[1]), 0) == 0, prev_ref[...], rolled[0:SUBLANES])
    prev = jnp.concatenate([head, rolled[SUBLANES:]], axis=0)
    prev_ref[...] = p[tm - 1:tm, :]
    ps = p + (prev - p) * mu_ref[...]

    r = ps[:, 0:width]
    k = ps[:, width:2 * width]
    v = ps[:, 2 * width:3 * width]
    lowrank = ps[:, 3 * width:3 * width + LANES]
    gl = ps[:, 3 * width + LANES:3 * width + 2 * LANES]

    lane = _iota(lowrank.shape, 1)
    lr_in = jnp.where(lane < HEAD64, jnp.tanh(lowrank), lowrank)
    t12 = _dot_split(lr_in, wcomb_ref[0], wcomb_ref[1])
    wlog = -_softplus(-(w0_ref[...] + t12[:, :width])) - 0.5
    lw = -jnp.exp(wlog)
    a = _sigmoid(a0_ref[...] + t12[:, width:])
    g = _dot_split(_sigmoid(gl), g2_ref[0], g2_ref[1])

    seg = jnp.where((_iota((LANES, LANES), 0) >> HEAD_SHIFT)
                    == (_iota((LANES, LANES), 1) >> HEAD_SHIFT),
                    1.0, 0.0).astype(BF16)

    def head_sums(x):
        return jnp.concatenate([_dot_exact_rhs(x[:, q * LANES:(q + 1) * LANES], seg)
                                for q in range(width // LANES)], axis=1)

    kk = k * kk_s_ref[...]
    nrm = jnp.sqrt(head_sums(kk * kk))
    kk = kk / jnp.maximum(nrm, 1e-12)
    k2 = k * (1.0 + (a - 1.0) * ka_ref[...])
    bonus = head_sums(r * k2 * rk_ref[...]) * v

    b = kk * a

    n_chunk = tm // CHUNK
    cum = _dot_exact_lhs(tril_ref[...], lw)
    cum3 = cum.reshape(n_chunk, CHUNK, width)
    cend = cum3[:, CHUNK - 1:CHUNK, :]
    e_neg = jnp.exp(-cum)

    rt_o[0] = (r * jnp.exp(cum)).astype(BF16)
    kkt_o[0] = (kk * jnp.exp(cum - lw)).astype(BF16)
    kh_o[0] = (k2 * e_neg).astype(BF16)
    bh_o[0] = (b * e_neg).astype(BF16)
    v_o[0] = v.astype(BF16)
    g_o[0] = g.astype(BF16)
    bonus_o[0] = bonus.astype(BF16)
    gc_o[0] = jnp.exp(cend)


def _rwkv_prep(p_a, mu, w0, w2, a0, a2, g2, k_k, k_a, r_k, tm=512):
    b, t, cols = p_a.shape
    width = w0.shape[0]
    rank = w2.shape[0]
    assert rank == HEAD64 and a2.shape[0] == HEAD64 and g2.shape[0] == LANES
    zeros = jnp.zeros((rank, width), F32)
    wcomb = jnp.concatenate([jnp.concatenate([w2, zeros], 1),
                             jnp.concatenate([zeros, a2], 1)], 0)
    row = lambda a_: a_.reshape(1, -1)
    const = lambda shp: pl.BlockSpec(shp, lambda i, j: (0,) * len(shp))
    tok = pl.BlockSpec((1, tm, width), lambda i, j: (i, j, 0))
    n_chunk = tm // CHUNK
    ti = jnp.arange(tm, dtype=I32)
    tril_bd = (((ti[:, None] >> CHUNK_SHIFT) == (ti[None, :] >> CHUNK_SHIFT))
               & (ti[None, :] <= ti[:, None])).astype(BF16)
    return pl.pallas_call(
        functools.partial(_rwkv_prep_kernel, width=width),
        out_shape=([jax.ShapeDtypeStruct((b, t, width), BF16)] * 7
                   + [jax.ShapeDtypeStruct((b, t // CHUNK, 1, width), F32)]),
        grid=(b, t // tm),
        in_specs=[pl.BlockSpec((1, tm, cols), lambda i, j: (i, j, 0)),
                  const((1, cols)), const((1, width)), const((1, width)), const((1, width)),
                  const((1, width)), const((1, width)),
                  const((2, LANES, 2 * width)), const((2, LANES, width)), const((tm, tm))],
        out_specs=[tok] * 7 + [pl.BlockSpec((1, n_chunk, 1, width), lambda i, j: (i, j, 0, 0))],
        scratch_shapes=[pltpu.VMEM((1, cols), F32)],
        compiler_params=_cparams(("parallel", "arbitrary")),
        name="rwkv_prep",
    )(p_a, row(mu), row(w0), row(a0), row(k_k), row(k_a), row(r_k),
      jnp.stack(_hi_lo(wcomb)), jnp.stack(_hi_lo(g2)), tril_bd)


def _rwkv_scan_kernel(rt_ref, kkt_ref, kh_ref, bh_ref, v_ref, g_ref, bonus_ref,
                      gc_ref, lnw_ref, lnb_ref, o_ref, st_ref, *, n_batch, n_pairs):
    c = CHUNK

    @pl.when(pl.program_id(0) == 0)
    def _():
        st_ref[...] = jnp.zeros_like(st_ref)

    rr = _iota((LANES, LANES), 0)
    cc = _iota((LANES, LANES), 1)
    mask_bd = jnp.where((rr >> HEAD_SHIFT) == (cc >> HEAD_SHIFT), 1.0, 0.0)
    mask_bd16 = mask_bd.astype(BF16)
    t_i = _iota((c, LANES), 0)
    s_i = _iota((c, LANES), 1) & (c - 1)
    strict = jnp.where(s_i < t_i, 1.0, 0.0)
    incl = jnp.where(s_i <= t_i, 1.0, 0.0)
    eye_ss = jnp.where(s_i == t_i, 1.0, 0.0)

    def bd16(x):
        return jnp.concatenate([x, x], axis=0) * mask_bd16

    def b16(xs):
        return [x.astype(BF16) for x in xs]

    chains = [(bi, p) for bi in range(n_batch) for p in range(n_pairs)]
    n_ch = len(chains)
    idx = range(n_ch)

    def ld(ref):
        return [ref[bi, :, p * LANES:(p + 1) * LANES] for bi, p in chains]

    rt, kkt, kh, bh, v = (ld(ref) for ref in (rt_ref, kkt_ref, kh_ref, bh_ref, v_ref))

    lhs = [jnp.concatenate([kkt[i], rt[i]], axis=0) for i in idx]
    rhs = [jnp.concatenate([bd16(kh[i]), bd16(bh[i])], axis=0) for i in idx]
    gm = [_dot_nt(lhs[i], rhs[i]) for i in idx]
    l_kv = [gm[i][:c, :LANES] * strict for i in idx]
    xp = [-(gm[i][:c, LANES:] * strict) for i in idx]
    pm = [gm[i][c:, :LANES] * incl for i in idx]
    nqm = [-(gm[i][c:, LANES:] * incl) for i in idx]

    tinv = [eye_ss + xp[i] for i in idx]
    xp16 = b16(xp)
    xpb = [bd16(x) for x in xp16]
    xp16 = b16([_dot(xp16[i], xpb[i]) for i in idx])
    for _ in range(c.bit_length() - 3):
        xpb = [bd16(x) for x in xp16]
        both = [_dot(jnp.concatenate([tinv[i].astype(BF16), xp16[i]], axis=0), xpb[i])
                for i in idx]
        tinv = [tinv[i] + both[i][:c] for i in idx]
        xp16 = b16([both[i][c:] for i in idx])
    xpb = [bd16(x) for x in xp16]
    t16 = b16(tinv)
    tinv = [tinv[i] + _dot(t16[i], xpb[i]) for i in idx]
    tinv16 = b16(tinv)

    v_bd = [bd16(x) for x in v]
    l_kv16 = b16(l_kv)
    w16 = b16([_dot(l_kv16[i], v_bd[i]) for i in idx])
    tkw = [_dot(tinv16[i], jnp.concatenate([bd16(kkt[i]), bd16(w16[i])], axis=1)) for i in idx]
    tk16 = b16([x[:, :LANES] for x in tkw])

    st = [st_ref[i] for i in idx]
    st16 = b16(st)
    on_st = [_dot_nt(jnp.concatenate([tk16[i], rt[i]], axis=0), st16[i]) for i in idx]
    u = [on_st[i][:c] + tkw[i][:, LANES:] for i in idx]
    u16 = b16(u)
    pq16 = [jnp.concatenate([pm[i], nqm[i]], axis=1).astype(BF16) for i in idx]
    vu = [jnp.concatenate([v_bd[i], bd16(u16[i])], axis=0) for i in idx]
    y = [on_st[i][c:] + _dot(pq16[i], vu[i]) for i in idx]
    upd = [_dot_tn(jnp.concatenate([v[i], -u16[i]], axis=0),
                   jnp.concatenate([kh[i], bh[i]], axis=0)) for i in idx]
    for i, (bi, p) in enumerate(chains):
        st_ref[i] = (st[i] + upd[i] * mask_bd) * gc_ref[bi, 0, :, p * LANES:(p + 1) * LANES]

    ys = jnp.concatenate(y, axis=0)
    mean = _dot_exact_rhs(ys, mask_bd16) * (1.0 / HEAD64)
    d = ys - mean
    var = _dot_exact_rhs(d * d, mask_bd16) * (1.0 / HEAD64)
    dn = d * lax.rsqrt(var + RWKV_GN_EPS)
    for i, (bi, p) in enumerate(chains):
        sl = slice(p * LANES, (p + 1) * LANES)
        yn = dn[i * c:(i + 1) * c] * lnw_ref[:, sl] + lnb_ref[:, sl]
        o_ref[bi, :, sl] = ((yn + bonus_ref[bi, :, sl]) * g_ref[bi, :, sl]).astype(o_ref.dtype)


def _rwkv_scan(rt, kkt, kh, bh, v, g, bonus, gc, ln_w, ln_b):
    bsz, t, width = rt.shape
    n_pairs = width // LANES
    tok = pl.BlockSpec((bsz, CHUNK, width), lambda j: (0, j, 0))
    const = pl.BlockSpec((1, width), lambda j: (0, 0))
    return pl.pallas_call(
        functools.partial(_rwkv_scan_kernel, n_batch=bsz, n_pairs=n_pairs),
        out_shape=jax.ShapeDtypeStruct((bsz, t, width), BF16),
        grid=(t // CHUNK,),
        in_specs=[tok] * 7 + [pl.BlockSpec((bsz, 1, 1, width), lambda j: (0, j, 0, 0)),
                              const, const],
        out_specs=tok,
        scratch_shapes=[pltpu.VMEM((bsz * n_pairs, LANES, LANES), F32)],
        compiler_params=_cparams(("arbitrary",)),
        name="rwkv_scan",
    )(rt, kkt, kh, bh, v, g, bonus, gc, ln_w.reshape(1, -1), ln_b.reshape(1, -1))


def _proj_lru_kernel(x_ref, gn_ref, w_ref, cw_ref, cb_ref, wg_ref, ba_ref, bx_ref, lam_ref,
                     pa_o, o_ref, xcarry_ref, hcarry_ref, a_s, u_s, *, width, rw_cols):
    tm = x_ref.shape[1]

    @pl.when(pl.program_id(1) == 0)
    def _():
        xcarry_ref[...] = jnp.zeros_like(xcarry_ref)
        hcarry_ref[...] = jnp.zeros_like(hcarry_ref)

    x = x_ref[0]
    ms = jnp.mean(x * x, axis=-1, keepdims=True)
    hn = (x * lax.rsqrt(ms + RMS_EPS) * gn_ref[...]).astype(BF16)
    p_b = _dot(hn, w_ref[:, rw_cols:])
    pa_o[0] = _dot(hn, w_ref[:, :rw_cols])

    gate = p_b[:, 0:width]
    xb = p_b[:, width:2 * width]
    carry8 = xcarry_ref[...]
    row8 = _iota((8, width), 0)

    def shifted(s):
        rolled = pltpu.roll(xb, s, axis=0)
        first = jnp.where(row8 < s, pltpu.roll(carry8, s, axis=0), rolled[0:8])
        return jnp.concatenate([first, rolled[8:]], axis=0)

    xc = (cw_ref[0:1, :] * shifted(3) + cw_ref[1:2, :] * shifted(2)
          + cw_ref[2:3, :] * shifted(1) + cw_ref[3:4, :] * xb + cb_ref[...])
    xcarry_ref[...] = xb[tm - 8:tm, :]

    gates = _dot(xc.astype(BF16), wg_ref[...])
    ig = _sigmoid(gates[:, width:] + bx_ref[...])
    half_rate = (-0.5 * LRU_C) * _softplus(-lam_ref[...])
    log_a = half_rate * jnp.tanh(0.5 * (gates[:, :width] + ba_ref[...])) + half_rate
    a = jnp.exp(log_a)
    a_s[...] = a
    u_s[...] = jnp.sqrt(1.0 - a * a) * ig * xc

    m1 = row8 >= 1
    m2 = row8 >= 2
    m4 = row8 >= 4

    def body(i, h):
        off = pl.multiple_of(i * 8, 8)
        a8 = a_s[pl.ds(off, 8), :]
        u8 = u_s[pl.ds(off, 8), :]
        for s, m in ((1, m1), (2, m2), (4, m4)):
            u_sh = jnp.where(m, pltpu.roll(u8, s, axis=0), 0.0)
            a_sh = jnp.where(m, pltpu.roll(a8, s, axis=0), 1.0)
            u8 = u8 + a8 * u_sh
            a8 = a8 * a_sh
        h8 = u8 + a8 * h
        u_s[pl.ds(off, 8), :] = h8
        return jnp.broadcast_to(h8[7:8, :], (8, width))

    h_last = lax.fori_loop(0, tm // 8, body, hcarry_ref[...])
    hcarry_ref[...] = h_last
    o_ref[0] = (u_s[...] * _gelu_tanh(gate)).astype(o_ref.dtype)


def _block_diag(w):
    nb, di, do = w.shape
    eye = jnp.eye(nb, dtype=w.dtype)
    return (eye[:, None, :, None] * w[:, :, None, :]).reshape(nb * di, nb * do)


def _proj_lru(x3d, g_norm, w_in, rw_cols, conv_w, conv_b, w_a, b_a, w_x, b_x, lam, tm=512):
    b, t, d = x3d.shape
    tm = min(tm, t)
    cols = w_in.shape[1]
    width = (cols - rw_cols) // 2
    wg = jnp.concatenate([_block_diag(w_a), _block_diag(w_x)], axis=1).astype(BF16)
    row = lambda a_: a_.reshape(1, -1)
    const = lambda shp: pl.BlockSpec(shp, lambda i, j: (0, 0))
    return pl.pallas_call(
        functools.partial(_proj_lru_kernel, width=width, rw_cols=rw_cols),
        out_shape=[jax.ShapeDtypeStruct((b, t, rw_cols), F32),
                   jax.ShapeDtypeStruct((b, t, width), BF16)],
        grid=(b, t // tm),
        in_specs=[pl.BlockSpec((1, tm, d), lambda i, j: (i, j, 0)),
                  const((1, d)), const((d, cols)),
                  const(conv_w.shape), const((1, width)), const((width, 2 * width)),
                  const((1, width)), const((1, width)), const((1, width))],
        out_specs=[pl.BlockSpec((1, tm, rw_cols), lambda i, j: (i, j, 0)),
                   pl.BlockSpec((1, tm, width), lambda i, j: (i, j, 0))],
        scratch_shapes=[pltpu.VMEM((8, width), F32), pltpu.VMEM((8, width), F32),
                        pltpu.VMEM((tm, width), F32), pltpu.VMEM((tm, width), F32)],
        compiler_params=_cparams(("parallel", "arbitrary")),
        name="proj_rglru",
    )(x3d, row(g_norm), w_in.astype(BF16), conv_w, row(conv_b), wg, row(b_a), row(b_x), row(lam))


def _gla_kernel(q_ref, f_ref, i_ref, g_ref, lb_ref, nw_ref, o_ref, st_ref, *, n_batch, n_heads):
    c = CHUNK

    @pl.when(pl.program_id(0) == 0)
    def _():
        st_ref[...] = jnp.zeros_like(st_ref)

    trilf = jnp.where(_iota((c, c), 1) <= _iota((c, c), 0), 1.0, 0.0)
    tril16 = trilf.astype(BF16)
    mid = c // 2 - 1
    lb = lb_ref[...]
    fg_c0 = 0.5 + 0.5 * lb
    fg_c1 = 0.5 - 0.5 * lb

    qm, km, v16, gc, em, ecm = [], [], [], [], [], []
    for bi in range(n_batch):
        q = _silu(q_ref[bi].astype(F32))
        fg = fg_c0 + fg_c1 * jnp.tanh(0.5 * f_ref[bi])
        k = 1.0 - fg
        cum = _dot_exact_lhs(tril16, jnp.log(fg))
        cum_c = cum[c - 1:c, :]
        cum_m = cum[mid:mid + 1, :]
        qm.append((q * jnp.exp(cum - cum_m)).astype(BF16))
        km.append((k * jnp.exp(cum_m - cum)).astype(BF16))
        v16.append(i_ref[bi].astype(BF16))
        gc.append(jnp.exp(cum_c))
        em.append(jnp.exp(cum_m))
        ecm.append(jnp.exp(cum_c - cum_m))

    chains = [(bi, h) for bi in range(n_batch) for h in range(n_heads)]
    idx = range(len(chains))

    def hs(xs):
        return [xs[bi][:, h * LANES:(h + 1) * LANES] for bi, h in chains]

    qm_c, km_c, v_c, gc_c, em_c, ecm_c = hs(qm), hs(km), hs(v16), hs(gc), hs(em), hs(ecm)
    scores = [(_dot_nt(qm_c[i], km_c[i]) * trilf).astype(BF16) for i in idx]
    st = [st_ref[i] for i in idx]
    st16 = [(st[i] * em_c[i]).astype(BF16) for i in idx]
    o = [_dot(scores[i], v_c[i]) + _dot_nt(qm_c[i], st16[i]) for i in idx]
    for i in idx:
        st_ref[i] = st[i] * gc_c[i] + _dot_tn(v_c[i], km_c[i]) * ecm_c[i]

    for i, (bi, h) in enumerate(chains):
        sl = slice(h * LANES, (h + 1) * LANES)
        on = o[i] * lax.rsqrt(jnp.mean(o[i] * o[i], axis=-1, keepdims=True) + RMS_EPS)
        gate = _silu(g_ref[bi, :, sl].astype(F32))
        o_ref[bi, :, sl] = (on * nw_ref[:, sl] * gate).astype(o_ref.dtype)


def _gla(q, f, i, g, lower_bound, norm_w):
    bsz, t, width = q.shape
    n_heads = width // LANES
    tok = pl.BlockSpec((bsz, CHUNK, width), lambda j: (0, j, 0))
    const = pl.BlockSpec((1, width), lambda j: (0, 0))
    return pl.pallas_call(
        functools.partial(_gla_kernel, n_batch=bsz, n_heads=n_heads),
        out_shape=jax.ShapeDtypeStruct((bsz, t, width), BF16),
        grid=(t // CHUNK,),
        in_specs=[tok] * 4 + [const, const],
        out_specs=tok,
        scratch_shapes=[pltpu.VMEM((bsz * n_heads, LANES, LANES), F32)],
        compiler_params=_cparams(("arbitrary",)),
        name="hgrn2_gla",
    )(q, f, i, g, lower_bound.reshape(1, -1), norm_w.reshape(1, -1))


def _route_tile(x, g_ref, w_ref, b_ref, upper_ref, e_o, gate_o, rank_o, cnt_o, hn_o):
    ms = jnp.mean(x * x, axis=-1, keepdims=True)
    hn = x * lax.rsqrt(ms + RMS_EPS) * g_ref[...]
    hn_o[...] = hn.astype(BF16)
    lt = _dot_split(hn, w_ref[0], w_ref[1]).T + b_ref[:, 0:1]
    tm = x.shape[0]
    gl = lt[0:8, :]
    row8 = _iota((8, tm), 0)
    gl = jnp.where(row8 < N_GROUPS, gl, -jnp.inf)
    gmax = jnp.max(gl, axis=0, keepdims=True)
    g_sel = jnp.min(jnp.where(gl == gmax, row8, 8), axis=0, keepdims=True)
    g_gate = 1.0 / jnp.sum(jnp.exp(gl - gmax), axis=0, keepdims=True)

    el = jnp.zeros((EXPERTS_PER_GROUP, tm), F32)
    for gi in range(N_GROUPS):
        lo = 8 + gi * EXPERTS_PER_GROUP
        el = jnp.where(g_sel == gi, lt[lo:lo + EXPERTS_PER_GROUP, :], el)
    m1 = jnp.max(el, axis=0, keepdims=True)
    i1 = jnp.min(jnp.where(el == m1, row8, 8), axis=0, keepdims=True)
    el2 = jnp.where(row8 == i1, -jnp.inf, el)
    m2 = jnp.max(el2, axis=0, keepdims=True)
    i2 = jnp.min(jnp.where(el2 == m2, row8, 8), axis=0, keepdims=True)
    e2 = jnp.exp(m2 - m1)
    inv = 1.0 / (1.0 + e2)
    ea = g_sel * EXPERTS_PER_GROUP + i1
    eb = g_sel * EXPERTS_PER_GROUP + i2
    e_o[...] = jnp.concatenate([ea, eb], axis=0)
    gate_o[...] = jnp.concatenate([g_gate * inv, g_gate * e2 * inv], axis=0)

    erow = _iota((N_EXPERTS, tm), 0)
    oh_a = jnp.where(erow == ea, 1.0, 0.0)
    oh_b = jnp.where(erow == eb, 1.0, 0.0)
    pre_a = _dot(oh_a.astype(BF16), upper_ref[...])
    pre_b = _dot(oh_b.astype(BF16), upper_ref[...])
    cnt_a = jnp.sum(oh_a, axis=1, keepdims=True)
    cnt_b = jnp.sum(oh_b, axis=1, keepdims=True)
    rank_a = jnp.sum(oh_a * pre_a, axis=0, keepdims=True)
    rank_b = jnp.sum(oh_b * (pre_b + cnt_a), axis=0, keepdims=True)
    rank_o[...] = jnp.concatenate([rank_a, rank_b], axis=0).astype(I32)
    cnt_o[0] = jnp.broadcast_to(cnt_a + cnt_b, (N_EXPERTS, LANES))


def _seg_local_rows(tm):
    return 2 * tm + N_EXPERTS * SUBLANES


def _segment_copies(cnt_ref, off_ref, dst_ref, base, local, remote, sem, to_remote, wait):
    def body(e, carry):
        c = pl.multiple_of(cnt_ref[base + e], SUBLANES)

        @pl.when(c > 0)
        def _():
            off = 0 if off_ref is None else pl.multiple_of(off_ref[base + e], SUBLANES)
            loc = local.at[pl.ds(off, c), :]
            rem = remote.at[pl.ds(pl.multiple_of(dst_ref[base + e], SUBLANES), c), :]
            cp = (pltpu.make_async_copy(loc, rem, sem) if to_remote
                  else pltpu.make_async_copy(rem, loc, sem))
            if wait:
                cp.wait()
            else:
                cp.start()
        return carry
    lax.fori_loop(0, N_EXPERTS, body, 0, unroll=4)


def _local_index(e_ref, rank_ref, off_ref, tile):
    e = e_ref[...]
    lidx = rank_ref[...]
    for ex in range(N_EXPERTS):
        lidx = lidx + jnp.where(e == ex, off_ref[tile * N_EXPERTS + ex], 0)
    return lidx


def _segment_wait(tot_ref, tile, local, remote, sem, to_remote):
    tot = pl.multiple_of(tot_ref[tile], SUBLANES)

    @pl.when(tot > 0)
    def _():
        loc = local.at[pl.ds(0, tot), :]
        rem = remote.at[pl.ds(0, tot), :]
        cp = (pltpu.make_async_copy(loc, rem, sem) if to_remote
              else pltpu.make_async_copy(rem, loc, sem))
        cp.wait()


def _dispatch_kernel(cnt_ref, off_ref, dst_ref, tot_ref, tcnt_ref, tdst_ref, nused_ref,
                     hn_ref, e_ref, rank_ref, xs_out, sbuf, zbuf, sems, zsem):
    j = pl.program_id(0)
    nb = pl.num_programs(0)
    tm = hn_ref.shape[0]
    slot = lax.rem(j, 2)

    def copies(tile, s, wait):
        if wait:
            _segment_wait(tot_ref, tile, sbuf.at[s], xs_out, sems.at[s], True)
        else:
            _segment_copies(cnt_ref, off_ref, dst_ref, tile * N_EXPERTS, sbuf.at[s], xs_out,
                            sems.at[s], True, False)

    @pl.when(j == 0)
    def _():
        zbuf[...] = jnp.zeros_like(zbuf)
        rows = zbuf.shape[0]
        n_blocks = xs_out.shape[0] // rows
        for wait in (False, True):
            _segment_copies(tcnt_ref, None, tdst_ref, 0, zbuf, xs_out, zsem, True, wait)

            def body(b, carry):
                start = pl.multiple_of(b * rows, SUBLANES)
                cp = pltpu.make_async_copy(zbuf, xs_out.at[pl.ds(start, rows), :], zsem)
                if wait:
                    cp.wait()
                else:
                    cp.start()
                return carry
            lax.fori_loop(nused_ref[0], n_blocks, body, 0)

    @pl.when(j >= 2)
    def _():
        copies(j - 2, slot, True)

    r = _iota((_seg_local_rows(tm), tm), 0)
    lidx = _local_index(e_ref, rank_ref, off_ref, j)
    perm = jnp.where((r == lidx[0:1, :]) | (r == lidx[1:2, :]), 1.0, 0.0).astype(BF16)
    sbuf[slot] = _pack_pairs(_dot(perm, hn_ref[...]))
    copies(j, slot, False)

    @pl.when(j == nb - 1)
    def _():
        copies(j, slot, True)

    @pl.when((j == nb - 1) & (nb >= 2))
    def _():
        copies(j - 1, 1 - slot, True)


def _dispatch(hn16, e2, rank2, seg, tails, n_used, n_rows, rows, tm):
    n, d = hn16.shape
    return pl.pallas_call(
        _dispatch_kernel,
        out_shape=jax.ShapeDtypeStruct((n_rows, d // 2), U32),
        grid_spec=pltpu.PrefetchScalarGridSpec(
            num_scalar_prefetch=7,
            grid=(n // tm,),
            in_specs=[pl.BlockSpec((tm, d), lambda j, *_: (j, 0)),
                      pl.BlockSpec((2, tm), lambda j, *_: (0, j)),
                      pl.BlockSpec((2, tm), lambda j, *_: (0, j))],
            out_specs=pl.BlockSpec(memory_space=pl.ANY),
            scratch_shapes=[pltpu.VMEM((2, _seg_local_rows(tm), d // 2), U32),
                            pltpu.VMEM((rows, d // 2), U32),
                            pltpu.SemaphoreType.DMA((2,)), pltpu.SemaphoreType.DMA(())]),
        compiler_params=_cparams(("arbitrary",)),
        name="moe_dispatch",
    )(*seg, *tails, n_used, hn16, e2, rank2)


def _expert_kernel(be_ref, first_ref, nused_ref, slot_ref, next_ref, x_ref, wg_hbm, wu_hbm,
                   wd_hbm, o_ref, wg32, wu32, wd32, wg16, wu16, wd16, wsem, *, layer):
    j = pl.program_id(0)

    def weight_copies(e, s):
        return [pltpu.make_async_copy(hbm.at[layer, e], buf.at[s], wsem.at[s])
                for hbm, buf in ((wg_hbm, wg32), (wu_hbm, wu32), (wd_hbm, wd32))]

    @pl.when((first_ref[j] == 1) & (j < nused_ref[0]))
    def _():
        s = slot_ref[j]

        @pl.when(j == 0)
        def _():
            for cp in weight_copies(be_ref[0], 0):
                cp.start()

        for cp in weight_copies(be_ref[j], s):
            cp.wait()

        @pl.when(next_ref[j] >= 0)
        def _():
            for cp in weight_copies(next_ref[j], 1 - s):
                cp.start()

        wg16[...] = wg32[s].astype(BF16)
        wu16[...] = wu32[s].astype(BF16)
        wd16[...] = wd32[s].astype(BF16)

    @pl.when(j < nused_ref[0])
    def _():
        x_lo, x_hi = _unpack_pairs(x_ref[...])
        half = x_lo.shape[1]
        hg = _dot(x_lo, wg16[:half, :]) + _dot(x_hi, wg16[half:, :])
        hu = _dot(x_lo, wu16[:half, :]) + _dot(x_hi, wu16[half:, :])
        hid = (_silu(hg) * hu).astype(BF16)
        o_ref[...] = _pack_pairs(_dot(hid, wd16[...]))

    @pl.when(j >= nused_ref[0])
    def _():
        o_ref[...] = jnp.zeros_like(o_ref)


def _expert_ffn(xs, blocks, w_gate, w_up, w_down, layer, rows):
    n_rows, dh = xs.shape
    n_blocks = n_rows // rows
    d, ff = w_gate.shape[-2:]
    assert d == 2 * dh
    any_spec = pl.BlockSpec(memory_space=pl.ANY)
    return pl.pallas_call(
        functools.partial(_expert_kernel, layer=layer),
        out_shape=jax.ShapeDtypeStruct((n_rows, dh), U32),
        grid_spec=pltpu.PrefetchScalarGridSpec(
            num_scalar_prefetch=5,
            grid=(n_blocks,),
            in_specs=[pl.BlockSpec((rows, dh),
                                   lambda j, be, fi, nu, *_: (
                                       jnp.maximum(jnp.minimum(j, nu[0] - 1), 0), 0)),
                      any_spec, any_spec, any_spec],
            out_specs=pl.BlockSpec((rows, dh), lambda j, *_: (j, 0)),
            scratch_shapes=[pltpu.VMEM((2, d, ff), F32), pltpu.VMEM((2, d, ff), F32),
                            pltpu.VMEM((2, ff, d), F32),
                            pltpu.VMEM((d, ff), BF16), pltpu.VMEM((d, ff), BF16),
                            pltpu.VMEM((ff, d), BF16), pltpu.SemaphoreType.DMA((2,))]),
        compiler_params=_cparams(("arbitrary",)),
        name="moe_expert_ffn",
    )(*blocks, xs, w_gate, w_up, w_down)


def _combine_kernel(cnt_ref, off_ref, dst_ref, tot_ref, x_ref, e_ref, rank_ref, gt_ref, ys_hbm,
                    gfin_ref, o_ref, ybuf, sems, *, final_norm):
    j = pl.program_id(0)
    nb = pl.num_programs(0)
    tm = x_ref.shape[0]
    slot = lax.rem(j, 2)

    def copies(tile, s, wait):
        if wait:
            _segment_wait(tot_ref, tile, ybuf.at[s], ys_hbm, sems.at[s], False)
        else:
            _segment_copies(cnt_ref, off_ref, dst_ref, tile * N_EXPERTS, ybuf.at[s], ys_hbm,
                            sems.at[s], False, False)

    @pl.when(j == 0)
    def _():
        ybuf[...] = jnp.zeros_like(ybuf)
        copies(0, 0, False)

    @pl.when(j + 1 < nb)
    def _():
        copies(j + 1, 1 - slot, False)

    copies(j, slot, True)
    lidx = _local_index(e_ref, rank_ref, off_ref, j)
    hi = (lidx >> DIGIT_BITS).astype(F32)
    lo = (lidx & ((1 << DIGIT_BITS) - 1)).astype(F32)
    rows8 = jnp.concatenate([hi, lo, gt_ref[...], jnp.zeros((2, tm), F32)], axis=0).astype(BF16)
    sel = jnp.where(_iota((SUBLANES, 6 * LANES), 0)
                    == (_iota((SUBLANES, 6 * LANES), 1) >> (LANES.bit_length() - 1)),
                    1.0, 0.0).astype(BF16)
    cols = _dot_tn(rows8, sel)
    col = lambda k: cols[:, k * LANES:(k + 1) * LANES]
    la = col(0) * float(1 << DIGIT_BITS) + col(2)
    lb = col(1) * float(1 << DIGIT_BITS) + col(3)
    ga, gb = col(4), col(5)
    lane = _iota((tm, LANES), 1).astype(F32)
    mix = jnp.concatenate(
        [jnp.where(lane + float(q * LANES) == la, ga, 0.0)
         + jnp.where(lane + float(q * LANES) == lb, gb, 0.0)
         for q in range(_seg_local_rows(tm) // LANES)], axis=1).astype(BF16)
    y_lo, y_hi = _unpack_pairs(ybuf[slot])
    y = x_ref[...] + jnp.concatenate([_dot(mix, y_lo), _dot(mix, y_hi)], axis=1)
    if final_norm:
        ms = jnp.mean(y * y, axis=-1, keepdims=True)
        y = y * lax.rsqrt(ms + RMS_EPS) * gfin_ref[...]
    o_ref[...] = y


def _combine(x2d, ys, e2, rank2, gates2, seg, g_final, final_norm, tm):
    n, d = x2d.shape
    return pl.pallas_call(
        functools.partial(_combine_kernel, final_norm=final_norm),
        out_shape=jax.ShapeDtypeStruct((n, d), F32),
        grid_spec=pltpu.PrefetchScalarGridSpec(
            num_scalar_prefetch=4,
            grid=(n // tm,),
            in_specs=[pl.BlockSpec((tm, d), lambda j, *_: (j, 0)),
                      pl.BlockSpec((2, tm), lambda j, *_: (0, j)),
                      pl.BlockSpec((2, tm), lambda j, *_: (0, j)),
                      pl.BlockSpec((2, tm), lambda j, *_: (0, j)),
                      pl.BlockSpec(memory_space=pl.ANY),
                      pl.BlockSpec((1, d), lambda j, *_: (0, 0))],
            out_specs=pl.BlockSpec((tm, d), lambda j, *_: (j, 0)),
            scratch_shapes=[pltpu.VMEM((2, _seg_local_rows(tm), d // 2), U32),
                            pltpu.SemaphoreType.DMA((2,))]),
        compiler_params=_cparams(("arbitrary",)),
        name="moe_combine",
    )(*seg, x2d, e2, rank2, gates2, ys, g_final.reshape(1, d))


def _route_plan(tile_cnt, n_tok, rows, tm):
    n_tiles = n_tok // tm
    seg_cnt = (tile_cnt + SUBLANES - 1) // SUBLANES * SUBLANES
    counts = jnp.sum(seg_cnt, axis=0)
    padded = (counts + rows - 1) // rows * rows
    pend = jnp.cumsum(padded)
    pstart = pend - padded
    seg_dst = pstart[None, :] + jnp.cumsum(seg_cnt, axis=0) - seg_cnt
    seg_off = jnp.cumsum(seg_cnt, axis=1) - seg_cnt
    n_blocks = -(-(2 * n_tok + n_tiles * N_EXPERTS * SUBLANES) // rows) + N_EXPERTS
    starts = jnp.arange(n_blocks, dtype=I32) * rows
    block_e = jnp.minimum(jnp.sum((starts[:, None] >= pend[None, :]).astype(I32), axis=1),
                          N_EXPERTS - 1)
    first = jnp.concatenate([jnp.ones((1,), I32),
                             (block_e[1:] != block_e[:-1]).astype(I32)])
    n_used = (pend[-1] // rows).astype(I32).reshape(1)
    bidx = jnp.arange(n_blocks, dtype=I32)
    w_slot = (jnp.cumsum(first) - 1) % 2
    later_first = (bidx[None, :] > bidx[:, None]) & (first[None, :] == 1) & (bidx[None, :] < n_used)
    next_e = jnp.where(jnp.any(later_first, axis=1),
                       block_e[jnp.argmax(later_first, axis=1)], -1)
    blocks = (block_e.astype(I32), first, n_used, w_slot.astype(I32), next_e.astype(I32))
    seg = tuple(a.reshape(-1).astype(I32)
                for a in (seg_cnt, seg_off, seg_dst, jnp.sum(seg_cnt, axis=1)))
    tails = ((padded - counts).astype(I32), (pstart + counts).astype(I32))
    return seg, tails, blocks, n_blocks * rows


def _moe(x2d, route, w_gate, w_up, w_down, layer, g_final, final_norm, tm, rows=ROUTE_ROWS):
    e2, gates2, rank2, cnt, hn16 = route
    seg, tails, blocks, n_rows = _route_plan(cnt[:, :, 0].astype(I32), e2.shape[1], rows, tm)
    xs = _dispatch(hn16, e2, rank2, seg, tails, blocks[2], n_rows, rows, tm)
    ys = _expert_ffn(xs, blocks, w_gate, w_up, w_down, layer, rows)
    return _combine(x2d, ys, e2, rank2, gates2, seg, g_final, final_norm, tm)


def kernel(x, norm_mix, norm_ffn, norm_final, ab_w_in, rw_mu, rw_w0, rw_w2, rw_a0, rw_a2, rw_g2, rw_k_k, rw_k_a, rw_r_k, rw_ln_w, rw_ln_b, lru_conv_w, lru_conv_b, lru_w_a, lru_b_a, lru_w_x, lru_b_x, lru_lambda, ab_w_out, c_w_in, c_lower_bound, c_norm_w, c_w_out, moe_w_group, moe_b_group, moe_w_expert, moe_b_expert, moe_w_gate, moe_w_up, moe_w_down):
    bsz, t, d = x.shape
    depth = norm_mix.shape[0]
    n = bsz * t
    lbs = jnp.cumsum(jax.nn.softmax(c_lower_bound.astype(F32), axis=0), axis=0)
    lbs = lbs - lbs[0]
    x2d = x.reshape(n, d)
    for layer in range(depth):
        j = layer // 2
        if layer % 2 == 0:
            rw_cols = rw_mu.shape[1]
            width = rw_w0.shape[1]
            p_a, yb = _proj_lru(x2d.reshape(bsz, t, d), norm_mix[layer], ab_w_in[j], rw_cols,
                                lru_conv_w[j], lru_conv_b[j], lru_w_a[j], lru_b_a[j],
                                lru_w_x[j], lru_b_x[j], lru_lambda[j])
            prep = _rwkv_prep(p_a, rw_mu[j], rw_w0[j], rw_w2[j], rw_a0[j], rw_a2[j], rw_g2[j],
                              rw_k_k[j], rw_k_a[j], rw_r_k[j].reshape(-1))
            ya = _rwkv_scan(*prep, rw_ln_w[j], rw_ln_b[j])
            ys = [ya.reshape(n, width), yb.reshape(n, -1)]
            ws = [ab_w_out[j][:width], ab_w_out[j][width:]]
        else:
            hw = c_norm_w.shape[1]
            q, f, i_, g = _norm_matmul(x2d, norm_mix[layer], c_w_in[j], (hw,) * 4,
                                       (BF16, F32, BF16, BF16))
            shp = (bsz, t, hw)
            o = _gla(q.reshape(shp), f.reshape(shp), i_.reshape(shp), g.reshape(shp),
                     lbs[layer], c_norm_w[j])
            ys = [o.reshape(n, hw)]
            ws = [c_w_out[j]]
        tm = min(ROUTE_TILE, n)
        x2d = _proj_residual(x2d, ys, ws)
        route = _router(x2d, norm_ffn[layer], moe_w_group[layer], moe_b_group[layer],
                        moe_w_expert[layer], moe_b_expert[layer], tm)
        x2d = _moe(x2d, route, moe_w_gate, moe_w_up, moe_w_down, layer, norm_final,
                   layer == depth - 1, tm)
    return x2d.reshape(bsz, t, d)
```

```python
import functools

import jax
import jax.numpy as jnp
from jax import lax
from jax.experimental import pallas as pl
from jax.experimental.pallas import tpu as pltpu

F32 = jnp.float32
BF16 = jnp.bfloat16
I32 = jnp.int32
U32 = jnp.uint32

RMS_EPS = 1e-6
RWKV_GN_EPS = 64e-5
LRU_C = 8.0
CHUNK = 64
CHUNK_SHIFT = CHUNK.bit_length() - 1
HEAD64 = 64
HEAD_SHIFT = HEAD64.bit_length() - 1
LANES = 128
SUBLANES = 8
N_GROUPS = 4
EXPERTS_PER_GROUP = 8
N_EXPERTS = N_GROUPS * EXPERTS_PER_GROUP
ROUTE_ROWS = 512
ROUTE_TILE = 512
DIGIT_BITS = 8
VMEM_LIMIT = 56 * 1024 * 1024


def _cparams(sem):
    return pltpu.CompilerParams(dimension_semantics=sem, vmem_limit_bytes=VMEM_LIMIT)


def _sigmoid(x):
    return 0.5 * jnp.tanh(0.5 * x) + 0.5


def _pack_pairs(x):
    w = x.shape[1] // 2
    lo = lax.bitcast_convert_type(x[:, :w].astype(BF16).astype(F32), U32)
    hi = lax.bitcast_convert_type(x[:, w:].astype(BF16).astype(F32), U32)
    return (lo >> 16) | hi


def _unpack_pairs(p):
    lo = lax.bitcast_convert_type(p << 16, F32).astype(BF16)
    hi = lax.bitcast_convert_type(p & jnp.uint32(0xFFFF0000), F32).astype(BF16)
    return lo, hi


def _softplus(x):
    return jnp.maximum(x, 0.0) + jnp.log(1.0 + jnp.exp(-jnp.abs(x)))


def _silu(x):
    h = 0.5 * x
    return h * jnp.tanh(h) + h


def _gelu_tanh(x):
    return 0.5 * x * (1.0 + jnp.tanh(0.7978845608028654 * (x + 0.044715 * x * x * x)))


def _dot(a, b):
    return jnp.dot(a, b, preferred_element_type=F32)


def _dot_nt(a, b):
    return lax.dot_general(a, b, (((1,), (1,)), ((), ())), preferred_element_type=F32)


def _dot_tn(a, b):
    return lax.dot_general(a, b, (((0,), (0,)), ((), ())), preferred_element_type=F32)


def _split2(x):
    hi = x.astype(BF16)
    lo = (x - hi.astype(F32)).astype(BF16)
    return hi, lo


def _dot_exact_rhs(x, m_bf16):
    hi, lo = _split2(x)
    return _dot(hi, m_bf16) + _dot(lo, m_bf16)


def _dot_exact_lhs(m_bf16, x):
    hi, lo = _split2(x)
    return _dot(m_bf16, hi) + _dot(m_bf16, lo)


def _dot_split(a, w_hi, w_lo):
    a_hi, a_lo = _split2(a)
    return _dot(a_hi, w_hi) + (_dot(a_lo, w_hi) + _dot(a_hi, w_lo))


def _hi_lo(w):
    hi = w.astype(BF16)
    return hi, (w - hi.astype(F32)).astype(BF16)


def _iota(shape, dim):
    return lax.broadcasted_iota(I32, shape, dim)


def _norm_matmul_kernel(x_ref, g_ref, w_ref, *o_refs, splits):
    x = x_ref[...]
    ms = jnp.mean(x * x, axis=-1, keepdims=True)
    y = (x * lax.rsqrt(ms + RMS_EPS) * g_ref[...]).astype(BF16)
    off = 0
    for o_ref, n in zip(o_refs, splits):
        o_ref[...] = _dot(y, w_ref[:, off:off + n]).astype(o_ref.dtype)
        off += n


def _norm_matmul(x2d, g, w, splits, out_dtypes, tm=1024):
    n, d = x2d.shape
    tm = min(tm, n)
    ncols = w.shape[1]
    assert sum(splits) == ncols and n % tm == 0 and len(out_dtypes) == len(splits)
    return pl.pallas_call(
        functools.partial(_norm_matmul_kernel, splits=splits),
        out_shape=[jax.ShapeDtypeStruct((n, s), dt) for s, dt in zip(splits, out_dtypes)],
        grid=(n // tm,),
        in_specs=[pl.BlockSpec((tm, d), lambda i: (i, 0)),
                  pl.BlockSpec((1, d), lambda i: (0, 0)),
                  pl.BlockSpec((d, ncols), lambda i: (0, 0))],
        out_specs=[pl.BlockSpec((tm, s), lambda i: (i, 0)) for s in splits],
        compiler_params=_cparams(("parallel",)),
        name="norm_matmul",
    )(x2d, g.reshape(1, d), w.astype(BF16))


def _proj_residual_kernel(*refs, n_in):
    x_ref = refs[0]
    y_refs = refs[1:1 + n_in]
    w_refs = refs[1 + n_in:1 + 2 * n_in]
    o_ref = refs[1 + 2 * n_in]
    acc = x_ref[...]
    for y_ref, w_ref in zip(y_refs, w_refs):
        acc = acc + _dot(y_ref[...].astype(BF16), w_ref[...])
    o_ref[...] = acc


def _proj_residual(x2d, ys, ws, tm=1024):
    n, d = x2d.shape
    tm = min(tm, n)
    n_in = len(ys)
    in_specs = [pl.BlockSpec((tm, d), lambda i: (i, 0))]
    in_specs += [pl.BlockSpec((tm, y.shape[1]), lambda i: (i, 0)) for y in ys]
    in_specs += [pl.BlockSpec(w.shape, lambda i: (0, 0)) for w in ws]
    return pl.pallas_call(
        functools.partial(_proj_residual_kernel, n_in=n_in),
        out_shape=jax.ShapeDtypeStruct((n, d), F32),
        grid=(n // tm,),
        in_specs=in_specs,
        out_specs=pl.BlockSpec((tm, d), lambda i: (i, 0)),
        compiler_params=_cparams(("parallel",)),
        name="proj_residual",
    )(x2d, *ys, *[w.astype(BF16) for w in ws])


def _router_kernel(x_ref, g_ref, w_ref, b_ref, upper_ref, e_o, gate_o, rank_o, cnt_o, hn_o):
    _route_tile(x_ref[...], g_ref, w_ref, b_ref, upper_ref, e_o, gate_o, rank_o, cnt_o, hn_o)


def _router(x2d, g_ffn, w_group, b_group, w_expert, b_expert, tm):
    n, d = x2d.shape
    wt = jnp.zeros((d, LANES), F32)
    wt = wt.at[:, 0:N_GROUPS].set(w_group).at[:, 8:8 + N_EXPERTS].set(w_expert)
    wt = jnp.stack(_hi_lo(wt))
    bt = jnp.zeros((LANES,), F32)
    bt = bt.at[0:N_GROUPS].set(b_group).at[8:8 + N_EXPERTS].set(b_expert)
    bt = jnp.broadcast_to(bt[:, None], (LANES, LANES))
    ti = jnp.arange(tm, dtype=I32)
    upper = (ti[:, None] < ti[None, :]).astype(BF16)
    tok2 = pl.BlockSpec((2, tm), lambda i: (0, i))
    return pl.pallas_call(
        _router_kernel,
        out_shape=[jax.ShapeDtypeStruct((2, n), I32),
                   jax.ShapeDtypeStruct((2, n), F32),
                   jax.ShapeDtypeStruct((2, n), I32),
                   jax.ShapeDtypeStruct((n // tm, N_EXPERTS, LANES), F32),
                   jax.ShapeDtypeStruct((n, d), BF16)],
        grid=(n // tm,),
        in_specs=[pl.BlockSpec((tm, d), lambda i: (i, 0)),
                  pl.BlockSpec((1, d), lambda i: (0, 0)),
                  pl.BlockSpec((2, d, LANES), lambda i: (0, 0, 0)),
                  pl.BlockSpec((LANES, LANES), lambda i: (0, 0)),
                  pl.BlockSpec((tm, tm), lambda i: (0, 0))],
        out_specs=[tok2, tok2, tok2,
                   pl.BlockSpec((1, N_EXPERTS, LANES), lambda i: (i, 0, 0)),
                   pl.BlockSpec((tm, d), lambda i: (i, 0))],
        compiler_params=_cparams(("parallel",)),
        name="moe_router",
    )(x2d, g_ffn.reshape(1, d), wt, bt, upper)


def _rwkv_prep_kernel(p_ref, mu_ref, w0_ref, a0_ref, kk_s_ref, ka_ref, rk_ref,
                      wcomb_ref, g2_ref, tril_ref,
                      rt_o, kkt_o, kh_o, bh_o, v_o, g_o, bonus_o, gc_o,
                      prev_ref, *, width):
    tm = p_ref.shape[1]

    @pl.when(pl.program_id(1) == 0)
    def _():
        prev_ref[...] = jnp.zeros_like(prev_ref)

    p = p_ref[0]
    rolled = pltpu.roll(p, 1, axis=0)
    head = jnp.where(_iota((SUBLANES, p.shape[1]), 0) == 0, prev_ref[...], rolled[0:SUBLANES])
    prev = jnp.concatenate([head, rolled[SUBLANES:]], axis=0)
    prev_ref[...] = p[tm - 1:tm, :]
    ps = p + (prev - p) * mu_ref[...]

    r = ps[:, 0:width]
    k = ps[:, width:2 * width]
    v = ps[:, 2 * width:3 * width]
    lowrank = ps[:, 3 * width:3 * width + LANES]
    gl = ps[:, 3 * width + LANES:3 * width + 2 * LANES]

    lane = _iota(lowrank.shape, 1)
    lr_in = jnp.where(lane < HEAD64, jnp.tanh(lowrank), lowrank)
    t12 = _dot_split(lr_in, wcomb_ref[0], wcomb_ref[1])
    wlog = -_softplus(-(w0_ref[...] + t12[:, :width])) - 0.5
    lw = -jnp.exp(wlog)
    a = _sigmoid(a0_ref[...] + t12[:, width:])
    g = _dot_split(_sigmoid(gl), g2_ref[0], g2_ref[1])

    seg = jnp.where((_iota((LANES, LANES), 0) >> HEAD_SHIFT)
                    == (_iota((LANES, LANES), 1) >> HEAD_SHIFT),
                    1.0, 0.0).astype(BF16)

    def head_sums(x):
        return jnp.concatenate([_dot_exact_rhs(x[:, q * LANES:(q + 1) * LANES], seg)
                                for q in range(width // LANES)], axis=1)

    kk = k * kk_s_ref[...]
    nrm = jnp.sqrt(head_sums(kk * kk))
    kk = kk / jnp.maximum(nrm, 1e-12)
    k2 = k * (1.0 + (a - 1.0) * ka_ref[...])
    bonus = head_sums(r * k2 * rk_ref[...]) * v

    b = kk * a

    n_chunk = tm // CHUNK
    cum = _dot_exact_lhs(tril_ref[...], lw)
    cum3 = cum.reshape(n_chunk, CHUNK, width)
    cend = cum3[:, CHUNK - 1:CHUNK, :]
    e_neg = jnp.exp(-cum)

    rt_o[0] = (r * jnp.exp(cum)).astype(BF16)
    kkt_o[0] = (kk * jnp.exp(cum - lw)).astype(BF16)
    kh_o[0] = (k2 * e_neg).astype(BF16)
    bh_o[0] = (b * e_neg).astype(BF16)
    v_o[0] = v.astype(BF16)
    g_o[0] = g.astype(BF16)
    bonus_o[0] = bonus.astype(BF16)
    gc_o[0] = jnp.exp(cend)


def _rwkv_prep(p_a, mu, w0, w2, a0, a2, g2, k_k, k_a, r_k, tm=512):
    b, t, cols = p_a.shape
    width = w0.shape[0]
    rank = w2.shape[0]
    assert rank == HEAD64 and a2.shape[0] == HEAD64 and g2.shape[0] == LANES
    zeros = jnp.zeros((rank, width), F32)
    wcomb = jnp.concatenate([jnp.concatenate([w2, zeros], 1),
                             jnp.concatenate([zeros, a2], 1)], 0)
    row = lambda a_: a_.reshape(1, -1)
    const = lambda shp: pl.BlockSpec(shp, lambda i, j: (0,) * len(shp))
    tok = pl.BlockSpec((1, tm, width), lambda i, j: (i, j, 0))
    n_chunk = tm // CHUNK
    ti = jnp.arange(tm, dtype=I32)
    tril_bd = (((ti[:, None] >> CHUNK_SHIFT) == (ti[None, :] >> CHUNK_SHIFT))
               & (ti[None, :] <= ti[:, None])).astype(BF16)
    return pl.pallas_call(
        functools.partial(_rwkv_prep_kernel, width=width),
        out_shape=([jax.ShapeDtypeStruct((b, t, width), BF16)] * 7
                   + [jax.ShapeDtypeStruct((b, t // CHUNK, 1, width), F32)]),
        grid=(b, t // tm),
        in_specs=[pl.BlockSpec((1, tm, cols), lambda i, j: (i, j, 0)),
                  const((1, cols)), const((1, width)), const((1, width)), const((1, width)),
                  const((1, width)), const((1, width)),
                  const((2, LANES, 2 * width)), const((2, LANES, width)), const((tm, tm))],
        out_specs=[tok] * 7 + [pl.BlockSpec((1, n_chunk, 1, width), lambda i, j: (i, j, 0, 0))],
        scratch_shapes=[pltpu.VMEM((1, cols), F32)],
        compiler_params=_cparams(("parallel", "arbitrary")),
        name="rwkv_prep",
    )(p_a, row(mu), row(w0), row(a0), row(k_k), row(k_a), row(r_k),
      jnp.stack(_hi_lo(wcomb)), jnp.stack(_hi_lo(g2)), tril_bd)


def _rwkv_scan_kernel(rt_ref, kkt_ref, kh_ref, bh_ref, v_ref, g_ref, bonus_ref,
                      gc_ref, lnw_ref, lnb_ref, o_ref, st_ref, *, n_batch, n_pairs):
    c = CHUNK

    @pl.when(pl.program_id(0) == 0)
    def _():
        st_ref[...] = jnp.zeros_like(st_ref)

    rr = _iota((LANES, LANES), 0)
    cc = _iota((LANES, LANES), 1)
    mask_bd = jnp.where((rr >> HEAD_SHIFT) == (cc >> HEAD_SHIFT), 1.0, 0.0)
    mask_bd16 = mask_bd.astype(BF16)
    t_i = _iota((c, LANES), 0)
    s_i = _iota((c, LANES), 1) & (c - 1)
    strict = jnp.where(s_i < t_i, 1.0, 0.0)
    incl = jnp.where(s_i <= t_i, 1.0, 0.0)
    eye_ss = jnp.where(s_i == t_i, 1.0, 0.0)

    def bd16(x):
        return jnp.concatenate([x, x], axis=0) * mask_bd16

    def b16(xs):
        return [x.astype(BF16) for x in xs]

    chains = [(bi, p) for bi in range(n_batch) for p in range(n_pairs)]
    n_ch = len(chains)
    idx = range(n_ch)

    def ld(ref):
        return [ref[bi, :, p * LANES:(p + 1) * LANES] for bi, p in chains]

    rt, kkt, kh, bh, v = (ld(ref) for ref in (rt_ref, kkt_ref, kh_ref, bh_ref, v_ref))

    lhs = [jnp.concatenate([kkt[i], rt[i]], axis=0) for i in idx]
    rhs = [jnp.concatenate([bd16(kh[i]), bd16(bh[i])], axis=0) for i in idx]
    gm = [_dot_nt(lhs[i], rhs[i]) for i in idx]
    l_kv = [gm[i][:c, :LANES] * strict for i in idx]
    xp = [-(gm[i][:c, LANES:] * strict) for i in idx]
    pm = [gm[i][c:, :LANES] * incl for i in idx]
    nqm = [-(gm[i][c:, LANES:] * incl) for i in idx]

    tinv = [eye_ss + xp[i] for i in idx]
    xp16 = b16(xp)
    xpb = [bd16(x) for x in xp16]
    xp16 = b16([_dot(xp16[i], xpb[i]) for i in idx])
    for _ in range(c.bit_length() - 3):
        xpb = [bd16(x) for x in xp16]
        both = [_dot(jnp.concatenate([tinv[i].astype(BF16), xp16[i]], axis=0), xpb[i])
                for i in idx]
        tinv = [tinv[i] + both[i][:c] for i in idx]
        xp16 = b16([both[i][c:] for i in idx])
    xpb = [bd16(x) for x in xp16]
    t16 = b16(tinv)
    tinv = [tinv[i] + _dot(t16[i], xpb[i]) for i in idx]
    tinv16 = b16(tinv)

    v_bd = [bd16(x) for x in v]
    on_v = [_dot(jnp.concatenate([l_kv[i], pm[i]], axis=0).astype(BF16), v_bd[i]) for i in idx]
    w16 = b16([x[:c] for x in on_v])
    tkw = [_dot(tinv16[i], jnp.concatenate([bd16(kkt[i]), bd16(w16[i])], axis=1)) for i in idx]
    tk16 = b16([x[:, :LANES] for x in tkw])

    st = [st_ref[i] for i in idx]
    st16 = b16(st)
    on_st = [_dot_nt(jnp.concatenate([tk16[i], rt[i]], axis=0), st16[i]) for i in idx]
    u = [on_st[i][:c] + tkw[i][:, LANES:] for i in idx]
    u16 = b16(u)
    nqm16 = b16(nqm)
    y = [on_st[i][c:] + on_v[i][c:] + _dot(nqm16[i], bd16(u16[i])) for i in idx]
    upd = [_dot_tn(jnp.concatenate([v[i], -u16[i]], axis=0),
                   jnp.concatenate([kh[i], bh[i]], axis=0)) for i in idx]
    for i, (bi, p) in enumerate(chains):
        st_ref[i] = (st[i] + upd[i] * mask_bd) * gc_ref[bi, 0, :, p * LANES:(p + 1) * LANES]

    ys = jnp.concatenate(y, axis=0)
    mean = _dot_exact_rhs(ys, mask_bd16) * (1.0 / HEAD64)
    d = ys - mean
    var = _dot_exact_rhs(d * d, mask_bd16) * (1.0 / HEAD64)
    dn = d * lax.rsqrt(var + RWKV_GN_EPS)
    for i, (bi, p) in enumerate(chains):
        sl = slice(p * LANES, (p + 1) * LANES)
        yn = dn[i * c:(i + 1) * c] * lnw_ref[:, sl] + lnb_ref[:, sl]
        o_ref[bi, :, sl] = ((yn + bonus_ref[bi, :, sl]) * g_ref[bi, :, sl]).astype(o_ref.dtype)


def _rwkv_scan(rt, kkt, kh, bh, v, g, bonus, gc, ln_w, ln_b):
    bsz, t, width = rt.shape
    n_pairs = width // LANES
    tok = pl.BlockSpec((bsz, CHUNK, width), lambda j: (0, j, 0))
    const = pl.BlockSpec((1, width), lambda j: (0, 0))
    return pl.pallas_call(
        functools.partial(_rwkv_scan_kernel, n_batch=bsz, n_pairs=n_pairs),
        out_shape=jax.ShapeDtypeStruct((bsz, t, width), BF16),
        grid=(t // CHUNK,),
        in_specs=[tok] * 7 + [pl.BlockSpec((bsz, 1, 1, width), lambda j: (0, j, 0, 0)),
                              const, const],
        out_specs=tok,
        scratch_shapes=[pltpu.VMEM((bsz * n_pairs, LANES, LANES), F32)],
        compiler_params=_cparams(("arbitrary",)),
        name="rwkv_scan",
    )(rt, kkt, kh, bh, v, g, bonus, gc, ln_w.reshape(1, -1), ln_b.reshape(1, -1))


def _proj_lru_kernel(x_ref, gn_ref, w_ref, cw_ref, cb_ref, wg_ref, ba_ref, bx_ref, lam_ref,
                     pa_o, o_ref, xcarry_ref, hcarry_ref, a_s, u_s, *, width, rw_cols):
    tm = x_ref.shape[1]

    @pl.when(pl.program_id(1) == 0)
    def _():
        xcarry_ref[...] = jnp.zeros_like(xcarry_ref)
        hcarry_ref[...] = jnp.zeros_like(hcarry_ref)

    x = x_ref[0]
    ms = jnp.mean(x * x, axis=-1, keepdims=True)
    hn = (x * lax.rsqrt(ms + RMS_EPS) * gn_ref[...]).astype(BF16)
    p_b = _dot(hn, w_ref[:, rw_cols:])
    pa_o[0] = _dot(hn, w_ref[:, :rw_cols])

    gate = p_b[:, 0:width]
    xb = p_b[:, width:2 * width]
    carry8 = xcarry_ref[...]
    row8 = _iota((8, width), 0)

    def shifted(s):
        rolled = pltpu.roll(xb, s, axis=0)
        first = jnp.where(row8 < s, pltpu.roll(carry8, s, axis=0), rolled[0:8])
        return jnp.concatenate([first, rolled[8:]], axis=0)

    xc = (cw_ref[0:1, :] * shifted(3) + cw_ref[1:2, :] * shifted(2)
          + cw_ref[2:3, :] * shifted(1) + cw_ref[3:4, :] * xb + cb_ref[...])
    xcarry_ref[...] = xb[tm - 8:tm, :]

    gates = _dot(xc.astype(BF16), wg_ref[...])
    ig = _sigmoid(gates[:, width:] + bx_ref[...])
    half_rate = (-0.5 * LRU_C) * _softplus(-lam_ref[...])
    log_a = half_rate * jnp.tanh(0.5 * (gates[:, :width] + ba_ref[...])) + half_rate
    a = jnp.exp(log_a)
    a_s[...] = a
    u_s[...] = jnp.sqrt(1.0 - a * a) * ig * xc

    m1 = row8 >= 1
    m2 = row8 >= 2
    m4 = row8 >= 4

    def body(i, h):
        off = pl.multiple_of(i * 8, 8)
        a8 = a_s[pl.ds(off, 8), :]
        u8 = u_s[pl.ds(off, 8), :]
        for s, m in ((1, m1), (2, m2), (4, m4)):
            u_sh = jnp.where(m, pltpu.roll(u8, s, axis=0), 0.0)
            a_sh = jnp.where(m, pltpu.roll(a8, s, axis=0), 1.0)
            u8 = u8 + a8 * u_sh
            a8 = a8 * a_sh
        h8 = u8 + a8 * h
        u_s[pl.ds(off, 8), :] = h8
        return jnp.broadcast_to(h8[7:8, :], (8, width))

    h_last = lax.fori_loop(0, tm // 8, body, hcarry_ref[...])
    hcarry_ref[...] = h_last
    o_ref[0] = (u_s[...] * _gelu_tanh(gate)).astype(o_ref.dtype)


def _block_diag(w):
    nb, di, do = w.shape
    eye = jnp.eye(nb, dtype=w.dtype)
    return (eye[:, None, :, None] * w[:, :, None, :]).reshape(nb * di, nb * do)


def _proj_lru(x3d, g_norm, w_in, rw_cols, conv_w, conv_b, w_a, b_a, w_x, b_x, lam, tm=512):
    b, t, d = x3d.shape
    tm = min(tm, t)
    cols = w_in.shape[1]
    width = (cols - rw_cols) // 2
    wg = jnp.concatenate([_block_diag(w_a), _block_diag(w_x)], axis=1).astype(BF16)
    row = lambda a_: a_.reshape(1, -1)
    const = lambda shp: pl.BlockSpec(shp, lambda i, j: (0, 0))
    return pl.pallas_call(
        functools.partial(_proj_lru_kernel, width=width, rw_cols=rw_cols),
        out_shape=[jax.ShapeDtypeStruct((b, t, rw_cols), F32),
                   jax.ShapeDtypeStruct((b, t, width), BF16)],
        grid=(b, t // tm),
        in_specs=[pl.BlockSpec((1, tm, d), lambda i, j: (i, j, 0)),
                  const((1, d)), const((d, cols)),
                  const(conv_w.shape), const((1, width)), const((width, 2 * width)),
                  const((1, width)), const((1, width)), const((1, width))],
        out_specs=[pl.BlockSpec((1, tm, rw_cols), lambda i, j: (i, j, 0)),
                   pl.BlockSpec((1, tm, width), lambda i, j: (i, j, 0))],
        scratch_shapes=[pltpu.VMEM((8, width), F32), pltpu.VMEM((8, width), F32),
                        pltpu.VMEM((tm, width), F32), pltpu.VMEM((tm, width), F32)],
        compiler_params=_cparams(("parallel", "arbitrary")),
        name="proj_rglru",
    )(x3d, row(g_norm), w_in.astype(BF16), conv_w, row(conv_b), wg, row(b_a), row(b_x), row(lam))


def _gla_kernel(q_ref, f_ref, i_ref, g_ref, lb_ref, nw_ref, o_ref, st_ref, *, n_batch, n_heads):
    c = CHUNK

    @pl.when(pl.program_id(0) == 0)
    def _():
        st_ref[...] = jnp.zeros_like(st_ref)

    trilf = jnp.where(_iota((c, c), 1) <= _iota((c, c), 0), 1.0, 0.0)
    tril16 = trilf.astype(BF16)
    mid = c // 2 - 1
    lb = lb_ref[...]
    fg_c0 = 0.5 + 0.5 * lb
    fg_c1 = 0.5 - 0.5 * lb

    qm, km, v16, gc, em, ecm = [], [], [], [], [], []
    for bi in range(n_batch):
        q = _silu(q_ref[bi].astype(F32))
        fg = fg_c0 + fg_c1 * jnp.tanh(0.5 * f_ref[bi])
        k = 1.0 - fg
        cum = _dot_exact_lhs(tril16, jnp.log(fg))
        cum_c = cum[c - 1:c, :]
        cum_m = cum[mid:mid + 1, :]
        qm.append((q * jnp.exp(cum - cum_m)).astype(BF16))
        km.append((k * jnp.exp(cum_m - cum)).astype(BF16))
        v16.append(i_ref[bi].astype(BF16))
        gc.append(jnp.exp(cum_c))
        em.append(jnp.exp(cum_m))
        ecm.append(jnp.exp(cum_c - cum_m))

    chains = [(bi, h) for bi in range(n_batch) for h in range(n_heads)]
    idx = range(len(chains))

    def hs(xs):
        return [xs[bi][:, h * LANES:(h + 1) * LANES] for bi, h in chains]

    qm_c, km_c, v_c, gc_c, em_c, ecm_c = hs(qm), hs(km), hs(v16), hs(gc), hs(em), hs(ecm)
    scores = [(_dot_nt(qm_c[i], km_c[i]) * trilf).astype(BF16) for i in idx]
    st = [st_ref[i] for i in idx]
    st16 = [(st[i] * em_c[i]).astype(BF16) for i in idx]
    o = [_dot(scores[i], v_c[i]) + _dot_nt(qm_c[i], st16[i]) for i in idx]
    for i in idx:
        st_ref[i] = st[i] * gc_c[i] + _dot_tn(v_c[i], km_c[i]) * ecm_c[i]

    for i, (bi, h) in enumerate(chains):
        sl = slice(h * LANES, (h + 1) * LANES)
        on = o[i] * lax.rsqrt(jnp.mean(o[i] * o[i], axis=-1, keepdims=True) + RMS_EPS)
        gate = _silu(g_ref[bi, :, sl].astype(F32))
        o_ref[bi, :, sl] = (on * nw_ref[:, sl] * gate).astype(o_ref.dtype)


def _gla(q, f, i, g, lower_bound, norm_w):
    bsz, t, width = q.shape
    n_heads = width // LANES
    tok = pl.BlockSpec((bsz, CHUNK, width), lambda j: (0, j, 0))
    const = pl.BlockSpec((1, width), lambda j: (0, 0))
    return pl.pallas_call(
        functools.partial(_gla_kernel, n_batch=bsz, n_heads=n_heads),
        out_shape=jax.ShapeDtypeStruct((bsz, t, width), BF16),
        grid=(t // CHUNK,),
        in_specs=[tok] * 4 + [const, const],
        out_specs=tok,
        scratch_shapes=[pltpu.VMEM((bsz * n_heads, LANES, LANES), F32)],
        compiler_params=_cparams(("arbitrary",)),
        name="hgrn2_gla",
    )(q, f, i, g, lower_bound.reshape(1, -1), norm_w.reshape(1, -1))


def _route_tile(x, g_ref, w_ref, b_ref, upper_ref, e_o, gate_o, rank_o, cnt_o, hn_o):
    ms = jnp.mean(x * x, axis=-1, keepdims=True)
    hn = x * lax.rsqrt(ms + RMS_EPS) * g_ref[...]
    hn_o[...] = hn.astype(BF16)
    lt = _dot_split(hn, w_ref[0], w_ref[1]).T + b_ref[:, 0:1]
    tm = x.shape[0]
    gl = lt[0:8, :]
    row8 = _iota((8, tm), 0)
    gl = jnp.where(row8 < N_GROUPS, gl, -jnp.inf)
    gmax = jnp.max(gl, axis=0, keepdims=True)
    g_sel = jnp.min(jnp.where(gl == gmax, row8, 8), axis=0, keepdims=True)
    g_gate = 1.0 / jnp.sum(jnp.exp(gl - gmax), axis=0, keepdims=True)

    el = jnp.zeros((EXPERTS_PER_GROUP, tm), F32)
    for gi in range(N_GROUPS):
        lo = 8 + gi * EXPERTS_PER_GROUP
        el = jnp.where(g_sel == gi, lt[lo:lo + EXPERTS_PER_GROUP, :], el)
    m1 = jnp.max(el, axis=0, keepdims=True)
    i1 = jnp.min(jnp.where(el == m1, row8, 8), axis=0, keepdims=True)
    el2 = jnp.where(row8 == i1, -jnp.inf, el)
    m2 = jnp.max(el2, axis=0, keepdims=True)
    i2 = jnp.min(jnp.where(el2 == m2, row8, 8), axis=0, keepdims=True)
    e2 = jnp.exp(m2 - m1)
    inv = 1.0 / (1.0 + e2)
    ea = g_sel * EXPERTS_PER_GROUP + i1
    eb = g_sel * EXPERTS_PER_GROUP + i2
    e_o[...] = jnp.concatenate([ea, eb], axis=0)
    gate_o[...] = jnp.concatenate([g_gate * inv, g_gate * e2 * inv], axis=0)

    erow = _iota((N_EXPERTS, tm), 0)
    oh_a = jnp.where(erow == ea, 1.0, 0.0)
    oh_b = jnp.where(erow == eb, 1.0, 0.0)
    pre_a = _dot(oh_a.astype(BF16), upper_ref[...])
    pre_b = _dot(oh_b.astype(BF16), upper_ref[...])
    cnt_a = jnp.sum(oh_a, axis=1, keepdims=True)
    cnt_b = jnp.sum(oh_b, axis=1, keepdims=True)
    rank_a = jnp.sum(oh_a * pre_a, axis=0, keepdims=True)
    rank_b = jnp.sum(oh_b * (pre_b + cnt_a), axis=0, keepdims=True)
    rank_o[...] = jnp.concatenate([rank_a, rank_b], axis=0).astype(I32)
    cnt_o[0] = jnp.broadcast_to(cnt_a + cnt_b, (N_EXPERTS, LANES))


def _seg_local_rows(tm):
    return 2 * tm + N_EXPERTS * SUBLANES


def _segment_copies(cnt_ref, off_ref, dst_ref, base, local, remote, sem, to_remote, wait):
    def body(e, carry):
        c = pl.multiple_of(cnt_ref[base + e], SUBLANES)

        @pl.when(c > 0)
        def _():
            off = 0 if off_ref is None else pl.multiple_of(off_ref[base + e], SUBLANES)
            loc = local.at[pl.ds(off, c), :]
            rem = remote.at[pl.ds(pl.multiple_of(dst_ref[base + e], SUBLANES), c), :]
            cp = (pltpu.make_async_copy(loc, rem, sem) if to_remote
                  else pltpu.make_async_copy(rem, loc, sem))
            if wait:
                cp.wait()
            else:
                cp.start()
        return carry
    lax.fori_loop(0, N_EXPERTS, body, 0, unroll=4)


def _local_index(e_ref, rank_ref, off_ref, tile):
    e = e_ref[...]
    lidx = rank_ref[...]
    for ex in range(N_EXPERTS):
        lidx = lidx + jnp.where(e == ex, off_ref[tile * N_EXPERTS + ex], 0)
    return lidx


def _segment_wait(tot_ref, tile, local, remote, sem, to_remote):
    tot = pl.multiple_of(tot_ref[tile], SUBLANES)

    @pl.when(tot > 0)
    def _():
        loc = local.at[pl.ds(0, tot), :]
        rem = remote.at[pl.ds(0, tot), :]
        cp = (pltpu.make_async_copy(loc, rem, sem) if to_remote
              else pltpu.make_async_copy(rem, loc, sem))
        cp.wait()


def _dispatch_kernel(cnt_ref, off_ref, dst_ref, tot_ref, tcnt_ref, tdst_ref, nused_ref,
                     hn_ref, e_ref, rank_ref, xs_out, sbuf, zbuf, sems, zsem):
    j = pl.program_id(0)
    nb = pl.num_programs(0)
    tm = hn_ref.shape[0]
    slot = lax.rem(j, 2)

    def copies(tile, s, wait):
        if wait:
            _segment_wait(tot_ref, tile, sbuf.at[s], xs_out, sems.at[s], True)
        else:
            _segment_copies(cnt_ref, off_ref, dst_ref, tile * N_EXPERTS, sbuf.at[s], xs_out,
                            sems.at[s], True, False)

    @pl.when(j == 0)
    def _():
        zbuf[...] = jnp.zeros_like(zbuf)
        rows = zbuf.shape[0]
        n_blocks = xs_out.shape[0] // rows
        for wait in (False, True):
            _segment_copies(tcnt_ref, None, tdst_ref, 0, zbuf, xs_out, zsem, True, wait)

            def body(b, carry):
                start = pl.multiple_of(b * rows, SUBLANES)
                cp = pltpu.make_async_copy(zbuf, xs_out.at[pl.ds(start, rows), :], zsem)
                if wait:
                    cp.wait()
                else:
                    cp.start()
                return carry
            lax.fori_loop(nused_ref[0], n_blocks, body, 0)

    @pl.when(j >= 2)
    def _():
        copies(j - 2, slot, True)

    r = _iota((_seg_local_rows(tm), tm), 0)
    lidx = _local_index(e_ref, rank_ref, off_ref, j)
    perm = jnp.where((r == lidx[0:1, :]) | (r == lidx[1:2, :]), 1.0, 0.0).astype(BF16)
    sbuf[slot] = _pack_pairs(_dot(perm, hn_ref[...]))
    copies(j, slot, False)

    @pl.when(j == nb - 1)
    def _():
        copies(j, slot, True)

    @pl.when((j == nb - 1) & (nb >= 2))
    def _():
        copies(j - 1, 1 - slot, True)


def _dispatch(hn16, e2, rank2, seg, tails, n_used, n_rows, rows, tm):
    n, d = hn16.shape
    return pl.pallas_call(
        _dispatch_kernel,
        out_shape=jax.ShapeDtypeStruct((n_rows, d // 2), U32),
        grid_spec=pltpu.PrefetchScalarGridSpec(
            num_scalar_prefetch=7,
            grid=(n // tm,),
            in_specs=[pl.BlockSpec((tm, d), lambda j, *_: (j, 0)),
                      pl.BlockSpec((2, tm), lambda j, *_: (0, j)),
                      pl.BlockSpec((2, tm), lambda j, *_: (0, j))],
            out_specs=pl.BlockSpec(memory_space=pl.ANY),
            scratch_shapes=[pltpu.VMEM((2, _seg_local_rows(tm), d // 2), U32),
                            pltpu.VMEM((rows, d // 2), U32),
                            pltpu.SemaphoreType.DMA((2,)), pltpu.SemaphoreType.DMA(())]),
        compiler_params=_cparams(("arbitrary",)),
        name="moe_dispatch",
    )(*seg, *tails, n_used, hn16, e2, rank2)


def _expert_kernel(be_ref, first_ref, nused_ref, slot_ref, next_ref, x_ref, wg_hbm, wu_hbm,
                   wd_hbm, o_ref, wg32, wu32, wd32, wg16, wu16, wd16, wsem, *, layer):
    j = pl.program_id(0)

    def weight_copies(e, s):
        return [pltpu.make_async_copy(hbm.at[layer, e], buf.at[s], wsem.at[s])
                for hbm, buf in ((wg_hbm, wg32), (wu_hbm, wu32), (wd_hbm, wd32))]

    @pl.when((first_ref[j] == 1) & (j < nused_ref[0]))
    def _():
        s = slot_ref[j]

        @pl.when(j == 0)
        def _():
            for cp in weight_copies(be_ref[0], 0):
                cp.start()

        for cp in weight_copies(be_ref[j], s):
            cp.wait()

        @pl.when(next_ref[j] >= 0)
        def _():
            for cp in weight_copies(next_ref[j], 1 - s):
                cp.start()

        wg16[...] = wg32[s].astype(BF16)
        wu16[...] = wu32[s].astype(BF16)
        wd16[...] = wd32[s].astype(BF16)

    @pl.when(j < nused_ref[0])
    def _():
        x_lo, x_hi = _unpack_pairs(x_ref[...])
        half = x_lo.shape[1]
        hg = _dot(x_lo, wg16[:half, :]) + _dot(x_hi, wg16[half:, :])
        hu = _dot(x_lo, wu16[:half, :]) + _dot(x_hi, wu16[half:, :])
        hid = (_silu(hg) * hu).astype(BF16)
        o_ref[...] = _pack_pairs(_dot(hid, wd16[...]))

    @pl.when(j >= nused_ref[0])
    def _():
        o_ref[...] = jnp.zeros_like(o_ref)


def _expert_ffn(xs, blocks, w_gate, w_up, w_down, layer, rows):
    n_rows, dh = xs.shape
    n_blocks = n_rows // rows
    d, ff = w_gate.shape[-2:]
    assert d == 2 * dh
    any_spec = pl.BlockSpec(memory_space=pl.ANY)
    return pl.pallas_call(
        functools.partial(_expert_kernel, layer=layer),
        out_shape=jax.ShapeDtypeStruct((n_rows, dh), U32),
        grid_spec=pltpu.PrefetchScalarGridSpec(
            num_scalar_prefetch=5,
            grid=(n_blocks,),
            in_specs=[pl.BlockSpec((rows, dh),
                                   lambda j, be, fi, nu, *_: (
                                       jnp.maximum(jnp.minimum(j, nu[0] - 1), 0), 0)),
                      any_spec, any_spec, any_spec],
            out_specs=pl.BlockSpec((rows, dh), lambda j, *_: (j, 0)),
            scratch_shapes=[pltpu.VMEM((2, d, ff), F32), pltpu.VMEM((2, d, ff), F32),
                            pltpu.VMEM((2, ff, d), F32),
                            pltpu.VMEM((d, ff), BF16), pltpu.VMEM((d, ff), BF16),
                            pltpu.VMEM((ff, d), BF16), pltpu.SemaphoreType.DMA((2,))]),
        compiler_params=_cparams(("arbitrary",)),
        name="moe_expert_ffn",
    )(*blocks, xs, w_gate, w_up, w_down)


def _combine_kernel(cnt_ref, off_ref, dst_ref, tot_ref, x_ref, e_ref, rank_ref, gt_ref, ys_hbm,
                    gfin_ref, o_ref, ybuf, sems, *, final_norm):
    j = pl.program_id(0)
    nb = pl.num_programs(0)
    tm = x_ref.shape[0]
    slot = lax.rem(j, 2)

    def copies(tile, s, wait):
        if wait:
            _segment_wait(tot_ref, tile, ybuf.at[s], ys_hbm, sems.at[s], False)
        else:
            _segment_copies(cnt_ref, off_ref, dst_ref, tile * N_EXPERTS, ybuf.at[s], ys_hbm,
                            sems.at[s], False, False)

    @pl.when(j == 0)
    def _():
        ybuf[...] = jnp.zeros_like(ybuf)
        copies(0, 0, False)

    @pl.when(j + 1 < nb)
    def _():
        copies(j + 1, 1 - slot, False)

    copies(j, slot, True)
    lidx = _local_index(e_ref, rank_ref, off_ref, j)
    hi = (lidx >> DIGIT_BITS).astype(F32)
    lo = (lidx & ((1 << DIGIT_BITS) - 1)).astype(F32)
    rows8 = jnp.concatenate([hi, lo, gt_ref[...], jnp.zeros((2, tm), F32)], axis=0).astype(BF16)
    sel = jnp.where(_iota((SUBLANES, 6 * LANES), 0)
                    == (_iota((SUBLANES, 6 * LANES), 1) >> (LANES.bit_length() - 1)),
                    1.0, 0.0).astype(BF16)
    cols = _dot_tn(rows8, sel)
    col = lambda k: cols[:, k * LANES:(k + 1) * LANES]
    la = col(0) * float(1 << DIGIT_BITS) + col(2)
    lb = col(1) * float(1 << DIGIT_BITS) + col(3)
    ga, gb = col(4), col(5)
    lane = _iota((tm, LANES), 1).astype(F32)
    mix = jnp.concatenate(
        [jnp.where(lane + float(q * LANES) == la, ga, 0.0)
         + jnp.where(lane + float(q * LANES) == lb, gb, 0.0)
         for q in range(_seg_local_rows(tm) // LANES)], axis=1).astype(BF16)
    y_lo, y_hi = _unpack_pairs(ybuf[slot])
    y = x_ref[...] + jnp.concatenate([_dot(mix, y_lo), _dot(mix, y_hi)], axis=1)
    if final_norm:
        ms = jnp.mean(y * y, axis=-1, keepdims=True)
        y = y * lax.rsqrt(ms + RMS_EPS) * gfin_ref[...]
    o_ref[...] = y


def _combine(x2d, ys, e2, rank2, gates2, seg, g_final, final_norm, tm):
    n, d = x2d.shape
    return pl.pallas_call(
        functools.partial(_combine_kernel, final_norm=final_norm),
        out_shape=jax.ShapeDtypeStruct((n, d), F32),
        grid_spec=pltpu.PrefetchScalarGridSpec(
            num_scalar_prefetch=4,
            grid=(n // tm,),
            in_specs=[pl.BlockSpec((tm, d), lambda j, *_: (j, 0)),
                      pl.BlockSpec((2, tm), lambda j, *_: (0, j)),
                      pl.BlockSpec((2, tm), lambda j, *_: (0, j)),
                      pl.BlockSpec((2, tm), lambda j, *_: (0, j)),
                      pl.BlockSpec(memory_space=pl.ANY),
                      pl.BlockSpec((1, d), lambda j, *_: (0, 0))],
            out_specs=pl.BlockSpec((tm, d), lambda j, *_: (j, 0)),
            scratch_shapes=[pltpu.VMEM((2, _seg_local_rows(tm), d // 2), U32),
                            pltpu.SemaphoreType.DMA((2,))]),
        compiler_params=_cparams(("arbitrary",)),
        name="moe_combine",
    )(*seg, x2d, e2, rank2, gates2, ys, g_final.reshape(1, d))


def _route_plan(tile_cnt, n_tok, rows, tm):
    n_tiles = n_tok // tm
    seg_cnt = (tile_cnt + SUBLANES - 1) // SUBLANES * SUBLANES
    counts = jnp.sum(seg_cnt, axis=0)
    padded = (counts + rows - 1) // rows * rows
    pend = jnp.cumsum(padded)
    pstart = pend - padded
    seg_dst = pstart[None, :] + jnp.cumsum(seg_cnt, axis=0) - seg_cnt
    seg_off = jnp.cumsum(seg_cnt, axis=1) - seg_cnt
    n_blocks = -(-(2 * n_tok + n_tiles * N_EXPERTS * SUBLANES) // rows) + N_EXPERTS
    starts = jnp.arange(n_blocks, dtype=I32) * rows
    block_e = jnp.minimum(jnp.sum((starts[:, None] >= pend[None, :]).astype(I32), axis=1),
                          N_EXPERTS - 1)
    first = jnp.concatenate([jnp.ones((1,), I32),
                             (block_e[1:] != block_e[:-1]).astype(I32)])
    n_used = (pend[-1] // rows).astype(I32).reshape(1)
    bidx = jnp.arange(n_blocks, dtype=I32)
    w_slot = (jnp.cumsum(first) - 1) % 2
    later_first = (bidx[None, :] > bidx[:, None]) & (first[None, :] == 1) & (bidx[None, :] < n_used)
    next_e = jnp.where(jnp.any(later_first, axis=1),
                       block_e[jnp.argmax(later_first, axis=1)], -1)
    blocks = (block_e.astype(I32), first, n_used, w_slot.astype(I32), next_e.astype(I32))
    seg = tuple(a.reshape(-1).astype(I32)
                for a in (seg_cnt, seg_off, seg_dst, jnp.sum(seg_cnt, axis=1)))
    tails = ((padded - counts).astype(I32), (pstart + counts).astype(I32))
    return seg, tails, blocks, n_blocks * rows


def _moe(x2d, route, w_gate, w_up, w_down, layer, g_final, final_norm, tm, rows=ROUTE_ROWS):
    e2, gates2, rank2, cnt, hn16 = route
    seg, tails, blocks, n_rows = _route_plan(cnt[:, :, 0].astype(I32), e2.shape[1], rows, tm)
    xs = _dispatch(hn16, e2, rank2, seg, tails, blocks[2], n_rows, rows, tm)
    ys = _expert_ffn(xs, blocks, w_gate, w_up, w_down, layer, rows)
    return _combine(x2d, ys, e2, rank2, gates2, seg, g_final, final_norm, tm)


def kernel(x, norm_mix, norm_ffn, norm_final, ab_w_in, rw_mu, rw_w0, rw_w2, rw_a0, rw_a2, rw_g2, rw_k_k, rw_k_a, rw_r_k, rw_ln_w, rw_ln_b, lru_conv_w, lru_conv_b, lru_w_a, lru_b_a, lru_w_x, lru_b_x, lru_lambda, ab_w_out, c_w_in, c_lower_bound, c_norm_w, c_w_out, moe_w_group, moe_b_group, moe_w_expert, moe_b_expert, moe_w_gate, moe_w_up, moe_w_down):
    bsz, t, d = x.shape
    depth = norm_mix.shape[0]
    n = bsz * t
    lbs = jnp.cumsum(jax.nn.softmax(c_lower_bound.astype(F32), axis=0), axis=0)
    lbs = lbs - lbs[0]
    x2d = x.reshape(n, d)
    for layer in range(depth):
        j = layer // 2
        if layer % 2 == 0:
            rw_cols = rw_mu.shape[1]
            width = rw_w0.shape[1]
            p_a, yb = _proj_lru(x2d.reshape(bsz, t, d), norm_mix[layer], ab_w_in[j], rw_cols,
                                lru_conv_w[j], lru_conv_b[j], lru_w_a[j], lru_b_a[j],
                                lru_w_x[j], lru_b_x[j], lru_lambda[j])
            prep = _rwkv_prep(p_a, rw_mu[j], rw_w0[j], rw_w2[j], rw_a0[j], rw_a2[j], rw_g2[j],
                              rw_k_k[j], rw_k_a[j], rw_r_k[j].reshape(-1))
            ya = _rwkv_scan(*prep, rw_ln_w[j], rw_ln_b[j])
            ys = [ya.reshape(n, width), yb.reshape(n, -1)]
            ws = [ab_w_out[j][:width], ab_w_out[j][width:]]
        else:
            hw = c_norm_w.shape[1]
            q, f, i_, g = _norm_matmul(x2d, norm_mix[layer], c_w_in[j], (hw,) * 4,
                                       (BF16, F32, BF16, BF16))
            shp = (bsz, t, hw)
            o = _gla(q.reshape(shp), f.reshape(shp), i_.reshape(shp), g.reshape(shp),
                     lbs[layer], c_norm_w[j])
            ys = [o.reshape(n, hw)]
            ws = [c_w_out[j]]
        tm = min(ROUTE_TILE, n)
        x2d = _proj_residual(x2d, ys, ws)
        route = _router(x2d, norm_ffn[layer], moe_w_group[layer], moe_b_group[layer],
                        moe_w_expert[layer], moe_b_expert[layer], tm)
        x2d = _moe(x2d, route, moe_w_gate, moe_w_up, moe_w_down, layer, norm_final,
                   layer == depth - 1, tm)
    return x2d.reshape(bsz, t, d)
```
